```python
import math
import jax, jax.numpy as jnp
from jax import lax
import numpy as np

D_MODEL = 1024
BATCH = 16
SEQ = 256
DEPTH = 2
DEC_BATCH = 2
DEC_SEQ = 2048
PAST_LEN = 256

GRID_W = 64
BLOCK = 128
WINDOW = 128
N_HEADS = 8
N_KV = 2
HEAD_DIM = 64
Q_GROUP = N_HEADS // N_KV
ATTN_W = N_HEADS * HEAD_DIM
KV_W = N_KV * HEAD_DIM
HY_W = 256
GM_W = 256
GM_HEADS = 4
GM_HEAD_DIM = GM_W // GM_HEADS
GM_CHUNK = 128
MIX_W = ATTN_W + HY_W + GM_W
IN_W = ATTN_W + 2 * KV_W + 3 * HY_W + 2 * GM_W
SHORT_CONV = 3
FILTER_EMB = 33
FILTER_HID = 64
HY_DECAY_HI_PCT = 0.3
HY_DECAY_LO_PCT = 1.5
HY_DECAY_TARGET = 1e-2
N_EXPERTS = 16
EC_CAPACITY = 2
D_EXPERT = 1024
ROPE_THETA = 10000.0
EPS = 1e-6
NEG = -1e30

kernel_name = 'hybrid_prefix_diffusion_step'


def _rmsnorm(x, g):
    xf = x.astype(jnp.float32)
    y = xf * lax.rsqrt(jnp.mean(xf * xf, axis=-1, keepdims=True) + EPS)
    return (y * g.astype(jnp.float32)).astype(x.dtype)


def _modulate(x, g, shift, scale):
    return _rmsnorm(x, g) * (1 + scale) + shift


def _axial_rope(x):
    L = x.shape[1]
    t = jnp.arange(L)
    half = HEAD_DIM // 2
    nf = half // 2
    inv = ROPE_THETA ** (-jnp.arange(nf, dtype=jnp.float32) / nf)

    def rot(xa, pos):
        ang = pos.astype(jnp.float32)[:, None] * inv[None, :]
        cos = jnp.cos(ang)[None, :, None, :].astype(x.dtype)
        sin = jnp.sin(ang)[None, :, None, :].astype(x.dtype)
        x1, x2 = xa[..., :nf], xa[..., nf:]
        return jnp.concatenate([x1 * cos - x2 * sin, x2 * cos + x1 * sin], axis=-1)

    return jnp.concatenate([rot(x[..., :half], t // GRID_W), rot(x[..., half:], t % GRID_W)], axis=-1)


def _attend(q, k, v, bias, sink):
    s = jnp.einsum('bqkgd,bskd->bkgqs', q, k).astype(jnp.float32) * (1.0 / math.sqrt(HEAD_DIM))
    if bias is not None:
        s = s + bias
    sk = jnp.broadcast_to(sink.astype(jnp.float32).reshape(1, N_KV, Q_GROUP, 1, 1), s.shape[:-1] + (1,))
    p = jax.nn.softmax(jnp.concatenate([s, sk], axis=-1), axis=-1)[..., :-1]
    return jnp.einsum('bkgqs,bskd->bqkgd', p.astype(v.dtype), v)


def _ctx_attention(q, k, v, sink):
    B, L = q.shape[:2]
    nb = L // BLOCK
    qb = q.reshape(B, nb, BLOCK, N_KV, Q_GROUP, HEAD_DIM).transpose(1, 0, 2, 3, 4, 5)
    out = lax.map(lambda qi: _attend(qi, k, v, None, sink), qb)
    return out.transpose(1, 0, 2, 3, 4, 5).reshape(B, L, ATTN_W)


def _latent_attention(q, k, v, ck, cv, sink):
    B, L = q.shape[:2]
    nb = L // BLOCK
    P = ck.shape[1]
    pad = ((0, 0), (BLOCK, BLOCK), (0, 0), (0, 0))
    kp, vp = jnp.pad(k, pad), jnp.pad(v, pad)
    ck, cv = ck.astype(k.dtype), cv.astype(v.dtype)
    qb = q.reshape(B, nb, BLOCK, N_KV, Q_GROUP, HEAD_DIM).transpose(1, 0, 2, 3, 4, 5)
    rel = jnp.arange(3 * BLOCK)[None, :] - BLOCK - jnp.arange(BLOCK)[:, None]
    band = jnp.abs(rel) <= WINDOW
    ctx_bias = jnp.zeros((BLOCK, P), jnp.float32)

    def block(args):
        i, qi = args
        start = i * BLOCK
        kw = lax.dynamic_slice_in_dim(kp, start, 3 * BLOCK, axis=1)
        vw = lax.dynamic_slice_in_dim(vp, start, 3 * BLOCK, axis=1)
        kpos = start - BLOCK + jnp.arange(3 * BLOCK)
        valid = band & ((kpos >= 0) & (kpos < L))[None, :]
        bias = jnp.concatenate([jnp.where(valid, 0.0, NEG).astype(jnp.float32), ctx_bias], axis=1)
        return _attend(qi, jnp.concatenate([kw, ck], axis=1), jnp.concatenate([vw, cv], axis=1), bias, sink)

    out = lax.map(block, (jnp.arange(nb), qb))
    return out.transpose(1, 0, 2, 3, 4, 5).reshape(B, L, ATTN_W)


def _hyena_filter(L, w1, b1, w2, b2, w3, freq):
    f32 = jnp.float32
    t = jnp.linspace(0.0, 1.0, L, dtype=f32)[:, None]
    bands = (FILTER_EMB - 1) // 2
    fb = jnp.linspace(1e-4, bands - 1, bands, dtype=f32)[None, :]
    w = 2.0 * math.pi * jnp.arange(L, dtype=f32)[:, None] / L
    feats = jnp.concatenate([t, jnp.cos(fb * w), -jnp.sin(fb * w)], axis=-1)
    fr = freq.astype(f32)
    h = jnp.sin(fr * (feats @ w1.astype(f32) + b1.astype(f32)))
    h = jnp.sin(fr * (h @ w2.astype(f32) + b2.astype(f32)))
    h = h @ w3.astype(f32)
    decay_hi = math.log(HY_DECAY_TARGET) / HY_DECAY_HI_PCT
    decay_lo = math.log(HY_DECAY_TARGET) / HY_DECAY_LO_PCT
    deltas = jnp.abs(jnp.linspace(decay_lo, decay_hi, HY_W, dtype=f32))
    window = jnp.exp(-t * deltas[None, :])
    h_fwd = h[:, :HY_W] * window
    h_bwd = h[:, HY_W:] * window
    return jnp.concatenate([h_fwd, jnp.zeros((1, HY_W), f32), h_bwd[1:][::-1]], axis=0)


def _hyena(z, conv_w, conv_b, w1, b1, w2, b2, w3, freq, d_bias):
    L = z.shape[1]
    r = SHORT_CONV // 2
    zp = jnp.pad(z, ((0, 0), (r, r), (0, 0)))
    z = sum(zp[:, j:j + L] * conv_w[j] for j in range(SHORT_CONV)) + conv_b
    x0, x1, v = jnp.split(z, 3, axis=-1)
    u = (x1 * v).astype(jnp.float32)
    hc = _hyena_filter(L, w1, b1, w2, b2, w3, freq)
    uf = jnp.fft.rfft(u, n=2 * L, axis=1)
    hf = jnp.fft.rfft(hc, n=2 * L, axis=0)
    y = jnp.fft.irfft(uf * hf[None], n=2 * L, axis=1)[:, :L] + u * d_bias.astype(jnp.float32)
    return (x0.astype(jnp.float32) * y).astype(z.dtype)


def _gmlp(z, ws, b):
    B, L = z.shape[:2]
    u, v = jnp.split(jax.nn.gelu(z), 2, axis=-1)
    vc = v.reshape(B, L // GM_CHUNK, GM_CHUNK, GM_HEADS, GM_HEAD_DIM)
    s = jnp.einsum('hpq,bnqhc->bnphc', ws, vc) + b.T[None, None, :, :, None]
    return u * s.reshape(B, L, GM_W)


def _expert_choice(h, router_w, w1, w3, w2):
    B, L, D = h.shape
    xs = h.reshape(B * L, D)
    N = B * L
    C = max(1, EC_CAPACITY * N // N_EXPERTS)
    aff = jax.nn.softmax((xs @ router_w).astype(jnp.float32), axis=-1)
    gate, idx = lax.top_k(aff.T, C)
    xe = xs[idx]
    he = jax.nn.silu(jnp.einsum('ecd,edf->ecf', xe, w1)) * jnp.einsum('ecd,edf->ecf', xe, w3)
    ye = jnp.einsum('ecf,efd->ecd', he, w2) * gate[..., None].astype(h.dtype)
    out = jnp.zeros_like(xs).at[idx.reshape(-1)].add(ye.reshape(-1, D))
    return out.reshape(B, L, D)


def _layer(x, mod, lp, ctx_k=None, ctx_v=None):
    shift1, scale1, gate1, shift2, scale2, gate2 = jnp.split(mod, 6, axis=-1)
    B, L, _ = x.shape
    h = _modulate(x, lp['norm1_g'], shift1, scale1)
    z = h @ lp['w_in']
    o = 0
    q = z[..., o:o + ATTN_W].reshape(B, L, N_HEADS, HEAD_DIM)
    o += ATTN_W
    k = z[..., o:o + KV_W].reshape(B, L, N_KV, HEAD_DIM)
    o += KV_W
    v = z[..., o:o + KV_W].reshape(B, L, N_KV, HEAD_DIM)
    o += KV_W
    zh = z[..., o:o + 3 * HY_W]
    o += 3 * HY_W
    zg = z[..., o:o + 2 * GM_W]
    if ctx_k is None:
        a = _ctx_attention(q, k, v, lp['attn_sink'])
    else:
        a = _latent_attention(_axial_rope(q), _axial_rope(k), v, ctx_k, ctx_v, lp['attn_sink'])
    yh = _hyena(zh, lp['hy_conv_w'], lp['hy_conv_b'], lp['hy_f_w1'], lp['hy_f_b1'], lp['hy_f_w2'],
                lp['hy_f_b2'], lp['hy_f_w3'], lp['hy_freq'], lp['hy_bias'])
    yg = _gmlp(zg, lp['gm_ws'], lp['gm_b'])
    g = lp['out_norm_g']
    mixed = jnp.concatenate([_rmsnorm(a, g[:ATTN_W]),
                             _rmsnorm(yh, g[ATTN_W:ATTN_W + HY_W]),
                             _rmsnorm(yg, g[ATTN_W + HY_W:])], axis=-1) @ lp['w_out']
    x = x + gate1 * mixed
    h = _modulate(x, lp['norm2_g'], shift2, scale2)
    x = x + gate2 * _expert_choice(h, lp['router_w'], lp['exp_w1'], lp['exp_w3'], lp['exp_w2'])
    return x, k, v


def setup_inputs(seed: int = 0) -> dict:
    key = jax.random.key(seed)
    ks = jax.random.split(key, 32)

    def nrm(k, shape, scale):
        return jax.random.normal(k, shape, jnp.float32) * scale

    D, NL = D_MODEL, DEPTH
    return {
        'x_prompt': nrm(ks[0], (BATCH, SEQ, D), 1.0),
        'x_sample': nrm(ks[1], (DEC_BATCH, DEC_SEQ, D), 1.0),
        'c': nrm(ks[2], (DEC_BATCH, D), 1.0),
        'cache_k': nrm(ks[3], (DEC_BATCH, DEPTH, PAST_LEN, N_KV, HEAD_DIM), 1.0),
        'cache_v': nrm(ks[4], (DEC_BATCH, DEPTH, PAST_LEN, N_KV, HEAD_DIM), 1.0),
        'c_ctx': nrm(ks[5], (D,), 1.0),
        'norm1_g': 1.0 + nrm(ks[6], (NL, D), 0.02),
        'norm2_g': 1.0 + nrm(ks[7], (NL, D), 0.02),
        'ada_w': nrm(ks[8], (NL, D, 6 * D), D ** -0.5),
        'ada_b': nrm(ks[9], (NL, 6 * D), 0.01),
        'w_in': nrm(ks[10], (NL, D, IN_W), D ** -0.5),
        'attn_sink': nrm(ks[11], (NL, N_HEADS), 0.5),
        'hy_conv_w': nrm(ks[12], (NL, SHORT_CONV, 3 * HY_W), SHORT_CONV ** -0.5),
        'hy_conv_b': nrm(ks[13], (NL, 3 * HY_W), 0.01),
        'hy_f_w1': nrm(ks[14], (NL, FILTER_EMB, FILTER_HID), FILTER_EMB ** -0.5),
        'hy_f_b1': nrm(ks[15], (NL, FILTER_HID), 0.1),
        'hy_f_w2': nrm(ks[16], (NL, FILTER_HID, FILTER_HID), FILTER_HID ** -0.5),
        'hy_f_b2': nrm(ks[17], (NL, FILTER_HID), 0.1),
        'hy_f_w3': nrm(ks[18], (NL, FILTER_HID, 2 * HY_W), FILTER_HID ** -0.5),
        'hy_freq': 1.0 + nrm(ks[19], (NL, FILTER_HID), 0.1),
        'hy_bias': nrm(ks[20], (NL, HY_W), 0.1),
        'gm_ws': nrm(ks[21], (NL, GM_HEADS, GM_CHUNK, GM_CHUNK), GM_CHUNK ** -0.5),
        'gm_b': 1.0 + nrm(ks[22], (NL, GM_HEADS, GM_CHUNK), 0.01),
        'out_norm_g': 1.0 + nrm(ks[23], (NL, MIX_W), 0.02),
        'w_out': nrm(ks[24], (NL, MIX_W, D), MIX_W ** -0.5),
        'router_w': nrm(ks[25], (NL, D, N_EXPERTS), D ** -0.5),
        'exp_w1': nrm(ks[26], (NL, N_EXPERTS, D, D_EXPERT), D ** -0.5),
        'exp_w3': nrm(ks[27], (NL, N_EXPERTS, D, D_EXPERT), D ** -0.5),
        'exp_w2': nrm(ks[28], (NL, N_EXPERTS, D_EXPERT, D), D_EXPERT ** -0.5),
        'final_g': 1.0 + nrm(ks[29], (D,), 0.02),
    }


def reference(x_prompt, x_sample, c, cache_k, cache_v, c_ctx, norm1_g, norm2_g, ada_w, ada_b, w_in,
              attn_sink, hy_conv_w, hy_conv_b, hy_f_w1, hy_f_b1, hy_f_w2, hy_f_b2, hy_f_w3, hy_freq,
              hy_bias, gm_ws, gm_b, out_norm_g, w_out, router_w, exp_w1, exp_w3, exp_w2, final_g):
    xp, xs = x_prompt, x_sample
    ks, vs = [], []
    for l in range(DEPTH):
        lp = {
            'norm1_g': norm1_g[l], 'norm2_g': norm2_g[l], 'w_in': w_in[l], 'attn_sink': attn_sink[l],
            'hy_conv_w': hy_conv_w[l], 'hy_conv_b': hy_conv_b[l], 'hy_f_w1': hy_f_w1[l],
            'hy_f_b1': hy_f_b1[l], 'hy_f_w2': hy_f_w2[l], 'hy_f_b2': hy_f_b2[l], 'hy_f_w3': hy_f_w3[l],
            'hy_freq': hy_freq[l], 'hy_bias': hy_bias[l], 'gm_ws': gm_ws[l], 'gm_b': gm_b[l],
            'out_norm_g': out_norm_g[l], 'w_out': w_out[l], 'router_w': router_w[l],
            'exp_w1': exp_w1[l], 'exp_w3': exp_w3[l], 'exp_w2': exp_w2[l],
        }
        mod_ctx = (jax.nn.silu(c_ctx) @ ada_w[l] + ada_b[l])[None, None, :]
        mod_lat = (jax.nn.silu(c) @ ada_w[l] + ada_b[l])[:, None, :]
        xp, k_l, v_l = _layer(xp, mod_ctx, lp)
        xs, _, _ = _layer(xs, mod_lat, lp, cache_k[:, l], cache_v[:, l])
        ks.append(k_l)
        vs.append(v_l)
    y_prompt = _rmsnorm(xp, final_g)
    y_sample = _rmsnorm(xs, final_g)
    new_cache_k = jnp.stack(ks, axis=1)
    new_cache_v = jnp.stack(vs, axis=1)
    return (y_prompt, y_sample, new_cache_k, new_cache_v)
```

```python
import functools
import math

import numpy as np
import jax
import jax.numpy as jnp
from jax import lax
from jax.experimental import pallas as pl
from jax.experimental.pallas import tpu as pltpu

f32 = jnp.float32
bf16 = jnp.bfloat16

D_MODEL = 1024
DEPTH = 2
GRID_W = 64
BLOCK = 128
N_HEADS = 8
N_KV = 2
HEAD_DIM = 64
Q_GROUP = N_HEADS // N_KV
ATTN_W = N_HEADS * HEAD_DIM
KV_W = N_KV * HEAD_DIM
QKV_W = ATTN_W + 2 * KV_W
HY_W = 256
GM_W = 256
GM_HEADS = 4
GM_HEAD_DIM = GM_W // GM_HEADS
GM_CHUNK = 128
MIX_W = ATTN_W + HY_W + GM_W
IN_W = ATTN_W + 2 * KV_W + 3 * HY_W + 2 * GM_W
FILTER_EMB = 33
FILTER_HID = 64
HY_DECAY_HI_PCT = 0.3
HY_DECAY_LO_PCT = 1.5
HY_DECAY_TARGET = 1e-2
N_EXPERTS = 16
EC_CAPACITY = 2
D_EXPERT = 1024
ROPE_THETA = 10000.0
EPS = 1e-6
NEG = -1e30

LANES = 128
SUBLANES = 8
ROW_CHUNKS = D_MODEL // LANES
VMEM_LIMIT = 56 * 1024 * 1024
TM = 512
MOE_F_TILE = 256


def _params(*sem):
    return pltpu.CompilerParams(dimension_semantics=sem, vmem_limit_bytes=VMEM_LIMIT)


def _rms(x):
    return x * lax.rsqrt(jnp.mean(x * x, axis=-1, keepdims=True) + EPS)


def _sigmoid(x):
    return 1.0 / (1.0 + jnp.exp(-x))


def _ada_kernel(c_ref, w_ref, b_ref, o_ref):
    c = c_ref[...]
    s = (c * _sigmoid(c)).astype(bf16)
    o_ref[0] = jnp.dot(s, w_ref[0].astype(bf16), preferred_element_type=f32) + b_ref[0]


def _ada(cvec, ada_w, ada_b):
    nt = 6
    return pl.pallas_call(
        _ada_kernel,
        grid=(DEPTH, nt),
        in_specs=[
            pl.BlockSpec((SUBLANES, D_MODEL), lambda l, j: (0, 0)),
            pl.BlockSpec((1, D_MODEL, D_MODEL), lambda l, j: (l, 0, j)),
            pl.BlockSpec((1, 1, D_MODEL), lambda l, j: (l, 0, j)),
        ],
        out_specs=pl.BlockSpec((1, SUBLANES, D_MODEL), lambda l, j: (l, 0, j)),
        out_shape=jax.ShapeDtypeStruct((DEPTH, SUBLANES, 6 * D_MODEL), f32),
        compiler_params=_params("arbitrary", "arbitrary"),
        name="ada",
    )(cvec, ada_w, ada_b.reshape(DEPTH, 1, 6 * D_MODEL))


def _rope_swap(x):
    w = x.shape[-1]
    lane = lax.broadcasted_iota(jnp.int32, x.shape, 1)
    first = (lane % 32) < 16
    return jnp.where(first, pltpu.roll(x, w - 16, axis=1), pltpu.roll(x, 16, axis=1))


def _in_kernel(has_moe, rope, *refs):
    it = iter(refs)
    x_ref = next(it)
    moe_ref = next(it) if has_moe else None
    modp_ref = next(it) if has_moe else None
    mod_ref = next(it)
    g_ref = next(it)
    w_ref = next(it)
    cos_ref = next(it) if rope else None
    sin_ref = next(it) if rope else None
    z_ref = next(it)
    xo_ref = next(it) if has_moe else None

    x = x_ref[...]
    if has_moe:
        x = x + modp_ref[0][5:6] * moe_ref[...]
        xo_ref[...] = x
    m = mod_ref[0]
    h = _rms(x) * g_ref[...] * (1.0 + m[1:2]) + m[0:1]
    z = jnp.dot(h.astype(bf16), w_ref[...], preferred_element_type=f32)
    if rope:
        reps = (ATTN_W + KV_W) // LANES
        cos = jnp.concatenate([cos_ref[...]] * reps, axis=1)
        sin = jnp.concatenate([sin_ref[...]] * reps, axis=1)
        qk = z[:, :ATTN_W + KV_W]
        z_ref[:, :ATTN_W + KV_W] = qk * cos + _rope_swap(qk) * sin
        z_ref[:, ATTN_W + KV_W:] = z[:, ATTN_W + KV_W:]
    else:
        z_ref[...] = z


def _in_proj(x, mod, g1, w_in_bf, seq_len, moe=None, modp=None, rope_tabs=None):
    n = x.shape[0]
    has_moe = moe is not None
    rope = rope_tabs is not None
    per_mod = n // mod.shape[0]
    row = pl.BlockSpec((TM, D_MODEL), lambda i: (i, 0))
    modspec = pl.BlockSpec((1, 6, D_MODEL), lambda i: (i * TM // per_mod, 0, 0))
    in_specs, args = [row], [x]
    if has_moe:
        in_specs += [row, modspec]
        args += [moe, modp]
    in_specs += [modspec, pl.BlockSpec((1, D_MODEL), lambda i: (0, 0)),
                 pl.BlockSpec((D_MODEL, IN_W), lambda i: (0, 0))]
    args += [mod, g1.reshape(1, D_MODEL), w_in_bf]
    if rope:
        nt = seq_len // TM
        tab = pl.BlockSpec((TM, LANES), lambda i: (i % nt, 0))
        in_specs += [tab, tab]
        args += list(rope_tabs)
    out_specs = [pl.BlockSpec((TM, IN_W), lambda i: (i, 0))]
    out_shape = [jax.ShapeDtypeStruct((n, IN_W), f32)]
    if has_moe:
        out_specs.append(row)
        out_shape.append(jax.ShapeDtypeStruct((n, D_MODEL), f32))
    res = pl.pallas_call(
        functools.partial(_in_kernel, has_moe, rope),
        grid=(n // TM,),
        in_specs=in_specs,
        out_specs=out_specs,
        out_shape=out_shape,
        compiler_params=_params("parallel"),
        name="in_proj",
    )(*args)
    return (res[0], res[1]) if has_moe else (res[0], x)


def _rope_tables(seq_len):
    nf = HEAD_DIM // 4
    t = np.arange(seq_len)
    inv = (ROPE_THETA ** (-np.arange(nf, dtype=np.float32) / nf)).astype(np.float32)
    d = np.arange(HEAD_DIM)
    pos = np.where((d // 32)[None, :] == 0, (t // GRID_W)[:, None], (t % GRID_W)[:, None]).astype(np.float32)
    ang = (pos * inv[d % nf][None, :]).astype(np.float32)
    cos = np.cos(ang).astype(np.float32)
    sin = np.sin(ang).astype(np.float32) * np.where((d % 32) < 16, -1.0, 1.0)[None, :].astype(np.float32)
    return jnp.asarray(np.tile(cos, (1, 2))), jnp.asarray(np.tile(sin, (1, 2)))


def _softmax_pv(s, sink, v_bf):
    m = jnp.maximum(jnp.max(s, axis=-1, keepdims=True), sink)
    p = jnp.exp(s - m)
    den = jnp.sum(p, axis=-1, keepdims=True) + jnp.exp(sink - m)
    return jnp.dot(p.astype(bf16), v_bf, preferred_element_type=f32) / den


def _qk(q, k_bf):
    s = lax.dot_general(q.astype(bf16), k_bf, (((1,), (1,)), ((), ())), preferred_element_type=f32)
    return s * (1.0 / math.sqrt(HEAD_DIM))


def _ctx_attn_kernel(sink_ref, z_ref, o_ref):
    z = z_ref[...]
    outs = []
    for kv in range(N_KV):
        k = z[:, ATTN_W + kv * HEAD_DIM:ATTN_W + (kv + 1) * HEAD_DIM].astype(bf16)
        v = z[:, ATTN_W + KV_W + kv * HEAD_DIM:ATTN_W + KV_W + (kv + 1) * HEAD_DIM].astype(bf16)
        for g in range(Q_GROUP):
            h = kv * Q_GROUP + g
            q = z[:, h * HEAD_DIM:(h + 1) * HEAD_DIM]
            outs.append(_softmax_pv(_qk(q, k), sink_ref[h], v))
    o_ref[...] = jnp.concatenate(outs, axis=1)


def _ctx_attention(z, sink, batch, seq_len):
    return pl.pallas_call(
        _ctx_attn_kernel,
        grid=(batch,),
        in_specs=[pl.BlockSpec(memory_space=pltpu.SMEM),
                  pl.BlockSpec((seq_len, QKV_W), lambda b: (b, 0))],
        out_specs=pl.BlockSpec((seq_len, ATTN_W), lambda b: (b, 0)),
        out_shape=jax.ShapeDtypeStruct((batch * seq_len, ATTN_W), f32),
        compiler_params=_params("parallel"),
        name="ctx_attn",
    )(sink, z)


def _lat_attn_kernel(nb, sink_ref, zp_ref, zc_ref, zn_ref, ck_ref, cv_ref, o_ref):
    i = pl.program_id(1)
    zp, zc, zn = zp_ref[...], zc_ref[...], zn_ref[...]
    ck, cv = ck_ref[0], cv_ref[0]
    r = lax.broadcasted_iota(jnp.int32, (BLOCK, BLOCK), 0)
    j = lax.broadcasted_iota(jnp.int32, (BLOCK, BLOCK), 1)
    ok_prev = (j >= r) & (i > 0)
    ok_next = (j <= r) & (i < nb - 1)
    outs = []
    for kv in range(N_KV):
        ks = slice(ATTN_W + kv * HEAD_DIM, ATTN_W + (kv + 1) * HEAD_DIM)
        vs = slice(ATTN_W + KV_W + kv * HEAD_DIM, ATTN_W + KV_W + (kv + 1) * HEAD_DIM)
        cs = slice(kv * HEAD_DIM, (kv + 1) * HEAD_DIM)
        kp, kc, kn, kx = zp[:, ks].astype(bf16), zc[:, ks].astype(bf16), zn[:, ks].astype(bf16), ck[:, cs].astype(bf16)
        vals = jnp.concatenate([zp[:, vs], zc[:, vs], zn[:, vs], cv[:, cs]], axis=0).astype(bf16)
        for g in range(Q_GROUP):
            h = kv * Q_GROUP + g
            q = zc[:, h * HEAD_DIM:(h + 1) * HEAD_DIM]
            s = jnp.concatenate([
                jnp.where(ok_prev, _qk(q, kp), NEG),
                _qk(q, kc),
                jnp.where(ok_next, _qk(q, kn), NEG),
                _qk(q, kx)], axis=1)
            outs.append(_softmax_pv(s, sink_ref[h], vals))
    o_ref[...] = jnp.concatenate(outs, axis=1)


def _lat_attention(z, sink, ck, cv, batch, seq_len):
    nb = seq_len // BLOCK
    blk = lambda f: pl.BlockSpec((BLOCK, QKV_W), f)
    past = ck.shape[1]
    cache = pl.BlockSpec((1, past, KV_W), lambda b, i: (b, 0, 0))
    return pl.pallas_call(
        functools.partial(_lat_attn_kernel, nb),
        grid=(batch, nb),
        in_specs=[pl.BlockSpec(memory_space=pltpu.SMEM),
                  blk(lambda b, i: (b * nb + jnp.maximum(i - 1, 0), 0)),
                  blk(lambda b, i: (b * nb + i, 0)),
                  blk(lambda b, i: (b * nb + jnp.minimum(i + 1, nb - 1), 0)),
                  cache, cache],
        out_specs=pl.BlockSpec((BLOCK, ATTN_W), lambda b, i: (b * nb + i, 0)),
        out_shape=jax.ShapeDtypeStruct((batch * seq_len, ATTN_W), f32),
        compiler_params=_params("parallel", "parallel"),
        name="lat_attn",
    )(sink, z, z, z, ck, cv)


HY_TL = 256


def _hy_prep_kernel(nt, zp_ref, zc_ref, zn_ref, w_ref, b_ref, x0_ref, u_ref):
    i = pl.program_id(1)
    z = zc_ref[...]
    row = lax.broadcasted_iota(jnp.int32, z.shape, 0)
    prev_row = jnp.where(i > 0, zp_ref[SUBLANES - 1:SUBLANES, :], 0.0)
    next_row = jnp.where(i < nt - 1, zn_ref[0:1, :], 0.0)
    z_prev = jnp.where(row == 0, prev_row, pltpu.roll(z, 1, axis=0))
    z_next = jnp.where(row == HY_TL - 1, next_row, pltpu.roll(z, HY_TL - 1, axis=0))
    w = w_ref[...]
    zc = z_prev * w[0:1] + z * w[1:2] + z_next * w[2:3] + b_ref[...]
    x0_ref[...] = zc[:, :HY_W]
    u_ref[...] = zc[:, HY_W:2 * HY_W] * zc[:, 2 * HY_W:]


def _hy_prep(z, conv_w, conv_b, batch, seq_len):
    n = batch * seq_len
    nt = seq_len // HY_TL
    r8 = HY_TL // SUBLANES
    last8 = n // SUBLANES - 1
    return pl.pallas_call(
        functools.partial(_hy_prep_kernel, nt),
        grid=(batch, nt),
        in_specs=[
            pl.BlockSpec((SUBLANES, 3 * HY_W), lambda b, i: (jnp.maximum((b * nt + i) * r8 - 1, 0), 1)),
            pl.BlockSpec((HY_TL, 3 * HY_W), lambda b, i: (b * nt + i, 1)),
            pl.BlockSpec((SUBLANES, 3 * HY_W), lambda b, i: (jnp.minimum((b * nt + i + 1) * r8, last8), 1)),
            pl.BlockSpec((3, 3 * HY_W), lambda b, i: (0, 0)),
            pl.BlockSpec((1, 3 * HY_W), lambda b, i: (0, 0)),
        ],
        out_specs=[pl.BlockSpec((HY_TL, HY_W), lambda b, i: (b * nt + i, 0))] * 2,
        out_shape=[jax.ShapeDtypeStruct((n, HY_W), f32)] * 2,
        compiler_params=_params("parallel", "parallel"),
        name="hy_prep",
    )(z, z, z, conv_w, conv_b.reshape(1, 3 * HY_W))


def _hy_filter_kernel(seq_len, feats_ref, win_ref, w1_ref, b1_ref, w2_ref, b2_ref, w3_ref, fr_ref, o_ref):
    hi = lax.Precision.HIGHEST
    fr = fr_ref[...]
    h = jnp.sin(fr * (jnp.dot(feats_ref[...], w1_ref[...], precision=hi, preferred_element_type=f32) + b1_ref[...]))
    h = jnp.sin(fr * (jnp.dot(h, w2_ref[...], precision=hi, preferred_element_type=f32) + b2_ref[...]))
    h = jnp.dot(h, w3_ref[...], precision=hi, preferred_element_type=f32)
    win = win_ref[...]
    row = lax.broadcasted_iota(jnp.int32, (seq_len, HY_W), 0)
    o_ref[0:seq_len, :] = h[:, :HY_W] * win
    o_ref[seq_len:, :] = jnp.where(row == 0, 0.0, h[:, HY_W:] * win)


def _hy_filter(seq_len, consts, w1, b1, w2, b2, w3, freq):
    w1p = jnp.zeros((LANES, FILTER_HID), f32).at[:FILTER_EMB].set(w1)
    args = (consts["feats"], consts["window"], w1p, b1.reshape(1, -1), w2, b2.reshape(1, -1), w3, freq.reshape(1, -1))
    return pl.pallas_call(
        functools.partial(_hy_filter_kernel, seq_len),
        in_specs=[pl.BlockSpec(memory_space=pltpu.VMEM)] * len(args),
        out_specs=pl.BlockSpec(memory_space=pltpu.VMEM),
        out_shape=jax.ShapeDtypeStruct((2 * seq_len, HY_W), f32),
        compiler_params=pltpu.CompilerParams(vmem_limit_bytes=VMEM_LIMIT),
        name="hy_filter",
    )(*args)


def _dft_fwd_kernel(m_ref, u_ref, o_ref):
    o_ref[...] = jnp.dot(m_ref[...], u_ref[...].astype(bf16), preferred_element_type=f32)


def _dft_fwd(mat, u, batch, seq_len):
    tm = min(TM, 2 * seq_len)
    nt = 2 * seq_len // tm
    return pl.pallas_call(
        _dft_fwd_kernel,
        grid=(batch, nt),
        in_specs=[pl.BlockSpec((tm, seq_len), lambda b, i: (i, 0)),
                  pl.BlockSpec((seq_len, HY_W), lambda b, i: (b, 0))],
        out_specs=pl.BlockSpec((tm, HY_W), lambda b, i: (b * nt + i, 0)),
        out_shape=jax.ShapeDtypeStruct((batch * 2 * seq_len, HY_W), f32),
        compiler_params=_params("parallel", "parallel"),
        name="dft_fwd",
    )(mat, u)


def _hy_spec_kernel(pu_ref, wu_ref, pf_ref, qf_ref, pb_ref, qb_ref, r_ref, z_ref):
    i = pl.program_id(1)
    pu, wu = pu_ref[...], wu_ref[...]
    hr = pf_ref[...] + pb_ref[...]
    qs = qf_ref[...] + qb_ref[...]
    hi = qb_ref[...] - qf_ref[...]
    first = (lax.broadcasted_iota(jnp.int32, pu.shape, 0) == 0) & (i == 0)
    r = 2.0 * (pu * hr + wu * hi)
    z = 2.0 * (wu * hr - pu * hi)
    r_ref[...] = jnp.where(first, pu * hr, r).astype(bf16)
    z_ref[...] = jnp.where(first, wu * qs, z).astype(bf16)


def _hy_spec(su, sh, batch, seq_len):
    tk = min(TM, seq_len)
    nt = seq_len // tk
    blk = lambda f: pl.BlockSpec((tk, HY_W), f)
    res = pl.pallas_call(
        _hy_spec_kernel,
        grid=(batch, nt),
        in_specs=[blk(lambda b, i: (b * 2 * nt + i, 0)), blk(lambda b, i: (b * 2 * nt + nt + i, 0)),
                  blk(lambda b, i: (i, 0)), blk(lambda b, i: (nt + i, 0)),
                  blk(lambda b, i: (2 * nt + i, 0)), blk(lambda b, i: (3 * nt + i, 0))],
        out_specs=[blk(lambda b, i: (b * nt + i, 0))] * 2,
        out_shape=[jax.ShapeDtypeStruct((batch * seq_len, HY_W), bf16)] * 2,
        compiler_params=_params("parallel", "parallel"),
        name="hy_spec",
    )(su, su, sh, sh, sh, sh)
    return res


def _dft_inv_kernel(seq_len, ma_ref, mb_ref, r_ref, z_ref, u_ref, x0_ref, d_ref, o_ref):
    acc = jnp.dot(ma_ref[...], r_ref[...], preferred_element_type=f32)
    acc = acc + jnp.dot(mb_ref[...], z_ref[...], preferred_element_type=f32)
    u = u_ref[...]
    y = acc * (1.0 / (2 * seq_len)) + u * d_ref[...]
    o_ref[...] = x0_ref[...] * y


def _dft_inv(mat_a, mat_bt, r, zz, u, x0, d_bias, batch, seq_len):
    tm = min(TM, seq_len)
    nt = seq_len // tm
    mspec = pl.BlockSpec((tm, seq_len), lambda b, i: (i, 0))
    full = pl.BlockSpec((seq_len, HY_W), lambda b, i: (b, 0))
    row = pl.BlockSpec((tm, HY_W), lambda b, i: (b * nt + i, 0))
    return pl.pallas_call(
        functools.partial(_dft_inv_kernel, seq_len),
        grid=(batch, nt),
        in_specs=[mspec, mspec, full, full, row, row, pl.BlockSpec((1, HY_W), lambda b, i: (0, 0))],
        out_specs=row,
        out_shape=jax.ShapeDtypeStruct((batch * seq_len, HY_W), f32),
        compiler_params=_params("parallel", "parallel"),
        name="dft_inv",
    )(mat_a, mat_bt, r, zz, u, x0, d_bias.reshape(1, HY_W))


@functools.lru_cache(maxsize=None)
def _hy_consts_np(seq_len):
    n = 2 * seq_len
    k = np.arange(seq_len, dtype=np.int64)
    ang = (2.0 * np.pi / n) * ((k[:, None] * k[None, :]) % n).astype(np.float64)
    cosm = np.cos(ang)
    sinm = np.sin(ang)
    sinm[0, :] = np.where(k % 2 == 0, 1.0, -1.0)
    fwd = np.concatenate([cosm, sinm], axis=0)
    t = np.linspace(0.0, 1.0, seq_len, dtype=np.float32)[:, None]
    bands = (FILTER_EMB - 1) // 2
    fb = np.linspace(1e-4, bands - 1, bands, dtype=np.float32)[None, :]
    w = (np.float32(2.0 * math.pi) * np.arange(seq_len, dtype=np.float32)[:, None] / np.float32(seq_len)).astype(np.float32)
    feats = np.concatenate([t, np.cos(fb * w), -np.sin(fb * w)], axis=-1).astype(np.float32)
    feats = np.pad(feats, ((0, 0), (0, LANES - FILTER_EMB)))
    decay_hi = math.log(HY_DECAY_TARGET) / HY_DECAY_HI_PCT
    decay_lo = math.log(HY_DECAY_TARGET) / HY_DECAY_LO_PCT
    deltas = np.abs(np.linspace(decay_lo, decay_hi, HY_W, dtype=np.float32))
    window = np.exp(-t * deltas[None, :]).astype(np.float32)
    return fwd, cosm, np.ascontiguousarray(sinm.T), feats, window


def _hy_consts(seq_len):
    fwd, cosm, sint, feats, window = _hy_consts_np(seq_len)
    as_bf = lambda m: jnp.asarray(m, dtype=f32).astype(bf16)
    return {"fwd": as_bf(fwd), "inv_a": as_bf(cosm), "inv_b": as_bf(sint),
            "feats": jnp.asarray(feats), "window": jnp.asarray(window)}


def _hyena(z, lp, consts, batch, seq_len):
    x0, u = _hy_prep(z, lp["hy_conv_w"], lp["hy_conv_b"], batch, seq_len)
    hb = _hy_filter(seq_len, consts, lp["hy_f_w1"], lp["hy_f_b1"], lp["hy_f_w2"], lp["hy_f_b2"],
                    lp["hy_f_w3"], lp["hy_freq"])
    su = _dft_fwd(consts["fwd"], u, batch, seq_len)
    sh = _dft_fwd(consts["fwd"], hb, 2, seq_len)
    r, zz = _hy_spec(su, sh, batch, seq_len)
    return _dft_inv(consts["inv_a"], consts["inv_b"], r, zz, u, x0, lp["hy_bias"], batch, seq_len)


def _gelu(x):
    return 0.5 * x * (1.0 + jnp.tanh(math.sqrt(2.0 / math.pi) * (x + 0.044715 * (x * x * x))))


def _gmlp_kernel(z_ref, ws_ref, b_ref, o_ref):
    g = _gelu(z_ref[...])
    bias = b_ref[...]
    for c in range(TM // GM_CHUNK):
        rows = slice(c * GM_CHUNK, (c + 1) * GM_CHUNK)
        u = g[rows, :GM_W]
        v = g[rows, GM_W:].astype(bf16)
        s = jnp.concatenate(
            [jnp.dot(ws_ref[h], v[:, h * GM_HEAD_DIM:(h + 1) * GM_HEAD_DIM], preferred_element_type=f32)
             for h in range(GM_HEADS)], axis=1)
        o_ref[rows, :] = u * (s + bias)


def _gmlp(z, ws_bf, gm_b):
    n = z.shape[0]
    bias = jnp.repeat(gm_b.T, GM_HEAD_DIM, axis=1)
    return pl.pallas_call(
        _gmlp_kernel,
        grid=(n // TM,),
        in_specs=[pl.BlockSpec((TM, 2 * GM_W), lambda i: (i, 3)),
                  pl.BlockSpec((GM_HEADS, GM_CHUNK, GM_CHUNK), lambda i: (0, 0, 0)),
                  pl.BlockSpec((GM_CHUNK, GM_W), lambda i: (0, 0))],
        out_specs=pl.BlockSpec((TM, GM_W), lambda i: (i, 0)),
        out_shape=jax.ShapeDtypeStruct((n, GM_W), f32),
        compiler_params=_params("parallel"),
        name="gmlp",
    )(z, ws_bf, bias)


def _out_kernel(a_ref, yh_ref, yg_ref, x_ref, mod_ref, og_ref, w_ref, g2_ref, rw_ref, x1_ref, h2_ref, aff_ref):
    og = og_ref[...]
    mixed_in = jnp.concatenate([
        _rms(a_ref[...]) * og[:, :ATTN_W],
        _rms(yh_ref[...]) * og[:, ATTN_W:ATTN_W + HY_W],
        _rms(yg_ref[...]) * og[:, ATTN_W + HY_W:]], axis=1)
    mixed = jnp.dot(mixed_in.astype(bf16), w_ref[...], preferred_element_type=f32)
    m = mod_ref[0]
    x1 = x_ref[...] + m[2:3] * mixed
    x1_ref[...] = x1
    h2 = _rms(x1) * g2_ref[...] * (1.0 + m[4:5]) + m[3:4]
    h2_ref[...] = h2
    logits = lax.dot_general(rw_ref[...], h2, (((1,), (1,)), ((), ())),
                             precision=lax.Precision.HIGHEST, preferred_element_type=f32)
    e = jnp.exp(logits - jnp.max(logits, axis=0, keepdims=True))
    aff_ref[...] = e / jnp.sum(e, axis=0, keepdims=True)


def _out_proj(a, yh, yg, x, mod, out_g, w_out_bf, g2, router_wt):
    n = x.shape[0]
    per_mod = n // mod.shape[0]
    row = lambda w: pl.BlockSpec((TM, w), lambda i: (i, 0))
    const = lambda s: pl.BlockSpec(s, lambda i: (0,) * len(s))
    return pl.pallas_call(
        _out_kernel,
        grid=(n // TM,),
        in_specs=[row(ATTN_W), row(HY_W), row(GM_W), row(D_MODEL),
                  pl.BlockSpec((1, 6, D_MODEL), lambda i: (i * TM // per_mod, 0, 0)),
                  const((1, MIX_W)), const((MIX_W, D_MODEL)), const((1, D_MODEL)), const((N_EXPERTS, D_MODEL))],
        out_specs=[row(D_MODEL), row(D_MODEL), pl.BlockSpec((N_EXPERTS, TM), lambda i: (0, i))],
        out_shape=[jax.ShapeDtypeStruct((n, D_MODEL), f32), jax.ShapeDtypeStruct((n, D_MODEL), f32),
                   jax.ShapeDtypeStruct((N_EXPERTS, n), f32)],
        compiler_params=_params("parallel"),
        name="out_proj",
    )(a, yh, yg, x, mod, out_g.reshape(1, MIX_W), w_out_bf, g2.reshape(1, D_MODEL), router_wt)


RT_CHUNK = 512


def _prefix_incl(x01, tri):
    n = x01.shape[1]
    carry = jnp.zeros((x01.shape[0], 1), f32)
    parts = []
    for c in range(n // RT_CHUNK):
        piece = x01[:, c * RT_CHUNK:(c + 1) * RT_CHUNK]
        parts.append(jnp.dot(piece.astype(bf16), tri, preferred_element_type=f32) + carry)
        carry = carry + jnp.sum(piece, axis=1, keepdims=True)
    return jnp.concatenate(parts, axis=1)


def _route_select_kernel(cap, aff_ref, pos_ref):
    aff = aff_ref[...]

    def step(it, t):
        cand = t | (jnp.int32(1) << (30 - it))
        cnt = jnp.sum(jnp.where(aff >= lax.bitcast_convert_type(cand, f32), 1.0, 0.0), axis=1, keepdims=True)
        return jnp.where(cnt >= cap, cand, t)

    thr_bits = lax.fori_loop(0, 31, step, jnp.zeros((aff.shape[0], 1), jnp.int32))
    thr = lax.bitcast_convert_type(thr_bits, f32)
    gt = jnp.where(aff > thr, 1.0, 0.0)
    eq = jnp.where(aff == thr, 1.0, 0.0)
    room = cap - jnp.sum(gt, axis=1, keepdims=True)
    r = lax.broadcasted_iota(jnp.int32, (RT_CHUNK, RT_CHUNK), 0)
    c = lax.broadcasted_iota(jnp.int32, (RT_CHUNK, RT_CHUNK), 1)
    tri = jnp.where(r <= c, 1.0, 0.0).astype(bf16)
    sel = jnp.maximum(gt, jnp.where(_prefix_incl(eq, tri) <= room, eq, 0.0))
    pos = _prefix_incl(sel, tri) - 1.0
    pos_ref[...] = jnp.where(sel > 0.0, pos, -1.0)


def _route_select(aff_t, cap):
    return pl.pallas_call(
        functools.partial(_route_select_kernel, cap),
        in_specs=[pl.BlockSpec(memory_space=pltpu.VMEM)],
        out_specs=pl.BlockSpec(memory_space=pltpu.VMEM),
        out_shape=jax.ShapeDtypeStruct(aff_t.shape, f32),
        compiler_params=pltpu.CompilerParams(vmem_limit_bytes=VMEM_LIMIT),
        name="route_select",
    )(aff_t)


def _route_compact_kernel(cap, aff_ref, pos_ref, o_ref):
    e = pl.program_id(0)
    c = pl.program_id(1)

    @pl.when(c == 0)
    def _():
        o_ref[...] = jnp.zeros_like(o_ref)

    pos = pos_ref[pl.ds(e, 1), :]
    g = aff_ref[pl.ds(e, 1), :]
    slot = lax.broadcasted_iota(jnp.int32, (cap, RT_CHUNK), 0).astype(f32)
    onehot = jnp.where(slot == pos, 1.0, 0.0).astype(bf16)
    tok = lax.broadcasted_iota(jnp.int32, (1, RT_CHUNK), 1) + c * RT_CHUNK
    g_hi = g.astype(bf16)
    g1 = g - g_hi.astype(f32)
    g_mid = g1.astype(bf16)
    g_lo = (g1 - g_mid.astype(f32)).astype(bf16)
    zero = jnp.zeros((1, RT_CHUNK), bf16)
    vals = jnp.concatenate([(tok // 64).astype(f32).astype(bf16), (tok % 64).astype(f32).astype(bf16),
                            g_hi, g_mid, g_lo, zero, zero, zero], axis=0)
    o_ref[0] += lax.dot_general(vals, onehot, (((1,), (1,)), ((), ())), preferred_element_type=f32)


def _route_compact(aff_t, pos, cap):
    n = aff_t.shape[1]
    blk = pl.BlockSpec((N_EXPERTS, RT_CHUNK), lambda e, c: (0, c))
    packed = pl.pallas_call(
        functools.partial(_route_compact_kernel, cap),
        grid=(N_EXPERTS, n // RT_CHUNK),
        in_specs=[blk, blk],
        out_specs=pl.BlockSpec((1, SUBLANES, cap), lambda e, c: (e, 0, 0)),
        out_shape=jax.ShapeDtypeStruct((N_EXPERTS, SUBLANES, cap), f32),
        compiler_params=_params("parallel", "arbitrary"),
        name="route_compact",
    )(aff_t, pos)
    idx = (packed[:, 0] * 64.0 + packed[:, 1]).astype(jnp.int32)
    gate = packed[:, 2] + packed[:, 3] + packed[:, 4]
    return idx, gate


def _moe_kernel(cap, stride, nf, idx_ref, h_ref, g_ref, w1_ref, w3_ref, w2_ref, o_ref, tile_ref, xb_ref, acc_ref):
    e = pl.program_id(0)
    f = pl.program_id(1)
    base = e * cap

    @pl.when((e == 0) & (f == 0))
    def _():
        o_ref[...] = jnp.zeros_like(o_ref)

    @pl.when(f == 0)
    def _():
        def gather(r, carry):
            tile_ref[pl.ds(r, ROW_CHUNKS, stride=stride), :] = h_ref[idx_ref[base + r]]
            return carry
        lax.fori_loop(0, cap, gather, 0)
        xb_ref[...] = jnp.concatenate(
            [tile_ref[j * stride:j * stride + cap, :] for j in range(ROW_CHUNKS)], axis=1).astype(bf16)
        acc_ref[...] = jnp.zeros_like(acc_ref)

    xb = xb_ref[...]
    a = jnp.dot(xb, w1_ref[0].astype(bf16), preferred_element_type=f32)
    b = jnp.dot(xb, w3_ref[0].astype(bf16), preferred_element_type=f32)
    he = (a * _sigmoid(a) * b).astype(bf16)
    acc_ref[...] += jnp.dot(he, w2_ref[0].astype(bf16), preferred_element_type=f32)

    @pl.when(f == nf - 1)
    def _():
        y = acc_ref[...] * g_ref[0]
        for j in range(ROW_CHUNKS):
            tile_ref[j * stride:j * stride + cap, :] = y[:, j * LANES:(j + 1) * LANES]
        unroll = 4

        def scatter(q, carry):
            rows = [idx_ref[base + q * unroll + t] for t in range(unroll)]
            new = [o_ref[rows[t]] + tile_ref[pl.ds(q * unroll + t, ROW_CHUNKS, stride=stride), :]
                   for t in range(unroll)]
            for t in range(unroll):
                o_ref[rows[t]] = new[t]
            return carry
        lax.fori_loop(0, cap // unroll, scatter, 0)


def _moe(h3, idx, gate, w1, w3, w2):
    n = h3.shape[0]
    cap = idx.shape[1]
    stride = cap + SUBLANES
    nf = D_EXPERT // MOE_F_TILE
    grid_spec = pltpu.PrefetchScalarGridSpec(
        num_scalar_prefetch=1,
        grid=(N_EXPERTS, nf),
        in_specs=[
            pl.BlockSpec(memory_space=pltpu.VMEM),
            pl.BlockSpec((1, cap, 1), lambda e, f, idx: (e, 0, 0)),
            pl.BlockSpec((1, D_MODEL, MOE_F_TILE), lambda e, f, idx: (e, 0, f)),
            pl.BlockSpec((1, D_MODEL, MOE_F_TILE), lambda e, f, idx: (e, 0, f)),
            pl.BlockSpec((1, MOE_F_TILE, D_MODEL), lambda e, f, idx: (e, f, 0)),
        ],
        out_specs=pl.BlockSpec(memory_space=pltpu.VMEM),
        scratch_shapes=[pltpu.VMEM((ROW_CHUNKS * stride, LANES), f32),
                        pltpu.VMEM((cap, D_MODEL), bf16),
                        pltpu.VMEM((cap, D_MODEL), f32)],
    )
    return pl.pallas_call(
        functools.partial(_moe_kernel, cap, stride, nf),
        grid_spec=grid_spec,
        out_shape=jax.ShapeDtypeStruct((n, ROW_CHUNKS, LANES), f32),
        compiler_params=_params("arbitrary", "arbitrary"),
        name="moe",
    )(idx.reshape(-1), h3, gate.reshape(N_EXPERTS, cap, 1), w1, w3, w2)


def _expert_choice(h2, aff_t, w1, w3, w2):
    n = h2.shape[0]
    cap = max(1, EC_CAPACITY * n // N_EXPERTS)
    pos = _route_select(aff_t, cap)
    idx, gate = _route_compact(aff_t, pos, cap)
    out3 = _moe(h2.reshape(n, ROW_CHUNKS, LANES), idx, gate, w1, w3, w2)
    return out3.reshape(n, D_MODEL)


def _final_kernel(x_ref, moe_ref, mod_ref, g_ref, o_ref):
    x = x_ref[...] + mod_ref[0][5:6] * moe_ref[...]
    o_ref[...] = _rms(x) * g_ref[...]


def _final(x, moe, mod, g):
    n = x.shape[0]
    per_mod = n // mod.shape[0]
    row = pl.BlockSpec((TM, D_MODEL), lambda i: (i, 0))
    return pl.pallas_call(
        _final_kernel,
        grid=(n // TM,),
        in_specs=[row, row, pl.BlockSpec((1, 6, D_MODEL), lambda i: (i * TM // per_mod, 0, 0)),
                  pl.BlockSpec((1, D_MODEL), lambda i: (0, 0))],
        out_specs=row,
        out_shape=jax.ShapeDtypeStruct((n, D_MODEL), f32),
        compiler_params=_params("parallel"),
        name="final_norm",
    )(x, moe, mod, g.reshape(1, D_MODEL))


def _stream(x, mods, layers, batch, seq_len, final_g, caches=None):
    consts = _hy_consts(seq_len)
    rope_tabs = _rope_tables(seq_len) if caches is not None else None
    moe = None
    kvs = []
    for l, lp in enumerate(layers):
        z, x = _in_proj(x, mods[l], lp["norm1_g"], lp["w_in"], seq_len, moe=moe,
                        modp=mods[l - 1] if l else None, rope_tabs=rope_tabs)
        if caches is None:
            a = _ctx_attention(z, lp["attn_sink"], batch, seq_len)
            kvs.append((z[:, ATTN_W:ATTN_W + KV_W], z[:, ATTN_W + KV_W:QKV_W]))
        else:
            a = _lat_attention(z, lp["attn_sink"], caches[0][:, l], caches[1][:, l], batch, seq_len)
        yh = _hyena(z, lp, consts, batch, seq_len)
        yg = _gmlp(z, lp["gm_ws"], lp["gm_b"])
        x, h2, aff_t = _out_proj(a, yh, yg, x, mods[l], lp["out_norm_g"], lp["w_out"], lp["norm2_g"], lp["router_wt"])
        moe = _expert_choice(h2, aff_t, lp["exp_w1"], lp["exp_w3"], lp["exp_w2"])
    y = _final(x, moe, mods[-1], final_g)
    return y, kvs


def kernel(x_prompt, x_sample, c, cache_k, cache_v, c_ctx, norm1_g, norm2_g, ada_w, ada_b, w_in, attn_sink,
           hy_conv_w, hy_conv_b, hy_f_w1, hy_f_b1, hy_f_w2, hy_f_b2, hy_f_w3, hy_freq, hy_bias, gm_ws, gm_b,
           out_norm_g, w_out, router_w, exp_w1, exp_w3, exp_w2, final_g):
    batch, seq, _ = x_prompt.shape
    dbatch, dseq, _ = x_sample.shape
    past = cache_k.shape[2]

    cvec = jnp.zeros((SUBLANES, D_MODEL), f32).at[0].set(c_ctx).at[1:1 + dbatch].set(c)
    mod = _ada(cvec, ada_w, ada_b)
    mods_ctx = [mod[l, 0:1].reshape(1, 6, D_MODEL) for l in range(DEPTH)]
    mods_lat = [mod[l, 1:1 + dbatch].reshape(dbatch, 6, D_MODEL) for l in range(DEPTH)]

    w_in_bf, w_out_bf, gm_ws_bf = w_in.astype(bf16), w_out.astype(bf16), gm_ws.astype(bf16)
    layers = []
    for l in range(DEPTH):
        layers.append({
            "norm1_g": norm1_g[l], "norm2_g": norm2_g[l], "w_in": w_in_bf[l], "attn_sink": attn_sink[l],
            "hy_conv_w": hy_conv_w[l], "hy_conv_b": hy_conv_b[l], "hy_f_w1": hy_f_w1[l], "hy_f_b1": hy_f_b1[l],
            "hy_f_w2": hy_f_w2[l], "hy_f_b2": hy_f_b2[l], "hy_f_w3": hy_f_w3[l], "hy_freq": hy_freq[l],
            "hy_bias": hy_bias[l], "gm_ws": gm_ws_bf[l], "gm_b": gm_b[l], "out_norm_g": out_norm_g[l],
            "w_out": w_out_bf[l], "router_wt": router_w[l].T, "exp_w1": exp_w1[l], "exp_w3": exp_w3[l],
            "exp_w2": exp_w2[l]})

    yp, kvs = _stream(x_prompt.reshape(batch * seq, D_MODEL), mods_ctx, layers, batch, seq, final_g)
    caches = (cache_k.reshape(dbatch, DEPTH, past, KV_W), cache_v.reshape(dbatch, DEPTH, past, KV_W))
    ys, _ = _stream(x_sample.reshape(dbatch * dseq, D_MODEL), mods_lat, layers, dbatch, dseq, final_g, caches=caches)

    new_k = jnp.stack([k.reshape(batch, seq, N_KV, HEAD_DIM) for k, _ in kvs], axis=1)
    new_v = jnp.stack([v.reshape(batch, seq, N_KV, HEAD_DIM) for _, v in kvs], axis=1)
    return (yp.reshape(batch, seq, D_MODEL), ys.reshape(dbatch, dseq, D_MODEL), new_k, new_v)
```

```python
import functools
import math

import numpy as np
import jax
import jax.numpy as jnp
from jax import lax
from jax.experimental import pallas as pl
from jax.experimental.pallas import tpu as pltpu

f32 = jnp.float32
bf16 = jnp.bfloat16

D_MODEL = 1024
DEPTH = 2
GRID_W = 64
BLOCK = 128
N_HEADS = 8
N_KV = 2
HEAD_DIM = 64
Q_GROUP = N_HEADS // N_KV
ATTN_W = N_HEADS * HEAD_DIM
KV_W = N_KV * HEAD_DIM
QKV_W = ATTN_W + 2 * KV_W
HY_W = 256
GM_W = 256
GM_HEADS = 4
GM_HEAD_DIM = GM_W // GM_HEADS
GM_CHUNK = 128
MIX_W = ATTN_W + HY_W + GM_W
IN_W = ATTN_W + 2 * KV_W + 3 * HY_W + 2 * GM_W
FILTER_EMB = 33
FILTER_HID = 64
HY_DECAY_HI_PCT = 0.3
HY_DECAY_LO_PCT = 1.5
HY_DECAY_TARGET = 1e-2
N_EXPERTS = 16
EC_CAPACITY = 2
D_EXPERT = 1024
ROPE_THETA = 10000.0
EPS = 1e-6
NEG = -1e30

LANES = 128
SUBLANES = 8
ROW_CHUNKS = D_MODEL // LANES
VMEM_LIMIT = 56 * 1024 * 1024
TM = 512
MOE_F_TILE = 256


def _params(*sem):
    return pltpu.CompilerParams(dimension_semantics=sem, vmem_limit_bytes=VMEM_LIMIT)


def _rms(x):
    return x * lax.rsqrt(jnp.mean(x * x, axis=-1, keepdims=True) + EPS)


def _sigmoid(x):
    return 1.0 / (1.0 + jnp.exp(-x))


def _ada_kernel(c_ref, w_ref, b_ref, o_ref):
    c = c_ref[...]
    s = (c * _sigmoid(c)).astype(bf16)
    o_ref[0] = jnp.dot(s, w_ref[0].astype(bf16), preferred_element_type=f32) + b_ref[0]


def _ada(cvec, ada_w, ada_b):
    nt = 6
    return pl.pallas_call(
        _ada_kernel,
        grid=(DEPTH, nt),
        in_specs=[
            pl.BlockSpec((SUBLANES, D_MODEL), lambda l, j: (0, 0)),
            pl.BlockSpec((1, D_MODEL, D_MODEL), lambda l, j: (l, 0, j)),
            pl.BlockSpec((1, 1, D_MODEL), lambda l, j: (l, 0, j)),
        ],
        out_specs=pl.BlockSpec((1, SUBLANES, D_MODEL), lambda l, j: (l, 0, j)),
        out_shape=jax.ShapeDtypeStruct((DEPTH, SUBLANES, 6 * D_MODEL), f32),
        compiler_params=_params("arbitrary", "arbitrary"),
        name="ada",
    )(cvec, ada_w, ada_b.reshape(DEPTH, 1, 6 * D_MODEL))


def _rope_swap(x):
    w = x.shape[-1]
    lane = lax.broadcasted_iota(jnp.int32, x.shape, 1)
    first = (lane % 32) < 16
    return jnp.where(first, pltpu.roll(x, w - 16, axis=1), pltpu.roll(x, 16, axis=1))


def _in_kernel(has_moe, rope, *refs):
    it = iter(refs)
    x_ref = next(it)
    moe_ref = next(it) if has_moe else None
    modp_ref = next(it) if has_moe else None
    mod_ref = next(it)
    g_ref = next(it)
    w_ref = next(it)
    cos_ref = next(it) if rope else None
    sin_ref = next(it) if rope else None
    z_ref = next(it)
    xo_ref = next(it) if has_moe else None

    x = x_ref[...]
    if has_moe:
        x = x + modp_ref[0][5:6] * moe_ref[...]
        xo_ref[...] = x
    m = mod_ref[0]
    h = _rms(x) * g_ref[...] * (1.0 + m[1:2]) + m[0:1]
    z = jnp.dot(h.astype(bf16), w_ref[...], preferred_element_type=f32)
    if rope:
        reps = (ATTN_W + KV_W) // LANES
        cos = jnp.concatenate([cos_ref[...]] * reps, axis=1)
        sin = jnp.concatenate([sin_ref[...]] * reps, axis=1)
        qk = z[:, :ATTN_W + KV_W]
        z_ref[:, :ATTN_W + KV_W] = qk * cos + _rope_swap(qk) * sin
        z_ref[:, ATTN_W + KV_W:] = z[:, ATTN_W + KV_W:]
    else:
        z_ref[...] = z


def _in_proj(x, mod, g1, w_in_bf, seq_len, moe=None, modp=None, rope_tabs=None):
    n = x.shape[0]
    has_moe = moe is not None
    rope = rope_tabs is not None
    per_mod = n // mod.shape[0]
    row = pl.BlockSpec((TM, D_MODEL), lambda i: (i, 0))
    modspec = pl.BlockSpec((1, 6, D_MODEL), lambda i: (i * TM // per_mod, 0, 0))
    in_specs, args = [row], [x]
    if has_moe:
        in_specs += [row, modspec]
        args += [moe, modp]
    in_specs += [modspec, pl.BlockSpec((1, D_MODEL), lambda i: (0, 0)),
                 pl.BlockSpec((D_MODEL, IN_W), lambda i: (0, 0))]
    args += [mod, g1.reshape(1, D_MODEL), w_in_bf]
    if rope:
        nt = seq_len // TM
        tab = pl.BlockSpec((TM, LANES), lambda i: (i % nt, 0))
        in_specs += [tab, tab]
        args += list(rope_tabs)
    out_specs = [pl.BlockSpec((TM, IN_W), lambda i: (i, 0))]
    out_shape = [jax.ShapeDtypeStruct((n, IN_W), f32)]
    if has_moe:
        out_specs.append(row)
        out_shape.append(jax.ShapeDtypeStruct((n, D_MODEL), f32))
    res = pl.pallas_call(
        functools.partial(_in_kernel, has_moe, rope),
        grid=(n // TM,),
        in_specs=in_specs,
        out_specs=out_specs,
        out_shape=out_shape,
        compiler_params=_params("parallel"),
        name="in_proj",
    )(*args)
    return (res[0], res[1]) if has_moe else (res[0], x)


def _rope_tables(seq_len):
    nf = HEAD_DIM // 4
    t = np.arange(seq_len)
    inv = (ROPE_THETA ** (-np.arange(nf, dtype=np.float32) / nf)).astype(np.float32)
    d = np.arange(HEAD_DIM)
    pos = np.where((d // 32)[None, :] == 0, (t // GRID_W)[:, None], (t % GRID_W)[:, None]).astype(np.float32)
    ang = (pos * inv[d % nf][None, :]).astype(np.float32)
    cos = np.cos(ang).astype(np.float32)
    sin = np.sin(ang).astype(np.float32) * np.where((d % 32) < 16, -1.0, 1.0)[None, :].astype(np.float32)
    return jnp.asarray(np.tile(cos, (1, 2))), jnp.asarray(np.tile(sin, (1, 2)))


def _softmax_pv(s, sink, v_bf):
    m = jnp.maximum(jnp.max(s, axis=-1, keepdims=True), sink)
    p = jnp.exp(s - m)
    den = jnp.sum(p, axis=-1, keepdims=True) + jnp.exp(sink - m)
    return jnp.dot(p.astype(bf16), v_bf, preferred_element_type=f32) / den


def _qk(q, k_bf):
    s = lax.dot_general(q.astype(bf16), k_bf, (((1,), (1,)), ((), ())), preferred_element_type=f32)
    return s * (1.0 / math.sqrt(HEAD_DIM))


def _ctx_attn_kernel(sink_ref, z_ref, o_ref):
    z = z_ref[...]
    outs = []
    for kv in range(N_KV):
        k = z[:, ATTN_W + kv * HEAD_DIM:ATTN_W + (kv + 1) * HEAD_DIM].astype(bf16)
        v = z[:, ATTN_W + KV_W + kv * HEAD_DIM:ATTN_W + KV_W + (kv + 1) * HEAD_DIM].astype(bf16)
        for g in range(Q_GROUP):
            h = kv * Q_GROUP + g
            q = z[:, h * HEAD_DIM:(h + 1) * HEAD_DIM]
            outs.append(_softmax_pv(_qk(q, k), sink_ref[h], v))
    o_ref[...] = jnp.concatenate(outs, axis=1)


def _ctx_attention(z, sink, batch, seq_len):
    return pl.pallas_call(
        _ctx_attn_kernel,
        grid=(batch,),
        in_specs=[pl.BlockSpec(memory_space=pltpu.SMEM),
                  pl.BlockSpec((seq_len, QKV_W), lambda b: (b, 0))],
        out_specs=pl.BlockSpec((seq_len, ATTN_W), lambda b: (b, 0)),
        out_shape=jax.ShapeDtypeStruct((batch * seq_len, ATTN_W), f32),
        compiler_params=_params("parallel"),
        name="ctx_attn",
    )(sink, z)


def _lat_attn_kernel(nb, sink_ref, zp_ref, zc_ref, zn_ref, ck_ref, cv_ref, o_ref):
    i = pl.program_id(1)
    zp, zc, zn = zp_ref[...], zc_ref[...], zn_ref[...]
    ck, cv = ck_ref[0], cv_ref[0]
    r = lax.broadcasted_iota(jnp.int32, (BLOCK, BLOCK), 0)
    j = lax.broadcasted_iota(jnp.int32, (BLOCK, BLOCK), 1)
    ok_prev = (j >= r) & (i > 0)
    ok_next = (j <= r) & (i < nb - 1)
    outs = []
    for kv in range(N_KV):
        ks = slice(ATTN_W + kv * HEAD_DIM, ATTN_W + (kv + 1) * HEAD_DIM)
        vs = slice(ATTN_W + KV_W + kv * HEAD_DIM, ATTN_W + KV_W + (kv + 1) * HEAD_DIM)
        cs = slice(kv * HEAD_DIM, (kv + 1) * HEAD_DIM)
        kp, kc, kn, kx = zp[:, ks].astype(bf16), zc[:, ks].astype(bf16), zn[:, ks].astype(bf16), ck[:, cs].astype(bf16)
        vals = jnp.concatenate([zp[:, vs], zc[:, vs], zn[:, vs], cv[:, cs]], axis=0).astype(bf16)
        for g in range(Q_GROUP):
            h = kv * Q_GROUP + g
            q = zc[:, h * HEAD_DIM:(h + 1) * HEAD_DIM]
            s = jnp.concatenate([
                jnp.where(ok_prev, _qk(q, kp), NEG),
                _qk(q, kc),
                jnp.where(ok_next, _qk(q, kn), NEG),
                _qk(q, kx)], axis=1)
            outs.append(_softmax_pv(s, sink_ref[h], vals))
    o_ref[...] = jnp.concatenate(outs, axis=1)


def _lat_attention(z, sink, ck, cv, batch, seq_len):
    nb = seq_len // BLOCK
    blk = lambda f: pl.BlockSpec((BLOCK, QKV_W), f)
    past = ck.shape[1]
    cache = pl.BlockSpec((1, past, KV_W), lambda b, i: (b, 0, 0))
    return pl.pallas_call(
        functools.partial(_lat_attn_kernel, nb),
        grid=(batch, nb),
        in_specs=[pl.BlockSpec(memory_space=pltpu.SMEM),
                  blk(lambda b, i: (b * nb + jnp.maximum(i - 1, 0), 0)),
                  blk(lambda b, i: (b * nb + i, 0)),
                  blk(lambda b, i: (b * nb + jnp.minimum(i + 1, nb - 1), 0)),
                  cache, cache],
        out_specs=pl.BlockSpec((BLOCK, ATTN_W), lambda b, i: (b * nb + i, 0)),
        out_shape=jax.ShapeDtypeStruct((batch * seq_len, ATTN_W), f32),
        compiler_params=_params("parallel", "parallel"),
        name="lat_attn",
    )(sink, z, z, z, ck, cv)


HY_TL = 256


def _hy_prep_kernel(nt, zp_ref, zc_ref, zn_ref, w_ref, b_ref, x0_ref, u_ref):
    i = pl.program_id(1)
    z = zc_ref[...]
    row = lax.broadcasted_iota(jnp.int32, z.shape, 0)
    prev_row = jnp.where(i > 0, zp_ref[SUBLANES - 1:SUBLANES, :], 0.0)
    next_row = jnp.where(i < nt - 1, zn_ref[0:1, :], 0.0)
    z_prev = jnp.where(row == 0, prev_row, pltpu.roll(z, 1, axis=0))
    z_next = jnp.where(row == HY_TL - 1, next_row, pltpu.roll(z, HY_TL - 1, axis=0))
    w = w_ref[...]
    zc = z_prev * w[0:1] + z * w[1:2] + z_next * w[2:3] + b_ref[...]
    x0_ref[...] = zc[:, :HY_W]
    u_ref[...] = zc[:, HY_W:2 * HY_W] * zc[:, 2 * HY_W:]


def _hy_prep(z, conv_w, conv_b, batch, seq_len):
    n = batch * seq_len
    nt = seq_len // HY_TL
    r8 = HY_TL // SUBLANES
    last8 = n // SUBLANES - 1
    return pl.pallas_call(
        functools.partial(_hy_prep_kernel, nt),
        grid=(batch, nt),
        in_specs=[
            pl.BlockSpec((SUBLANES, 3 * HY_W), lambda b, i: (jnp.maximum((b * nt + i) * r8 - 1, 0), 1)),
            pl.BlockSpec((HY_TL, 3 * HY_W), lambda b, i: (b * nt + i, 1)),
            pl.BlockSpec((SUBLANES, 3 * HY_W), lambda b, i: (jnp.minimum((b * nt + i + 1) * r8, last8), 1)),
            pl.BlockSpec((3, 3 * HY_W), lambda b, i: (0, 0)),
            pl.BlockSpec((1, 3 * HY_W), lambda b, i: (0, 0)),
        ],
        out_specs=[pl.BlockSpec((HY_TL, HY_W), lambda b, i: (b * nt + i, 0))] * 2,
        out_shape=[jax.ShapeDtypeStruct((n, HY_W), f32)] * 2,
        compiler_params=_params("parallel", "parallel"),
        name="hy_prep",
    )(z, z, z, conv_w, conv_b.reshape(1, 3 * HY_W))


def _hy_filter_kernel(seq_len, feats_ref, win_ref, w1_ref, b1_ref, w2_ref, b2_ref, w3_ref, fr_ref, o_ref):
    hi = lax.Precision.HIGHEST
    fr = fr_ref[...]
    h = jnp.sin(fr * (jnp.dot(feats_ref[...], w1_ref[...], precision=hi, preferred_element_type=f32) + b1_ref[...]))
    h = jnp.sin(fr * (jnp.dot(h, w2_ref[...], precision=hi, preferred_element_type=f32) + b2_ref[...]))
    h = jnp.dot(h, w3_ref[...], precision=hi, preferred_element_type=f32)
    win = win_ref[...]
    row = lax.broadcasted_iota(jnp.int32, (seq_len, HY_W), 0)
    o_ref[0:seq_len, :] = h[:, :HY_W] * win
    o_ref[seq_len:, :] = jnp.where(row == 0, 0.0, h[:, HY_W:] * win)


def _hy_filter(seq_len, consts, w1, b1, w2, b2, w3, freq):
    w1p = jnp.zeros((LANES, FILTER_HID), f32).at[:FILTER_EMB].set(w1)
    args = (consts["feats"], consts["window"], w1p, b1.reshape(1, -1), w2, b2.reshape(1, -1), w3, freq.reshape(1, -1))
    return pl.pallas_call(
        functools.partial(_hy_filter_kernel, seq_len),
        in_specs=[pl.BlockSpec(memory_space=pltpu.VMEM)] * len(args),
        out_specs=pl.BlockSpec(memory_space=pltpu.VMEM),
        out_shape=jax.ShapeDtypeStruct((2 * seq_len, HY_W), f32),
        compiler_params=pltpu.CompilerParams(vmem_limit_bytes=VMEM_LIMIT),
        name="hy_filter",
    )(*args)


def _dft_fwd_kernel(m_ref, u_ref, o_ref):
    o_ref[...] = jnp.dot(m_ref[...], u_ref[...].astype(bf16), preferred_element_type=f32)


def _dft_fwd(mat, u, batch, seq_len):
    tm = min(TM, 2 * seq_len)
    nt = 2 * seq_len // tm
    return pl.pallas_call(
        _dft_fwd_kernel,
        grid=(batch, nt),
        in_specs=[pl.BlockSpec((tm, seq_len), lambda b, i: (i, 0)),
                  pl.BlockSpec((seq_len, HY_W), lambda b, i: (b, 0))],
        out_specs=pl.BlockSpec((tm, HY_W), lambda b, i: (b * nt + i, 0)),
        out_shape=jax.ShapeDtypeStruct((batch * 2 * seq_len, HY_W), f32),
        compiler_params=_params("parallel", "parallel"),
        name="dft_fwd",
    )(mat, u)


def _hy_spec_kernel(pu_ref, wu_ref, pf_ref, qf_ref, pb_ref, qb_ref, r_ref, z_ref):
    i = pl.program_id(1)
    pu, wu = pu_ref[...], wu_ref[...]
    hr = pf_ref[...] + pb_ref[...]
    qs = qf_ref[...] + qb_ref[...]
    hi = qb_ref[...] - qf_ref[...]
    first = (lax.broadcasted_iota(jnp.int32, pu.shape, 0) == 0) & (i == 0)
    r = 2.0 * (pu * hr + wu * hi)
    z = 2.0 * (wu * hr - pu * hi)
    r_ref[...] = jnp.where(first, pu * hr, r).astype(bf16)
    z_ref[...] = jnp.where(first, wu * qs, z).astype(bf16)


def _hy_spec(su, sh, batch, seq_len):
    tk = min(TM, seq_len)
    nt = seq_len // tk
    blk = lambda f: pl.BlockSpec((tk, HY_W), f)
    res = pl.pallas_call(
        _hy_spec_kernel,
        grid=(batch, nt),
        in_specs=[blk(lambda b, i: (b * 2 * nt + i, 0)), blk(lambda b, i: (b * 2 * nt + nt + i, 0)),
                  blk(lambda b, i: (i, 0)), blk(lambda b, i: (nt + i, 0)),
                  blk(lambda b, i: (2 * nt + i, 0)), blk(lambda b, i: (3 * nt + i, 0))],
        out_specs=[blk(lambda b, i: (b * nt + i, 0))] * 2,
        out_shape=[jax.ShapeDtypeStruct((batch * seq_len, HY_W), bf16)] * 2,
        compiler_params=_params("parallel", "parallel"),
        name="hy_spec",
    )(su, su, sh, sh, sh, sh)
    return res


def _dft_inv_kernel(seq_len, ma_ref, mb_ref, r_ref, z_ref, u_ref, x0_ref, d_ref, o_ref):
    acc = jnp.dot(ma_ref[...], r_ref[...], preferred_element_type=f32)
    acc = acc + jnp.dot(mb_ref[...], z_ref[...], preferred_element_type=f32)
    u = u_ref[...]
    y = acc * (1.0 / (2 * seq_len)) + u * d_ref[...]
    o_ref[...] = x0_ref[...] * y


def _dft_inv(mat_a, mat_bt, r, zz, u, x0, d_bias, batch, seq_len):
    tm = min(TM, seq_len)
    nt = seq_len // tm
    mspec = pl.BlockSpec((tm, seq_len), lambda b, i: (i, 0))
    full = pl.BlockSpec((seq_len, HY_W), lambda b, i: (b, 0))
    row = pl.BlockSpec((tm, HY_W), lambda b, i: (b * nt + i, 0))
    return pl.pallas_call(
        functools.partial(_dft_inv_kernel, seq_len),
        grid=(batch, nt),
        in_specs=[mspec, mspec, full, full, row, row, pl.BlockSpec((1, HY_W), lambda b, i: (0, 0))],
        out_specs=row,
        out_shape=jax.ShapeDtypeStruct((batch * seq_len, HY_W), f32),
        compiler_params=_params("parallel", "parallel"),
        name="dft_inv",
    )(mat_a, mat_bt, r, zz, u, x0, d_bias.reshape(1, HY_W))


@functools.lru_cache(maxsize=None)
def _hy_consts_np(seq_len):
    n = 2 * seq_len
    k = np.arange(seq_len, dtype=np.int64)
    ang = (2.0 * np.pi / n) * ((k[:, None] * k[None, :]) % n).astype(np.float64)
    cosm = np.cos(ang)
    sinm = np.sin(ang)
    sinm[0, :] = np.where(k % 2 == 0, 1.0, -1.0)
    fwd = np.concatenate([cosm, sinm], axis=0)
    t = np.linspace(0.0, 1.0, seq_len, dtype=np.float32)[:, None]
    bands = (FILTER_EMB - 1) // 2
    fb = np.linspace(1e-4, bands - 1, bands, dtype=np.float32)[None, :]
    w = (np.float32(2.0 * math.pi) * np.arange(seq_len, dtype=np.float32)[:, None] / np.float32(seq_len)).astype(np.float32)
    feats = np.concatenate([t, np.cos(fb * w), -np.sin(fb * w)], axis=-1).astype(np.float32)
    feats = np.pad(feats, ((0, 0), (0, LANES - FILTER_EMB)))
    decay_hi = math.log(HY_DECAY_TARGET) / HY_DECAY_HI_PCT
    decay_lo = math.log(HY_DECAY_TARGET) / HY_DECAY_LO_PCT
    deltas = np.abs(np.linspace(decay_lo, decay_hi, HY_W, dtype=np.float32))
    window = np.exp(-t * deltas[None, :]).astype(np.float32)
    return fwd, cosm, np.ascontiguousarray(sinm.T), feats, window


def _hy_consts(seq_len):
    fwd, cosm, sint, feats, window = _hy_consts_np(seq_len)
    as_bf = lambda m: jnp.asarray(m, dtype=f32).astype(bf16)
    return {"fwd": as_bf(fwd), "inv_a": as_bf(cosm), "inv_b": as_bf(sint),
            "feats": jnp.asarray(feats), "window": jnp.asarray(window)}


def _hyena(z, lp, consts, batch, seq_len):
    x0, u = _hy_prep(z, lp["hy_conv_w"], lp["hy_conv_b"], batch, seq_len)
    hb = _hy_filter(seq_len, consts, lp["hy_f_w1"], lp["hy_f_b1"], lp["hy_f_w2"], lp["hy_f_b2"],
                    lp["hy_f_w3"], lp["hy_freq"])
    su = _dft_fwd(consts["fwd"], u, batch, seq_len)
    sh = _dft_fwd(consts["fwd"], hb, 2, seq_len)
    r, zz = _hy_spec(su, sh, batch, seq_len)
    return _dft_inv(consts["inv_a"], consts["inv_b"], r, zz, u, x0, lp["hy_bias"], batch, seq_len)


def _gelu(x):
    return 0.5 * x * (1.0 + jnp.tanh(math.sqrt(2.0 / math.pi) * (x + 0.044715 * (x * x * x))))


def _gmlp_kernel(z_ref, ws_ref, b_ref, o_ref):
    g = _gelu(z_ref[...])
    bias = b_ref[...]
    for c in range(TM // GM_CHUNK):
        rows = slice(c * GM_CHUNK, (c + 1) * GM_CHUNK)
        u = g[rows, :GM_W]
        v = g[rows, GM_W:].astype(bf16)
        s = jnp.concatenate(
            [jnp.dot(ws_ref[h], v[:, h * GM_HEAD_DIM:(h + 1) * GM_HEAD_DIM], preferred_element_type=f32)
             for h in range(GM_HEADS)], axis=1)
        o_ref[rows, :] = u * (s + bias)


def _gmlp(z, ws_bf, gm_b):
    n = z.shape[0]
    bias = jnp.repeat(gm_b.T, GM_HEAD_DIM, axis=1)
    return pl.pallas_call(
        _gmlp_kernel,
        grid=(n // TM,),
        in_specs=[pl.BlockSpec((TM, 2 * GM_W), lambda i: (i, 3)),
                  pl.BlockSpec((GM_HEADS, GM_CHUNK, GM_CHUNK), lambda i: (0, 0, 0)),
                  pl.BlockSpec((GM_CHUNK, GM_W), lambda i: (0, 0))],
        out_specs=pl.BlockSpec((TM, GM_W), lambda i: (i, 0)),
        out_shape=jax.ShapeDtypeStruct((n, GM_W), f32),
        compiler_params=_params("parallel"),
        name="gmlp",
    )(z, ws_bf, bias)


def _out_kernel(a_ref, yh_ref, yg_ref, x_ref, mod_ref, og_ref, w_ref, g2_ref, rw_ref, x1_ref, h2_ref, aff_ref):
    og = og_ref[...]
    mixed_in = jnp.concatenate([
        _rms(a_ref[...]) * og[:, :ATTN_W],
        _rms(yh_ref[...]) * og[:, ATTN_W:ATTN_W + HY_W],
        _rms(yg_ref[...]) * og[:, ATTN_W + HY_W:]], axis=1)
    mixed = jnp.dot(mixed_in.astype(bf16), w_ref[...], preferred_element_type=f32)
    m = mod_ref[0]
    x1 = x_ref[...] + m[2:3] * mixed
    x1_ref[...] = x1
    h2 = _rms(x1) * g2_ref[...] * (1.0 + m[4:5]) + m[3:4]
    h2_ref[...] = h2
    logits = lax.dot_general(rw_ref[...], h2, (((1,), (1,)), ((), ())),
                             precision=lax.Precision.HIGHEST, preferred_element_type=f32)
    e = jnp.exp(logits - jnp.max(logits, axis=0, keepdims=True))
    aff_ref[...] = e / jnp.sum(e, axis=0, keepdims=True)


def _out_proj(a, yh, yg, x, mod, out_g, w_out_bf, g2, router_wt):
    n = x.shape[0]
    per_mod = n // mod.shape[0]
    row = lambda w: pl.BlockSpec((TM, w), lambda i: (i, 0))
    const = lambda s: pl.BlockSpec(s, lambda i: (0,) * len(s))
    return pl.pallas_call(
        _out_kernel,
        grid=(n // TM,),
        in_specs=[row(ATTN_W), row(HY_W), row(GM_W), row(D_MODEL),
                  pl.BlockSpec((1, 6, D_MODEL), lambda i: (i * TM // per_mod, 0, 0)),
                  const((1, MIX_W)), const((MIX_W, D_MODEL)), const((1, D_MODEL)), const((N_EXPERTS, D_MODEL))],
        out_specs=[row(D_MODEL), row(D_MODEL), pl.BlockSpec((N_EXPERTS, TM), lambda i: (0, i))],
        out_shape=[jax.ShapeDtypeStruct((n, D_MODEL), f32), jax.ShapeDtypeStruct((n, D_MODEL), f32),
                   jax.ShapeDtypeStruct((N_EXPERTS, n), f32)],
        compiler_params=_params("parallel"),
        name="out_proj",
    )(a, yh, yg, x, mod, out_g.reshape(1, MIX_W), w_out_bf, g2.reshape(1, D_MODEL), router_wt)


RT_CHUNK = 512


def _prefix_incl(x01, tri):
    n = x01.shape[1]
    carry = jnp.zeros((x01.shape[0], 1), f32)
    parts = []
    for c in range(n // RT_CHUNK):
        piece = x01[:, c * RT_CHUNK:(c + 1) * RT_CHUNK]
        parts.append(jnp.dot(piece.astype(bf16), tri, preferred_element_type=f32) + carry)
        carry = carry + jnp.sum(piece, axis=1, keepdims=True)
    return jnp.concatenate(parts, axis=1)


def _route_kernel(cap, aff_ref, idx_ref, gate_ref):
    aff = aff_ref[...]
    n = aff.shape[1]

    def step(it, t):
        cand = t | (jnp.int32(1) << (30 - it))
        cnt = jnp.sum(jnp.where(aff >= lax.bitcast_convert_type(cand, f32), 1.0, 0.0), axis=1, keepdims=True)
        return jnp.where(cnt >= cap, cand, t)

    thr_bits = lax.fori_loop(0, 31, step, jnp.zeros((aff.shape[0], 1), jnp.int32))
    thr = lax.bitcast_convert_type(thr_bits, f32)
    gt = jnp.where(aff > thr, 1.0, 0.0)
    eq = jnp.where(aff == thr, 1.0, 0.0)
    room = cap - jnp.sum(gt, axis=1, keepdims=True)
    r = lax.broadcasted_iota(jnp.int32, (RT_CHUNK, RT_CHUNK), 0)
    c = lax.broadcasted_iota(jnp.int32, (RT_CHUNK, RT_CHUNK), 1)
    tri = jnp.where(r <= c, 1.0, 0.0).astype(bf16)
    sel = jnp.maximum(gt, jnp.where(_prefix_incl(eq, tri) <= room, eq, 0.0))
    slot = _prefix_incl(sel, tri) - 1.0

    tok = lax.broadcasted_iota(jnp.int32, aff.shape, 1)
    dist = jnp.where(sel > 0.0, tok - slot.astype(jnp.int32), 0)
    g = aff
    for b in range(max(1, (n - 1).bit_length())):
        sh = 1 << b
        dist_s = pltpu.roll(dist, n - sh, axis=1)
        take = (dist_s & sh) != 0
        leave = (dist & sh) != 0
        tok = jnp.where(take, pltpu.roll(tok, n - sh, axis=1), tok)
        g = jnp.where(take, pltpu.roll(g, n - sh, axis=1), g)
        dist = jnp.where(take, dist_s, jnp.where(leave, 0, dist))
    idx_ref[...] = tok[:, :cap]
    gate_ref[...] = g[:, :cap]


def _route(aff_t, cap):
    return pl.pallas_call(
        functools.partial(_route_kernel, cap),
        in_specs=[pl.BlockSpec(memory_space=pltpu.VMEM)],
        out_specs=[pl.BlockSpec(memory_space=pltpu.VMEM)] * 2,
        out_shape=[jax.ShapeDtypeStruct((N_EXPERTS, cap), jnp.int32), jax.ShapeDtypeStruct((N_EXPERTS, cap), f32)],
        compiler_params=pltpu.CompilerParams(vmem_limit_bytes=VMEM_LIMIT),
        name="route",
    )(aff_t)


GATHER_UNROLL = 8
SCATTER_UNROLL = 4


def _moe_ffn_kernel(cap, stride, idx_ref, h_ref, g_ref, w1_ref, w3_ref, w2_ref, y_ref, tile_ref, xb_ref, acc_ref):
    base = pl.program_id(0) * cap

    def gather(q, carry):
        for t in range(GATHER_UNROLL):
            r = q * GATHER_UNROLL + t
            tile_ref[pl.ds(r, ROW_CHUNKS, stride=stride), :] = h_ref[idx_ref[base + r]]
        return carry
    lax.fori_loop(0, cap // GATHER_UNROLL, gather, 0)
    xb_ref[...] = jnp.concatenate(
        [tile_ref[j * stride:j * stride + cap, :] for j in range(ROW_CHUNKS)], axis=1).astype(bf16)

    xb = xb_ref[...]
    for f in range(D_EXPERT // MOE_F_TILE):
        cols = slice(f * MOE_F_TILE, (f + 1) * MOE_F_TILE)
        a = jnp.dot(xb, w1_ref[0, 0, :, cols].astype(bf16), preferred_element_type=f32)
        b = jnp.dot(xb, w3_ref[0, 0, :, cols].astype(bf16), preferred_element_type=f32)
        he = (a * _sigmoid(a) * b).astype(bf16)
        part = jnp.dot(he, w2_ref[0, 0, cols, :].astype(bf16), preferred_element_type=f32)
        if f == 0:
            acc_ref[...] = part
        else:
            acc_ref[...] += part

    y = acc_ref[...] * g_ref[0]
    for j in range(ROW_CHUNKS):
        y_ref[0, j * stride:j * stride + cap, :] = y[:, j * LANES:(j + 1) * LANES]
        y_ref[0, j * stride + cap:(j + 1) * stride, :] = jnp.zeros((stride - cap, LANES), f32)


def _moe_ffn(layer, h3, idx, gate, w1, w3, w2):
    cap = idx.shape[1]
    stride = cap + SUBLANES
    grid_spec = pltpu.PrefetchScalarGridSpec(
        num_scalar_prefetch=1,
        grid=(N_EXPERTS,),
        in_specs=[
            pl.BlockSpec(memory_space=pltpu.VMEM),
            pl.BlockSpec((1, cap, 1), lambda e, idx: (e, 0, 0)),
            pl.BlockSpec((1, 1, D_MODEL, D_EXPERT), lambda e, idx: (layer, e, 0, 0)),
            pl.BlockSpec((1, 1, D_MODEL, D_EXPERT), lambda e, idx: (layer, e, 0, 0)),
            pl.BlockSpec((1, 1, D_EXPERT, D_MODEL), lambda e, idx: (layer, e, 0, 0)),
        ],
        out_specs=pl.BlockSpec((1, ROW_CHUNKS * stride, LANES), lambda e, idx: (e, 0, 0)),
        scratch_shapes=[pltpu.VMEM((ROW_CHUNKS * stride, LANES), f32),
                        pltpu.VMEM((cap, D_MODEL), bf16),
                        pltpu.VMEM((cap, D_MODEL), f32)],
    )
    return pl.pallas_call(
        functools.partial(_moe_ffn_kernel, cap, stride),
        grid_spec=grid_spec,
        out_shape=jax.ShapeDtypeStruct((N_EXPERTS, ROW_CHUNKS * stride, LANES), f32),
        compiler_params=_params("arbitrary"),
        name="moe_ffn",
    )(idx.reshape(-1), h3, gate.reshape(N_EXPERTS, cap, 1), w1, w3, w2)


def _moe_combine_kernel(cap, stride, idx_ref, y_ref, o_ref):
    e = pl.program_id(0)
    base = e * cap

    @pl.when(e == 0)
    def _():
        o_ref[...] = jnp.zeros_like(o_ref)

    def scatter(q, carry):
        rows = [idx_ref[base + q * SCATTER_UNROLL + t] for t in range(SCATTER_UNROLL)]
        new = [o_ref[rows[t]] + y_ref[0, pl.ds(q * SCATTER_UNROLL + t, ROW_CHUNKS, stride=stride), :]
               for t in range(SCATTER_UNROLL)]
        for t in range(SCATTER_UNROLL):
            o_ref[rows[t]] = new[t]
        return carry
    lax.fori_loop(0, cap // SCATTER_UNROLL, scatter, 0)


def _moe_combine(n, idx, y_cm):
    cap = idx.shape[1]
    stride = y_cm.shape[1] // ROW_CHUNKS
    grid_spec = pltpu.PrefetchScalarGridSpec(
        num_scalar_prefetch=1,
        grid=(N_EXPERTS,),
        in_specs=[pl.BlockSpec((1, ROW_CHUNKS * stride, LANES), lambda e, idx: (e, 0, 0))],
        out_specs=pl.BlockSpec(memory_space=pltpu.VMEM),
    )
    return pl.pallas_call(
        functools.partial(_moe_combine_kernel, cap, stride),
        grid_spec=grid_spec,
        out_shape=jax.ShapeDtypeStruct((n, ROW_CHUNKS, LANES), f32),
        compiler_params=_params("arbitrary"),
        name="moe_combine",
    )(idx.reshape(-1), y_cm)


def _expert_choice(layer, h2, aff_t, w1, w3, w2):
    n = h2.shape[0]
    cap = max(1, EC_CAPACITY * n // N_EXPERTS)
    idx, gate = _route(aff_t, cap)
    y_cm = _moe_ffn(layer, h2.reshape(n, ROW_CHUNKS, LANES), idx, gate, w1, w3, w2)
    return _moe_combine(n, idx, y_cm).reshape(n, D_MODEL)


def _final_kernel(x_ref, moe_ref, mod_ref, g_ref, o_ref):
    x = x_ref[...] + mod_ref[0][5:6] * moe_ref[...]
    o_ref[...] = _rms(x) * g_ref[...]


def _final(x, moe, mod, g):
    n = x.shape[0]
    per_mod = n // mod.shape[0]
    row = pl.BlockSpec((TM, D_MODEL), lambda i: (i, 0))
    return pl.pallas_call(
        _final_kernel,
        grid=(n // TM,),
        in_specs=[row, row, pl.BlockSpec((1, 6, D_MODEL), lambda i: (i * TM // per_mod, 0, 0)),
                  pl.BlockSpec((1, D_MODEL), lambda i: (0, 0))],
        out_specs=row,
        out_shape=jax.ShapeDtypeStruct((n, D_MODEL), f32),
        compiler_params=_params("parallel"),
        name="final_norm",
    )(x, moe, mod, g.reshape(1, D_MODEL))


def _stream(x, mods, layers, experts, batch, seq_len, final_g, caches=None):
    consts = _hy_consts(seq_len)
    rope_tabs = _rope_tables(seq_len) if caches is not None else None
    moe = None
    kvs = []
    for l, lp in enumerate(layers):
        z, x = _in_proj(x, mods[l], lp["norm1_g"], lp["w_in"], seq_len, moe=moe,
                        modp=mods[l - 1] if l else None, rope_tabs=rope_tabs)
        if caches is None:
            a = _ctx_attention(z, lp["attn_sink"], batch, seq_len)
            kvs.append((z[:, ATTN_W:ATTN_W + KV_W], z[:, ATTN_W + KV_W:QKV_W]))
        else:
            a = _lat_attention(z, lp["attn_sink"], caches[0][:, l], caches[1][:, l], batch, seq_len)
        yh = _hyena(z, lp, consts, batch, seq_len)
        yg = _gmlp(z, lp["gm_ws"], lp["gm_b"])
        x, h2, aff_t = _out_proj(a, yh, yg, x, mods[l], lp["out_norm_g"], lp["w_out"], lp["norm2_g"], lp["router_wt"])
        moe = _expert_choice(l, h2, aff_t, *experts)
    y = _final(x, moe, mods[-1], final_g)
    return y, kvs


def kernel(x_prompt, x_sample, c, cache_k, cache_v, c_ctx, norm1_g, norm2_g, ada_w, ada_b, w_in, attn_sink,
           hy_conv_w, hy_conv_b, hy_f_w1, hy_f_b1, hy_f_w2, hy_f_b2, hy_f_w3, hy_freq, hy_bias, gm_ws, gm_b,
           out_norm_g, w_out, router_w, exp_w1, exp_w3, exp_w2, final_g):
    batch, seq, _ = x_prompt.shape
    dbatch, dseq, _ = x_sample.shape
    past = cache_k.shape[2]

    cvec = jnp.zeros((SUBLANES, D_MODEL), f32).at[0].set(c_ctx).at[1:1 + dbatch].set(c)
    mod = _ada(cvec, ada_w, ada_b)
    mods_ctx = [mod[l, 0:1].reshape(1, 6, D_MODEL) for l in range(DEPTH)]
    mods_lat = [mod[l, 1:1 + dbatch].reshape(dbatch, 6, D_MODEL) for l in range(DEPTH)]

    w_in_bf, w_out_bf, gm_ws_bf = w_in.astype(bf16), w_out.astype(bf16), gm_ws.astype(bf16)
    layers = []
    for l in range(DEPTH):
        layers.append({
            "norm1_g": norm1_g[l], "norm2_g": norm2_g[l], "w_in": w_in_bf[l], "attn_sink": attn_sink[l],
            "hy_conv_w": hy_conv_w[l], "hy_conv_b": hy_conv_b[l], "hy_f_w1": hy_f_w1[l], "hy_f_b1": hy_f_b1[l],
            "hy_f_w2": hy_f_w2[l], "hy_f_b2": hy_f_b2[l], "hy_f_w3": hy_f_w3[l], "hy_freq": hy_freq[l],
            "hy_bias": hy_bias[l], "gm_ws": gm_ws_bf[l], "gm_b": gm_b[l], "out_norm_g": out_norm_g[l],
            "w_out": w_out_bf[l], "router_wt": router_w[l].T})
    experts = (exp_w1, exp_w3, exp_w2)

    yp, kvs = _stream(x_prompt.reshape(batch * seq, D_MODEL), mods_ctx, layers, experts, batch, seq, final_g)
    caches = (cache_k.reshape(dbatch, DEPTH, past, KV_W), cache_v.reshape(dbatch, DEPTH, past, KV_W))
    ys, _ = _stream(x_sample.reshape(dbatch * dseq, D_MODEL), mods_lat, layers, experts, dbatch, dseq, final_g,
                   caches=caches)

    new_k = jnp.stack([k.reshape(batch, seq, N_KV, HEAD_DIM) for k, _ in kvs], axis=1)
    new_v = jnp.stack([v.reshape(batch, seq, N_KV, HEAD_DIM) for _, v in kvs], axis=1)
    return (yp.reshape(batch, seq, D_MODEL), ys.reshape(dbatch, dseq, D_MODEL), new_k, new_v)
```

```python
import functools
import math

import numpy as np
import jax
import jax.numpy as jnp
from jax import lax
from jax.experimental import pallas as pl
from jax.experimental.pallas import tpu as pltpu

f32 = jnp.float32
bf16 = jnp.bfloat16

D_MODEL = 1024
DEPTH = 2
GRID_W = 64
BLOCK = 128
N_HEADS = 8
N_KV = 2
HEAD_DIM = 64
Q_GROUP = N_HEADS // N_KV
ATTN_W = N_HEADS * HEAD_DIM
KV_W = N_KV * HEAD_DIM
QKV_W = ATTN_W + 2 * KV_W
HY_W = 256
GM_W = 256
GM_HEADS = 4
GM_HEAD_DIM = GM_W // GM_HEADS
GM_CHUNK = 128
MIX_W = ATTN_W + HY_W + GM_W
IN_W = ATTN_W + 2 * KV_W + 3 * HY_W + 2 * GM_W
FILTER_EMB = 33
FILTER_HID = 64
HY_DECAY_HI_PCT = 0.3
HY_DECAY_LO_PCT = 1.5
HY_DECAY_TARGET = 1e-2
N_EXPERTS = 16
EC_CAPACITY = 2
D_EXPERT = 1024
ROPE_THETA = 10000.0
EPS = 1e-6
NEG = -1e30

LANES = 128
SUBLANES = 8
ROW_CHUNKS = D_MODEL // LANES
VMEM_LIMIT = 56 * 1024 * 1024
TM = 512
MOE_F_TILE = 256


def _params(*sem):
    return pltpu.CompilerParams(dimension_semantics=sem, vmem_limit_bytes=VMEM_LIMIT)


def _rms(x):
    return x * lax.rsqrt(jnp.mean(x * x, axis=-1, keepdims=True) + EPS)


def _sigmoid(x):
    return 1.0 / (1.0 + jnp.exp(-x))


def _ada_kernel(c_ref, w_ref, b_ref, o_ref):
    c = c_ref[...]
    s = (c * _sigmoid(c)).astype(bf16)
    o_ref[0] = jnp.dot(s, w_ref[0].astype(bf16), preferred_element_type=f32) + b_ref[0]


def _ada(cvec, ada_w, ada_b):
    nt = 6
    return pl.pallas_call(
        _ada_kernel,
        grid=(DEPTH, nt),
        in_specs=[
            pl.BlockSpec((SUBLANES, D_MODEL), lambda l, j: (0, 0)),
            pl.BlockSpec((1, D_MODEL, D_MODEL), lambda l, j: (l, 0, j)),
            pl.BlockSpec((1, 1, D_MODEL), lambda l, j: (l, 0, j)),
        ],
        out_specs=pl.BlockSpec((1, SUBLANES, D_MODEL), lambda l, j: (l, 0, j)),
        out_shape=jax.ShapeDtypeStruct((DEPTH, SUBLANES, 6 * D_MODEL), f32),
        compiler_params=_params("arbitrary", "arbitrary"),
        name="ada",
    )(cvec, ada_w, ada_b.reshape(DEPTH, 1, 6 * D_MODEL))


def _rope_swap(x):
    w = x.shape[-1]
    lane = lax.broadcasted_iota(jnp.int32, x.shape, 1)
    first = (lane % 32) < 16
    return jnp.where(first, pltpu.roll(x, w - 16, axis=1), pltpu.roll(x, 16, axis=1))


def _in_kernel(has_moe, rope, *refs):
    it = iter(refs)
    x_ref = next(it)
    moe_ref = next(it) if has_moe else None
    modp_ref = next(it) if has_moe else None
    mod_ref = next(it)
    g_ref = next(it)
    w_ref = next(it)
    cos_ref = next(it) if rope else None
    sin_ref = next(it) if rope else None
    z_ref = next(it)
    xo_ref = next(it) if has_moe else None

    x = x_ref[...]
    if has_moe:
        x = x + modp_ref[0][5:6] * moe_ref[...]
        xo_ref[...] = x
    m = mod_ref[0]
    h = _rms(x) * g_ref[...] * (1.0 + m[1:2]) + m[0:1]
    z = jnp.dot(h.astype(bf16), w_ref[...], preferred_element_type=f32)
    if rope:
        reps = (ATTN_W + KV_W) // LANES
        cos = jnp.concatenate([cos_ref[...]] * reps, axis=1)
        sin = jnp.concatenate([sin_ref[...]] * reps, axis=1)
        qk = z[:, :ATTN_W + KV_W]
        z_ref[:, :ATTN_W + KV_W] = qk * cos + _rope_swap(qk) * sin
        z_ref[:, ATTN_W + KV_W:] = z[:, ATTN_W + KV_W:]
    else:
        z_ref[...] = z


def _in_proj(x, mod, g1, w_in_bf, seq_len, moe=None, modp=None, rope_tabs=None):
    n = x.shape[0]
    has_moe = moe is not None
    rope = rope_tabs is not None
    per_mod = n // mod.shape[0]
    row = pl.BlockSpec((TM, D_MODEL), lambda i: (i, 0))
    modspec = pl.BlockSpec((1, 6, D_MODEL), lambda i: (i * TM // per_mod, 0, 0))
    in_specs, args = [row], [x]
    if has_moe:
        in_specs += [row, modspec]
        args += [moe, modp]
    in_specs += [modspec, pl.BlockSpec((1, D_MODEL), lambda i: (0, 0)),
                 pl.BlockSpec((D_MODEL, IN_W), lambda i: (0, 0))]
    args += [mod, g1.reshape(1, D_MODEL), w_in_bf]
    if rope:
        nt = seq_len // TM
        tab = pl.BlockSpec((TM, LANES), lambda i: (i % nt, 0))
        in_specs += [tab, tab]
        args += list(rope_tabs)
    out_specs = [pl.BlockSpec((TM, IN_W), lambda i: (i, 0))]
    out_shape = [jax.ShapeDtypeStruct((n, IN_W), f32)]
    if has_moe:
        out_specs.append(row)
        out_shape.append(jax.ShapeDtypeStruct((n, D_MODEL), f32))
    res = pl.pallas_call(
        functools.partial(_in_kernel, has_moe, rope),
        grid=(n // TM,),
        in_specs=in_specs,
        out_specs=out_specs,
        out_shape=out_shape,
        compiler_params=_params("parallel"),
        name="in_proj",
    )(*args)
    return (res[0], res[1]) if has_moe else (res[0], x)


def _rope_tables(seq_len):
    nf = HEAD_DIM // 4
    t = np.arange(seq_len)
    inv = (ROPE_THETA ** (-np.arange(nf, dtype=np.float32) / nf)).astype(np.float32)
    d = np.arange(HEAD_DIM)
    pos = np.where((d // 32)[None, :] == 0, (t // GRID_W)[:, None], (t % GRID_W)[:, None]).astype(np.float32)
    ang = (pos * inv[d % nf][None, :]).astype(np.float32)
    cos = np.cos(ang).astype(np.float32)
    sin = np.sin(ang).astype(np.float32) * np.where((d % 32) < 16, -1.0, 1.0)[None, :].astype(np.float32)
    return jnp.asarray(np.tile(cos, (1, 2))), jnp.asarray(np.tile(sin, (1, 2)))


def _softmax_pv(s, sink, v_bf):
    m = jnp.maximum(jnp.max(s, axis=-1, keepdims=True), sink)
    p = jnp.exp(s - m)
    den = jnp.sum(p, axis=-1, keepdims=True) + jnp.exp(sink - m)
    return jnp.dot(p.astype(bf16), v_bf, preferred_element_type=f32) / den


def _qk(q, k_bf):
    s = lax.dot_general(q.astype(bf16), k_bf, (((1,), (1,)), ((), ())), preferred_element_type=f32)
    return s * (1.0 / math.sqrt(HEAD_DIM))


def _ctx_attn_kernel(sink_ref, z_ref, o_ref):
    z = z_ref[...]
    outs = []
    for kv in range(N_KV):
        k = z[:, ATTN_W + kv * HEAD_DIM:ATTN_W + (kv + 1) * HEAD_DIM].astype(bf16)
        v = z[:, ATTN_W + KV_W + kv * HEAD_DIM:ATTN_W + KV_W + (kv + 1) * HEAD_DIM].astype(bf16)
        for g in range(Q_GROUP):
            h = kv * Q_GROUP + g
            q = z[:, h * HEAD_DIM:(h + 1) * HEAD_DIM]
            outs.append(_softmax_pv(_qk(q, k), sink_ref[h], v))
    o_ref[...] = jnp.concatenate(outs, axis=1)


def _ctx_attention(z, sink, batch, seq_len):
    return pl.pallas_call(
        _ctx_attn_kernel,
        grid=(batch,),
        in_specs=[pl.BlockSpec(memory_space=pltpu.SMEM),
                  pl.BlockSpec((seq_len, QKV_W), lambda b: (b, 0))],
        out_specs=pl.BlockSpec((seq_len, ATTN_W), lambda b: (b, 0)),
        out_shape=jax.ShapeDtypeStruct((batch * seq_len, ATTN_W), f32),
        compiler_params=_params("parallel"),
        name="ctx_attn",
    )(sink, z)


def _lat_attn_kernel(nb, sink_ref, zp_ref, zc_ref, zn_ref, ck_ref, cv_ref, o_ref):
    i = pl.program_id(1)
    zp, zc, zn = zp_ref[...], zc_ref[...], zn_ref[...]
    ck, cv = ck_ref[0], cv_ref[0]
    r = lax.broadcasted_iota(jnp.int32, (BLOCK, BLOCK), 0)
    j = lax.broadcasted_iota(jnp.int32, (BLOCK, BLOCK), 1)
    ok_prev = (j >= r) & (i > 0)
    ok_next = (j <= r) & (i < nb - 1)
    outs = []
    for kv in range(N_KV):
        ks = slice(ATTN_W + kv * HEAD_DIM, ATTN_W + (kv + 1) * HEAD_DIM)
        vs = slice(ATTN_W + KV_W + kv * HEAD_DIM, ATTN_W + KV_W + (kv + 1) * HEAD_DIM)
        cs = slice(kv * HEAD_DIM, (kv + 1) * HEAD_DIM)
        kp, kc, kn, kx = zp[:, ks].astype(bf16), zc[:, ks].astype(bf16), zn[:, ks].astype(bf16), ck[:, cs].astype(bf16)
        vals = jnp.concatenate([zp[:, vs], zc[:, vs], zn[:, vs], cv[:, cs]], axis=0).astype(bf16)
        for g in range(Q_GROUP):
            h = kv * Q_GROUP + g
            q = zc[:, h * HEAD_DIM:(h + 1) * HEAD_DIM]
            s = jnp.concatenate([
                jnp.where(ok_prev, _qk(q, kp), NEG),
                _qk(q, kc),
                jnp.where(ok_next, _qk(q, kn), NEG),
                _qk(q, kx)], axis=1)
            outs.append(_softmax_pv(s, sink_ref[h], vals))
    o_ref[...] = jnp.concatenate(outs, axis=1)


def _lat_attention(z, sink, ck, cv, batch, seq_len):
    nb = seq_len // BLOCK
    blk = lambda f: pl.BlockSpec((BLOCK, QKV_W), f)
    past = ck.shape[1]
    cache = pl.BlockSpec((1, past, KV_W), lambda b, i: (b, 0, 0))
    return pl.pallas_call(
        functools.partial(_lat_attn_kernel, nb),
        grid=(batch, nb),
        in_specs=[pl.BlockSpec(memory_space=pltpu.SMEM),
                  blk(lambda b, i: (b * nb + jnp.maximum(i - 1, 0), 0)),
                  blk(lambda b, i: (b * nb + i, 0)),
                  blk(lambda b, i: (b * nb + jnp.minimum(i + 1, nb - 1), 0)),
                  cache, cache],
        out_specs=pl.BlockSpec((BLOCK, ATTN_W), lambda b, i: (b * nb + i, 0)),
        out_shape=jax.ShapeDtypeStruct((batch * seq_len, ATTN_W), f32),
        compiler_params=_params("parallel", "parallel"),
        name="lat_attn",
    )(sink, z, z, z, ck, cv)


HY_TL = 256


def _hy_prep_kernel(nt, zp_ref, zc_ref, zn_ref, w_ref, b_ref, x0_ref, u_ref):
    i = pl.program_id(1)
    z = zc_ref[...]
    row = lax.broadcasted_iota(jnp.int32, z.shape, 0)
    prev_row = jnp.where(i > 0, zp_ref[SUBLANES - 1:SUBLANES, :], 0.0)
    next_row = jnp.where(i < nt - 1, zn_ref[0:1, :], 0.0)
    z_prev = jnp.where(row == 0, prev_row, pltpu.roll(z, 1, axis=0))
    z_next = jnp.where(row == HY_TL - 1, next_row, pltpu.roll(z, HY_TL - 1, axis=0))
    w = w_ref[...]
    zc = z_prev * w[0:1] + z * w[1:2] + z_next * w[2:3] + b_ref[...]
    x0_ref[...] = zc[:, :HY_W]
    u_ref[...] = zc[:, HY_W:2 * HY_W] * zc[:, 2 * HY_W:]


def _hy_prep(z, conv_w, conv_b, batch, seq_len):
    n = batch * seq_len
    nt = seq_len // HY_TL
    r8 = HY_TL // SUBLANES
    last8 = n // SUBLANES - 1
    return pl.pallas_call(
        functools.partial(_hy_prep_kernel, nt),
        grid=(batch, nt),
        in_specs=[
            pl.BlockSpec((SUBLANES, 3 * HY_W), lambda b, i: (jnp.maximum((b * nt + i) * r8 - 1, 0), 1)),
            pl.BlockSpec((HY_TL, 3 * HY_W), lambda b, i: (b * nt + i, 1)),
            pl.BlockSpec((SUBLANES, 3 * HY_W), lambda b, i: (jnp.minimum((b * nt + i + 1) * r8, last8), 1)),
            pl.BlockSpec((3, 3 * HY_W), lambda b, i: (0, 0)),
            pl.BlockSpec((1, 3 * HY_W), lambda b, i: (0, 0)),
        ],
        out_specs=[pl.BlockSpec((HY_TL, HY_W), lambda b, i: (b * nt + i, 0))] * 2,
        out_shape=[jax.ShapeDtypeStruct((n, HY_W), f32)] * 2,
        compiler_params=_params("parallel", "parallel"),
        name="hy_prep",
    )(z, z, z, conv_w, conv_b.reshape(1, 3 * HY_W))


def _hy_filter_kernel(seq_len, feats_ref, win_ref, w1_ref, b1_ref, w2_ref, b2_ref, w3_ref, fr_ref, o_ref):
    hi = lax.Precision.HIGHEST
    fr = fr_ref[...]
    h = jnp.sin(fr * (jnp.dot(feats_ref[...], w1_ref[...], precision=hi, preferred_element_type=f32) + b1_ref[...]))
    h = jnp.sin(fr * (jnp.dot(h, w2_ref[...], precision=hi, preferred_element_type=f32) + b2_ref[...]))
    h = jnp.dot(h, w3_ref[...], precision=hi, preferred_element_type=f32)
    win = win_ref[...]
    row = lax.broadcasted_iota(jnp.int32, (seq_len, HY_W), 0)
    o_ref[0:seq_len, :] = h[:, :HY_W] * win
    o_ref[seq_len:, :] = jnp.where(row == 0, 0.0, h[:, HY_W:] * win)


def _hy_filter(seq_len, consts, w1, b1, w2, b2, w3, freq):
    w1p = jnp.zeros((LANES, FILTER_HID), f32).at[:FILTER_EMB].set(w1)
    args = (consts["feats"], consts["window"], w1p, b1.reshape(1, -1), w2, b2.reshape(1, -1), w3, freq.reshape(1, -1))
    return pl.pallas_call(
        functools.partial(_hy_filter_kernel, seq_len),
        in_specs=[pl.BlockSpec(memory_space=pltpu.VMEM)] * len(args),
        out_specs=pl.BlockSpec(memory_space=pltpu.VMEM),
        out_shape=jax.ShapeDtypeStruct((2 * seq_len, HY_W), f32),
        compiler_params=pltpu.CompilerParams(vmem_limit_bytes=VMEM_LIMIT),
        name="hy_filter",
    )(*args)


def _hy_tiles(seq_len):
    return min(TM, seq_len // 2), min(TM, seq_len)


def _hy_fwd_kernel(nb, tk, m_ref, u_ref, hb_ref, r_ref, z_ref, rhs_ref):
    i = pl.program_id(0)

    @pl.when(i == 0)
    def _():
        for b in range(nb):
            rhs_ref[:, b * HY_W:(b + 1) * HY_W] = u_ref[b].astype(bf16)
        rhs_ref[:, nb * HY_W:(nb + 1) * HY_W] = hb_ref[0].astype(bf16)
        rhs_ref[:, (nb + 1) * HY_W:] = hb_ref[1].astype(bf16)

    acc = jnp.dot(m_ref[...], rhs_ref[...], preferred_element_type=f32)
    p, w = acc[:tk], acc[tk:]
    c0 = nb * HY_W
    pf, pb = p[:, c0:c0 + HY_W], p[:, c0 + HY_W:]
    qf, qb = w[:, c0:c0 + HY_W], w[:, c0 + HY_W:]
    hr = pf + pb
    qs = qf + qb
    hi = qb - qf
    first = (lax.broadcasted_iota(jnp.int32, (tk, HY_W), 0) == 0) & (i == 0)
    for b in range(nb):
        pu, wu = p[:, b * HY_W:(b + 1) * HY_W], w[:, b * HY_W:(b + 1) * HY_W]
        r_ref[b] = jnp.where(first, pu * hr, 2.0 * (pu * hr + wu * hi)).astype(bf16)
        z_ref[b] = jnp.where(first, wu * qs, 2.0 * (wu * hr - pu * hi)).astype(bf16)


def _hy_fwd(mat, u, hb, batch, seq_len):
    tk, _ = _hy_tiles(seq_len)
    out = pl.BlockSpec((batch, tk, HY_W), lambda i: (0, i, 0))
    return pl.pallas_call(
        functools.partial(_hy_fwd_kernel, batch, tk),
        grid=(seq_len // tk,),
        in_specs=[pl.BlockSpec((2 * tk, seq_len), lambda i: (i, 0)),
                  pl.BlockSpec((batch, seq_len, HY_W), lambda i: (0, 0, 0)),
                  pl.BlockSpec((2, seq_len, HY_W), lambda i: (0, 0, 0))],
        out_specs=[out, out],
        out_shape=[jax.ShapeDtypeStruct((batch, seq_len, HY_W), bf16)] * 2,
        scratch_shapes=[pltpu.VMEM((seq_len, (batch + 2) * HY_W), bf16)],
        compiler_params=_params("arbitrary"),
        name="hy_fwd",
    )(mat, u, hb)


def _hy_inv_kernel(nb, seq_len, m_ref, r_ref, z_ref, u_ref, x0_ref, d_ref, o_ref, rhs_ref):
    i = pl.program_id(0)

    @pl.when(i == 0)
    def _():
        for b in range(nb):
            rhs_ref[:seq_len, b * HY_W:(b + 1) * HY_W] = r_ref[b]
            rhs_ref[seq_len:, b * HY_W:(b + 1) * HY_W] = z_ref[b]

    acc = jnp.dot(m_ref[...], rhs_ref[...], preferred_element_type=f32)
    d = d_ref[...]
    for b in range(nb):
        y = acc[:, b * HY_W:(b + 1) * HY_W] * (1.0 / (2 * seq_len)) + u_ref[b] * d
        o_ref[b] = x0_ref[b] * y


def _hy_inv(mat, r, zz, u, x0, d_bias, batch, seq_len):
    _, tm = _hy_tiles(seq_len)
    full = pl.BlockSpec((batch, seq_len, HY_W), lambda i: (0, 0, 0))
    row = pl.BlockSpec((batch, tm, HY_W), lambda i: (0, i, 0))
    return pl.pallas_call(
        functools.partial(_hy_inv_kernel, batch, seq_len),
        grid=(seq_len // tm,),
        in_specs=[pl.BlockSpec((tm, 2 * seq_len), lambda i: (i, 0)), full, full, row, row,
                  pl.BlockSpec((1, HY_W), lambda i: (0, 0))],
        out_specs=row,
        out_shape=jax.ShapeDtypeStruct((batch, seq_len, HY_W), f32),
        scratch_shapes=[pltpu.VMEM((2 * seq_len, batch * HY_W), bf16)],
        compiler_params=_params("arbitrary"),
        name="hy_inv",
    )(mat, r, zz, u, x0, d_bias.reshape(1, HY_W))


@functools.lru_cache(maxsize=None)
def _hy_consts_np(seq_len):
    n = 2 * seq_len
    k = np.arange(seq_len, dtype=np.int64)
    ang = (2.0 * np.pi / n) * ((k[:, None] * k[None, :]) % n).astype(np.float64)
    cosm = np.cos(ang)
    sinm = np.sin(ang)
    sinm[0, :] = np.where(k % 2 == 0, 1.0, -1.0)
    tk, _ = _hy_tiles(seq_len)
    fwd = np.concatenate([np.concatenate([cosm[i:i + tk], sinm[i:i + tk]], axis=0)
                          for i in range(0, seq_len, tk)], axis=0)
    inv = np.concatenate([cosm, sinm.T], axis=1)
    t = np.linspace(0.0, 1.0, seq_len, dtype=np.float32)[:, None]
    bands = (FILTER_EMB - 1) // 2
    fb = np.linspace(1e-4, bands - 1, bands, dtype=np.float32)[None, :]
    w = (np.float32(2.0 * math.pi) * np.arange(seq_len, dtype=np.float32)[:, None] / np.float32(seq_len)).astype(np.float32)
    feats = np.concatenate([t, np.cos(fb * w), -np.sin(fb * w)], axis=-1).astype(np.float32)
    feats = np.pad(feats, ((0, 0), (0, LANES - FILTER_EMB)))
    decay_hi = math.log(HY_DECAY_TARGET) / HY_DECAY_HI_PCT
    decay_lo = math.log(HY_DECAY_TARGET) / HY_DECAY_LO_PCT
    deltas = np.abs(np.linspace(decay_lo, decay_hi, HY_W, dtype=np.float32))
    window = np.exp(-t * deltas[None, :]).astype(np.float32)
    return fwd, inv, feats, window


def _hy_consts(seq_len):
    fwd, inv, feats, window = _hy_consts_np(seq_len)
    as_bf = lambda m: jnp.asarray(m, dtype=f32).astype(bf16)
    return {"fwd": as_bf(fwd), "inv": as_bf(inv), "feats": jnp.asarray(feats), "window": jnp.asarray(window)}


def _hyena(z, lp, consts, batch, seq_len):
    x0, u = _hy_prep(z, lp["hy_conv_w"], lp["hy_conv_b"], batch, seq_len)
    x0, u = x0.reshape(batch, seq_len, HY_W), u.reshape(batch, seq_len, HY_W)
    hb = _hy_filter(seq_len, consts, lp["hy_f_w1"], lp["hy_f_b1"], lp["hy_f_w2"], lp["hy_f_b2"],
                    lp["hy_f_w3"], lp["hy_freq"]).reshape(2, seq_len, HY_W)
    r, zz = _hy_fwd(consts["fwd"], u, hb, batch, seq_len)
    yh = _hy_inv(consts["inv"], r, zz, u, x0, lp["hy_bias"], batch, seq_len)
    return yh.reshape(batch * seq_len, HY_W)


def _gelu(x):
    return 0.5 * x * (1.0 + jnp.tanh(math.sqrt(2.0 / math.pi) * (x + 0.044715 * (x * x * x))))


def _gmlp_kernel(z_ref, ws_ref, b_ref, o_ref):
    g = _gelu(z_ref[...])
    bias = b_ref[...]
    for c in range(TM // GM_CHUNK):
        rows = slice(c * GM_CHUNK, (c + 1) * GM_CHUNK)
        u = g[rows, :GM_W]
        v = g[rows, GM_W:].astype(bf16)
        s = jnp.concatenate(
            [jnp.dot(ws_ref[h], v[:, h * GM_HEAD_DIM:(h + 1) * GM_HEAD_DIM], preferred_element_type=f32)
             for h in range(GM_HEADS)], axis=1)
        o_ref[rows, :] = u * (s + bias)


def _gmlp(z, ws_bf, gm_b):
    n = z.shape[0]
    bias = jnp.repeat(gm_b.T, GM_HEAD_DIM, axis=1)
    return pl.pallas_call(
        _gmlp_kernel,
        grid=(n // TM,),
        in_specs=[pl.BlockSpec((TM, 2 * GM_W), lambda i: (i, 3)),
                  pl.BlockSpec((GM_HEADS, GM_CHUNK, GM_CHUNK), lambda i: (0, 0, 0)),
                  pl.BlockSpec((GM_CHUNK, GM_W), lambda i: (0, 0))],
        out_specs=pl.BlockSpec((TM, GM_W), lambda i: (i, 0)),
        out_shape=jax.ShapeDtypeStruct((n, GM_W), f32),
        compiler_params=_params("parallel"),
        name="gmlp",
    )(z, ws_bf, bias)


def _out_kernel(a_ref, yh_ref, yg_ref, x_ref, mod_ref, og_ref, w_ref, g2_ref, rw_ref, x1_ref, h2_ref, aff_ref):
    og = og_ref[...]
    mixed_in = jnp.concatenate([
        _rms(a_ref[...]) * og[:, :ATTN_W],
        _rms(yh_ref[...]) * og[:, ATTN_W:ATTN_W + HY_W],
        _rms(yg_ref[...]) * og[:, ATTN_W + HY_W:]], axis=1)
    mixed = jnp.dot(mixed_in.astype(bf16), w_ref[...], preferred_element_type=f32)
    m = mod_ref[0]
    x1 = x_ref[...] + m[2:3] * mixed
    x1_ref[...] = x1
    h2 = _rms(x1) * g2_ref[...] * (1.0 + m[4:5]) + m[3:4]
    h2_ref[...] = h2
    logits = lax.dot_general(rw_ref[...], h2, (((1,), (1,)), ((), ())),
                             precision=lax.Precision.HIGHEST, preferred_element_type=f32)
    e = jnp.exp(logits - jnp.max(logits, axis=0, keepdims=True))
    aff_ref[...] = e / jnp.sum(e, axis=0, keepdims=True)


def _out_proj(a, yh, yg, x, mod, out_g, w_out_bf, g2, router_wt):
    n = x.shape[0]
    per_mod = n // mod.shape[0]
    row = lambda w: pl.BlockSpec((TM, w), lambda i: (i, 0))
    const = lambda s: pl.BlockSpec(s, lambda i: (0,) * len(s))
    return pl.pallas_call(
        _out_kernel,
        grid=(n // TM,),
        in_specs=[row(ATTN_W), row(HY_W), row(GM_W), row(D_MODEL),
                  pl.BlockSpec((1, 6, D_MODEL), lambda i: (i * TM // per_mod, 0, 0)),
                  const((1, MIX_W)), const((MIX_W, D_MODEL)), const((1, D_MODEL)), const((N_EXPERTS, D_MODEL))],
        out_specs=[row(D_MODEL), row(D_MODEL), pl.BlockSpec((N_EXPERTS, TM), lambda i: (0, i))],
        out_shape=[jax.ShapeDtypeStruct((n, D_MODEL), f32), jax.ShapeDtypeStruct((n, D_MODEL), f32),
                   jax.ShapeDtypeStruct((N_EXPERTS, n), f32)],
        compiler_params=_params("parallel"),
        name="out_proj",
    )(a, yh, yg, x, mod, out_g.reshape(1, MIX_W), w_out_bf, g2.reshape(1, D_MODEL), router_wt)


RT_CHUNK = 512


def _prefix_incl(x01, tri):
    n = x01.shape[1]
    carry = jnp.zeros((x01.shape[0], 1), f32)
    parts = []
    for c in range(n // RT_CHUNK):
        piece = x01[:, c * RT_CHUNK:(c + 1) * RT_CHUNK]
        parts.append(jnp.dot(piece.astype(bf16), tri, preferred_element_type=f32) + carry)
        carry = carry + jnp.sum(piece, axis=1, keepdims=True)
    return jnp.concatenate(parts, axis=1)


def _route_kernel(cap, aff_ref, idx_ref, gate_ref):
    aff = aff_ref[...]
    n = aff.shape[1]

    def step(it, t):
        cand = t | (jnp.int32(1) << (30 - it))
        cnt = jnp.sum(jnp.where(aff >= lax.bitcast_convert_type(cand, f32), 1.0, 0.0), axis=1, keepdims=True)
        return jnp.where(cnt >= cap, cand, t)

    thr_bits = lax.fori_loop(0, 31, step, jnp.zeros((aff.shape[0], 1), jnp.int32))
    thr = lax.bitcast_convert_type(thr_bits, f32)
    gt = jnp.where(aff > thr, 1.0, 0.0)
    eq = jnp.where(aff == thr, 1.0, 0.0)
    room = cap - jnp.sum(gt, axis=1, keepdims=True)
    r = lax.broadcasted_iota(jnp.int32, (RT_CHUNK, RT_CHUNK), 0)
    c = lax.broadcasted_iota(jnp.int32, (RT_CHUNK, RT_CHUNK), 1)
    tri = jnp.where(r <= c, 1.0, 0.0).astype(bf16)
    sel = jnp.maximum(gt, jnp.where(_prefix_incl(eq, tri) <= room, eq, 0.0))
    slot = _prefix_incl(sel, tri) - 1.0

    tok = lax.broadcasted_iota(jnp.int32, aff.shape, 1)
    dist = jnp.where(sel > 0.0, tok - slot.astype(jnp.int32), 0)
    g = aff
    for b in range(max(1, (n - 1).bit_length())):
        sh = 1 << b
        dist_s = pltpu.roll(dist, n - sh, axis=1)
        take = (dist_s & sh) != 0
        leave = (dist & sh) != 0
        tok = jnp.where(take, pltpu.roll(tok, n - sh, axis=1), tok)
        g = jnp.where(take, pltpu.roll(g, n - sh, axis=1), g)
        dist = jnp.where(take, dist_s, jnp.where(leave, 0, dist))
    idx_ref[...] = tok[:, :cap]
    gate_ref[...] = g[:, :cap]


def _route(aff_t, cap):
    return pl.pallas_call(
        functools.partial(_route_kernel, cap),
        in_specs=[pl.BlockSpec(memory_space=pltpu.VMEM)],
        out_specs=[pl.BlockSpec(memory_space=pltpu.VMEM)] * 2,
        out_shape=[jax.ShapeDtypeStruct((N_EXPERTS, cap), jnp.int32), jax.ShapeDtypeStruct((N_EXPERTS, cap), f32)],
        compiler_params=pltpu.CompilerParams(vmem_limit_bytes=VMEM_LIMIT),
        name="route",
    )(aff_t)


GATHER_UNROLL = 8
SCATTER_UNROLL = 4


def _moe_ffn_kernel(cap, stride, idx_ref, h_ref, g_ref, w1_ref, w3_ref, w2_ref, y_ref, tile_ref, xb_ref, acc_ref):
    base = pl.program_id(0) * cap

    def gather(q, carry):
        for t in range(GATHER_UNROLL):
            r = q * GATHER_UNROLL + t
            tile_ref[pl.ds(r, ROW_CHUNKS, stride=stride), :] = h_ref[idx_ref[base + r]]
        return carry
    lax.fori_loop(0, cap // GATHER_UNROLL, gather, 0)
    xb_ref[...] = jnp.concatenate(
        [tile_ref[j * stride:j * stride + cap, :] for j in range(ROW_CHUNKS)], axis=1).astype(bf16)

    xb = xb_ref[...]
    for f in range(D_EXPERT // MOE_F_TILE):
        cols = slice(f * MOE_F_TILE, (f + 1) * MOE_F_TILE)
        a = jnp.dot(xb, w1_ref[0, 0, :, cols].astype(bf16), preferred_element_type=f32)
        b = jnp.dot(xb, w3_ref[0, 0, :, cols].astype(bf16), preferred_element_type=f32)
        he = (a * _sigmoid(a) * b).astype(bf16)
        part = jnp.dot(he, w2_ref[0, 0, cols, :].astype(bf16), preferred_element_type=f32)
        if f == 0:
            acc_ref[...] = part
        else:
            acc_ref[...] += part

    y = acc_ref[...] * g_ref[0]
    for j in range(ROW_CHUNKS):
        y_ref[0, j * stride:j * stride + cap, :] = y[:, j * LANES:(j + 1) * LANES]
        y_ref[0, j * stride + cap:(j + 1) * stride, :] = jnp.zeros((stride - cap, LANES), f32)


def _moe_ffn(layer, h3, idx, gate, w1, w3, w2):
    cap = idx.shape[1]
    stride = cap + SUBLANES
    grid_spec = pltpu.PrefetchScalarGridSpec(
        num_scalar_prefetch=1,
        grid=(N_EXPERTS,),
        in_specs=[
            pl.BlockSpec(memory_space=pltpu.VMEM),
            pl.BlockSpec((1, cap, 1), lambda e, idx: (e, 0, 0)),
            pl.BlockSpec((1, 1, D_MODEL, D_EXPERT), lambda e, idx: (layer, e, 0, 0)),
            pl.BlockSpec((1, 1, D_MODEL, D_EXPERT), lambda e, idx: (layer, e, 0, 0)),
            pl.BlockSpec((1, 1, D_EXPERT, D_MODEL), lambda e, idx: (layer, e, 0, 0)),
        ],
        out_specs=pl.BlockSpec((1, ROW_CHUNKS * stride, LANES), lambda e, idx: (e, 0, 0)),
        scratch_shapes=[pltpu.VMEM((ROW_CHUNKS * stride, LANES), f32),
                        pltpu.VMEM((cap, D_MODEL), bf16),
                        pltpu.VMEM((cap, D_MODEL), f32)],
    )
    return pl.pallas_call(
        functools.partial(_moe_ffn_kernel, cap, stride),
        grid_spec=grid_spec,
        out_shape=jax.ShapeDtypeStruct((N_EXPERTS, ROW_CHUNKS * stride, LANES), f32),
        compiler_params=_params("arbitrary"),
        name="moe_ffn",
    )(idx.reshape(-1), h3, gate.reshape(N_EXPERTS, cap, 1), w1, w3, w2)


def _moe_combine_kernel(cap, stride, idx_ref, y_ref, o_ref):
    e = pl.program_id(0)
    base = e * cap

    @pl.when(e == 0)
    def _():
        o_ref[...] = jnp.zeros_like(o_ref)

    def scatter(q, carry):
        rows = [idx_ref[base + q * SCATTER_UNROLL + t] for t in range(SCATTER_UNROLL)]
        new = [o_ref[rows[t]] + y_ref[0, pl.ds(q * SCATTER_UNROLL + t, ROW_CHUNKS, stride=stride), :]
               for t in range(SCATTER_UNROLL)]
        for t in range(SCATTER_UNROLL):
            o_ref[rows[t]] = new[t]
        return carry
    lax.fori_loop(0, cap // SCATTER_UNROLL, scatter, 0)


def _moe_combine(n, idx, y_cm):
    cap = idx.shape[1]
    stride = y_cm.shape[1] // ROW_CHUNKS
    grid_spec = pltpu.PrefetchScalarGridSpec(
        num_scalar_prefetch=1,
        grid=(N_EXPERTS,),
        in_specs=[pl.BlockSpec((1, ROW_CHUNKS * stride, LANES), lambda e, idx: (e, 0, 0))],
        out_specs=pl.BlockSpec(memory_space=pltpu.VMEM),
    )
    return pl.pallas_call(
        functools.partial(_moe_combine_kernel, cap, stride),
        grid_spec=grid_spec,
        out_shape=jax.ShapeDtypeStruct((n, ROW_CHUNKS, LANES), f32),
        compiler_params=_params("arbitrary"),
        name="moe_combine",
    )(idx.reshape(-1), y_cm)


def _expert_choice(layer, h2, aff_t, w1, w3, w2):
    n = h2.shape[0]
    cap = max(1, EC_CAPACITY * n // N_EXPERTS)
    idx, gate = _route(aff_t, cap)
    y_cm = _moe_ffn(layer, h2.reshape(n, ROW_CHUNKS, LANES), idx, gate, w1, w3, w2)
    return _moe_combine(n, idx, y_cm).reshape(n, D_MODEL)


def _final_kernel(x_ref, moe_ref, mod_ref, g_ref, o_ref):
    x = x_ref[...] + mod_ref[0][5:6] * moe_ref[...]
    o_ref[...] = _rms(x) * g_ref[...]


def _final(x, moe, mod, g):
    n = x.shape[0]
    per_mod = n // mod.shape[0]
    row = pl.BlockSpec((TM, D_MODEL), lambda i: (i, 0))
    return pl.pallas_call(
        _final_kernel,
        grid=(n // TM,),
        in_specs=[row, row, pl.BlockSpec((1, 6, D_MODEL), lambda i: (i * TM // per_mod, 0, 0)),
                  pl.BlockSpec((1, D_MODEL), lambda i: (0, 0))],
        out_specs=row,
        out_shape=jax.ShapeDtypeStruct((n, D_MODEL), f32),
        compiler_params=_params("parallel"),
        name="final_norm",
    )(x, moe, mod, g.reshape(1, D_MODEL))


def _stream(x, mods, layers, experts, batch, seq_len, final_g, caches=None):
    consts = _hy_consts(seq_len)
    rope_tabs = _rope_tables(seq_len) if caches is not None else None
    moe = None
    kvs = []
    for l, lp in enumerate(layers):
        z, x = _in_proj(x, mods[l], lp["norm1_g"], lp["w_in"], seq_len, moe=moe,
                        modp=mods[l - 1] if l else None, rope_tabs=rope_tabs)
        if caches is None:
            a = _ctx_attention(z, lp["attn_sink"], batch, seq_len)
            kvs.append((z[:, ATTN_W:ATTN_W + KV_W], z[:, ATTN_W + KV_W:QKV_W]))
        else:
            a = _lat_attention(z, lp["attn_sink"], caches[0][:, l], caches[1][:, l], batch, seq_len)
        yh = _hyena(z, lp, consts, batch, seq_len)
        yg = _gmlp(z, lp["gm_ws"], lp["gm_b"])
        x, h2, aff_t = _out_proj(a, yh, yg, x, mods[l], lp["out_norm_g"], lp["w_out"], lp["norm2_g"], lp["router_wt"])
        moe = _expert_choice(l, h2, aff_t, *experts)
    y = _final(x, moe, mods[-1], final_g)
    return y, kvs


def kernel(x_prompt, x_sample, c, cache_k, cache_v, c_ctx, norm1_g, norm2_g, ada_w, ada_b, w_in, attn_sink,
           hy_conv_w, hy_conv_b, hy_f_w1, hy_f_b1, hy_f_w2, hy_f_b2, hy_f_w3, hy_freq, hy_bias, gm_ws, gm_b,
           out_norm_g, w_out, router_w, exp_w1, exp_w3, exp_w2, final_g):
    batch, seq, _ = x_prompt.shape
    dbatch, dseq, _ = x_sample.shape
    past = cache_k.shape[2]

    cvec = jnp.zeros((SUBLANES, D_MODEL), f32).at[0].set(c_ctx).at[1:1 + dbatch].set(c)
    mod = _ada(cvec, ada_w, ada_b)
    mods_ctx = [mod[l, 0:1].reshape(1, 6, D_MODEL) for l in range(DEPTH)]
    mods_lat = [mod[l, 1:1 + dbatch].reshape(dbatch, 6, D_MODEL) for l in range(DEPTH)]

    w_in_bf, w_out_bf, gm_ws_bf = w_in.astype(bf16), w_out.astype(bf16), gm_ws.astype(bf16)
    layers = []
    for l in range(DEPTH):
        layers.append({
            "norm1_g": norm1_g[l], "norm2_g": norm2_g[l], "w_in": w_in_bf[l], "attn_sink": attn_sink[l],
            "hy_conv_w": hy_conv_w[l], "hy_conv_b": hy_conv_b[l], "hy_f_w1": hy_f_w1[l], "hy_f_b1": hy_f_b1[l],
            "hy_f_w2": hy_f_w2[l], "hy_f_b2": hy_f_b2[l], "hy_f_w3": hy_f_w3[l], "hy_freq": hy_freq[l],
            "hy_bias": hy_bias[l], "gm_ws": gm_ws_bf[l], "gm_b": gm_b[l], "out_norm_g": out_norm_g[l],
            "w_out": w_out_bf[l], "router_wt": router_w[l].T})
    experts = (exp_w1, exp_w3, exp_w2)

    yp, kvs = _stream(x_prompt.reshape(batch * seq, D_MODEL), mods_ctx, layers, experts, batch, seq, final_g)
    caches = (cache_k.reshape(dbatch, DEPTH, past, KV_W), cache_v.reshape(dbatch, DEPTH, past, KV_W))
    ys, _ = _stream(x_sample.reshape(dbatch * dseq, D_MODEL), mods_lat, layers, experts, dbatch, dseq, final_g,
                   caches=caches)

    new_k = jnp.stack([k.reshape(batch, seq, N_KV, HEAD_DIM) for k, _ in kvs], axis=1)
    new_v = jnp.stack([v.reshape(batch, seq, N_KV, HEAD_DIM) for _, v in kvs], axis=1)
    return (yp.reshape(batch, seq, D_MODEL), ys.reshape(dbatch, dseq, D_MODEL), new_k, new_v)
```

```python
import functools
import math

import numpy as np
import jax
import jax.numpy as jnp
from jax import lax
from jax.experimental import pallas as pl
from jax.experimental.pallas import tpu as pltpu

f32 = jnp.float32
bf16 = jnp.bfloat16

D_MODEL = 1024
DEPTH = 2
GRID_W = 64
BLOCK = 128
N_HEADS = 8
N_KV = 2
HEAD_DIM = 64
Q_GROUP = N_HEADS // N_KV
ATTN_W = N_HEADS * HEAD_DIM
KV_W = N_KV * HEAD_DIM
QKV_W = ATTN_W + 2 * KV_W
HY_W = 256
GM_W = 256
GM_HEADS = 4
GM_HEAD_DIM = GM_W // GM_HEADS
GM_CHUNK = 128
MIX_W = ATTN_W + HY_W + GM_W
IN_W = ATTN_W + 2 * KV_W + 3 * HY_W + 2 * GM_W
FILTER_EMB = 33
FILTER_HID = 64
HY_DECAY_HI_PCT = 0.3
HY_DECAY_LO_PCT = 1.5
HY_DECAY_TARGET = 1e-2
N_EXPERTS = 16
EC_CAPACITY = 2
D_EXPERT = 1024
ROPE_THETA = 10000.0
EPS = 1e-6
NEG = -1e30

LANES = 128
SUBLANES = 8
ROW_CHUNKS = D_MODEL // LANES
VMEM_LIMIT = 56 * 1024 * 1024
TM = 512
MOE_F_TILE = 256


def _params(*sem):
    return pltpu.CompilerParams(dimension_semantics=sem, vmem_limit_bytes=VMEM_LIMIT)


def _rms(x):
    return x * lax.rsqrt(jnp.mean(x * x, axis=-1, keepdims=True) + EPS)


def _sigmoid(x):
    return 1.0 / (1.0 + jnp.exp(-x))


def _ada_kernel(c_ref, w_ref, b_ref, o_ref):
    c = c_ref[...]
    s = (c * _sigmoid(c)).astype(bf16)
    o_ref[0] = jnp.dot(s, w_ref[0].astype(bf16), preferred_element_type=f32) + b_ref[0]


def _ada(cvec, ada_w, ada_b):
    nt = 6
    return pl.pallas_call(
        _ada_kernel,
        grid=(DEPTH, nt),
        in_specs=[
            pl.BlockSpec((SUBLANES, D_MODEL), lambda l, j: (0, 0)),
            pl.BlockSpec((1, D_MODEL, D_MODEL), lambda l, j: (l, 0, j)),
            pl.BlockSpec((1, 1, D_MODEL), lambda l, j: (l, 0, j)),
        ],
        out_specs=pl.BlockSpec((1, SUBLANES, D_MODEL), lambda l, j: (l, 0, j)),
        out_shape=jax.ShapeDtypeStruct((DEPTH, SUBLANES, 6 * D_MODEL), f32),
        compiler_params=_params("arbitrary", "arbitrary"),
        name="ada",
    )(cvec, ada_w, ada_b.reshape(DEPTH, 1, 6 * D_MODEL))


def _rope_swap(x):
    w = x.shape[-1]
    lane = lax.broadcasted_iota(jnp.int32, x.shape, 1)
    first = (lane % 32) < 16
    return jnp.where(first, pltpu.roll(x, w - 16, axis=1), pltpu.roll(x, 16, axis=1))


def _in_kernel(has_moe, rope, *refs):
    it = iter(refs)
    x_ref = next(it)
    moe_ref = next(it) if has_moe else None
    modp_ref = next(it) if has_moe else None
    mod_ref = next(it)
    g_ref = next(it)
    w_ref = next(it)
    cos_ref = next(it) if rope else None
    sin_ref = next(it) if rope else None
    z_ref = next(it)
    xo_ref = next(it) if has_moe else None

    x = x_ref[...]
    if has_moe:
        x = x + modp_ref[0][5:6] * moe_ref[...]
        xo_ref[...] = x
    m = mod_ref[0]
    h = _rms(x) * g_ref[...] * (1.0 + m[1:2]) + m[0:1]
    z = jnp.dot(h.astype(bf16), w_ref[...], preferred_element_type=f32)
    if rope:
        reps = (ATTN_W + KV_W) // LANES
        cos = jnp.concatenate([cos_ref[...]] * reps, axis=1)
        sin = jnp.concatenate([sin_ref[...]] * reps, axis=1)
        qk = z[:, :ATTN_W + KV_W]
        z_ref[:, :ATTN_W + KV_W] = qk * cos + _rope_swap(qk) * sin
        z_ref[:, ATTN_W + KV_W:] = z[:, ATTN_W + KV_W:]
    else:
        z_ref[...] = z


def _in_proj(x, mod, g1, w_in_bf, seq_len, moe=None, modp=None, rope_tabs=None):
    n = x.shape[0]
    has_moe = moe is not None
    rope = rope_tabs is not None
    per_mod = n // mod.shape[0]
    row = pl.BlockSpec((TM, D_MODEL), lambda i: (i, 0))
    modspec = pl.BlockSpec((1, 6, D_MODEL), lambda i: (i * TM // per_mod, 0, 0))
    in_specs, args = [row], [x]
    if has_moe:
        in_specs += [row, modspec]
        args += [moe, modp]
    in_specs += [modspec, pl.BlockSpec((1, D_MODEL), lambda i: (0, 0)),
                 pl.BlockSpec((D_MODEL, IN_W), lambda i: (0, 0))]
    args += [mod, g1.reshape(1, D_MODEL), w_in_bf]
    if rope:
        nt = seq_len // TM
        tab = pl.BlockSpec((TM, LANES), lambda i: (i % nt, 0))
        in_specs += [tab, tab]
        args += list(rope_tabs)
    out_specs = [pl.BlockSpec((TM, IN_W), lambda i: (i, 0))]
    out_shape = [jax.ShapeDtypeStruct((n, IN_W), f32)]
    if has_moe:
        out_specs.append(row)
        out_shape.append(jax.ShapeDtypeStruct((n, D_MODEL), f32))
    res = pl.pallas_call(
        functools.partial(_in_kernel, has_moe, rope),
        grid=(n // TM,),
        in_specs=in_specs,
        out_specs=out_specs,
        out_shape=out_shape,
        compiler_params=_params("parallel"),
        name="in_proj",
    )(*args)
    return (res[0], res[1]) if has_moe else (res[0], x)


def _rope_tables(seq_len):
    nf = HEAD_DIM // 4
    t = np.arange(seq_len)
    inv = (ROPE_THETA ** (-np.arange(nf, dtype=np.float32) / nf)).astype(np.float32)
    d = np.arange(HEAD_DIM)
    pos = np.where((d // 32)[None, :] == 0, (t // GRID_W)[:, None], (t % GRID_W)[:, None]).astype(np.float32)
    ang = (pos * inv[d % nf][None, :]).astype(np.float32)
    cos = np.cos(ang).astype(np.float32)
    sin = np.sin(ang).astype(np.float32) * np.where((d % 32) < 16, -1.0, 1.0)[None, :].astype(np.float32)
    return jnp.asarray(np.tile(cos, (1, 2))), jnp.asarray(np.tile(sin, (1, 2)))


LOG2E = math.log2(math.e)


def _stack_queries(z, sink_ref, kv):
    rows = z.shape[0]
    heads = range(kv * Q_GROUP, (kv + 1) * Q_GROUP)
    qs = jnp.concatenate([z[:, h * HEAD_DIM:(h + 1) * HEAD_DIM] for h in heads], axis=0)
    sink = jnp.concatenate([jnp.full((1, rows), sink_ref[h] * LOG2E, f32) for h in heads], axis=1)
    return (qs * (LOG2E / math.sqrt(HEAD_DIM))).astype(bf16), sink


def _scores_t(k, qs_bf):
    return lax.dot_general(k.astype(bf16), qs_bf, (((1,), (1,)), ((), ())), preferred_element_type=f32)


def _softmax_pv_t(st, sink, vals, rows):
    m = jnp.maximum(jnp.max(st, axis=0, keepdims=True), sink)
    pt = jnp.exp2(st - m).astype(bf16)
    v_bf = vals.astype(bf16)
    v_ext = jnp.concatenate([v_bf, jnp.ones_like(v_bf)], axis=1)
    ovt = lax.dot_general(v_ext, pt, (((0,), (0,)), ((), ())), preferred_element_type=f32)
    ot = ovt[:HEAD_DIM] / (ovt[HEAD_DIM:HEAD_DIM + 1] + jnp.exp2(sink - m))
    return [ot[:, g * rows:(g + 1) * rows].T for g in range(Q_GROUP)]


def _ctx_attn_kernel(sink_ref, z_ref, o_ref):
    z = z_ref[...]
    outs = []
    for kv in range(N_KV):
        k = z[:, ATTN_W + kv * HEAD_DIM:ATTN_W + (kv + 1) * HEAD_DIM]
        v = z[:, ATTN_W + KV_W + kv * HEAD_DIM:ATTN_W + KV_W + (kv + 1) * HEAD_DIM]
        qs, sink = _stack_queries(z, sink_ref, kv)
        outs += _softmax_pv_t(_scores_t(k, qs), sink, v, z.shape[0])
    o_ref[...] = jnp.concatenate(outs, axis=1)


def _ctx_attention(z, sink, batch, seq_len):
    return pl.pallas_call(
        _ctx_attn_kernel,
        grid=(batch,),
        in_specs=[pl.BlockSpec(memory_space=pltpu.SMEM),
                  pl.BlockSpec((seq_len, QKV_W), lambda b: (b, 0))],
        out_specs=pl.BlockSpec((seq_len, ATTN_W), lambda b: (b, 0)),
        out_shape=jax.ShapeDtypeStruct((batch * seq_len, ATTN_W), f32),
        compiler_params=_params("parallel"),
        name="ctx_attn",
    )(sink, z)


def _lat_attn_kernel(nb, sink_ref, zp_ref, zc_ref, zn_ref, ck_ref, cv_ref, o_ref):
    i = pl.program_id(1)
    zp, zc, zn = zp_ref[...], zc_ref[...], zn_ref[...]
    ck, cv = ck_ref[0], cv_ref[0]
    width = Q_GROUP * BLOCK
    j = lax.broadcasted_iota(jnp.int32, (BLOCK, width), 0)
    r = lax.broadcasted_iota(jnp.int32, (BLOCK, width), 1) % BLOCK
    ok_prev = j >= r + jnp.where(i > 0, 0, BLOCK)
    ok_next = j <= r - jnp.where(i < nb - 1, 0, BLOCK)
    outs = []
    for kv in range(N_KV):
        ks = slice(ATTN_W + kv * HEAD_DIM, ATTN_W + (kv + 1) * HEAD_DIM)
        vs = slice(ATTN_W + KV_W + kv * HEAD_DIM, ATTN_W + KV_W + (kv + 1) * HEAD_DIM)
        cs = slice(kv * HEAD_DIM, (kv + 1) * HEAD_DIM)
        qs, sink = _stack_queries(zc, sink_ref, kv)
        st = jnp.concatenate([
            jnp.where(ok_prev, _scores_t(zp[:, ks], qs), NEG),
            _scores_t(zc[:, ks], qs),
            jnp.where(ok_next, _scores_t(zn[:, ks], qs), NEG),
            _scores_t(ck[:, cs], qs)], axis=0)
        vals = jnp.concatenate([zp[:, vs], zc[:, vs], zn[:, vs], cv[:, cs]], axis=0)
        outs += _softmax_pv_t(st, sink, vals, BLOCK)
    o_ref[...] = jnp.concatenate(outs, axis=1)


def _lat_attention(z, sink, ck, cv, batch, seq_len):
    nb = seq_len // BLOCK
    blk = lambda f: pl.BlockSpec((BLOCK, QKV_W), f)
    past = ck.shape[1]
    cache = pl.BlockSpec((1, past, KV_W), lambda b, i: (b, 0, 0))
    return pl.pallas_call(
        functools.partial(_lat_attn_kernel, nb),
        grid=(batch, nb),
        in_specs=[pl.BlockSpec(memory_space=pltpu.SMEM),
                  blk(lambda b, i: (b * nb + jnp.maximum(i - 1, 0), 0)),
                  blk(lambda b, i: (b * nb + i, 0)),
                  blk(lambda b, i: (b * nb + jnp.minimum(i + 1, nb - 1), 0)),
                  cache, cache],
        out_specs=pl.BlockSpec((BLOCK, ATTN_W), lambda b, i: (b * nb + i, 0)),
        out_shape=jax.ShapeDtypeStruct((batch * seq_len, ATTN_W), f32),
        compiler_params=_params("parallel", "parallel"),
        name="lat_attn",
    )(sink, z, z, z, ck, cv)


HY_TL = 256


def _hy_prep_kernel(nt, zp_ref, zc_ref, zn_ref, w_ref, b_ref, x0_ref, u_ref):
    i = pl.program_id(1)
    z = zc_ref[...]
    row = lax.broadcasted_iota(jnp.int32, z.shape, 0)
    prev_row = jnp.where(i > 0, zp_ref[SUBLANES - 1:SUBLANES, :], 0.0)
    next_row = jnp.where(i < nt - 1, zn_ref[0:1, :], 0.0)
    z_prev = jnp.where(row == 0, prev_row, pltpu.roll(z, 1, axis=0))
    z_next = jnp.where(row == HY_TL - 1, next_row, pltpu.roll(z, HY_TL - 1, axis=0))
    w = w_ref[...]
    zc = z_prev * w[0:1] + z * w[1:2] + z_next * w[2:3] + b_ref[...]
    x0_ref[...] = zc[:, :HY_W]
    u_ref[...] = zc[:, HY_W:2 * HY_W] * zc[:, 2 * HY_W:]


def _hy_prep(z, conv_w, conv_b, batch, seq_len):
    n = batch * seq_len
    nt = seq_len // HY_TL
    r8 = HY_TL // SUBLANES
    last8 = n // SUBLANES - 1
    return pl.pallas_call(
        functools.partial(_hy_prep_kernel, nt),
        grid=(batch, nt),
        in_specs=[
            pl.BlockSpec((SUBLANES, 3 * HY_W), lambda b, i: (jnp.maximum((b * nt + i) * r8 - 1, 0), 1)),
            pl.BlockSpec((HY_TL, 3 * HY_W), lambda b, i: (b * nt + i, 1)),
            pl.BlockSpec((SUBLANES, 3 * HY_W), lambda b, i: (jnp.minimum((b * nt + i + 1) * r8, last8), 1)),
            pl.BlockSpec((3, 3 * HY_W), lambda b, i: (0, 0)),
            pl.BlockSpec((1, 3 * HY_W), lambda b, i: (0, 0)),
        ],
        out_specs=[pl.BlockSpec((HY_TL, HY_W), lambda b, i: (b * nt + i, 0))] * 2,
        out_shape=[jax.ShapeDtypeStruct((n, HY_W), f32)] * 2,
        compiler_params=_params("parallel", "parallel"),
        name="hy_prep",
    )(z, z, z, conv_w, conv_b.reshape(1, 3 * HY_W))


def _hy_filter_kernel(seq_len, feats_ref, win_ref, w1_ref, b1_ref, w2_ref, b2_ref, w3_ref, fr_ref, o_ref):
    hi = lax.Precision.HIGHEST
    fr = fr_ref[...]
    h = jnp.sin(fr * (jnp.dot(feats_ref[...], w1_ref[...], precision=hi, preferred_element_type=f32) + b1_ref[...]))
    h = jnp.sin(fr * (jnp.dot(h, w2_ref[...], precision=hi, preferred_element_type=f32) + b2_ref[...]))
    h = jnp.dot(h, w3_ref[...], precision=hi, preferred_element_type=f32)
    win = win_ref[...]
    row = lax.broadcasted_iota(jnp.int32, (seq_len, HY_W), 0)
    o_ref[0:seq_len, :] = h[:, :HY_W] * win
    o_ref[seq_len:, :] = jnp.where(row == 0, 0.0, h[:, HY_W:] * win)


def _hy_filter(seq_len, consts, w1, b1, w2, b2, w3, freq):
    w1p = jnp.zeros((LANES, FILTER_HID), f32).at[:FILTER_EMB].set(w1)
    args = (consts["feats"], consts["window"], w1p, b1.reshape(1, -1), w2, b2.reshape(1, -1), w3, freq.reshape(1, -1))
    return pl.pallas_call(
        functools.partial(_hy_filter_kernel, seq_len),
        in_specs=[pl.BlockSpec(memory_space=pltpu.VMEM)] * len(args),
        out_specs=pl.BlockSpec(memory_space=pltpu.VMEM),
        out_shape=jax.ShapeDtypeStruct((2 * seq_len, HY_W), f32),
        compiler_params=pltpu.CompilerParams(vmem_limit_bytes=VMEM_LIMIT),
        name="hy_filter",
    )(*args)


def _hy_tiles(seq_len):
    return min(TM, seq_len // 2), min(TM, seq_len)


def _hy_fwd_kernel(nb, tk, m_ref, u_ref, hb_ref, r_ref, z_ref, rhs_ref):
    i = pl.program_id(0)

    @pl.when(i == 0)
    def _():
        for b in range(nb):
            rhs_ref[:, b * HY_W:(b + 1) * HY_W] = u_ref[b].astype(bf16)
        rhs_ref[:, nb * HY_W:(nb + 1) * HY_W] = hb_ref[0].astype(bf16)
        rhs_ref[:, (nb + 1) * HY_W:] = hb_ref[1].astype(bf16)

    acc = jnp.dot(m_ref[...], rhs_ref[...], preferred_element_type=f32)
    p, w = acc[:tk], acc[tk:]
    c0 = nb * HY_W
    pf, pb = p[:, c0:c0 + HY_W], p[:, c0 + HY_W:]
    qf, qb = w[:, c0:c0 + HY_W], w[:, c0 + HY_W:]
    hr = pf + pb
    qs = qf + qb
    hi = qb - qf
    first = (lax.broadcasted_iota(jnp.int32, (tk, HY_W), 0) == 0) & (i == 0)
    for b in range(nb):
        pu, wu = p[:, b * HY_W:(b + 1) * HY_W], w[:, b * HY_W:(b + 1) * HY_W]
        r_ref[b] = jnp.where(first, pu * hr, 2.0 * (pu * hr + wu * hi)).astype(bf16)
        z_ref[b] = jnp.where(first, wu * qs, 2.0 * (wu * hr - pu * hi)).astype(bf16)


def _hy_fwd(mat, u, hb, batch, seq_len):
    tk, _ = _hy_tiles(seq_len)
    out = pl.BlockSpec((batch, tk, HY_W), lambda i: (0, i, 0))
    return pl.pallas_call(
        functools.partial(_hy_fwd_kernel, batch, tk),
        grid=(seq_len // tk,),
        in_specs=[pl.BlockSpec((2 * tk, seq_len), lambda i: (i, 0)),
                  pl.BlockSpec((batch, seq_len, HY_W), lambda i: (0, 0, 0)),
                  pl.BlockSpec((2, seq_len, HY_W), lambda i: (0, 0, 0))],
        out_specs=[out, out],
        out_shape=[jax.ShapeDtypeStruct((batch, seq_len, HY_W), bf16)] * 2,
        scratch_shapes=[pltpu.VMEM((seq_len, (batch + 2) * HY_W), bf16)],
        compiler_params=_params("arbitrary"),
        name="hy_fwd",
    )(mat, u, hb)


def _hy_inv_kernel(nb, seq_len, m_ref, r_ref, z_ref, u_ref, x0_ref, d_ref, o_ref, rhs_ref):
    i = pl.program_id(0)

    @pl.when(i == 0)
    def _():
        for b in range(nb):
            rhs_ref[:seq_len, b * HY_W:(b + 1) * HY_W] = r_ref[b]
            rhs_ref[seq_len:, b * HY_W:(b + 1) * HY_W] = z_ref[b]

    acc = jnp.dot(m_ref[...], rhs_ref[...], preferred_element_type=f32)
    d = d_ref[...]
    for b in range(nb):
        y = acc[:, b * HY_W:(b + 1) * HY_W] * (1.0 / (2 * seq_len)) + u_ref[b] * d
        o_ref[b] = x0_ref[b] * y


def _hy_inv(mat, r, zz, u, x0, d_bias, batch, seq_len):
    _, tm = _hy_tiles(seq_len)
    full = pl.BlockSpec((batch, seq_len, HY_W), lambda i: (0, 0, 0))
    row = pl.BlockSpec((batch, tm, HY_W), lambda i: (0, i, 0))
    return pl.pallas_call(
        functools.partial(_hy_inv_kernel, batch, seq_len),
        grid=(seq_len // tm,),
        in_specs=[pl.BlockSpec((tm, 2 * seq_len), lambda i: (i, 0)), full, full, row, row,
                  pl.BlockSpec((1, HY_W), lambda i: (0, 0))],
        out_specs=row,
        out_shape=jax.ShapeDtypeStruct((batch, seq_len, HY_W), f32),
        scratch_shapes=[pltpu.VMEM((2 * seq_len, batch * HY_W), bf16)],
        compiler_params=_params("arbitrary"),
        name="hy_inv",
    )(mat, r, zz, u, x0, d_bias.reshape(1, HY_W))


@functools.lru_cache(maxsize=None)
def _hy_consts_np(seq_len):
    n = 2 * seq_len
    k = np.arange(seq_len, dtype=np.int64)
    ang = (2.0 * np.pi / n) * ((k[:, None] * k[None, :]) % n).astype(np.float64)
    cosm = np.cos(ang)
    sinm = np.sin(ang)
    sinm[0, :] = np.where(k % 2 == 0, 1.0, -1.0)
    tk, _ = _hy_tiles(seq_len)
    fwd = np.concatenate([np.concatenate([cosm[i:i + tk], sinm[i:i + tk]], axis=0)
                          for i in range(0, seq_len, tk)], axis=0)
    inv = np.concatenate([cosm, sinm.T], axis=1)
    t = np.linspace(0.0, 1.0, seq_len, dtype=np.float32)[:, None]
    bands = (FILTER_EMB - 1) // 2
    fb = np.linspace(1e-4, bands - 1, bands, dtype=np.float32)[None, :]
    w = (np.float32(2.0 * math.pi) * np.arange(seq_len, dtype=np.float32)[:, None] / np.float32(seq_len)).astype(np.float32)
    feats = np.concatenate([t, np.cos(fb * w), -np.sin(fb * w)], axis=-1).astype(np.float32)
    feats = np.pad(feats, ((0, 0), (0, LANES - FILTER_EMB)))
    decay_hi = math.log(HY_DECAY_TARGET) / HY_DECAY_HI_PCT
    decay_lo = math.log(HY_DECAY_TARGET) / HY_DECAY_LO_PCT
    deltas = np.abs(np.linspace(decay_lo, decay_hi, HY_W, dtype=np.float32))
    window = np.exp(-t * deltas[None, :]).astype(np.float32)
    return fwd, inv, feats, window


def _hy_consts(seq_len):
    fwd, inv, feats, window = _hy_consts_np(seq_len)
    as_bf = lambda m: jnp.asarray(m, dtype=f32).astype(bf16)
    return {"fwd": as_bf(fwd), "inv": as_bf(inv), "feats": jnp.asarray(feats), "window": jnp.asarray(window)}


def _hyena(z, lp, consts, batch, seq_len):
    x0, u = _hy_prep(z, lp["hy_conv_w"], lp["hy_conv_b"], batch, seq_len)
    x0, u = x0.reshape(batch, seq_len, HY_W), u.reshape(batch, seq_len, HY_W)
    hb = _hy_filter(seq_len, consts, lp["hy_f_w1"], lp["hy_f_b1"], lp["hy_f_w2"], lp["hy_f_b2"],
                    lp["hy_f_w3"], lp["hy_freq"]).reshape(2, seq_len, HY_W)
    r, zz = _hy_fwd(consts["fwd"], u, hb, batch, seq_len)
    yh = _hy_inv(consts["inv"], r, zz, u, x0, lp["hy_bias"], batch, seq_len)
    return yh.reshape(batch * seq_len, HY_W)


def _gelu(x):
    return 0.5 * x * (1.0 + jnp.tanh(math.sqrt(2.0 / math.pi) * (x + 0.044715 * (x * x * x))))


def _gmlp_kernel(z_ref, ws_ref, b_ref, o_ref):
    g = _gelu(z_ref[...])
    bias = b_ref[...]
    for c in range(TM // GM_CHUNK):
        rows = slice(c * GM_CHUNK, (c + 1) * GM_CHUNK)
        u = g[rows, :GM_W]
        v = g[rows, GM_W:].astype(bf16)
        s = jnp.concatenate(
            [jnp.dot(ws_ref[h], v[:, h * GM_HEAD_DIM:(h + 1) * GM_HEAD_DIM], preferred_element_type=f32)
             for h in range(GM_HEADS)], axis=1)
        o_ref[rows, :] = u * (s + bias)


def _gmlp(z, ws_bf, gm_b):
    n = z.shape[0]
    bias = jnp.repeat(gm_b.T, GM_HEAD_DIM, axis=1)
    return pl.pallas_call(
        _gmlp_kernel,
        grid=(n // TM,),
        in_specs=[pl.BlockSpec((TM, 2 * GM_W), lambda i: (i, 3)),
                  pl.BlockSpec((GM_HEADS, GM_CHUNK, GM_CHUNK), lambda i: (0, 0, 0)),
                  pl.BlockSpec((GM_CHUNK, GM_W), lambda i: (0, 0))],
        out_specs=pl.BlockSpec((TM, GM_W), lambda i: (i, 0)),
        out_shape=jax.ShapeDtypeStruct((n, GM_W), f32),
        compiler_params=_params("parallel"),
        name="gmlp",
    )(z, ws_bf, bias)


def _out_kernel(a_ref, yh_ref, yg_ref, x_ref, mod_ref, og_ref, w_ref, g2_ref, rw_ref, x1_ref, h2_ref, aff_ref):
    og = og_ref[...]
    mixed_in = jnp.concatenate([
        _rms(a_ref[...]) * og[:, :ATTN_W],
        _rms(yh_ref[...]) * og[:, ATTN_W:ATTN_W + HY_W],
        _rms(yg_ref[...]) * og[:, ATTN_W + HY_W:]], axis=1)
    mixed = jnp.dot(mixed_in.astype(bf16), w_ref[...], preferred_element_type=f32)
    m = mod_ref[0]
    x1 = x_ref[...] + m[2:3] * mixed
    x1_ref[...] = x1
    h2 = _rms(x1) * g2_ref[...] * (1.0 + m[4:5]) + m[3:4]
    h2_ref[...] = h2
    logits = lax.dot_general(rw_ref[...], h2, (((1,), (1,)), ((), ())),
                             precision=lax.Precision.HIGHEST, preferred_element_type=f32)
    e = jnp.exp(logits - jnp.max(logits, axis=0, keepdims=True))
    aff_ref[...] = e / jnp.sum(e, axis=0, keepdims=True)


def _out_proj(a, yh, yg, x, mod, out_g, w_out_bf, g2, router_wt):
    n = x.shape[0]
    per_mod = n // mod.shape[0]
    row = lambda w: pl.BlockSpec((TM, w), lambda i: (i, 0))
    const = lambda s: pl.BlockSpec(s, lambda i: (0,) * len(s))
    return pl.pallas_call(
        _out_kernel,
        grid=(n // TM,),
        in_specs=[row(ATTN_W), row(HY_W), row(GM_W), row(D_MODEL),
                  pl.BlockSpec((1, 6, D_MODEL), lambda i: (i * TM // per_mod, 0, 0)),
                  const((1, MIX_W)), const((MIX_W, D_MODEL)), const((1, D_MODEL)), const((N_EXPERTS, D_MODEL))],
        out_specs=[row(D_MODEL), row(D_MODEL), pl.BlockSpec((N_EXPERTS, TM), lambda i: (0, i))],
        out_shape=[jax.ShapeDtypeStruct((n, D_MODEL), f32), jax.ShapeDtypeStruct((n, D_MODEL), f32),
                   jax.ShapeDtypeStruct((N_EXPERTS, n), f32)],
        compiler_params=_params("parallel"),
        name="out_proj",
    )(a, yh, yg, x, mod, out_g.reshape(1, MIX_W), w_out_bf, g2.reshape(1, D_MODEL), router_wt)


RT_CHUNK = 512


def _prefix_incl(x01, tri):
    n = x01.shape[1]
    carry = jnp.zeros((x01.shape[0], 1), f32)
    parts = []
    for c in range(n // RT_CHUNK):
        piece = x01[:, c * RT_CHUNK:(c + 1) * RT_CHUNK]
        parts.append(jnp.dot(piece.astype(bf16), tri, preferred_element_type=f32) + carry)
        carry = carry + jnp.sum(piece, axis=1, keepdims=True)
    return jnp.concatenate(parts, axis=1)


def _route_kernel(cap, aff_ref, idx_ref, gate_ref):
    aff = aff_ref[...]
    n = aff.shape[1]

    def step(it, t):
        cand = t | (jnp.int32(1) << (30 - it))
        cnt = jnp.sum(jnp.where(aff >= lax.bitcast_convert_type(cand, f32), 1.0, 0.0), axis=1, keepdims=True)
        return jnp.where(cnt >= cap, cand, t)

    thr_bits = lax.fori_loop(0, 31, step, jnp.zeros((aff.shape[0], 1), jnp.int32))
    thr = lax.bitcast_convert_type(thr_bits, f32)
    gt = jnp.where(aff > thr, 1.0, 0.0)
    eq = jnp.where(aff == thr, 1.0, 0.0)
    room = cap - jnp.sum(gt, axis=1, keepdims=True)
    r = lax.broadcasted_iota(jnp.int32, (RT_CHUNK, RT_CHUNK), 0)
    c = lax.broadcasted_iota(jnp.int32, (RT_CHUNK, RT_CHUNK), 1)
    tri = jnp.where(r <= c, 1.0, 0.0).astype(bf16)
    sel = jnp.maximum(gt, jnp.where(_prefix_incl(eq, tri) <= room, eq, 0.0))
    slot = _prefix_incl(sel, tri) - 1.0

    tok = lax.broadcasted_iota(jnp.int32, aff.shape, 1)
    dist = jnp.where(sel > 0.0, tok - slot.astype(jnp.int32), 0)
    g = aff
    for b in range(max(1, (n - 1).bit_length())):
        sh = 1 << b
        dist_s = pltpu.roll(dist, n - sh, axis=1)
        take = (dist_s & sh) != 0
        leave = (dist & sh) != 0
        tok = jnp.where(take, pltpu.roll(tok, n - sh, axis=1), tok)
        g = jnp.where(take, pltpu.roll(g, n - sh, axis=1), g)
        dist = jnp.where(take, dist_s, jnp.where(leave, 0, dist))
    idx_ref[...] = tok[:, :cap]
    gate_ref[...] = g[:, :cap]


def _route(aff_t, cap):
    return pl.pallas_call(
        functools.partial(_route_kernel, cap),
        in_specs=[pl.BlockSpec(memory_space=pltpu.VMEM)],
        out_specs=[pl.BlockSpec(memory_space=pltpu.VMEM)] * 2,
        out_shape=[jax.ShapeDtypeStruct((N_EXPERTS, cap), jnp.int32), jax.ShapeDtypeStruct((N_EXPERTS, cap), f32)],
        compiler_params=pltpu.CompilerParams(vmem_limit_bytes=VMEM_LIMIT),
        name="route",
    )(aff_t)


GATHER_UNROLL = 8
SCATTER_UNROLL = 4


def _moe_ffn_kernel(cap, stride, idx_ref, h_ref, g_ref, w1_ref, w3_ref, w2_ref, y_ref, tile_ref, xb_ref, acc_ref):
    base = pl.program_id(0) * cap

    def gather(q, carry):
        for t in range(GATHER_UNROLL):
            r = q * GATHER_UNROLL + t
            tile_ref[pl.ds(r, ROW_CHUNKS, stride=stride), :] = h_ref[idx_ref[base + r]]
        return carry
    lax.fori_loop(0, cap // GATHER_UNROLL, gather, 0)
    xb_ref[...] = jnp.concatenate(
        [tile_ref[j * stride:j * stride + cap, :] for j in range(ROW_CHUNKS)], axis=1).astype(bf16)

    xb = xb_ref[...]
    for f in range(D_EXPERT // MOE_F_TILE):
        cols = slice(f * MOE_F_TILE, (f + 1) * MOE_F_TILE)
        a = jnp.dot(xb, w1_ref[0, 0, :, cols].astype(bf16), preferred_element_type=f32)
        b = jnp.dot(xb, w3_ref[0, 0, :, cols].astype(bf16), preferred_element_type=f32)
        he = (a * _sigmoid(a) * b).astype(bf16)
        part = jnp.dot(he, w2_ref[0, 0, cols, :].astype(bf16), preferred_element_type=f32)
        if f == 0:
            acc_ref[...] = part
        else:
            acc_ref[...] += part

    y = acc_ref[...] * g_ref[0]
    for j in range(ROW_CHUNKS):
        y_ref[0, j * stride:j * stride + cap, :] = y[:, j * LANES:(j + 1) * LANES]
        y_ref[0, j * stride + cap:(j + 1) * stride, :] = jnp.zeros((stride - cap, LANES), f32)


def _moe_ffn(layer, h3, idx, gate, w1, w3, w2):
    cap = idx.shape[1]
    stride = cap + SUBLANES
    grid_spec = pltpu.PrefetchScalarGridSpec(
        num_scalar_prefetch=1,
        grid=(N_EXPERTS,),
        in_specs=[
            pl.BlockSpec(memory_space=pltpu.VMEM),
            pl.BlockSpec((1, cap, 1), lambda e, idx: (e, 0, 0)),
            pl.BlockSpec((1, 1, D_MODEL, D_EXPERT), lambda e, idx: (layer, e, 0, 0)),
            pl.BlockSpec((1, 1, D_MODEL, D_EXPERT), lambda e, idx: (layer, e, 0, 0)),
            pl.BlockSpec((1, 1, D_EXPERT, D_MODEL), lambda e, idx: (layer, e, 0, 0)),
        ],
        out_specs=pl.BlockSpec((1, ROW_CHUNKS * stride, LANES), lambda e, idx: (e, 0, 0)),
        scratch_shapes=[pltpu.VMEM((ROW_CHUNKS * stride, LANES), f32),
                        pltpu.VMEM((cap, D_MODEL), bf16),
                        pltpu.VMEM((cap, D_MODEL), f32)],
    )
    return pl.pallas_call(
        functools.partial(_moe_ffn_kernel, cap, stride),
        grid_spec=grid_spec,
        out_shape=jax.ShapeDtypeStruct((N_EXPERTS, ROW_CHUNKS * stride, LANES), f32),
        compiler_params=_params("arbitrary"),
        name="moe_ffn",
    )(idx.reshape(-1), h3, gate.reshape(N_EXPERTS, cap, 1), w1, w3, w2)


def _moe_combine_kernel(cap, stride, idx_ref, y_ref, o_ref):
    e = pl.program_id(0)
    base = e * cap

    @pl.when(e == 0)
    def _():
        o_ref[...] = jnp.zeros_like(o_ref)

    def scatter(q, carry):
        rows = [idx_ref[base + q * SCATTER_UNROLL + t] for t in range(SCATTER_UNROLL)]
        new = [o_ref[rows[t]] + y_ref[0, pl.ds(q * SCATTER_UNROLL + t, ROW_CHUNKS, stride=stride), :]
               for t in range(SCATTER_UNROLL)]
        for t in range(SCATTER_UNROLL):
            o_ref[rows[t]] = new[t]
        return carry
    lax.fori_loop(0, cap // SCATTER_UNROLL, scatter, 0)


def _moe_combine(n, idx, y_cm):
    cap = idx.shape[1]
    stride = y_cm.shape[1] // ROW_CHUNKS
    grid_spec = pltpu.PrefetchScalarGridSpec(
        num_scalar_prefetch=1,
        grid=(N_EXPERTS,),
        in_specs=[pl.BlockSpec((1, ROW_CHUNKS * stride, LANES), lambda e, idx: (e, 0, 0))],
        out_specs=pl.BlockSpec(memory_space=pltpu.VMEM),
    )
    return pl.pallas_call(
        functools.partial(_moe_combine_kernel, cap, stride),
        grid_spec=grid_spec,
        out_shape=jax.ShapeDtypeStruct((n, ROW_CHUNKS, LANES), f32),
        compiler_params=_params("arbitrary"),
        name="moe_combine",
    )(idx.reshape(-1), y_cm)


def _expert_choice(layer, h2, aff_t, w1, w3, w2):
    n = h2.shape[0]
    cap = max(1, EC_CAPACITY * n // N_EXPERTS)
    idx, gate = _route(aff_t, cap)
    y_cm = _moe_ffn(layer, h2.reshape(n, ROW_CHUNKS, LANES), idx, gate, w1, w3, w2)
    return _moe_combine(n, idx, y_cm).reshape(n, D_MODEL)


def _final_kernel(x_ref, moe_ref, mod_ref, g_ref, o_ref):
    x = x_ref[...] + mod_ref[0][5:6] * moe_ref[...]
    o_ref[...] = _rms(x) * g_ref[...]


def _final(x, moe, mod, g):
    n = x.shape[0]
    per_mod = n // mod.shape[0]
    row = pl.BlockSpec((TM, D_MODEL), lambda i: (i, 0))
    return pl.pallas_call(
        _final_kernel,
        grid=(n // TM,),
        in_specs=[row, row, pl.BlockSpec((1, 6, D_MODEL), lambda i: (i * TM // per_mod, 0, 0)),
                  pl.BlockSpec((1, D_MODEL), lambda i: (0, 0))],
        out_specs=row,
        out_shape=jax.ShapeDtypeStruct((n, D_MODEL), f32),
        compiler_params=_params("parallel"),
        name="final_norm",
    )(x, moe, mod, g.reshape(1, D_MODEL))


def _stream(x, mods, layers, experts, batch, seq_len, final_g, caches=None):
    consts = _hy_consts(seq_len)
    rope_tabs = _rope_tables(seq_len) if caches is not None else None
    moe = None
    kvs = []
    for l, lp in enumerate(layers):
        z, x = _in_proj(x, mods[l], lp["norm1_g"], lp["w_in"], seq_len, moe=moe,
                        modp=mods[l - 1] if l else None, rope_tabs=rope_tabs)
        if caches is None:
            a = _ctx_attention(z, lp["attn_sink"], batch, seq_len)
            kvs.append((z[:, ATTN_W:ATTN_W + KV_W], z[:, ATTN_W + KV_W:QKV_W]))
        else:
            a = _lat_attention(z, lp["attn_sink"], caches[0][:, l], caches[1][:, l], batch, seq_len)
        yh = _hyena(z, lp, consts, batch, seq_len)
        yg = _gmlp(z, lp["gm_ws"], lp["gm_b"])
        x, h2, aff_t = _out_proj(a, yh, yg, x, mods[l], lp["out_norm_g"], lp["w_out"], lp["norm2_g"], lp["router_wt"])
        moe = _expert_choice(l, h2, aff_t, *experts)
    y = _final(x, moe, mods[-1], final_g)
    return y, kvs


def kernel(x_prompt, x_sample, c, cache_k, cache_v, c_ctx, norm1_g, norm2_g, ada_w, ada_b, w_in, attn_sink,
           hy_conv_w, hy_conv_b, hy_f_w1, hy_f_b1, hy_f_w2, hy_f_b2, hy_f_w3, hy_freq, hy_bias, gm_ws, gm_b,
           out_norm_g, w_out, router_w, exp_w1, exp_w3, exp_w2, final_g):
    batch, seq, _ = x_prompt.shape
    dbatch, dseq, _ = x_sample.shape
    past = cache_k.shape[2]

    cvec = jnp.zeros((SUBLANES, D_MODEL), f32).at[0].set(c_ctx).at[1:1 + dbatch].set(c)
    mod = _ada(cvec, ada_w, ada_b)
    mods_ctx = [mod[l, 0:1].reshape(1, 6, D_MODEL) for l in range(DEPTH)]
    mods_lat = [mod[l, 1:1 + dbatch].reshape(dbatch, 6, D_MODEL) for l in range(DEPTH)]

    w_in_bf, w_out_bf, gm_ws_bf = w_in.astype(bf16), w_out.astype(bf16), gm_ws.astype(bf16)
    layers = []
    for l in range(DEPTH):
        layers.append({
            "norm1_g": norm1_g[l], "norm2_g": norm2_g[l], "w_in": w_in_bf[l], "attn_sink": attn_sink[l],
            "hy_conv_w": hy_conv_w[l], "hy_conv_b": hy_conv_b[l], "hy_f_w1": hy_f_w1[l], "hy_f_b1": hy_f_b1[l],
            "hy_f_w2": hy_f_w2[l], "hy_f_b2": hy_f_b2[l], "hy_f_w3": hy_f_w3[l], "hy_freq": hy_freq[l],
            "hy_bias": hy_bias[l], "gm_ws": gm_ws_bf[l], "gm_b": gm_b[l], "out_norm_g": out_norm_g[l],
            "w_out": w_out_bf[l], "router_wt": router_w[l].T})
    experts = (exp_w1, exp_w3, exp_w2)

    yp, kvs = _stream(x_prompt.reshape(batch * seq, D_MODEL), mods_ctx, layers, experts, batch, seq, final_g)
    caches = (cache_k.reshape(dbatch, DEPTH, past, KV_W), cache_v.reshape(dbatch, DEPTH, past, KV_W))
    ys, _ = _stream(x_sample.reshape(dbatch * dseq, D_MODEL), mods_lat, layers, experts, dbatch, dseq, final_g,
                   caches=caches)

    new_k = jnp.stack([k.reshape(batch, seq, N_KV, HEAD_DIM) for k, _ in kvs], axis=1)
    new_v = jnp.stack([v.reshape(batch, seq, N_KV, HEAD_DIM) for _, v in kvs], axis=1)
    return (yp.reshape(batch, seq, D_MODEL), ys.reshape(dbatch, dseq, D_MODEL), new_k, new_v)
```

```python
import functools
import math

import numpy as np
import jax
import jax.numpy as jnp
from jax import lax
from jax.experimental import pallas as pl
from jax.experimental.pallas import tpu as pltpu

f32 = jnp.float32
bf16 = jnp.bfloat16

D_MODEL = 1024
DEPTH = 2
GRID_W = 64
BLOCK = 128
N_HEADS = 8
N_KV = 2
HEAD_DIM = 64
Q_GROUP = N_HEADS // N_KV
ATTN_W = N_HEADS * HEAD_DIM
KV_W = N_KV * HEAD_DIM
QKV_W = ATTN_W + 2 * KV_W
HY_W = 256
GM_W = 256
GM_HEADS = 4
GM_HEAD_DIM = GM_W // GM_HEADS
GM_CHUNK = 128
MIX_W = ATTN_W + HY_W + GM_W
IN_W = ATTN_W + 2 * KV_W + 3 * HY_W + 2 * GM_W
FILTER_EMB = 33
FILTER_HID = 64
HY_DECAY_HI_PCT = 0.3
HY_DECAY_LO_PCT = 1.5
HY_DECAY_TARGET = 1e-2
N_EXPERTS = 16
EC_CAPACITY = 2
D_EXPERT = 1024
ROPE_THETA = 10000.0
EPS = 1e-6
NEG = -1e30

LANES = 128
SUBLANES = 8
ROW_CHUNKS = D_MODEL // LANES
VMEM_LIMIT = 56 * 1024 * 1024
TM = 512
MOE_F_TILE = 256


def _params(*sem):
    return pltpu.CompilerParams(dimension_semantics=sem, vmem_limit_bytes=VMEM_LIMIT)


def _rms(x):
    return x * lax.rsqrt(jnp.mean(x * x, axis=-1, keepdims=True) + EPS)


def _sigmoid(x):
    return 1.0 / (1.0 + jnp.exp(-x))


def _ada_kernel(c_ref, w_ref, b_ref, o_ref):
    c = c_ref[...]
    s = (c * _sigmoid(c)).astype(bf16)
    o_ref[0] = jnp.dot(s, w_ref[0].astype(bf16), preferred_element_type=f32) + b_ref[0]


def _ada(cvec, ada_w, ada_b):
    nt = 6
    return pl.pallas_call(
        _ada_kernel,
        grid=(DEPTH, nt),
        in_specs=[
            pl.BlockSpec((SUBLANES, D_MODEL), lambda l, j: (0, 0)),
            pl.BlockSpec((1, D_MODEL, D_MODEL), lambda l, j: (l, 0, j)),
            pl.BlockSpec((1, 1, D_MODEL), lambda l, j: (l, 0, j)),
        ],
        out_specs=pl.BlockSpec((1, SUBLANES, D_MODEL), lambda l, j: (l, 0, j)),
        out_shape=jax.ShapeDtypeStruct((DEPTH, SUBLANES, 6 * D_MODEL), f32),
        compiler_params=_params("arbitrary", "arbitrary"),
        name="ada",
    )(cvec, ada_w, ada_b.reshape(DEPTH, 1, 6 * D_MODEL))


def _rope_swap(x):
    w = x.shape[-1]
    lane = lax.broadcasted_iota(jnp.int32, x.shape, 1)
    first = (lane % 32) < 16
    return jnp.where(first, pltpu.roll(x, w - 16, axis=1), pltpu.roll(x, 16, axis=1))


def _in_kernel(rope, x_ref, mod_ref, g_ref, w_ref, *refs):
    if rope:
        cos_ref, sin_ref, z_ref = refs
    else:
        (z_ref,) = refs
    m = mod_ref[0]
    h = _rms(x_ref[...]) * g_ref[...] * (1.0 + m[1:2]) + m[0:1]
    z = jnp.dot(h.astype(bf16), w_ref[...], preferred_element_type=f32)
    if rope:
        reps = (ATTN_W + KV_W) // LANES
        cos = jnp.concatenate([cos_ref[...]] * reps, axis=1)
        sin = jnp.concatenate([sin_ref[...]] * reps, axis=1)
        qk = z[:, :ATTN_W + KV_W]
        z_ref[:, :ATTN_W + KV_W] = qk * cos + _rope_swap(qk) * sin
        z_ref[:, ATTN_W + KV_W:] = z[:, ATTN_W + KV_W:]
    else:
        z_ref[...] = z


def _in_proj(x, mod, g1, w_in_bf, seq_len, rope_tabs=None):
    n = x.shape[0]
    rope = rope_tabs is not None
    per_mod = n // mod.shape[0]
    in_specs = [pl.BlockSpec((TM, D_MODEL), lambda i: (i, 0)),
                pl.BlockSpec((1, 6, D_MODEL), lambda i: (i * TM // per_mod, 0, 0)),
                pl.BlockSpec((1, D_MODEL), lambda i: (0, 0)),
                pl.BlockSpec((D_MODEL, IN_W), lambda i: (0, 0))]
    args = [x, mod, g1.reshape(1, D_MODEL), w_in_bf]
    if rope:
        nt = seq_len // TM
        tab = pl.BlockSpec((TM, LANES), lambda i: (i % nt, 0))
        in_specs += [tab, tab]
        args += list(rope_tabs)
    return pl.pallas_call(
        functools.partial(_in_kernel, rope),
        grid=(n // TM,),
        in_specs=in_specs,
        out_specs=pl.BlockSpec((TM, IN_W), lambda i: (i, 0)),
        out_shape=jax.ShapeDtypeStruct((n, IN_W), f32),
        compiler_params=_params("parallel"),
        name="in_proj",
    )(*args)


def _rope_tables(seq_len):
    nf = HEAD_DIM // 4
    t = np.arange(seq_len)
    inv = (ROPE_THETA ** (-np.arange(nf, dtype=np.float32) / nf)).astype(np.float32)
    d = np.arange(HEAD_DIM)
    pos = np.where((d // 32)[None, :] == 0, (t // GRID_W)[:, None], (t % GRID_W)[:, None]).astype(np.float32)
    ang = (pos * inv[d % nf][None, :]).astype(np.float32)
    cos = np.cos(ang).astype(np.float32)
    sin = np.sin(ang).astype(np.float32) * np.where((d % 32) < 16, -1.0, 1.0)[None, :].astype(np.float32)
    return jnp.asarray(np.tile(cos, (1, 2))), jnp.asarray(np.tile(sin, (1, 2)))


LOG2E = math.log2(math.e)


def _stack_queries(z, sink_ref, kv):
    rows = z.shape[0]
    heads = range(kv * Q_GROUP, (kv + 1) * Q_GROUP)
    qs = jnp.concatenate([z[:, h * HEAD_DIM:(h + 1) * HEAD_DIM] for h in heads], axis=0)
    sink = jnp.concatenate([jnp.full((1, rows), sink_ref[h] * LOG2E, f32) for h in heads], axis=1)
    return (qs * (LOG2E / math.sqrt(HEAD_DIM))).astype(bf16), sink


def _scores_t(k, qs_bf):
    return lax.dot_general(k.astype(bf16), qs_bf, (((1,), (1,)), ((), ())), preferred_element_type=f32)


def _softmax_pv_t(st, sink, vals, rows):
    m = jnp.maximum(jnp.max(st, axis=0, keepdims=True), sink)
    pt = jnp.exp2(st - m).astype(bf16)
    v_bf = vals.astype(bf16)
    v_ext = jnp.concatenate([v_bf, jnp.ones_like(v_bf)], axis=1)
    ovt = lax.dot_general(v_ext, pt, (((0,), (0,)), ((), ())), preferred_element_type=f32)
    ot = ovt[:HEAD_DIM] / (ovt[HEAD_DIM:HEAD_DIM + 1] + jnp.exp2(sink - m))
    return [ot[:, g * rows:(g + 1) * rows].T for g in range(Q_GROUP)]


def _ctx_attn_kernel(sink_ref, z_ref, o_ref):
    z = z_ref[...]
    outs = []
    for kv in range(N_KV):
        k = z[:, ATTN_W + kv * HEAD_DIM:ATTN_W + (kv + 1) * HEAD_DIM]
        v = z[:, ATTN_W + KV_W + kv * HEAD_DIM:ATTN_W + KV_W + (kv + 1) * HEAD_DIM]
        qs, sink = _stack_queries(z, sink_ref, kv)
        outs += _softmax_pv_t(_scores_t(k, qs), sink, v, z.shape[0])
    o_ref[...] = jnp.concatenate(outs, axis=1)


def _ctx_attention(z, sink, batch, seq_len):
    return pl.pallas_call(
        _ctx_attn_kernel,
        grid=(batch,),
        in_specs=[pl.BlockSpec(memory_space=pltpu.SMEM),
                  pl.BlockSpec((seq_len, QKV_W), lambda b: (b, 0))],
        out_specs=pl.BlockSpec((seq_len, ATTN_W), lambda b: (b, 0)),
        out_shape=jax.ShapeDtypeStruct((batch * seq_len, ATTN_W), f32),
        compiler_params=_params("parallel"),
        name="ctx_attn",
    )(sink, z)


def _lat_attn_kernel(nb, sink_ref, zp_ref, zc_ref, zn_ref, ck_ref, cv_ref, o_ref):
    i = pl.program_id(1)
    zp, zc, zn = zp_ref[...], zc_ref[...], zn_ref[...]
    ck, cv = ck_ref[0], cv_ref[0]
    width = Q_GROUP * BLOCK
    j = lax.broadcasted_iota(jnp.int32, (BLOCK, width), 0)
    r = lax.broadcasted_iota(jnp.int32, (BLOCK, width), 1) % BLOCK
    ok_prev = j >= r + jnp.where(i > 0, 0, BLOCK)
    ok_next = j <= r - jnp.where(i < nb - 1, 0, BLOCK)
    outs = []
    for kv in range(N_KV):
        ks = slice(ATTN_W + kv * HEAD_DIM, ATTN_W + (kv + 1) * HEAD_DIM)
        vs = slice(ATTN_W + KV_W + kv * HEAD_DIM, ATTN_W + KV_W + (kv + 1) * HEAD_DIM)
        cs = slice(kv * HEAD_DIM, (kv + 1) * HEAD_DIM)
        qs, sink = _stack_queries(zc, sink_ref, kv)
        st = jnp.concatenate([
            jnp.where(ok_prev, _scores_t(zp[:, ks], qs), NEG),
            _scores_t(zc[:, ks], qs),
            jnp.where(ok_next, _scores_t(zn[:, ks], qs), NEG),
            _scores_t(ck[:, cs], qs)], axis=0)
        vals = jnp.concatenate([zp[:, vs], zc[:, vs], zn[:, vs], cv[:, cs]], axis=0)
        outs += _softmax_pv_t(st, sink, vals, BLOCK)
    o_ref[...] = jnp.concatenate(outs, axis=1)


def _lat_attention(z, sink, ck, cv, batch, seq_len):
    nb = seq_len // BLOCK
    blk = lambda f: pl.BlockSpec((BLOCK, QKV_W), f)
    past = ck.shape[1]
    cache = pl.BlockSpec((1, past, KV_W), lambda b, i: (b, 0, 0))
    return pl.pallas_call(
        functools.partial(_lat_attn_kernel, nb),
        grid=(batch, nb),
        in_specs=[pl.BlockSpec(memory_space=pltpu.SMEM),
                  blk(lambda b, i: (b * nb + jnp.maximum(i - 1, 0), 0)),
                  blk(lambda b, i: (b * nb + i, 0)),
                  blk(lambda b, i: (b * nb + jnp.minimum(i + 1, nb - 1), 0)),
                  cache, cache],
        out_specs=pl.BlockSpec((BLOCK, ATTN_W), lambda b, i: (b * nb + i, 0)),
        out_shape=jax.ShapeDtypeStruct((batch * seq_len, ATTN_W), f32),
        compiler_params=_params("parallel", "parallel"),
        name="lat_attn",
    )(sink, z, z, z, ck, cv)


HY_TL = 256


def _hy_prep_kernel(nt, zp_ref, zc_ref, zn_ref, w_ref, b_ref, x0_ref, u_ref):
    i = pl.program_id(1)
    z = zc_ref[...]
    row = lax.broadcasted_iota(jnp.int32, z.shape, 0)
    prev_row = jnp.where(i > 0, zp_ref[SUBLANES - 1:SUBLANES, :], 0.0)
    next_row = jnp.where(i < nt - 1, zn_ref[0:1, :], 0.0)
    z_prev = jnp.where(row == 0, prev_row, pltpu.roll(z, 1, axis=0))
    z_next = jnp.where(row == HY_TL - 1, next_row, pltpu.roll(z, HY_TL - 1, axis=0))
    w = w_ref[...]
    zc = z_prev * w[0:1] + z * w[1:2] + z_next * w[2:3] + b_ref[...]
    x0_ref[...] = zc[:, :HY_W]
    u_ref[...] = zc[:, HY_W:2 * HY_W] * zc[:, 2 * HY_W:]


def _hy_prep(z, conv_w, conv_b, batch, seq_len):
    n = batch * seq_len
    nt = seq_len // HY_TL
    r8 = HY_TL // SUBLANES
    last8 = n // SUBLANES - 1
    return pl.pallas_call(
        functools.partial(_hy_prep_kernel, nt),
        grid=(batch, nt),
        in_specs=[
            pl.BlockSpec((SUBLANES, 3 * HY_W), lambda b, i: (jnp.maximum((b * nt + i) * r8 - 1, 0), 1)),
            pl.BlockSpec((HY_TL, 3 * HY_W), lambda b, i: (b * nt + i, 1)),
            pl.BlockSpec((SUBLANES, 3 * HY_W), lambda b, i: (jnp.minimum((b * nt + i + 1) * r8, last8), 1)),
            pl.BlockSpec((3, 3 * HY_W), lambda b, i: (0, 0)),
            pl.BlockSpec((1, 3 * HY_W), lambda b, i: (0, 0)),
        ],
        out_specs=[pl.BlockSpec((HY_TL, HY_W), lambda b, i: (b * nt + i, 0))] * 2,
        out_shape=[jax.ShapeDtypeStruct((n, HY_W), f32)] * 2,
        compiler_params=_params("parallel", "parallel"),
        name="hy_prep",
    )(z, z, z, conv_w, conv_b.reshape(1, 3 * HY_W))


def _hy_filter_kernel(seq_len, feats_ref, win_ref, w1_ref, b1_ref, w2_ref, b2_ref, w3_ref, fr_ref, o_ref):
    hi = lax.Precision.HIGHEST
    fr = fr_ref[...]
    h = jnp.sin(fr * (jnp.dot(feats_ref[...], w1_ref[...], precision=hi, preferred_element_type=f32) + b1_ref[...]))
    h = jnp.sin(fr * (jnp.dot(h, w2_ref[...], precision=hi, preferred_element_type=f32) + b2_ref[...]))
    h = jnp.dot(h, w3_ref[...], precision=hi, preferred_element_type=f32)
    win = win_ref[...]
    row = lax.broadcasted_iota(jnp.int32, (seq_len, HY_W), 0)
    o_ref[0:seq_len, :] = h[:, :HY_W] * win
    o_ref[seq_len:, :] = jnp.where(row == 0, 0.0, h[:, HY_W:] * win)


def _hy_filter(seq_len, consts, w1, b1, w2, b2, w3, freq):
    w1p = jnp.zeros((LANES, FILTER_HID), f32).at[:FILTER_EMB].set(w1)
    args = (consts["feats"], consts["window"], w1p, b1.reshape(1, -1), w2, b2.reshape(1, -1), w3, freq.reshape(1, -1))
    return pl.pallas_call(
        functools.partial(_hy_filter_kernel, seq_len),
        in_specs=[pl.BlockSpec(memory_space=pltpu.VMEM)] * len(args),
        out_specs=pl.BlockSpec(memory_space=pltpu.VMEM),
        out_shape=jax.ShapeDtypeStruct((2 * seq_len, HY_W), f32),
        compiler_params=pltpu.CompilerParams(vmem_limit_bytes=VMEM_LIMIT),
        name="hy_filter",
    )(*args)


def _hy_tiles(seq_len):
    return min(TM, seq_len // 2), min(TM, seq_len)


def _hy_fwd_kernel(nb, tk, m_ref, u_ref, hb_ref, r_ref, z_ref, rhs_ref):
    i = pl.program_id(0)

    @pl.when(i == 0)
    def _():
        for b in range(nb):
            rhs_ref[:, b * HY_W:(b + 1) * HY_W] = u_ref[b].astype(bf16)
        rhs_ref[:, nb * HY_W:(nb + 1) * HY_W] = hb_ref[0].astype(bf16)
        rhs_ref[:, (nb + 1) * HY_W:] = hb_ref[1].astype(bf16)

    acc = jnp.dot(m_ref[...], rhs_ref[...], preferred_element_type=f32)
    p, w = acc[:tk], acc[tk:]
    c0 = nb * HY_W
    pf, pb = p[:, c0:c0 + HY_W], p[:, c0 + HY_W:]
    qf, qb = w[:, c0:c0 + HY_W], w[:, c0 + HY_W:]
    hr = pf + pb
    qs = qf + qb
    hi = qb - qf
    first = (lax.broadcasted_iota(jnp.int32, (tk, HY_W), 0) == 0) & (i == 0)
    for b in range(nb):
        pu, wu = p[:, b * HY_W:(b + 1) * HY_W], w[:, b * HY_W:(b + 1) * HY_W]
        r_ref[b] = jnp.where(first, pu * hr, 2.0 * (pu * hr + wu * hi)).astype(bf16)
        z_ref[b] = jnp.where(first, wu * qs, 2.0 * (wu * hr - pu * hi)).astype(bf16)


def _hy_fwd(mat, u, hb, batch, seq_len):
    tk, _ = _hy_tiles(seq_len)
    out = pl.BlockSpec((batch, tk, HY_W), lambda i: (0, i, 0))
    return pl.pallas_call(
        functools.partial(_hy_fwd_kernel, batch, tk),
        grid=(seq_len // tk,),
        in_specs=[pl.BlockSpec((2 * tk, seq_len), lambda i: (i, 0)),
                  pl.BlockSpec((batch, seq_len, HY_W), lambda i: (0, 0, 0)),
                  pl.BlockSpec((2, seq_len, HY_W), lambda i: (0, 0, 0))],
        out_specs=[out, out],
        out_shape=[jax.ShapeDtypeStruct((batch, seq_len, HY_W), bf16)] * 2,
        scratch_shapes=[pltpu.VMEM((seq_len, (batch + 2) * HY_W), bf16)],
        compiler_params=_params("arbitrary"),
        name="hy_fwd",
    )(mat, u, hb)


def _hy_inv_kernel(nb, seq_len, m_ref, r_ref, z_ref, u_ref, x0_ref, d_ref, o_ref, rhs_ref):
    i = pl.program_id(0)

    @pl.when(i == 0)
    def _():
        for b in range(nb):
            rhs_ref[:seq_len, b * HY_W:(b + 1) * HY_W] = r_ref[b]
            rhs_ref[seq_len:, b * HY_W:(b + 1) * HY_W] = z_ref[b]

    acc = jnp.dot(m_ref[...], rhs_ref[...], preferred_element_type=f32)
    d = d_ref[...]
    for b in range(nb):
        y = acc[:, b * HY_W:(b + 1) * HY_W] * (1.0 / (2 * seq_len)) + u_ref[b] * d
        o_ref[b] = x0_ref[b] * y


def _hy_inv(mat, r, zz, u, x0, d_bias, batch, seq_len):
    _, tm = _hy_tiles(seq_len)
    full = pl.BlockSpec((batch, seq_len, HY_W), lambda i: (0, 0, 0))
    row = pl.BlockSpec((batch, tm, HY_W), lambda i: (0, i, 0))
    return pl.pallas_call(
        functools.partial(_hy_inv_kernel, batch, seq_len),
        grid=(seq_len // tm,),
        in_specs=[pl.BlockSpec((tm, 2 * seq_len), lambda i: (i, 0)), full, full, row, row,
                  pl.BlockSpec((1, HY_W), lambda i: (0, 0))],
        out_specs=row,
        out_shape=jax.ShapeDtypeStruct((batch, seq_len, HY_W), f32),
        scratch_shapes=[pltpu.VMEM((2 * seq_len, batch * HY_W), bf16)],
        compiler_params=_params("arbitrary"),
        name="hy_inv",
    )(mat, r, zz, u, x0, d_bias.reshape(1, HY_W))


@functools.lru_cache(maxsize=None)
def _hy_consts_np(seq_len):
    n = 2 * seq_len
    k = np.arange(seq_len, dtype=np.int64)
    ang = (2.0 * np.pi / n) * ((k[:, None] * k[None, :]) % n).astype(np.float64)
    cosm = np.cos(ang)
    sinm = np.sin(ang)
    sinm[0, :] = np.where(k % 2 == 0, 1.0, -1.0)
    tk, _ = _hy_tiles(seq_len)
    fwd = np.concatenate([np.concatenate([cosm[i:i + tk], sinm[i:i + tk]], axis=0)
                          for i in range(0, seq_len, tk)], axis=0)
    inv = np.concatenate([cosm, sinm.T], axis=1)
    t = np.linspace(0.0, 1.0, seq_len, dtype=np.float32)[:, None]
    bands = (FILTER_EMB - 1) // 2
    fb = np.linspace(1e-4, bands - 1, bands, dtype=np.float32)[None, :]
    w = (np.float32(2.0 * math.pi) * np.arange(seq_len, dtype=np.float32)[:, None] / np.float32(seq_len)).astype(np.float32)
    feats = np.concatenate([t, np.cos(fb * w), -np.sin(fb * w)], axis=-1).astype(np.float32)
    feats = np.pad(feats, ((0, 0), (0, LANES - FILTER_EMB)))
    decay_hi = math.log(HY_DECAY_TARGET) / HY_DECAY_HI_PCT
    decay_lo = math.log(HY_DECAY_TARGET) / HY_DECAY_LO_PCT
    deltas = np.abs(np.linspace(decay_lo, decay_hi, HY_W, dtype=np.float32))
    window = np.exp(-t * deltas[None, :]).astype(np.float32)
    return fwd, inv, feats, window


def _hy_consts(seq_len):
    fwd, inv, feats, window = _hy_consts_np(seq_len)
    as_bf = lambda m: jnp.asarray(m, dtype=f32).astype(bf16)
    return {"fwd": as_bf(fwd), "inv": as_bf(inv), "feats": jnp.asarray(feats), "window": jnp.asarray(window)}


def _hyena(z, lp, consts, batch, seq_len):
    x0, u = _hy_prep(z, lp["hy_conv_w"], lp["hy_conv_b"], batch, seq_len)
    x0, u = x0.reshape(batch, seq_len, HY_W), u.reshape(batch, seq_len, HY_W)
    hb = _hy_filter(seq_len, consts, lp["hy_f_w1"], lp["hy_f_b1"], lp["hy_f_w2"], lp["hy_f_b2"],
                    lp["hy_f_w3"], lp["hy_freq"]).reshape(2, seq_len, HY_W)
    r, zz = _hy_fwd(consts["fwd"], u, hb, batch, seq_len)
    yh = _hy_inv(consts["inv"], r, zz, u, x0, lp["hy_bias"], batch, seq_len)
    return yh.reshape(batch * seq_len, HY_W)


def _gelu(x):
    return 0.5 * x * (1.0 + jnp.tanh(math.sqrt(2.0 / math.pi) * (x + 0.044715 * (x * x * x))))


def _gmlp_kernel(z_ref, ws_ref, b_ref, o_ref):
    g = _gelu(z_ref[...])
    bias = b_ref[...]
    for c in range(TM // GM_CHUNK):
        rows = slice(c * GM_CHUNK, (c + 1) * GM_CHUNK)
        u = g[rows, :GM_W]
        v = g[rows, GM_W:].astype(bf16)
        s = jnp.concatenate(
            [jnp.dot(ws_ref[h], v[:, h * GM_HEAD_DIM:(h + 1) * GM_HEAD_DIM], preferred_element_type=f32)
             for h in range(GM_HEADS)], axis=1)
        o_ref[rows, :] = u * (s + bias)


def _gmlp(z, ws_bf, gm_b):
    n = z.shape[0]
    bias = jnp.repeat(gm_b.T, GM_HEAD_DIM, axis=1)
    return pl.pallas_call(
        _gmlp_kernel,
        grid=(n // TM,),
        in_specs=[pl.BlockSpec((TM, 2 * GM_W), lambda i: (i, 3)),
                  pl.BlockSpec((GM_HEADS, GM_CHUNK, GM_CHUNK), lambda i: (0, 0, 0)),
                  pl.BlockSpec((GM_CHUNK, GM_W), lambda i: (0, 0))],
        out_specs=pl.BlockSpec((TM, GM_W), lambda i: (i, 0)),
        out_shape=jax.ShapeDtypeStruct((n, GM_W), f32),
        compiler_params=_params("parallel"),
        name="gmlp",
    )(z, ws_bf, bias)


def _out_kernel(a_ref, yh_ref, yg_ref, x_ref, mod_ref, og_ref, w_ref, g2_ref, rw_ref, x1_ref, h2_ref, aff_ref):
    og = og_ref[...]
    mixed_in = jnp.concatenate([
        _rms(a_ref[...]) * og[:, :ATTN_W],
        _rms(yh_ref[...]) * og[:, ATTN_W:ATTN_W + HY_W],
        _rms(yg_ref[...]) * og[:, ATTN_W + HY_W:]], axis=1)
    mixed = jnp.dot(mixed_in.astype(bf16), w_ref[...], preferred_element_type=f32)
    m = mod_ref[0]
    x1 = x_ref[...] + m[2:3] * mixed
    x1_ref[...] = x1
    h2 = _rms(x1) * g2_ref[...] * (1.0 + m[4:5]) + m[3:4]
    for j in range(ROW_CHUNKS):
        h2_ref[pl.ds(j, TM, stride=ROW_CHUNKS), :] = h2[:, j * LANES:(j + 1) * LANES]
    logits = lax.dot_general(rw_ref[...], h2, (((1,), (1,)), ((), ())),
                             precision=lax.Precision.HIGHEST, preferred_element_type=f32)
    e = jnp.exp(logits - jnp.max(logits, axis=0, keepdims=True))
    aff_ref[...] = e / jnp.sum(e, axis=0, keepdims=True)


def _out_proj(a, yh, yg, x, mod, out_g, w_out_bf, g2, router_wt):
    n = x.shape[0]
    per_mod = n // mod.shape[0]
    row = lambda w: pl.BlockSpec((TM, w), lambda i: (i, 0))
    const = lambda s: pl.BlockSpec(s, lambda i: (0,) * len(s))
    return pl.pallas_call(
        _out_kernel,
        grid=(n // TM,),
        in_specs=[row(ATTN_W), row(HY_W), row(GM_W), row(D_MODEL),
                  pl.BlockSpec((1, 6, D_MODEL), lambda i: (i * TM // per_mod, 0, 0)),
                  const((1, MIX_W)), const((MIX_W, D_MODEL)), const((1, D_MODEL)), const((N_EXPERTS, D_MODEL))],
        out_specs=[row(D_MODEL), pl.BlockSpec((TM * ROW_CHUNKS, LANES), lambda i: (i, 0)),
                   pl.BlockSpec((N_EXPERTS, TM), lambda i: (0, i))],
        out_shape=[jax.ShapeDtypeStruct((n, D_MODEL), f32), jax.ShapeDtypeStruct((n * ROW_CHUNKS, LANES), f32),
                   jax.ShapeDtypeStruct((N_EXPERTS, n), f32)],
        compiler_params=_params("parallel"),
        name="out_proj",
    )(a, yh, yg, x, mod, out_g.reshape(1, MIX_W), w_out_bf, g2.reshape(1, D_MODEL), router_wt)


RT_CHUNK = 512


def _prefix_incl(x01, tri):
    n = x01.shape[1]
    carry = jnp.zeros((x01.shape[0], 1), f32)
    parts = []
    for c in range(n // RT_CHUNK):
        piece = x01[:, c * RT_CHUNK:(c + 1) * RT_CHUNK]
        parts.append(jnp.dot(piece.astype(bf16), tri, preferred_element_type=f32) + carry)
        carry = carry + jnp.sum(piece, axis=1, keepdims=True)
    return jnp.concatenate(parts, axis=1)


def _route_kernel(cap, aff_ref, idx_ref, gate_ref):
    aff = aff_ref[...]
    n = aff.shape[1]

    def step(it, t):
        cand = t | (jnp.int32(1) << (30 - it))
        cnt = jnp.sum(jnp.where(aff >= lax.bitcast_convert_type(cand, f32), 1.0, 0.0), axis=1, keepdims=True)
        return jnp.where(cnt >= cap, cand, t)

    thr_bits = lax.fori_loop(0, 31, step, jnp.zeros((aff.shape[0], 1), jnp.int32))
    thr = lax.bitcast_convert_type(thr_bits, f32)
    gt = jnp.where(aff > thr, 1.0, 0.0)
    eq = jnp.where(aff == thr, 1.0, 0.0)
    room = cap - jnp.sum(gt, axis=1, keepdims=True)
    r = lax.broadcasted_iota(jnp.int32, (RT_CHUNK, RT_CHUNK), 0)
    c = lax.broadcasted_iota(jnp.int32, (RT_CHUNK, RT_CHUNK), 1)
    tri = jnp.where(r <= c, 1.0, 0.0).astype(bf16)
    sel = jnp.maximum(gt, jnp.where(_prefix_incl(eq, tri) <= room, eq, 0.0))
    slot = _prefix_incl(sel, tri) - 1.0

    tok = lax.broadcasted_iota(jnp.int32, aff.shape, 1)
    dist = jnp.where(sel > 0.0, tok - slot.astype(jnp.int32), 0)
    g = aff
    for b in range(max(1, (n - 1).bit_length())):
        sh = 1 << b
        dist_s = pltpu.roll(dist, n - sh, axis=1)
        take = (dist_s & sh) != 0
        leave = (dist & sh) != 0
        tok = jnp.where(take, pltpu.roll(tok, n - sh, axis=1), tok)
        g = jnp.where(take, pltpu.roll(g, n - sh, axis=1), g)
        dist = jnp.where(take, dist_s, jnp.where(leave, 0, dist))
    idx_ref[...] = tok[:, :cap]
    gate_ref[...] = g[:, :cap]


def _route(aff_t, cap):
    return pl.pallas_call(
        functools.partial(_route_kernel, cap),
        in_specs=[pl.BlockSpec(memory_space=pltpu.VMEM)],
        out_specs=[pl.BlockSpec(memory_space=pltpu.VMEM)] * 2,
        out_shape=[jax.ShapeDtypeStruct((N_EXPERTS, cap), jnp.int32), jax.ShapeDtypeStruct((N_EXPERTS, cap), f32)],
        compiler_params=pltpu.CompilerParams(vmem_limit_bytes=VMEM_LIMIT),
        name="route",
    )(aff_t)


GATHER_UNROLL = 8
SCATTER_UNROLL = 4


def _moe_ffn_kernel(cap, stride, idx_ref, h_ref, g_ref, w1_ref, w3_ref, w2_ref, y_ref, tile_ref, xb_ref, acc_ref):
    base = pl.program_id(0) * cap

    def gather(q, carry):
        for t in range(GATHER_UNROLL):
            r = q * GATHER_UNROLL + t
            src = pl.multiple_of(idx_ref[base + r] * ROW_CHUNKS, ROW_CHUNKS)
            tile_ref[pl.ds(r, ROW_CHUNKS, stride=stride), :] = h_ref[pl.ds(src, ROW_CHUNKS), :]
        return carry
    lax.fori_loop(0, cap // GATHER_UNROLL, gather, 0)
    xb_ref[...] = jnp.concatenate(
        [tile_ref[j * stride:j * stride + cap, :] for j in range(ROW_CHUNKS)], axis=1).astype(bf16)

    xb = xb_ref[...]
    for f in range(D_EXPERT // MOE_F_TILE):
        cols = slice(f * MOE_F_TILE, (f + 1) * MOE_F_TILE)
        a = jnp.dot(xb, w1_ref[0, 0, :, cols].astype(bf16), preferred_element_type=f32)
        b = jnp.dot(xb, w3_ref[0, 0, :, cols].astype(bf16), preferred_element_type=f32)
        he = (a * _sigmoid(a) * b).astype(bf16)
        part = jnp.dot(he, w2_ref[0, 0, cols, :].astype(bf16), preferred_element_type=f32)
        if f == 0:
            acc_ref[...] = part
        else:
            acc_ref[...] += part

    y = acc_ref[...] * g_ref[0]
    for j in range(ROW_CHUNKS):
        y_ref[0, j * stride:j * stride + cap, :] = y[:, j * LANES:(j + 1) * LANES]
        y_ref[0, j * stride + cap:(j + 1) * stride, :] = jnp.zeros((stride - cap, LANES), f32)


def _moe_ffn(layer, h3, idx, gate, w1, w3, w2):
    cap = idx.shape[1]
    stride = cap + SUBLANES
    grid_spec = pltpu.PrefetchScalarGridSpec(
        num_scalar_prefetch=1,
        grid=(N_EXPERTS,),
        in_specs=[
            pl.BlockSpec(memory_space=pltpu.VMEM),
            pl.BlockSpec((1, cap, 1), lambda e, idx: (e, 0, 0)),
            pl.BlockSpec((1, 1, D_MODEL, D_EXPERT), lambda e, idx: (layer, e, 0, 0)),
            pl.BlockSpec((1, 1, D_MODEL, D_EXPERT), lambda e, idx: (layer, e, 0, 0)),
            pl.BlockSpec((1, 1, D_EXPERT, D_MODEL), lambda e, idx: (layer, e, 0, 0)),
        ],
        out_specs=pl.BlockSpec((1, ROW_CHUNKS * stride, LANES), lambda e, idx: (e, 0, 0)),
        scratch_shapes=[pltpu.VMEM((ROW_CHUNKS * stride, LANES), f32),
                        pltpu.VMEM((cap, D_MODEL), bf16),
                        pltpu.VMEM((cap, D_MODEL), f32)],
    )
    return pl.pallas_call(
        functools.partial(_moe_ffn_kernel, cap, stride),
        grid_spec=grid_spec,
        out_shape=jax.ShapeDtypeStruct((N_EXPERTS, ROW_CHUNKS * stride, LANES), f32),
        compiler_params=_params("arbitrary"),
        name="moe_ffn",
    )(idx.reshape(-1), h3, gate.reshape(N_EXPERTS, cap, 1), w1, w3, w2)


def _moe_combine_kernel(cap, stride, final, idx_ref, y_ref, x_ref, mod_ref, g_ref, o_ref, acc_ref):
    s = pl.program_id(0)

    @pl.when(s == 0)
    def _():
        acc_ref[...] = jnp.zeros_like(acc_ref)

    @pl.when(s < N_EXPERTS)
    def _():
        base = s * cap

        def scatter(q, carry):
            rows = [pl.multiple_of(idx_ref[base + q * SCATTER_UNROLL + t] * ROW_CHUNKS, ROW_CHUNKS)
                    for t in range(SCATTER_UNROLL)]
            new = [acc_ref[pl.ds(rows[t], ROW_CHUNKS), :]
                   + y_ref[0, pl.ds(q * SCATTER_UNROLL + t, ROW_CHUNKS, stride=stride), :]
                   for t in range(SCATTER_UNROLL)]
            for t in range(SCATTER_UNROLL):
                acc_ref[pl.ds(rows[t], ROW_CHUNKS), :] = new[t]
            return carry
        lax.fori_loop(0, cap // SCATTER_UNROLL, scatter, 0)

    @pl.when(s >= N_EXPERTS)
    def _():
        first = pl.multiple_of((s - N_EXPERTS) * (TM * ROW_CHUNKS), TM * ROW_CHUNKS)
        moe = jnp.concatenate([acc_ref[pl.ds(first + j, TM, stride=ROW_CHUNKS), :] for j in range(ROW_CHUNKS)],
                              axis=1)
        x = x_ref[...] + mod_ref[0][5:6] * moe
        o_ref[...] = _rms(x) * g_ref[...] if final else x


def _moe_combine(idx, y_cm, x1, mod, final_g, final):
    n = x1.shape[0]
    cap = idx.shape[1]
    stride = y_cm.shape[1] // ROW_CHUNKS
    per_mod = n // mod.shape[0]
    tile = lambda s: jnp.maximum(s - N_EXPERTS, 0)
    row = pl.BlockSpec((TM, D_MODEL), lambda s, idx: (tile(s), 0))
    grid_spec = pltpu.PrefetchScalarGridSpec(
        num_scalar_prefetch=1,
        grid=(N_EXPERTS + n // TM,),
        in_specs=[pl.BlockSpec((1, ROW_CHUNKS * stride, LANES), lambda s, idx: (jnp.minimum(s, N_EXPERTS - 1), 0, 0)),
                  row,
                  pl.BlockSpec((1, 6, D_MODEL), lambda s, idx: (tile(s) * TM // per_mod, 0, 0)),
                  pl.BlockSpec((1, D_MODEL), lambda s, idx: (0, 0))],
        out_specs=row,
        scratch_shapes=[pltpu.VMEM((n * ROW_CHUNKS, LANES), f32)],
    )
    return pl.pallas_call(
        functools.partial(_moe_combine_kernel, cap, stride, final),
        grid_spec=grid_spec,
        out_shape=jax.ShapeDtypeStruct((n, D_MODEL), f32),
        compiler_params=_params("arbitrary"),
        name="moe_combine",
    )(idx.reshape(-1), y_cm, x1, mod, final_g.reshape(1, D_MODEL))


def _expert_choice(layer, h3, aff_t, x1, mod, final_g, final, w1, w3, w2):
    n = x1.shape[0]
    cap = max(1, EC_CAPACITY * n // N_EXPERTS)
    idx, gate = _route(aff_t, cap)
    y_cm = _moe_ffn(layer, h3, idx, gate, w1, w3, w2)
    return _moe_combine(idx, y_cm, x1, mod, final_g, final)


def _stream(x, mods, layers, experts, batch, seq_len, final_g, caches=None):
    consts = _hy_consts(seq_len)
    rope_tabs = _rope_tables(seq_len) if caches is not None else None
    kvs = []
    for l, lp in enumerate(layers):
        z = _in_proj(x, mods[l], lp["norm1_g"], lp["w_in"], seq_len, rope_tabs=rope_tabs)
        if caches is None:
            a = _ctx_attention(z, lp["attn_sink"], batch, seq_len)
            kvs.append((z[:, ATTN_W:ATTN_W + KV_W], z[:, ATTN_W + KV_W:QKV_W]))
        else:
            a = _lat_attention(z, lp["attn_sink"], caches[0][:, l], caches[1][:, l], batch, seq_len)
        yh = _hyena(z, lp, consts, batch, seq_len)
        yg = _gmlp(z, lp["gm_ws"], lp["gm_b"])
        x1, h3, aff_t = _out_proj(a, yh, yg, x, mods[l], lp["out_norm_g"], lp["w_out"], lp["norm2_g"], lp["router_wt"])
        x = _expert_choice(l, h3, aff_t, x1, mods[l], final_g, l == len(layers) - 1, *experts)
    return x, kvs


def kernel(x_prompt, x_sample, c, cache_k, cache_v, c_ctx, norm1_g, norm2_g, ada_w, ada_b, w_in, attn_sink,
           hy_conv_w, hy_conv_b, hy_f_w1, hy_f_b1, hy_f_w2, hy_f_b2, hy_f_w3, hy_freq, hy_bias, gm_ws, gm_b,
           out_norm_g, w_out, router_w, exp_w1, exp_w3, exp_w2, final_g):
    batch, seq, _ = x_prompt.shape
    dbatch, dseq, _ = x_sample.shape
    past = cache_k.shape[2]

    cvec = jnp.zeros((SUBLANES, D_MODEL), f32).at[0].set(c_ctx).at[1:1 + dbatch].set(c)
    mod = _ada(cvec, ada_w, ada_b)
    mods_ctx = [mod[l, 0:1].reshape(1, 6, D_MODEL) for l in range(DEPTH)]
    mods_lat = [mod[l, 1:1 + dbatch].reshape(dbatch, 6, D_MODEL) for l in range(DEPTH)]

    w_in_bf, w_out_bf, gm_ws_bf = w_in.astype(bf16), w_out.astype(bf16), gm_ws.astype(bf16)
    layers = []
    for l in range(DEPTH):
        layers.append({
            "norm1_g": norm1_g[l], "norm2_g": norm2_g[l], "w_in": w_in_bf[l], "attn_sink": attn_sink[l],
            "hy_conv_w": hy_conv_w[l], "hy_conv_b": hy_conv_b[l], "hy_f_w1": hy_f_w1[l], "hy_f_b1": hy_f_b1[l],
            "hy_f_w2": hy_f_w2[l], "hy_f_b2": hy_f_b2[l], "hy_f_w3": hy_f_w3[l], "hy_freq": hy_freq[l],
            "hy_bias": hy_bias[l], "gm_ws": gm_ws_bf[l], "gm_b": gm_b[l], "out_norm_g": out_norm_g[l],
            "w_out": w_out_bf[l], "router_wt": router_w[l].T})
    experts = (exp_w1, exp_w3, exp_w2)

    yp, kvs = _stream(x_prompt.reshape(batch * seq, D_MODEL), mods_ctx, layers, experts, batch, seq, final_g)
    caches = (cache_k.reshape(dbatch, DEPTH, past, KV_W), cache_v.reshape(dbatch, DEPTH, past, KV_W))
    ys, _ = _stream(x_sample.reshape(dbatch * dseq, D_MODEL), mods_lat, layers, experts, dbatch, dseq, final_g,
                   caches=caches)

    new_k = jnp.stack([k.reshape(batch, seq, N_KV, HEAD_DIM) for k, _ in kvs], axis=1)
    new_v = jnp.stack([v.reshape(batch, seq, N_KV, HEAD_DIM) for _, v in kvs], axis=1)
    return (yp.reshape(batch, seq, D_MODEL), ys.reshape(dbatch, dseq, D_MODEL), new_k, new_v)
```

```python
import functools
import math

import numpy as np
import jax
import jax.numpy as jnp
from jax import lax
from jax.experimental import pallas as pl
from jax.experimental.pallas import tpu as pltpu

f32 = jnp.float32
bf16 = jnp.bfloat16

D_MODEL = 1024
DEPTH = 2
GRID_W = 64
BLOCK = 128
N_HEADS = 8
N_KV = 2
HEAD_DIM = 64
Q_GROUP = N_HEADS // N_KV
ATTN_W = N_HEADS * HEAD_DIM
KV_W = N_KV * HEAD_DIM
QKV_W = ATTN_W + 2 * KV_W
HY_W = 256
GM_W = 256
GM_HEADS = 4
GM_HEAD_DIM = GM_W // GM_HEADS
GM_CHUNK = 128
MIX_W = ATTN_W + HY_W + GM_W
IN_W = ATTN_W + 2 * KV_W + 3 * HY_W + 2 * GM_W
FILTER_EMB = 33
FILTER_HID = 64
HY_DECAY_HI_PCT = 0.3
HY_DECAY_LO_PCT = 1.5
HY_DECAY_TARGET = 1e-2
N_EXPERTS = 16
EC_CAPACITY = 2
D_EXPERT = 1024
ROPE_THETA = 10000.0
EPS = 1e-6
NEG = -1e30

LANES = 128
SUBLANES = 8
ROW_CHUNKS = D_MODEL // LANES
VMEM_LIMIT = 56 * 1024 * 1024
TM = 512
MOE_F_TILE = 256


def _params(*sem):
    return pltpu.CompilerParams(dimension_semantics=sem, vmem_limit_bytes=VMEM_LIMIT)


def _rms(x):
    return x * lax.rsqrt(jnp.mean(x * x, axis=-1, keepdims=True) + EPS)


def _sigmoid(x):
    return 1.0 / (1.0 + jnp.exp(-x))


def _ada_kernel(c_ref, w_ref, b_ref, o_ref):
    c = c_ref[...]
    s = (c * _sigmoid(c)).astype(bf16)
    o_ref[0] = jnp.dot(s, w_ref[0].astype(bf16), preferred_element_type=f32) + b_ref[0]


def _ada(cvec, ada_w, ada_b):
    nt = 6
    return pl.pallas_call(
        _ada_kernel,
        grid=(DEPTH, nt),
        in_specs=[
            pl.BlockSpec((SUBLANES, D_MODEL), lambda l, j: (0, 0)),
            pl.BlockSpec((1, D_MODEL, D_MODEL), lambda l, j: (l, 0, j)),
            pl.BlockSpec((1, 1, D_MODEL), lambda l, j: (l, 0, j)),
        ],
        out_specs=pl.BlockSpec((1, SUBLANES, D_MODEL), lambda l, j: (l, 0, j)),
        out_shape=jax.ShapeDtypeStruct((DEPTH, SUBLANES, 6 * D_MODEL), f32),
        compiler_params=_params("arbitrary", "arbitrary"),
        name="ada",
    )(cvec, ada_w, ada_b.reshape(DEPTH, 1, 6 * D_MODEL))


def _rope_swap(x):
    w = x.shape[-1]
    lane = lax.broadcasted_iota(jnp.int32, x.shape, 1)
    first = (lane % 32) < 16
    return jnp.where(first, pltpu.roll(x, w - 16, axis=1), pltpu.roll(x, 16, axis=1))


def _in_kernel(rope, x_ref, mod_ref, g_ref, w_ref, *refs):
    if rope:
        cos_ref, sin_ref, qkv_ref, zh_ref, zg_ref = refs
    else:
        qkv_ref, zh_ref, zg_ref = refs
    m = mod_ref[0]
    h = _rms(x_ref[...]) * g_ref[...] * (1.0 + m[1:2]) + m[0:1]
    z = jnp.dot(h.astype(bf16), w_ref[...], preferred_element_type=f32)
    if rope:
        reps = (ATTN_W + KV_W) // LANES
        cos = jnp.concatenate([cos_ref[...]] * reps, axis=1)
        sin = jnp.concatenate([sin_ref[...]] * reps, axis=1)
        qk = z[:, :ATTN_W + KV_W]
        qkv_ref[:, :ATTN_W + KV_W] = qk * cos + _rope_swap(qk) * sin
        qkv_ref[:, ATTN_W + KV_W:] = z[:, ATTN_W + KV_W:QKV_W]
    else:
        qkv_ref[...] = z[:, :QKV_W]
    zh_ref[...] = z[:, QKV_W:QKV_W + 3 * HY_W].astype(bf16)
    zg_ref[...] = z[:, QKV_W + 3 * HY_W:].astype(bf16)


def _in_proj(x, mod, g1, w_in_bf, seq_len, rope_tabs=None):
    n = x.shape[0]
    rope = rope_tabs is not None
    per_mod = n // mod.shape[0]
    in_specs = [pl.BlockSpec((TM, D_MODEL), lambda i: (i, 0)),
                pl.BlockSpec((1, 6, D_MODEL), lambda i: (i * TM // per_mod, 0, 0)),
                pl.BlockSpec((1, D_MODEL), lambda i: (0, 0)),
                pl.BlockSpec((D_MODEL, IN_W), lambda i: (0, 0))]
    args = [x, mod, g1.reshape(1, D_MODEL), w_in_bf]
    if rope:
        nt = seq_len // TM
        tab = pl.BlockSpec((TM, LANES), lambda i: (i % nt, 0))
        in_specs += [tab, tab]
        args += list(rope_tabs)
    return pl.pallas_call(
        functools.partial(_in_kernel, rope),
        grid=(n // TM,),
        in_specs=in_specs,
        out_specs=[pl.BlockSpec((TM, w), lambda i: (i, 0)) for w in (QKV_W, 3 * HY_W, 2 * GM_W)],
        out_shape=[jax.ShapeDtypeStruct((n, QKV_W), f32), jax.ShapeDtypeStruct((n, 3 * HY_W), bf16),
                   jax.ShapeDtypeStruct((n, 2 * GM_W), bf16)],
        compiler_params=_params("parallel"),
        name="in_proj",
    )(*args)


def _rope_tables(seq_len):
    nf = HEAD_DIM // 4
    t = np.arange(seq_len)
    inv = (ROPE_THETA ** (-np.arange(nf, dtype=np.float32) / nf)).astype(np.float32)
    d = np.arange(HEAD_DIM)
    pos = np.where((d // 32)[None, :] == 0, (t // GRID_W)[:, None], (t % GRID_W)[:, None]).astype(np.float32)
    ang = (pos * inv[d % nf][None, :]).astype(np.float32)
    cos = np.cos(ang).astype(np.float32)
    sin = np.sin(ang).astype(np.float32) * np.where((d % 32) < 16, -1.0, 1.0)[None, :].astype(np.float32)
    return jnp.asarray(np.tile(cos, (1, 2))), jnp.asarray(np.tile(sin, (1, 2)))


LOG2E = math.log2(math.e)


def _stack_queries(z, sink_ref, kv):
    rows = z.shape[0]
    heads = range(kv * Q_GROUP, (kv + 1) * Q_GROUP)
    qs = jnp.concatenate([z[:, h * HEAD_DIM:(h + 1) * HEAD_DIM] for h in heads], axis=0)
    sink = jnp.concatenate([jnp.full((1, rows), sink_ref[h] * LOG2E, f32) for h in heads], axis=1)
    return (qs * (LOG2E / math.sqrt(HEAD_DIM))).astype(bf16), sink


def _scores_t(k, qs_bf):
    return lax.dot_general(k.astype(bf16), qs_bf, (((1,), (1,)), ((), ())), preferred_element_type=f32)


def _softmax_pv_t(st, sink, vals, rows):
    m = jnp.maximum(jnp.max(st, axis=0, keepdims=True), sink)
    pt = jnp.exp2(st - m).astype(bf16)
    v_bf = vals.astype(bf16)
    v_ext = jnp.concatenate([v_bf, jnp.ones_like(v_bf)], axis=1)
    ovt = lax.dot_general(v_ext, pt, (((0,), (0,)), ((), ())), preferred_element_type=f32)
    ot = ovt[:HEAD_DIM] / (ovt[HEAD_DIM:HEAD_DIM + 1] + jnp.exp2(sink - m))
    return [ot[:, g * rows:(g + 1) * rows].T for g in range(Q_GROUP)]


def _ctx_attn_kernel(sink_ref, z_ref, o_ref):
    z = z_ref[...]
    outs = []
    for kv in range(N_KV):
        k = z[:, ATTN_W + kv * HEAD_DIM:ATTN_W + (kv + 1) * HEAD_DIM]
        v = z[:, ATTN_W + KV_W + kv * HEAD_DIM:ATTN_W + KV_W + (kv + 1) * HEAD_DIM]
        qs, sink = _stack_queries(z, sink_ref, kv)
        outs += _softmax_pv_t(_scores_t(k, qs), sink, v, z.shape[0])
    o_ref[...] = jnp.concatenate(outs, axis=1)


def _ctx_attention(z, sink, batch, seq_len):
    return pl.pallas_call(
        _ctx_attn_kernel,
        grid=(batch,),
        in_specs=[pl.BlockSpec(memory_space=pltpu.SMEM),
                  pl.BlockSpec((seq_len, QKV_W), lambda b: (b, 0))],
        out_specs=pl.BlockSpec((seq_len, ATTN_W), lambda b: (b, 0)),
        out_shape=jax.ShapeDtypeStruct((batch * seq_len, ATTN_W), f32),
        compiler_params=_params("parallel"),
        name="ctx_attn",
    )(sink, z)


def _lat_attn_kernel(nb, sink_ref, zp_ref, zc_ref, zn_ref, ck_ref, cv_ref, o_ref):
    i = pl.program_id(1)
    zp, zc, zn = zp_ref[...], zc_ref[...], zn_ref[...]
    ck, cv = ck_ref[0], cv_ref[0]
    width = Q_GROUP * BLOCK
    j = lax.broadcasted_iota(jnp.int32, (BLOCK, width), 0)
    r = lax.broadcasted_iota(jnp.int32, (BLOCK, width), 1) % BLOCK
    ok_prev = j >= r + jnp.where(i > 0, 0, BLOCK)
    ok_next = j <= r - jnp.where(i < nb - 1, 0, BLOCK)
    outs = []
    for kv in range(N_KV):
        ks = slice(ATTN_W + kv * HEAD_DIM, ATTN_W + (kv + 1) * HEAD_DIM)
        vs = slice(ATTN_W + KV_W + kv * HEAD_DIM, ATTN_W + KV_W + (kv + 1) * HEAD_DIM)
        cs = slice(kv * HEAD_DIM, (kv + 1) * HEAD_DIM)
        qs, sink = _stack_queries(zc, sink_ref, kv)
        st = jnp.concatenate([
            jnp.where(ok_prev, _scores_t(zp[:, ks], qs), NEG),
            _scores_t(zc[:, ks], qs),
            jnp.where(ok_next, _scores_t(zn[:, ks], qs), NEG),
            _scores_t(ck[:, cs], qs)], axis=0)
        vals = jnp.concatenate([zp[:, vs], zc[:, vs], zn[:, vs], cv[:, cs]], axis=0)
        outs += _softmax_pv_t(st, sink, vals, BLOCK)
    o_ref[...] = jnp.concatenate(outs, axis=1)


def _lat_attention(z, sink, ck, cv, batch, seq_len):
    nb = seq_len // BLOCK
    blk = lambda f: pl.BlockSpec((BLOCK, QKV_W), f)
    past = ck.shape[1]
    cache = pl.BlockSpec((1, past, KV_W), lambda b, i: (b, 0, 0))
    return pl.pallas_call(
        functools.partial(_lat_attn_kernel, nb),
        grid=(batch, nb),
        in_specs=[pl.BlockSpec(memory_space=pltpu.SMEM),
                  blk(lambda b, i: (b * nb + jnp.maximum(i - 1, 0), 0)),
                  blk(lambda b, i: (b * nb + i, 0)),
                  blk(lambda b, i: (b * nb + jnp.minimum(i + 1, nb - 1), 0)),
                  cache, cache],
        out_specs=pl.BlockSpec((BLOCK, ATTN_W), lambda b, i: (b * nb + i, 0)),
        out_shape=jax.ShapeDtypeStruct((batch * seq_len, ATTN_W), f32),
        compiler_params=_params("parallel", "parallel"),
        name="lat_attn",
    )(sink, z, z, z, ck, cv)


HY_TL = 256
HALO_ROWS = 2 * SUBLANES


def _hy_prep_kernel(nt, zp_ref, zc_ref, zn_ref, w_ref, b_ref, x0_ref, u_ref):
    i = pl.program_id(1)
    z = zc_ref[...].astype(f32)
    row = lax.broadcasted_iota(jnp.int32, z.shape, 0)
    prev_row = jnp.where(i > 0, zp_ref[...].astype(f32)[HALO_ROWS - 1:HALO_ROWS, :], 0.0)
    next_row = jnp.where(i < nt - 1, zn_ref[...].astype(f32)[0:1, :], 0.0)
    z_prev = jnp.where(row == 0, prev_row, pltpu.roll(z, 1, axis=0))
    z_next = jnp.where(row == HY_TL - 1, next_row, pltpu.roll(z, HY_TL - 1, axis=0))
    w = w_ref[...]
    zc = z_prev * w[0:1] + z * w[1:2] + z_next * w[2:3] + b_ref[...]
    x0_ref[...] = zc[:, :HY_W]
    u_ref[...] = zc[:, HY_W:2 * HY_W] * zc[:, 2 * HY_W:]


def _hy_prep(z, conv_w, conv_b, batch, seq_len):
    n = batch * seq_len
    nt = seq_len // HY_TL
    rh = HY_TL // HALO_ROWS
    last = n // HALO_ROWS - 1
    return pl.pallas_call(
        functools.partial(_hy_prep_kernel, nt),
        grid=(batch, nt),
        in_specs=[
            pl.BlockSpec((HALO_ROWS, 3 * HY_W), lambda b, i: (jnp.maximum((b * nt + i) * rh - 1, 0), 0)),
            pl.BlockSpec((HY_TL, 3 * HY_W), lambda b, i: (b * nt + i, 0)),
            pl.BlockSpec((HALO_ROWS, 3 * HY_W), lambda b, i: (jnp.minimum((b * nt + i + 1) * rh, last), 0)),
            pl.BlockSpec((3, 3 * HY_W), lambda b, i: (0, 0)),
            pl.BlockSpec((1, 3 * HY_W), lambda b, i: (0, 0)),
        ],
        out_specs=[pl.BlockSpec((HY_TL, HY_W), lambda b, i: (b * nt + i, 0))] * 2,
        out_shape=[jax.ShapeDtypeStruct((n, HY_W), f32)] * 2,
        compiler_params=_params("parallel", "parallel"),
        name="hy_prep",
    )(z, z, z, conv_w, conv_b.reshape(1, 3 * HY_W))


def _hy_filter_kernel(seq_len, feats_ref, win_ref, w1_ref, b1_ref, w2_ref, b2_ref, w3_ref, fr_ref, o_ref):
    hi = lax.Precision.HIGHEST
    fr = fr_ref[...]
    h = jnp.sin(fr * (jnp.dot(feats_ref[...], w1_ref[...], precision=hi, preferred_element_type=f32) + b1_ref[...]))
    h = jnp.sin(fr * (jnp.dot(h, w2_ref[...], precision=hi, preferred_element_type=f32) + b2_ref[...]))
    h = jnp.dot(h, w3_ref[...], precision=hi, preferred_element_type=f32)
    win = win_ref[...]
    row = lax.broadcasted_iota(jnp.int32, (seq_len, HY_W), 0)
    o_ref[0:seq_len, :] = h[:, :HY_W] * win
    o_ref[seq_len:, :] = jnp.where(row == 0, 0.0, h[:, HY_W:] * win)


def _hy_filter(seq_len, consts, w1, b1, w2, b2, w3, freq):
    w1p = jnp.zeros((LANES, FILTER_HID), f32).at[:FILTER_EMB].set(w1)
    args = (consts["feats"], consts["window"], w1p, b1.reshape(1, -1), w2, b2.reshape(1, -1), w3, freq.reshape(1, -1))
    return pl.pallas_call(
        functools.partial(_hy_filter_kernel, seq_len),
        in_specs=[pl.BlockSpec(memory_space=pltpu.VMEM)] * len(args),
        out_specs=pl.BlockSpec(memory_space=pltpu.VMEM),
        out_shape=jax.ShapeDtypeStruct((2 * seq_len, HY_W), f32),
        compiler_params=pltpu.CompilerParams(vmem_limit_bytes=VMEM_LIMIT),
        name="hy_filter",
    )(*args)


def _hy_tiles(seq_len):
    return min(TM, seq_len // 2), min(TM, seq_len)


def _hy_fwd_kernel(nb, tk, m_ref, u_ref, hb_ref, r_ref, z_ref, rhs_ref):
    i = pl.program_id(0)

    @pl.when(i == 0)
    def _():
        for b in range(nb):
            rhs_ref[:, b * HY_W:(b + 1) * HY_W] = u_ref[b].astype(bf16)
        rhs_ref[:, nb * HY_W:(nb + 1) * HY_W] = hb_ref[0].astype(bf16)
        rhs_ref[:, (nb + 1) * HY_W:] = hb_ref[1].astype(bf16)

    acc = jnp.dot(m_ref[...], rhs_ref[...], preferred_element_type=f32)
    p, w = acc[:tk], acc[tk:]
    c0 = nb * HY_W
    pf, pb = p[:, c0:c0 + HY_W], p[:, c0 + HY_W:]
    qf, qb = w[:, c0:c0 + HY_W], w[:, c0 + HY_W:]
    hr = pf + pb
    qs = qf + qb
    hi = qb - qf
    first = (lax.broadcasted_iota(jnp.int32, (tk, HY_W), 0) == 0) & (i == 0)
    for b in range(nb):
        pu, wu = p[:, b * HY_W:(b + 1) * HY_W], w[:, b * HY_W:(b + 1) * HY_W]
        r_ref[b] = jnp.where(first, pu * hr, 2.0 * (pu * hr + wu * hi)).astype(bf16)
        z_ref[b] = jnp.where(first, wu * qs, 2.0 * (wu * hr - pu * hi)).astype(bf16)


def _hy_fwd(mat, u, hb, batch, seq_len):
    tk, _ = _hy_tiles(seq_len)
    out = pl.BlockSpec((batch, tk, HY_W), lambda i: (0, i, 0))
    return pl.pallas_call(
        functools.partial(_hy_fwd_kernel, batch, tk),
        grid=(seq_len // tk,),
        in_specs=[pl.BlockSpec((2 * tk, seq_len), lambda i: (i, 0)),
                  pl.BlockSpec((batch, seq_len, HY_W), lambda i: (0, 0, 0)),
                  pl.BlockSpec((2, seq_len, HY_W), lambda i: (0, 0, 0))],
        out_specs=[out, out],
        out_shape=[jax.ShapeDtypeStruct((batch, seq_len, HY_W), bf16)] * 2,
        scratch_shapes=[pltpu.VMEM((seq_len, (batch + 2) * HY_W), bf16)],
        compiler_params=_params("arbitrary"),
        name="hy_fwd",
    )(mat, u, hb)


def _hy_inv_kernel(nb, seq_len, m_ref, r_ref, z_ref, u_ref, x0_ref, d_ref, o_ref, rhs_ref):
    i = pl.program_id(0)

    @pl.when(i == 0)
    def _():
        for b in range(nb):
            rhs_ref[:seq_len, b * HY_W:(b + 1) * HY_W] = r_ref[b]
            rhs_ref[seq_len:, b * HY_W:(b + 1) * HY_W] = z_ref[b]

    acc = jnp.dot(m_ref[...], rhs_ref[...], preferred_element_type=f32)
    d = d_ref[...]
    for b in range(nb):
        y = acc[:, b * HY_W:(b + 1) * HY_W] * (1.0 / (2 * seq_len)) + u_ref[b] * d
        o_ref[b] = x0_ref[b] * y


def _hy_inv(mat, r, zz, u, x0, d_bias, batch, seq_len):
    _, tm = _hy_tiles(seq_len)
    full = pl.BlockSpec((batch, seq_len, HY_W), lambda i: (0, 0, 0))
    row = pl.BlockSpec((batch, tm, HY_W), lambda i: (0, i, 0))
    return pl.pallas_call(
        functools.partial(_hy_inv_kernel, batch, seq_len),
        grid=(seq_len // tm,),
        in_specs=[pl.BlockSpec((tm, 2 * seq_len), lambda i: (i, 0)), full, full, row, row,
                  pl.BlockSpec((1, HY_W), lambda i: (0, 0))],
        out_specs=row,
        out_shape=jax.ShapeDtypeStruct((batch, seq_len, HY_W), f32),
        scratch_shapes=[pltpu.VMEM((2 * seq_len, batch * HY_W), bf16)],
        compiler_params=_params("arbitrary"),
        name="hy_inv",
    )(mat, r, zz, u, x0, d_bias.reshape(1, HY_W))


@functools.lru_cache(maxsize=None)
def _hy_consts_np(seq_len):
    n = 2 * seq_len
    k = np.arange(seq_len, dtype=np.int64)
    ang = (2.0 * np.pi / n) * ((k[:, None] * k[None, :]) % n).astype(np.float64)
    cosm = np.cos(ang)
    sinm = np.sin(ang)
    sinm[0, :] = np.where(k % 2 == 0, 1.0, -1.0)
    tk, _ = _hy_tiles(seq_len)
    fwd = np.concatenate([np.concatenate([cosm[i:i + tk], sinm[i:i + tk]], axis=0)
                          for i in range(0, seq_len, tk)], axis=0)
    inv = np.concatenate([cosm, sinm.T], axis=1)
    t = np.linspace(0.0, 1.0, seq_len, dtype=np.float32)[:, None]
    bands = (FILTER_EMB - 1) // 2
    fb = np.linspace(1e-4, bands - 1, bands, dtype=np.float32)[None, :]
    w = (np.float32(2.0 * math.pi) * np.arange(seq_len, dtype=np.float32)[:, None] / np.float32(seq_len)).astype(np.float32)
    feats = np.concatenate([t, np.cos(fb * w), -np.sin(fb * w)], axis=-1).astype(np.float32)
    feats = np.pad(feats, ((0, 0), (0, LANES - FILTER_EMB)))
    decay_hi = math.log(HY_DECAY_TARGET) / HY_DECAY_HI_PCT
    decay_lo = math.log(HY_DECAY_TARGET) / HY_DECAY_LO_PCT
    deltas = np.abs(np.linspace(decay_lo, decay_hi, HY_W, dtype=np.float32))
    window = np.exp(-t * deltas[None, :]).astype(np.float32)
    return fwd, inv, feats, window


def _hy_consts(seq_len):
    fwd, inv, feats, window = _hy_consts_np(seq_len)
    as_bf = lambda m: jnp.asarray(m, dtype=f32).astype(bf16)
    return {"fwd": as_bf(fwd), "inv": as_bf(inv), "feats": jnp.asarray(feats), "window": jnp.asarray(window)}


def _hyena(z, lp, consts, batch, seq_len):
    x0, u = _hy_prep(z, lp["hy_conv_w"], lp["hy_conv_b"], batch, seq_len)
    x0, u = x0.reshape(batch, seq_len, HY_W), u.reshape(batch, seq_len, HY_W)
    hb = _hy_filter(seq_len, consts, lp["hy_f_w1"], lp["hy_f_b1"], lp["hy_f_w2"], lp["hy_f_b2"],
                    lp["hy_f_w3"], lp["hy_freq"]).reshape(2, seq_len, HY_W)
    r, zz = _hy_fwd(consts["fwd"], u, hb, batch, seq_len)
    yh = _hy_inv(consts["inv"], r, zz, u, x0, lp["hy_bias"], batch, seq_len)
    return yh.reshape(batch * seq_len, HY_W)


def _gelu(x):
    return 0.5 * x * (1.0 + jnp.tanh(math.sqrt(2.0 / math.pi) * (x + 0.044715 * (x * x * x))))


def _gmlp(z, ws_ref, bias):
    g = _gelu(z)
    outs = []
    for c in range(TM // GM_CHUNK):
        rows = slice(c * GM_CHUNK, (c + 1) * GM_CHUNK)
        u = g[rows, :GM_W]
        v = g[rows, GM_W:].astype(bf16)
        s = jnp.concatenate(
            [jnp.dot(ws_ref[h], v[:, h * GM_HEAD_DIM:(h + 1) * GM_HEAD_DIM], preferred_element_type=f32)
             for h in range(GM_HEADS)], axis=1)
        outs.append(u * (s + bias))
    return jnp.concatenate(outs, axis=0)


def _out_kernel(a_ref, yh_ref, zg_ref, ws_ref, gb_ref, x_ref, mod_ref, og_ref, w_ref, g2_ref, rw_ref,
                x1_ref, h2_ref, aff_ref):
    og = og_ref[...]
    yg = _gmlp(zg_ref[...].astype(f32), ws_ref, gb_ref[...])
    mixed_in = jnp.concatenate([
        _rms(a_ref[...]) * og[:, :ATTN_W],
        _rms(yh_ref[...]) * og[:, ATTN_W:ATTN_W + HY_W],
        _rms(yg) * og[:, ATTN_W + HY_W:]], axis=1)
    mixed = jnp.dot(mixed_in.astype(bf16), w_ref[...], preferred_element_type=f32)
    m = mod_ref[0]
    x1 = x_ref[...] + m[2:3] * mixed
    x1_ref[...] = x1
    h2 = _rms(x1) * g2_ref[...] * (1.0 + m[4:5]) + m[3:4]
    for j in range(ROW_CHUNKS):
        h2_ref[pl.ds(j, TM, stride=ROW_CHUNKS), :] = h2[:, j * LANES:(j + 1) * LANES]
    logits = lax.dot_general(rw_ref[...], h2, (((1,), (1,)), ((), ())),
                             precision=lax.Precision.HIGHEST, preferred_element_type=f32)
    e = jnp.exp(logits - jnp.max(logits, axis=0, keepdims=True))
    aff_ref[...] = e / jnp.sum(e, axis=0, keepdims=True)


def _out_proj(a, yh, zg, ws_bf, gm_b, x, mod, out_g, w_out_bf, g2, router_wt):
    n = x.shape[0]
    per_mod = n // mod.shape[0]
    row = lambda w: pl.BlockSpec((TM, w), lambda i: (i, 0))
    const = lambda s: pl.BlockSpec(s, lambda i: (0,) * len(s))
    gm_bias = jnp.repeat(gm_b.T, GM_HEAD_DIM, axis=1)
    return pl.pallas_call(
        _out_kernel,
        grid=(n // TM,),
        in_specs=[row(ATTN_W), row(HY_W), row(2 * GM_W), const((GM_HEADS, GM_CHUNK, GM_CHUNK)),
                  const((GM_CHUNK, GM_W)), row(D_MODEL),
                  pl.BlockSpec((1, 6, D_MODEL), lambda i: (i * TM // per_mod, 0, 0)),
                  const((1, MIX_W)), const((MIX_W, D_MODEL)), const((1, D_MODEL)), const((N_EXPERTS, D_MODEL))],
        out_specs=[row(D_MODEL), pl.BlockSpec((TM * ROW_CHUNKS, LANES), lambda i: (i, 0)),
                   pl.BlockSpec((N_EXPERTS, TM), lambda i: (0, i))],
        out_shape=[jax.ShapeDtypeStruct((n, D_MODEL), f32), jax.ShapeDtypeStruct((n * ROW_CHUNKS, LANES), f32),
                   jax.ShapeDtypeStruct((N_EXPERTS, n), f32)],
        compiler_params=_params("parallel"),
        name="out_proj",
    )(a, yh, zg, ws_bf, gm_bias, x, mod, out_g.reshape(1, MIX_W), w_out_bf, g2.reshape(1, D_MODEL), router_wt)


RT_CHUNK = 512


def _prefix_incl(x01, tri):
    n = x01.shape[1]
    carry = jnp.zeros((x01.shape[0], 1), f32)
    parts = []
    for c in range(n // RT_CHUNK):
        piece = x01[:, c * RT_CHUNK:(c + 1) * RT_CHUNK]
        parts.append(jnp.dot(piece.astype(bf16), tri, preferred_element_type=f32) + carry)
        carry = carry + jnp.sum(piece, axis=1, keepdims=True)
    return jnp.concatenate(parts, axis=1)


def _route_kernel(cap, aff_ref, idx_ref, gate_ref):
    aff = aff_ref[...]
    n = aff.shape[1]

    def step(it, t):
        cand = t | (jnp.int32(1) << (30 - it))
        cnt = jnp.sum(jnp.where(aff >= lax.bitcast_convert_type(cand, f32), 1.0, 0.0), axis=1, keepdims=True)
        return jnp.where(cnt >= cap, cand, t)

    thr_bits = lax.fori_loop(0, 31, step, jnp.zeros((aff.shape[0], 1), jnp.int32))
    thr = lax.bitcast_convert_type(thr_bits, f32)
    gt = jnp.where(aff > thr, 1.0, 0.0)
    eq = jnp.where(aff == thr, 1.0, 0.0)
    room = cap - jnp.sum(gt, axis=1, keepdims=True)
    r = lax.broadcasted_iota(jnp.int32, (RT_CHUNK, RT_CHUNK), 0)
    c = lax.broadcasted_iota(jnp.int32, (RT_CHUNK, RT_CHUNK), 1)
    tri = jnp.where(r <= c, 1.0, 0.0).astype(bf16)
    sel = jnp.maximum(gt, jnp.where(_prefix_incl(eq, tri) <= room, eq, 0.0))
    slot = _prefix_incl(sel, tri) - 1.0

    tok = lax.broadcasted_iota(jnp.int32, aff.shape, 1)
    dist = jnp.where(sel > 0.0, tok - slot.astype(jnp.int32), 0)
    g = aff
    for b in range(max(1, (n - 1).bit_length())):
        sh = 1 << b
        dist_s = pltpu.roll(dist, n - sh, axis=1)
        take = (dist_s & sh) != 0
        leave = (dist & sh) != 0
        tok = jnp.where(take, pltpu.roll(tok, n - sh, axis=1), tok)
        g = jnp.where(take, pltpu.roll(g, n - sh, axis=1), g)
        dist = jnp.where(take, dist_s, jnp.where(leave, 0, dist))
    idx_ref[...] = tok[:, :cap]
    gate_ref[...] = g[:, :cap]


def _route(aff_t, cap):
    return pl.pallas_call(
        functools.partial(_route_kernel, cap),
        in_specs=[pl.BlockSpec(memory_space=pltpu.VMEM)],
        out_specs=[pl.BlockSpec(memory_space=pltpu.VMEM)] * 2,
        out_shape=[jax.ShapeDtypeStruct((N_EXPERTS, cap), jnp.int32), jax.ShapeDtypeStruct((N_EXPERTS, cap), f32)],
        compiler_params=pltpu.CompilerParams(vmem_limit_bytes=VMEM_LIMIT),
        name="route",
    )(aff_t)


GATHER_UNROLL = 8
SCATTER_UNROLL = 8


def _moe_ffn_kernel(cap, stride, idx_ref, h_ref, g_ref, w1_ref, w3_ref, w2_ref, y_ref, tile_ref, xb_ref, acc_ref):
    base = pl.program_id(0) * cap

    def gather(q, carry):
        for t in range(GATHER_UNROLL):
            r = q * GATHER_UNROLL + t
            src = pl.multiple_of(idx_ref[base + r] * ROW_CHUNKS, ROW_CHUNKS)
            tile_ref[pl.ds(r, ROW_CHUNKS, stride=stride), :] = h_ref[pl.ds(src, ROW_CHUNKS), :]
        return carry
    lax.fori_loop(0, cap // GATHER_UNROLL, gather, 0)
    xb_ref[...] = jnp.concatenate(
        [tile_ref[j * stride:j * stride + cap, :] for j in range(ROW_CHUNKS)], axis=1).astype(bf16)

    xb = xb_ref[...]
    for f in range(D_EXPERT // MOE_F_TILE):
        cols = slice(f * MOE_F_TILE, (f + 1) * MOE_F_TILE)
        a = jnp.dot(xb, w1_ref[0, 0, :, cols].astype(bf16), preferred_element_type=f32)
        b = jnp.dot(xb, w3_ref[0, 0, :, cols].astype(bf16), preferred_element_type=f32)
        he = (a * _sigmoid(a) * b).astype(bf16)
        part = jnp.dot(he, w2_ref[0, 0, cols, :].astype(bf16), preferred_element_type=f32)
        if f == 0:
            acc_ref[...] = part
        else:
            acc_ref[...] += part

    y = acc_ref[...] * g_ref[0]
    for j in range(ROW_CHUNKS):
        y_ref[0, j * stride:j * stride + cap, :] = y[:, j * LANES:(j + 1) * LANES]
        y_ref[0, j * stride + cap:(j + 1) * stride, :] = jnp.zeros((stride - cap, LANES), f32)


def _moe_ffn(layer, h3, idx, gate, w1, w3, w2):
    cap = idx.shape[1]
    stride = cap + SUBLANES
    grid_spec = pltpu.PrefetchScalarGridSpec(
        num_scalar_prefetch=1,
        grid=(N_EXPERTS,),
        in_specs=[
            pl.BlockSpec(memory_space=pltpu.VMEM),
            pl.BlockSpec((1, cap, 1), lambda e, idx: (e, 0, 0)),
            pl.BlockSpec((1, 1, D_MODEL, D_EXPERT), lambda e, idx: (layer, e, 0, 0)),
            pl.BlockSpec((1, 1, D_MODEL, D_EXPERT), lambda e, idx: (layer, e, 0, 0)),
            pl.BlockSpec((1, 1, D_EXPERT, D_MODEL), lambda e, idx: (layer, e, 0, 0)),
        ],
        out_specs=pl.BlockSpec((1, ROW_CHUNKS * stride, LANES), lambda e, idx: (e, 0, 0)),
        scratch_shapes=[pltpu.VMEM((ROW_CHUNKS * stride, LANES), f32),
                        pltpu.VMEM((cap, D_MODEL), bf16),
                        pltpu.VMEM((cap, D_MODEL), f32)],
    )
    return pl.pallas_call(
        functools.partial(_moe_ffn_kernel, cap, stride),
        grid_spec=grid_spec,
        out_shape=jax.ShapeDtypeStruct((N_EXPERTS, ROW_CHUNKS * stride, LANES), f32),
        compiler_params=_params("arbitrary"),
        name="moe_ffn",
    )(idx.reshape(-1), h3, gate.reshape(N_EXPERTS, cap, 1), w1, w3, w2)


def _moe_combine_kernel(cap, stride, final, idx_ref, y_ref, x_ref, mod_ref, g_ref, o_ref, acc_ref):
    s = pl.program_id(0)

    @pl.when(s == 0)
    def _():
        acc_ref[...] = jnp.zeros_like(acc_ref)

    @pl.when(s < N_EXPERTS)
    def _():
        base = s * cap

        def scatter(q, carry):
            rows = [pl.multiple_of(idx_ref[base + q * SCATTER_UNROLL + t] * ROW_CHUNKS, ROW_CHUNKS)
                    for t in range(SCATTER_UNROLL)]
            new = [acc_ref[pl.ds(rows[t], ROW_CHUNKS), :]
                   + y_ref[0, pl.ds(q * SCATTER_UNROLL + t, ROW_CHUNKS, stride=stride), :]
                   for t in range(SCATTER_UNROLL)]
            for t in range(SCATTER_UNROLL):
                acc_ref[pl.ds(rows[t], ROW_CHUNKS), :] = new[t]
            return carry
        lax.fori_loop(0, cap // SCATTER_UNROLL, scatter, 0)

    @pl.when(s >= N_EXPERTS)
    def _():
        first = pl.multiple_of((s - N_EXPERTS) * (TM * ROW_CHUNKS), TM * ROW_CHUNKS)
        moe = jnp.concatenate([acc_ref[pl.ds(first + j, TM, stride=ROW_CHUNKS), :] for j in range(ROW_CHUNKS)],
                              axis=1)
        x = x_ref[...] + mod_ref[0][5:6] * moe
        o_ref[...] = _rms(x) * g_ref[...] if final else x


def _moe_combine(idx, y_cm, x1, mod, final_g, final):
    n = x1.shape[0]
    cap = idx.shape[1]
    stride = y_cm.shape[1] // ROW_CHUNKS
    per_mod = n // mod.shape[0]
    tile = lambda s: jnp.maximum(s - N_EXPERTS, 0)
    row = pl.BlockSpec((TM, D_MODEL), lambda s, idx: (tile(s), 0))
    grid_spec = pltpu.PrefetchScalarGridSpec(
        num_scalar_prefetch=1,
        grid=(N_EXPERTS + n // TM,),
        in_specs=[pl.BlockSpec((1, ROW_CHUNKS * stride, LANES), lambda s, idx: (jnp.minimum(s, N_EXPERTS - 1), 0, 0)),
                  row,
                  pl.BlockSpec((1, 6, D_MODEL), lambda s, idx: (tile(s) * TM // per_mod, 0, 0)),
                  pl.BlockSpec((1, D_MODEL), lambda s, idx: (0, 0))],
        out_specs=row,
        scratch_shapes=[pltpu.VMEM((n * ROW_CHUNKS, LANES), f32)],
    )
    return pl.pallas_call(
        functools.partial(_moe_combine_kernel, cap, stride, final),
        grid_spec=grid_spec,
        out_shape=jax.ShapeDtypeStruct((n, D_MODEL), f32),
        compiler_params=_params("arbitrary"),
        name="moe_combine",
    )(idx.reshape(-1), y_cm, x1, mod, final_g.reshape(1, D_MODEL))


def _expert_choice(layer, h3, aff_t, x1, mod, final_g, final, w1, w3, w2):
    n = x1.shape[0]
    cap = max(1, EC_CAPACITY * n // N_EXPERTS)
    idx, gate = _route(aff_t, cap)
    y_cm = _moe_ffn(layer, h3, idx, gate, w1, w3, w2)
    return _moe_combine(idx, y_cm, x1, mod, final_g, final)


def _stream(x, mods, layers, experts, batch, seq_len, final_g, caches=None):
    consts = _hy_consts(seq_len)
    rope_tabs = _rope_tables(seq_len) if caches is not None else None
    kvs = []
    for l, lp in enumerate(layers):
        qkv, zh, zg = _in_proj(x, mods[l], lp["norm1_g"], lp["w_in"], seq_len, rope_tabs=rope_tabs)
        if caches is None:
            a = _ctx_attention(qkv, lp["attn_sink"], batch, seq_len)
            kvs.append((qkv[:, ATTN_W:ATTN_W + KV_W], qkv[:, ATTN_W + KV_W:]))
        else:
            a = _lat_attention(qkv, lp["attn_sink"], caches[0][:, l], caches[1][:, l], batch, seq_len)
        yh = _hyena(zh, lp, consts, batch, seq_len)
        x1, h3, aff_t = _out_proj(a, yh, zg, lp["gm_ws"], lp["gm_b"], x, mods[l], lp["out_norm_g"], lp["w_out"],
                                  lp["norm2_g"], lp["router_wt"])
        x = _expert_choice(l, h3, aff_t, x1, mods[l], final_g, l == len(layers) - 1, *experts)
    return x, kvs


def kernel(x_prompt, x_sample, c, cache_k, cache_v, c_ctx, norm1_g, norm2_g, ada_w, ada_b, w_in, attn_sink,
           hy_conv_w, hy_conv_b, hy_f_w1, hy_f_b1, hy_f_w2, hy_f_b2, hy_f_w3, hy_freq, hy_bias, gm_ws, gm_b,
           out_norm_g, w_out, router_w, exp_w1, exp_w3, exp_w2, final_g):
    batch, seq, _ = x_prompt.shape
    dbatch, dseq, _ = x_sample.shape
    past = cache_k.shape[2]

    cvec = jnp.zeros((SUBLANES, D_MODEL), f32).at[0].set(c_ctx).at[1:1 + dbatch].set(c)
    mod = _ada(cvec, ada_w, ada_b)
    mods_ctx = [mod[l, 0:1].reshape(1, 6, D_MODEL) for l in range(DEPTH)]
    mods_lat = [mod[l, 1:1 + dbatch].reshape(dbatch, 6, D_MODEL) for l in range(DEPTH)]

    w_in_bf, w_out_bf, gm_ws_bf = w_in.astype(bf16), w_out.astype(bf16), gm_ws.astype(bf16)
    layers = []
    for l in range(DEPTH):
        layers.append({
            "norm1_g": norm1_g[l], "norm2_g": norm2_g[l], "w_in": w_in_bf[l], "attn_sink": attn_sink[l],
            "hy_conv_w": hy_conv_w[l], "hy_conv_b": hy_conv_b[l], "hy_f_w1": hy_f_w1[l], "hy_f_b1": hy_f_b1[l],
            "hy_f_w2": hy_f_w2[l], "hy_f_b2": hy_f_b2[l], "hy_f_w3": hy_f_w3[l], "hy_freq": hy_freq[l],
            "hy_bias": hy_bias[l], "gm_ws": gm_ws_bf[l], "gm_b": gm_b[l], "out_norm_g": out_norm_g[l],
            "w_out": w_out_bf[l], "router_wt": router_w[l].T})
    experts = (exp_w1, exp_w3, exp_w2)

    yp, kvs = _stream(x_prompt.reshape(batch * seq, D_MODEL), mods_ctx, layers, experts, batch, seq, final_g)
    caches = (cache_k.reshape(dbatch, DEPTH, past, KV_W), cache_v.reshape(dbatch, DEPTH, past, KV_W))
    ys, _ = _stream(x_sample.reshape(dbatch * dseq, D_MODEL), mods_lat, layers, experts, dbatch, dseq, final_g,
                   caches=caches)

    new_k = jnp.stack([k.reshape(batch, seq, N_KV, HEAD_DIM) for k, _ in kvs], axis=1)
    new_v = jnp.stack([v.reshape(batch, seq, N_KV, HEAD_DIM) for _, v in kvs], axis=1)
    return (yp.reshape(batch, seq, D_MODEL), ys.reshape(dbatch, dseq, D_MODEL), new_k, new_v)
```

```python
import functools
import math

import numpy as np
import jax
import jax.numpy as jnp
from jax import lax
from jax.experimental import pallas as pl
from jax.experimental.pallas import tpu as pltpu

f32 = jnp.float32
bf16 = jnp.bfloat16

D_MODEL = 1024
DEPTH = 2
GRID_W = 64
BLOCK = 128
N_HEADS = 8
N_KV = 2
HEAD_DIM = 64
Q_GROUP = N_HEADS // N_KV
ATTN_W = N_HEADS * HEAD_DIM
KV_W = N_KV * HEAD_DIM
QKV_W = ATTN_W + 2 * KV_W
HY_W = 256
GM_W = 256
GM_HEADS = 4
GM_HEAD_DIM = GM_W // GM_HEADS
GM_CHUNK = 128
MIX_W = ATTN_W + HY_W + GM_W
IN_W = ATTN_W + 2 * KV_W + 3 * HY_W + 2 * GM_W
FILTER_EMB = 33
FILTER_HID = 64
HY_DECAY_HI_PCT = 0.3
HY_DECAY_LO_PCT = 1.5
HY_DECAY_TARGET = 1e-2
N_EXPERTS = 16
EC_CAPACITY = 2
D_EXPERT = 1024
ROPE_THETA = 10000.0
EPS = 1e-6
NEG = -1e30

LANES = 128
SUBLANES = 8
ROW_CHUNKS = D_MODEL // LANES
VMEM_LIMIT = 56 * 1024 * 1024
TM = 512
MOE_F_TILE = 256


def _params(*sem):
    return pltpu.CompilerParams(dimension_semantics=sem, vmem_limit_bytes=VMEM_LIMIT)


def _rms(x):
    return x * lax.rsqrt(jnp.mean(x * x, axis=-1, keepdims=True) + EPS)


def _sigmoid(x):
    return 1.0 / (1.0 + jnp.exp(-x))


def _ada_kernel(c_ref, w_ref, b_ref, o_ref):
    c = c_ref[...]
    s = (c * _sigmoid(c)).astype(bf16)
    o_ref[0] = jnp.dot(s, w_ref[0].astype(bf16), preferred_element_type=f32) + b_ref[0]


def _ada(cvec, ada_w, ada_b):
    nt = 6
    return pl.pallas_call(
        _ada_kernel,
        grid=(DEPTH, nt),
        in_specs=[
            pl.BlockSpec((SUBLANES, D_MODEL), lambda l, j: (0, 0)),
            pl.BlockSpec((1, D_MODEL, D_MODEL), lambda l, j: (l, 0, j)),
            pl.BlockSpec((1, 1, D_MODEL), lambda l, j: (l, 0, j)),
        ],
        out_specs=pl.BlockSpec((1, SUBLANES, D_MODEL), lambda l, j: (l, 0, j)),
        out_shape=jax.ShapeDtypeStruct((DEPTH, SUBLANES, 6 * D_MODEL), f32),
        compiler_params=_params("arbitrary", "arbitrary"),
        name="ada",
    )(cvec, ada_w, ada_b.reshape(DEPTH, 1, 6 * D_MODEL))


def _rope_swap(x):
    w = x.shape[-1]
    lane = lax.broadcasted_iota(jnp.int32, x.shape, 1)
    first = (lane % 32) < 16
    return jnp.where(first, pltpu.roll(x, w - 16, axis=1), pltpu.roll(x, 16, axis=1))


def _in_kernel(rope, x_ref, mod_ref, g_ref, w_ref, *refs):
    if rope:
        cos_ref, sin_ref, qkv_ref, zh_ref, zg_ref = refs
    else:
        qkv_ref, zh_ref, zg_ref = refs
    m = mod_ref[0]
    h = _rms(x_ref[...]) * g_ref[...] * (1.0 + m[1:2]) + m[0:1]
    z = jnp.dot(h.astype(bf16), w_ref[...], preferred_element_type=f32)
    if rope:
        reps = (ATTN_W + KV_W) // LANES
        cos = jnp.concatenate([cos_ref[...]] * reps, axis=1)
        sin = jnp.concatenate([sin_ref[...]] * reps, axis=1)
        qk = z[:, :ATTN_W + KV_W]
        qkv_ref[:, :ATTN_W + KV_W] = qk * cos + _rope_swap(qk) * sin
        qkv_ref[:, ATTN_W + KV_W:] = z[:, ATTN_W + KV_W:QKV_W]
    else:
        qkv_ref[...] = z[:, :QKV_W]
    zh_ref[...] = z[:, QKV_W:QKV_W + 3 * HY_W].astype(bf16)
    zg_ref[...] = z[:, QKV_W + 3 * HY_W:].astype(bf16)


def _in_proj(x, mod, g1, w_in_bf, seq_len, rope_tabs=None):
    n = x.shape[0]
    rope = rope_tabs is not None
    per_mod = n // mod.shape[0]
    in_specs = [pl.BlockSpec((TM, D_MODEL), lambda i: (i, 0)),
                pl.BlockSpec((1, 6, D_MODEL), lambda i: (i * TM // per_mod, 0, 0)),
                pl.BlockSpec((1, D_MODEL), lambda i: (0, 0)),
                pl.BlockSpec((D_MODEL, IN_W), lambda i: (0, 0))]
    args = [x, mod, g1.reshape(1, D_MODEL), w_in_bf]
    if rope:
        nt = seq_len // TM
        tab = pl.BlockSpec((TM, LANES), lambda i: (i % nt, 0))
        in_specs += [tab, tab]
        args += list(rope_tabs)
    return pl.pallas_call(
        functools.partial(_in_kernel, rope),
        grid=(n // TM,),
        in_specs=in_specs,
        out_specs=[pl.BlockSpec((TM, w), lambda i: (i, 0)) for w in (QKV_W, 3 * HY_W, 2 * GM_W)],
        out_shape=[jax.ShapeDtypeStruct((n, QKV_W), f32), jax.ShapeDtypeStruct((n, 3 * HY_W), bf16),
                   jax.ShapeDtypeStruct((n, 2 * GM_W), bf16)],
        compiler_params=_params("parallel"),
        name="in_proj",
    )(*args)


def _rope_tables(seq_len):
    nf = HEAD_DIM // 4
    t = np.arange(seq_len)
    inv = (ROPE_THETA ** (-np.arange(nf, dtype=np.float32) / nf)).astype(np.float32)
    d = np.arange(HEAD_DIM)
    pos = np.where((d // 32)[None, :] == 0, (t // GRID_W)[:, None], (t % GRID_W)[:, None]).astype(np.float32)
    ang = (pos * inv[d % nf][None, :]).astype(np.float32)
    cos = np.cos(ang).astype(np.float32)
    sin = np.sin(ang).astype(np.float32) * np.where((d % 32) < 16, -1.0, 1.0)[None, :].astype(np.float32)
    return jnp.asarray(np.tile(cos, (1, 2))), jnp.asarray(np.tile(sin, (1, 2)))


LOG2E = math.log2(math.e)


def _stack_queries(z, sink_ref, kv):
    rows = z.shape[0]
    heads = range(kv * Q_GROUP, (kv + 1) * Q_GROUP)
    qs = jnp.concatenate([z[:, h * HEAD_DIM:(h + 1) * HEAD_DIM] for h in heads], axis=0)
    sink = jnp.concatenate([jnp.full((1, rows), sink_ref[h] * LOG2E, f32) for h in heads], axis=1)
    return (qs * (LOG2E / math.sqrt(HEAD_DIM))).astype(bf16), sink


def _scores_t(k, qs_bf):
    return lax.dot_general(k.astype(bf16), qs_bf, (((1,), (1,)), ((), ())), preferred_element_type=f32)


def _softmax_pv_t(st, sink, vals, rows):
    m = jnp.maximum(jnp.max(st, axis=0, keepdims=True), sink)
    pt = jnp.exp2(st - m).astype(bf16)
    v_bf = vals.astype(bf16)
    v_ext = jnp.concatenate([v_bf, jnp.ones_like(v_bf)], axis=1)
    ovt = lax.dot_general(v_ext, pt, (((0,), (0,)), ((), ())), preferred_element_type=f32)
    ot = ovt[:HEAD_DIM] / (ovt[HEAD_DIM:HEAD_DIM + 1] + jnp.exp2(sink - m))
    return [ot[:, g * rows:(g + 1) * rows].T for g in range(Q_GROUP)]


def _ctx_attn_kernel(sink_ref, z_ref, o_ref):
    z = z_ref[...]
    outs = []
    for kv in range(N_KV):
        k = z[:, ATTN_W + kv * HEAD_DIM:ATTN_W + (kv + 1) * HEAD_DIM]
        v = z[:, ATTN_W + KV_W + kv * HEAD_DIM:ATTN_W + KV_W + (kv + 1) * HEAD_DIM]
        qs, sink = _stack_queries(z, sink_ref, kv)
        outs += _softmax_pv_t(_scores_t(k, qs), sink, v, z.shape[0])
    o_ref[...] = jnp.concatenate(outs, axis=1)


def _ctx_attention(z, sink, batch, seq_len):
    return pl.pallas_call(
        _ctx_attn_kernel,
        grid=(batch,),
        in_specs=[pl.BlockSpec(memory_space=pltpu.SMEM),
                  pl.BlockSpec((seq_len, QKV_W), lambda b: (b, 0))],
        out_specs=pl.BlockSpec((seq_len, ATTN_W), lambda b: (b, 0)),
        out_shape=jax.ShapeDtypeStruct((batch * seq_len, ATTN_W), f32),
        compiler_params=_params("parallel"),
        name="ctx_attn",
    )(sink, z)


def _lat_attn_kernel(nb, sink_ref, zp_ref, zc_ref, zn_ref, ck_ref, cv_ref, o_ref):
    i = pl.program_id(1)
    zp, zc, zn = zp_ref[...], zc_ref[...], zn_ref[...]
    ck, cv = ck_ref[0], cv_ref[0]
    width = Q_GROUP * BLOCK
    j = lax.broadcasted_iota(jnp.int32, (BLOCK, width), 0)
    r = lax.broadcasted_iota(jnp.int32, (BLOCK, width), 1) % BLOCK
    ok_prev = j >= r + jnp.where(i > 0, 0, BLOCK)
    ok_next = j <= r - jnp.where(i < nb - 1, 0, BLOCK)
    outs = []
    for kv in range(N_KV):
        ks = slice(ATTN_W + kv * HEAD_DIM, ATTN_W + (kv + 1) * HEAD_DIM)
        vs = slice(ATTN_W + KV_W + kv * HEAD_DIM, ATTN_W + KV_W + (kv + 1) * HEAD_DIM)
        cs = slice(kv * HEAD_DIM, (kv + 1) * HEAD_DIM)
        qs, sink = _stack_queries(zc, sink_ref, kv)
        st = jnp.concatenate([
            jnp.where(ok_prev, _scores_t(zp[:, ks], qs), NEG),
            _scores_t(zc[:, ks], qs),
            jnp.where(ok_next, _scores_t(zn[:, ks], qs), NEG),
            _scores_t(ck[:, cs], qs)], axis=0)
        vals = jnp.concatenate([zp[:, vs], zc[:, vs], zn[:, vs], cv[:, cs]], axis=0)
        outs += _softmax_pv_t(st, sink, vals, BLOCK)
    o_ref[...] = jnp.concatenate(outs, axis=1)


def _lat_attention(z, sink, ck, cv, batch, seq_len):
    nb = seq_len // BLOCK
    blk = lambda f: pl.BlockSpec((BLOCK, QKV_W), f)
    past = ck.shape[1]
    cache = pl.BlockSpec((1, past, KV_W), lambda b, i: (b, 0, 0))
    return pl.pallas_call(
        functools.partial(_lat_attn_kernel, nb),
        grid=(batch, nb),
        in_specs=[pl.BlockSpec(memory_space=pltpu.SMEM),
                  blk(lambda b, i: (b * nb + jnp.maximum(i - 1, 0), 0)),
                  blk(lambda b, i: (b * nb + i, 0)),
                  blk(lambda b, i: (b * nb + jnp.minimum(i + 1, nb - 1), 0)),
                  cache, cache],
        out_specs=pl.BlockSpec((BLOCK, ATTN_W), lambda b, i: (b * nb + i, 0)),
        out_shape=jax.ShapeDtypeStruct((batch * seq_len, ATTN_W), f32),
        compiler_params=_params("parallel", "parallel"),
        name="lat_attn",
    )(sink, z, z, z, ck, cv)


HY_TL = 256
HALO_ROWS = 2 * SUBLANES


def _short_conv(z, prev_tile, next_tile, has_prev, has_next, w, b):
    z = z.astype(f32)
    rows = z.shape[0]
    row = lax.broadcasted_iota(jnp.int32, z.shape, 0)
    prev_row = jnp.where(has_prev, prev_tile.astype(f32)[HALO_ROWS - 1:HALO_ROWS, :], 0.0)
    next_row = jnp.where(has_next, next_tile.astype(f32)[0:1, :], 0.0)
    z_prev = jnp.where(row == 0, prev_row, pltpu.roll(z, 1, axis=0))
    z_next = jnp.where(row == rows - 1, next_row, pltpu.roll(z, rows - 1, axis=0))
    zc = z_prev * w[0:1] + z * w[1:2] + z_next * w[2:3] + b
    return zc[:, :HY_W], zc[:, HY_W:2 * HY_W] * zc[:, 2 * HY_W:]


def _hy_filter_kernel(seq_len, feats_ref, win_ref, w1_ref, b1_ref, w2_ref, b2_ref, w3_ref, fr_ref, o_ref):
    hi = lax.Precision.HIGHEST
    fr = fr_ref[...]
    h = jnp.sin(fr * (jnp.dot(feats_ref[...], w1_ref[...], precision=hi, preferred_element_type=f32) + b1_ref[...]))
    h = jnp.sin(fr * (jnp.dot(h, w2_ref[...], precision=hi, preferred_element_type=f32) + b2_ref[...]))
    h = jnp.dot(h, w3_ref[...], precision=hi, preferred_element_type=f32)
    win = win_ref[...]
    row = lax.broadcasted_iota(jnp.int32, (seq_len, HY_W), 0)
    o_ref[0:seq_len, :] = h[:, :HY_W] * win
    o_ref[seq_len:, :] = jnp.where(row == 0, 0.0, h[:, HY_W:] * win)


def _hy_filter(seq_len, consts, w1, b1, w2, b2, w3, freq):
    w1p = jnp.zeros((LANES, FILTER_HID), f32).at[:FILTER_EMB].set(w1)
    args = (consts["feats"], consts["window"], w1p, b1.reshape(1, -1), w2, b2.reshape(1, -1), w3, freq.reshape(1, -1))
    return pl.pallas_call(
        functools.partial(_hy_filter_kernel, seq_len),
        in_specs=[pl.BlockSpec(memory_space=pltpu.VMEM)] * len(args),
        out_specs=pl.BlockSpec(memory_space=pltpu.VMEM),
        out_shape=jax.ShapeDtypeStruct((2 * seq_len, HY_W), f32),
        compiler_params=pltpu.CompilerParams(vmem_limit_bytes=VMEM_LIMIT),
        name="hy_filter",
    )(*args)


def _hy_tiles(seq_len):
    return min(TM, seq_len // 2), min(TM, seq_len)


def _hy_fwd_kernel(nb, tk, seq_len, m_ref, zh_ref, cw_ref, cb_ref, hb_ref, r_ref, z_ref, rhs_ref):
    i = pl.program_id(0)
    nc = seq_len // HY_TL

    @pl.when(i == 0)
    def _():
        w, bias = cw_ref[...], cb_ref[...]
        for b in range(nb):
            def chunk(c, carry, b=b):
                r0 = pl.multiple_of(c * HY_TL, HY_TL)
                prev = zh_ref[b, pl.ds(pl.multiple_of(jnp.maximum(r0 - HALO_ROWS, 0), HALO_ROWS), HALO_ROWS), :]
                nxt = zh_ref[b, pl.ds(pl.multiple_of(jnp.minimum(r0 + HY_TL, seq_len - HALO_ROWS), HALO_ROWS),
                                      HALO_ROWS), :]
                _, u = _short_conv(zh_ref[b, pl.ds(r0, HY_TL), :], prev, nxt, c > 0, c < nc - 1, w, bias)
                rhs_ref[pl.ds(r0, HY_TL), b * HY_W:(b + 1) * HY_W] = u.astype(bf16)
                return carry
            lax.fori_loop(0, nc, chunk, 0)
        rhs_ref[:, nb * HY_W:(nb + 1) * HY_W] = hb_ref[0].astype(bf16)
        rhs_ref[:, (nb + 1) * HY_W:] = hb_ref[1].astype(bf16)

    acc = jnp.dot(m_ref[...], rhs_ref[...], preferred_element_type=f32)
    p, w = acc[:tk], acc[tk:]
    c0 = nb * HY_W
    pf, pb = p[:, c0:c0 + HY_W], p[:, c0 + HY_W:]
    qf, qb = w[:, c0:c0 + HY_W], w[:, c0 + HY_W:]
    hr = pf + pb
    qs = qf + qb
    hi = qb - qf
    first = (lax.broadcasted_iota(jnp.int32, (tk, HY_W), 0) == 0) & (i == 0)
    for b in range(nb):
        pu, wu = p[:, b * HY_W:(b + 1) * HY_W], w[:, b * HY_W:(b + 1) * HY_W]
        r_ref[b] = jnp.where(first, pu * hr, 2.0 * (pu * hr + wu * hi)).astype(bf16)
        z_ref[b] = jnp.where(first, wu * qs, 2.0 * (wu * hr - pu * hi)).astype(bf16)


def _hy_fwd(mat, zh, conv_w, conv_b, hb, batch, seq_len):
    tk, _ = _hy_tiles(seq_len)
    out = pl.BlockSpec((batch, tk, HY_W), lambda i: (0, i, 0))
    return pl.pallas_call(
        functools.partial(_hy_fwd_kernel, batch, tk, seq_len),
        grid=(seq_len // tk,),
        in_specs=[pl.BlockSpec((2 * tk, seq_len), lambda i: (i, 0)),
                  pl.BlockSpec((batch, seq_len, 3 * HY_W), lambda i: (0, 0, 0)),
                  pl.BlockSpec((3, 3 * HY_W), lambda i: (0, 0)),
                  pl.BlockSpec((1, 3 * HY_W), lambda i: (0, 0)),
                  pl.BlockSpec((2, seq_len, HY_W), lambda i: (0, 0, 0))],
        out_specs=[out, out],
        out_shape=[jax.ShapeDtypeStruct((batch, seq_len, HY_W), bf16)] * 2,
        scratch_shapes=[pltpu.VMEM((seq_len, (batch + 2) * HY_W), bf16)],
        compiler_params=_params("arbitrary"),
        name="hy_fwd",
    )(mat, zh, conv_w, conv_b.reshape(1, 3 * HY_W), hb)


def _hy_inv_kernel(nb, seq_len, nt, m_ref, r_ref, z_ref, zp_ref, zc_ref, zn_ref, cw_ref, cb_ref, d_ref, o_ref,
                   rhs_ref):
    i = pl.program_id(0)

    @pl.when(i == 0)
    def _():
        for b in range(nb):
            rhs_ref[:seq_len, b * HY_W:(b + 1) * HY_W] = r_ref[b]
            rhs_ref[seq_len:, b * HY_W:(b + 1) * HY_W] = z_ref[b]

    acc = jnp.dot(m_ref[...], rhs_ref[...], preferred_element_type=f32)
    d, w, bias = d_ref[...], cw_ref[...], cb_ref[...]
    for b in range(nb):
        x0, u = _short_conv(zc_ref[b], zp_ref[b], zn_ref[b], i > 0, i < nt - 1, w, bias)
        y = acc[:, b * HY_W:(b + 1) * HY_W] * (1.0 / (2 * seq_len)) + u * d
        o_ref[b] = x0 * y


def _hy_inv(mat, r, zz, zh, conv_w, conv_b, d_bias, batch, seq_len):
    _, tm = _hy_tiles(seq_len)
    nt = seq_len // tm
    rh = tm // HALO_ROWS
    last = seq_len // HALO_ROWS - 1
    full = pl.BlockSpec((batch, seq_len, HY_W), lambda i: (0, 0, 0))
    halo = lambda f: pl.BlockSpec((batch, HALO_ROWS, 3 * HY_W), f)
    return pl.pallas_call(
        functools.partial(_hy_inv_kernel, batch, seq_len, nt),
        grid=(nt,),
        in_specs=[pl.BlockSpec((tm, 2 * seq_len), lambda i: (i, 0)), full, full,
                  halo(lambda i: (0, jnp.maximum(i * rh - 1, 0), 0)),
                  pl.BlockSpec((batch, tm, 3 * HY_W), lambda i: (0, i, 0)),
                  halo(lambda i: (0, jnp.minimum((i + 1) * rh, last), 0)),
                  pl.BlockSpec((3, 3 * HY_W), lambda i: (0, 0)),
                  pl.BlockSpec((1, 3 * HY_W), lambda i: (0, 0)),
                  pl.BlockSpec((1, HY_W), lambda i: (0, 0))],
        out_specs=pl.BlockSpec((batch, tm, HY_W), lambda i: (0, i, 0)),
        out_shape=jax.ShapeDtypeStruct((batch, seq_len, HY_W), f32),
        scratch_shapes=[pltpu.VMEM((2 * seq_len, batch * HY_W), bf16)],
        compiler_params=_params("arbitrary"),
        name="hy_inv",
    )(mat, r, zz, zh, zh, zh, conv_w, conv_b.reshape(1, 3 * HY_W), d_bias.reshape(1, HY_W))


@functools.lru_cache(maxsize=None)
def _hy_consts_np(seq_len):
    n = 2 * seq_len
    k = np.arange(seq_len, dtype=np.int64)
    ang = (2.0 * np.pi / n) * ((k[:, None] * k[None, :]) % n).astype(np.float64)
    cosm = np.cos(ang)
    sinm = np.sin(ang)
    sinm[0, :] = np.where(k % 2 == 0, 1.0, -1.0)
    tk, _ = _hy_tiles(seq_len)
    fwd = np.concatenate([np.concatenate([cosm[i:i + tk], sinm[i:i + tk]], axis=0)
                          for i in range(0, seq_len, tk)], axis=0)
    inv = np.concatenate([cosm, sinm.T], axis=1)
    t = np.linspace(0.0, 1.0, seq_len, dtype=np.float32)[:, None]
    bands = (FILTER_EMB - 1) // 2
    fb = np.linspace(1e-4, bands - 1, bands, dtype=np.float32)[None, :]
    w = (np.float32(2.0 * math.pi) * np.arange(seq_len, dtype=np.float32)[:, None] / np.float32(seq_len)).astype(np.float32)
    feats = np.concatenate([t, np.cos(fb * w), -np.sin(fb * w)], axis=-1).astype(np.float32)
    feats = np.pad(feats, ((0, 0), (0, LANES - FILTER_EMB)))
    decay_hi = math.log(HY_DECAY_TARGET) / HY_DECAY_HI_PCT
    decay_lo = math.log(HY_DECAY_TARGET) / HY_DECAY_LO_PCT
    deltas = np.abs(np.linspace(decay_lo, decay_hi, HY_W, dtype=np.float32))
    window = np.exp(-t * deltas[None, :]).astype(np.float32)
    return fwd, inv, feats, window


def _hy_consts(seq_len):
    fwd, inv, feats, window = _hy_consts_np(seq_len)
    as_bf = lambda m: jnp.asarray(m, dtype=f32).astype(bf16)
    return {"fwd": as_bf(fwd), "inv": as_bf(inv), "feats": jnp.asarray(feats), "window": jnp.asarray(window)}


def _hyena(zh, lp, consts, batch, seq_len):
    zh = zh.reshape(batch, seq_len, 3 * HY_W)
    hb = _hy_filter(seq_len, consts, lp["hy_f_w1"], lp["hy_f_b1"], lp["hy_f_w2"], lp["hy_f_b2"],
                    lp["hy_f_w3"], lp["hy_freq"]).reshape(2, seq_len, HY_W)
    r, zz = _hy_fwd(consts["fwd"], zh, lp["hy_conv_w"], lp["hy_conv_b"], hb, batch, seq_len)
    yh = _hy_inv(consts["inv"], r, zz, zh, lp["hy_conv_w"], lp["hy_conv_b"], lp["hy_bias"], batch, seq_len)
    return yh.reshape(batch * seq_len, HY_W)


def _gelu(x):
    return 0.5 * x * (1.0 + jnp.tanh(math.sqrt(2.0 / math.pi) * (x + 0.044715 * (x * x * x))))


def _gmlp(z, ws_ref, bias):
    g = _gelu(z)
    outs = []
    for c in range(TM // GM_CHUNK):
        rows = slice(c * GM_CHUNK, (c + 1) * GM_CHUNK)
        u = g[rows, :GM_W]
        v = g[rows, GM_W:].astype(bf16)
        s = jnp.concatenate(
            [jnp.dot(ws_ref[h], v[:, h * GM_HEAD_DIM:(h + 1) * GM_HEAD_DIM], preferred_element_type=f32)
             for h in range(GM_HEADS)], axis=1)
        outs.append(u * (s + bias))
    return jnp.concatenate(outs, axis=0)


def _out_kernel(a_ref, yh_ref, zg_ref, ws_ref, gb_ref, x_ref, mod_ref, og_ref, w_ref, g2_ref, rw_ref,
                x1_ref, h2_ref, aff_ref):
    og = og_ref[...]
    yg = _gmlp(zg_ref[...].astype(f32), ws_ref, gb_ref[...])
    mixed_in = jnp.concatenate([
        _rms(a_ref[...]) * og[:, :ATTN_W],
        _rms(yh_ref[...]) * og[:, ATTN_W:ATTN_W + HY_W],
        _rms(yg) * og[:, ATTN_W + HY_W:]], axis=1)
    mixed = jnp.dot(mixed_in.astype(bf16), w_ref[...], preferred_element_type=f32)
    m = mod_ref[0]
    x1 = x_ref[...] + m[2:3] * mixed
    x1_ref[...] = x1
    h2 = _rms(x1) * g2_ref[...] * (1.0 + m[4:5]) + m[3:4]
    for j in range(ROW_CHUNKS):
        h2_ref[pl.ds(j, TM, stride=ROW_CHUNKS), :] = h2[:, j * LANES:(j + 1) * LANES]
    nt = (((1,), (1,)), ((), ()))
    h_hi = h2.astype(bf16)
    h_lo = (h2 - h_hi.astype(f32)).astype(bf16)
    rw = rw_ref[...]
    t = lax.dot_general(rw, h_hi, nt, preferred_element_type=f32)
    logits = (t[:N_EXPERTS] + t[N_EXPERTS:]) + lax.dot_general(rw[:N_EXPERTS], h_lo, nt, preferred_element_type=f32)
    e = jnp.exp(logits - jnp.max(logits, axis=0, keepdims=True))
    aff_ref[...] = e / jnp.sum(e, axis=0, keepdims=True)


def _out_proj(a, yh, zg, ws_bf, gm_b, x, mod, out_g, w_out_bf, g2, router_wt):
    n = x.shape[0]
    per_mod = n // mod.shape[0]
    row = lambda w: pl.BlockSpec((TM, w), lambda i: (i, 0))
    const = lambda s: pl.BlockSpec(s, lambda i: (0,) * len(s))
    gm_bias = jnp.repeat(gm_b.T, GM_HEAD_DIM, axis=1)
    return pl.pallas_call(
        _out_kernel,
        grid=(n // TM,),
        in_specs=[row(ATTN_W), row(HY_W), row(2 * GM_W), const((GM_HEADS, GM_CHUNK, GM_CHUNK)),
                  const((GM_CHUNK, GM_W)), row(D_MODEL),
                  pl.BlockSpec((1, 6, D_MODEL), lambda i: (i * TM // per_mod, 0, 0)),
                  const((1, MIX_W)), const((MIX_W, D_MODEL)), const((1, D_MODEL)), const((2 * N_EXPERTS, D_MODEL))],
        out_specs=[row(D_MODEL), pl.BlockSpec((TM * ROW_CHUNKS, LANES), lambda i: (i, 0)),
                   pl.BlockSpec((N_EXPERTS, TM), lambda i: (0, i))],
        out_shape=[jax.ShapeDtypeStruct((n, D_MODEL), f32), jax.ShapeDtypeStruct((n * ROW_CHUNKS, LANES), f32),
                   jax.ShapeDtypeStruct((N_EXPERTS, n), f32)],
        compiler_params=_params("parallel"),
        name="out_proj",
    )(a, yh, zg, ws_bf, gm_bias, x, mod, out_g.reshape(1, MIX_W), w_out_bf, g2.reshape(1, D_MODEL), router_wt)


RT_CHUNK = 512


def _prefix_incl(x01, tri):
    n = x01.shape[1]
    carry = jnp.zeros((x01.shape[0], 1), f32)
    parts = []
    for c in range(n // RT_CHUNK):
        piece = x01[:, c * RT_CHUNK:(c + 1) * RT_CHUNK]
        parts.append(jnp.dot(piece.astype(bf16), tri, preferred_element_type=f32) + carry)
        carry = carry + jnp.sum(piece, axis=1, keepdims=True)
    return jnp.concatenate(parts, axis=1)


def _route_kernel(cap, aff_ref, idx_ref, gate_ref):
    aff = aff_ref[...]
    n = aff.shape[1]

    def step(it, t):
        cand = t | (jnp.int32(1) << (30 - it))
        cnt = jnp.sum(jnp.where(aff >= lax.bitcast_convert_type(cand, f32), 1.0, 0.0), axis=1, keepdims=True)
        return jnp.where(cnt >= cap, cand, t)

    thr_bits = lax.fori_loop(0, 31, step, jnp.zeros((aff.shape[0], 1), jnp.int32))
    thr = lax.bitcast_convert_type(thr_bits, f32)
    gt = jnp.where(aff > thr, 1.0, 0.0)
    eq = jnp.where(aff == thr, 1.0, 0.0)
    room = cap - jnp.sum(gt, axis=1, keepdims=True)
    r = lax.broadcasted_iota(jnp.int32, (RT_CHUNK, RT_CHUNK), 0)
    c = lax.broadcasted_iota(jnp.int32, (RT_CHUNK, RT_CHUNK), 1)
    tri = jnp.where(r <= c, 1.0, 0.0).astype(bf16)
    sel = jnp.maximum(gt, jnp.where(_prefix_incl(eq, tri) <= room, eq, 0.0))
    slot = _prefix_incl(sel, tri) - 1.0

    tok = lax.broadcasted_iota(jnp.int32, aff.shape, 1)
    dist = jnp.where(sel > 0.0, tok - slot.astype(jnp.int32), 0)
    g = aff
    for b in range(max(1, (n - 1).bit_length())):
        sh = 1 << b
        dist_s = pltpu.roll(dist, n - sh, axis=1)
        take = (dist_s & sh) != 0
        leave = (dist & sh) != 0
        tok = jnp.where(take, pltpu.roll(tok, n - sh, axis=1), tok)
        g = jnp.where(take, pltpu.roll(g, n - sh, axis=1), g)
        dist = jnp.where(take, dist_s, jnp.where(leave, 0, dist))
    idx_ref[...] = tok[:, :cap]
    gate_ref[...] = g[:, :cap]


def _route(aff_t, cap):
    return pl.pallas_call(
        functools.partial(_route_kernel, cap),
        in_specs=[pl.BlockSpec(memory_space=pltpu.VMEM)],
        out_specs=[pl.BlockSpec(memory_space=pltpu.VMEM)] * 2,
        out_shape=[jax.ShapeDtypeStruct((N_EXPERTS, cap), jnp.int32), jax.ShapeDtypeStruct((N_EXPERTS, cap), f32)],
        compiler_params=pltpu.CompilerParams(vmem_limit_bytes=VMEM_LIMIT),
        name="route",
    )(aff_t)


GATHER_UNROLL = 8
SCATTER_UNROLL = 8


def _moe_ffn_kernel(cap, stride, idx_ref, h_ref, g_ref, w1_ref, w3_ref, w2_ref, y_ref, tile_ref, xb_ref, acc_ref):
    base = pl.program_id(0) * cap

    def gather(q, carry):
        for t in range(GATHER_UNROLL):
            r = q * GATHER_UNROLL + t
            src = pl.multiple_of(idx_ref[base + r] * ROW_CHUNKS, ROW_CHUNKS)
            tile_ref[pl.ds(r, ROW_CHUNKS, stride=stride), :] = h_ref[pl.ds(src, ROW_CHUNKS), :]
        return carry
    lax.fori_loop(0, cap // GATHER_UNROLL, gather, 0)
    xb_ref[...] = jnp.concatenate(
        [tile_ref[j * stride:j * stride + cap, :] for j in range(ROW_CHUNKS)], axis=1).astype(bf16)

    xb = xb_ref[...]
    for f in range(D_EXPERT // MOE_F_TILE):
        cols = slice(f * MOE_F_TILE, (f + 1) * MOE_F_TILE)
        a = jnp.dot(xb, w1_ref[0, 0, :, cols].astype(bf16), preferred_element_type=f32)
        b = jnp.dot(xb, w3_ref[0, 0, :, cols].astype(bf16), preferred_element_type=f32)
        he = (a * _sigmoid(a) * b).astype(bf16)
        part = jnp.dot(he, w2_ref[0, 0, cols, :].astype(bf16), preferred_element_type=f32)
        if f == 0:
            acc_ref[...] = part
        else:
            acc_ref[...] += part

    y = acc_ref[...] * g_ref[0]
    for j in range(ROW_CHUNKS):
        y_ref[0, j * stride:j * stride + cap, :] = y[:, j * LANES:(j + 1) * LANES]
        y_ref[0, j * stride + cap:(j + 1) * stride, :] = jnp.zeros((stride - cap, LANES), f32)


def _moe_ffn(layer, h3, idx, gate, w1, w3, w2):
    cap = idx.shape[1]
    stride = cap + SUBLANES
    grid_spec = pltpu.PrefetchScalarGridSpec(
        num_scalar_prefetch=1,
        grid=(N_EXPERTS,),
        in_specs=[
            pl.BlockSpec(memory_space=pltpu.VMEM),
            pl.BlockSpec((1, cap, 1), lambda e, idx: (e, 0, 0)),
            pl.BlockSpec((1, 1, D_MODEL, D_EXPERT), lambda e, idx: (layer, e, 0, 0)),
            pl.BlockSpec((1, 1, D_MODEL, D_EXPERT), lambda e, idx: (layer, e, 0, 0)),
            pl.BlockSpec((1, 1, D_EXPERT, D_MODEL), lambda e, idx: (layer, e, 0, 0)),
        ],
        out_specs=pl.BlockSpec((1, ROW_CHUNKS * stride, LANES), lambda e, idx: (e, 0, 0)),
        scratch_shapes=[pltpu.VMEM((ROW_CHUNKS * stride, LANES), f32),
                        pltpu.VMEM((cap, D_MODEL), bf16),
                        pltpu.VMEM((cap, D_MODEL), f32)],
    )
    return pl.pallas_call(
        functools.partial(_moe_ffn_kernel, cap, stride),
        grid_spec=grid_spec,
        out_shape=jax.ShapeDtypeStruct((N_EXPERTS, ROW_CHUNKS * stride, LANES), f32),
        compiler_params=_params("arbitrary"),
        name="moe_ffn",
    )(idx.reshape(-1), h3, gate.reshape(N_EXPERTS, cap, 1), w1, w3, w2)


def _moe_combine_kernel(cap, stride, final, idx_ref, y_ref, x_ref, mod_ref, g_ref, o_ref, acc_ref):
    s = pl.program_id(0)

    @pl.when(s == 0)
    def _():
        acc_ref[...] = jnp.zeros_like(acc_ref)

    @pl.when(s < N_EXPERTS)
    def _():
        base = s * cap

        def scatter(q, carry):
            rows = [pl.multiple_of(idx_ref[base + q * SCATTER_UNROLL + t] * ROW_CHUNKS, ROW_CHUNKS)
                    for t in range(SCATTER_UNROLL)]
            new = [acc_ref[pl.ds(rows[t], ROW_CHUNKS), :]
                   + y_ref[0, pl.ds(q * SCATTER_UNROLL + t, ROW_CHUNKS, stride=stride), :]
                   for t in range(SCATTER_UNROLL)]
            for t in range(SCATTER_UNROLL):
                acc_ref[pl.ds(rows[t], ROW_CHUNKS), :] = new[t]
            return carry
        lax.fori_loop(0, cap // SCATTER_UNROLL, scatter, 0)

    @pl.when(s >= N_EXPERTS)
    def _():
        first = pl.multiple_of((s - N_EXPERTS) * (TM * ROW_CHUNKS), TM * ROW_CHUNKS)
        moe = jnp.concatenate([acc_ref[pl.ds(first + j, TM, stride=ROW_CHUNKS), :] for j in range(ROW_CHUNKS)],
                              axis=1)
        x = x_ref[...] + mod_ref[0][5:6] * moe
        o_ref[...] = _rms(x) * g_ref[...] if final else x


def _moe_combine(idx, y_cm, x1, mod, final_g, final):
    n = x1.shape[0]
    cap = idx.shape[1]
    stride = y_cm.shape[1] // ROW_CHUNKS
    per_mod = n // mod.shape[0]
    tile = lambda s: jnp.maximum(s - N_EXPERTS, 0)
    row = pl.BlockSpec((TM, D_MODEL), lambda s, idx: (tile(s), 0))
    grid_spec = pltpu.PrefetchScalarGridSpec(
        num_scalar_prefetch=1,
        grid=(N_EXPERTS + n // TM,),
        in_specs=[pl.BlockSpec((1, ROW_CHUNKS * stride, LANES), lambda s, idx: (jnp.minimum(s, N_EXPERTS - 1), 0, 0)),
                  row,
                  pl.BlockSpec((1, 6, D_MODEL), lambda s, idx: (tile(s) * TM // per_mod, 0, 0)),
                  pl.BlockSpec((1, D_MODEL), lambda s, idx: (0, 0))],
        out_specs=row,
        scratch_shapes=[pltpu.VMEM((n * ROW_CHUNKS, LANES), f32)],
    )
    return pl.pallas_call(
        functools.partial(_moe_combine_kernel, cap, stride, final),
        grid_spec=grid_spec,
        out_shape=jax.ShapeDtypeStruct((n, D_MODEL), f32),
        compiler_params=_params("arbitrary"),
        name="moe_combine",
    )(idx.reshape(-1), y_cm, x1, mod, final_g.reshape(1, D_MODEL))


def _expert_choice(layer, h3, aff_t, x1, mod, final_g, final, w1, w3, w2):
    n = x1.shape[0]
    cap = max(1, EC_CAPACITY * n // N_EXPERTS)
    idx, gate = _route(aff_t, cap)
    y_cm = _moe_ffn(layer, h3, idx, gate, w1, w3, w2)
    return _moe_combine(idx, y_cm, x1, mod, final_g, final)


def _split_bf16(w):
    hi = w.astype(bf16)
    lo = (w - hi.astype(f32)).astype(bf16)
    return jnp.concatenate([hi, lo], axis=0)


def _stream(x, mods, layers, experts, batch, seq_len, final_g, caches=None):
    consts = _hy_consts(seq_len)
    rope_tabs = _rope_tables(seq_len) if caches is not None else None
    kvs = []
    for l, lp in enumerate(layers):
        qkv, zh, zg = _in_proj(x, mods[l], lp["norm1_g"], lp["w_in"], seq_len, rope_tabs=rope_tabs)
        if caches is None:
            a = _ctx_attention(qkv, lp["attn_sink"], batch, seq_len)
            kvs.append((qkv[:, ATTN_W:ATTN_W + KV_W], qkv[:, ATTN_W + KV_W:]))
        else:
            a = _lat_attention(qkv, lp["attn_sink"], caches[0][:, l], caches[1][:, l], batch, seq_len)
        yh = _hyena(zh, lp, consts, batch, seq_len)
        x1, h3, aff_t = _out_proj(a, yh, zg, lp["gm_ws"], lp["gm_b"], x, mods[l], lp["out_norm_g"], lp["w_out"],
                                  lp["norm2_g"], lp["router_wt"])
        x = _expert_choice(l, h3, aff_t, x1, mods[l], final_g, l == len(layers) - 1, *experts)
    return x, kvs


def kernel(x_prompt, x_sample, c, cache_k, cache_v, c_ctx, norm1_g, norm2_g, ada_w, ada_b, w_in, attn_sink,
           hy_conv_w, hy_conv_b, hy_f_w1, hy_f_b1, hy_f_w2, hy_f_b2, hy_f_w3, hy_freq, hy_bias, gm_ws, gm_b,
           out_norm_g, w_out, router_w, exp_w1, exp_w3, exp_w2, final_g):
    batch, seq, _ = x_prompt.shape
    dbatch, dseq, _ = x_sample.shape
    past = cache_k.shape[2]

    cvec = jnp.zeros((SUBLANES, D_MODEL), f32).at[0].set(c_ctx).at[1:1 + dbatch].set(c)
    mod = _ada(cvec, ada_w, ada_b)
    mods_ctx = [mod[l, 0:1].reshape(1, 6, D_MODEL) for l in range(DEPTH)]
    mods_lat = [mod[l, 1:1 + dbatch].reshape(dbatch, 6, D_MODEL) for l in range(DEPTH)]

    w_in_bf, w_out_bf, gm_ws_bf = w_in.astype(bf16), w_out.astype(bf16), gm_ws.astype(bf16)
    layers = []
    for l in range(DEPTH):
        layers.append({
            "norm1_g": norm1_g[l], "norm2_g": norm2_g[l], "w_in": w_in_bf[l], "attn_sink": attn_sink[l],
            "hy_conv_w": hy_conv_w[l], "hy_conv_b": hy_conv_b[l], "hy_f_w1": hy_f_w1[l], "hy_f_b1": hy_f_b1[l],
            "hy_f_w2": hy_f_w2[l], "hy_f_b2": hy_f_b2[l], "hy_f_w3": hy_f_w3[l], "hy_freq": hy_freq[l],
            "hy_bias": hy_bias[l], "gm_ws": gm_ws_bf[l], "gm_b": gm_b[l], "out_norm_g": out_norm_g[l],
            "w_out": w_out_bf[l], "router_wt": _split_bf16(router_w[l].T)})
    experts = (exp_w1, exp_w3, exp_w2)

    yp, kvs = _stream(x_prompt.reshape(batch * seq, D_MODEL), mods_ctx, layers, experts, batch, seq, final_g)
    caches = (cache_k.reshape(dbatch, DEPTH, past, KV_W), cache_v.reshape(dbatch, DEPTH, past, KV_W))
    ys, _ = _stream(x_sample.reshape(dbatch * dseq, D_MODEL), mods_lat, layers, experts, dbatch, dseq, final_g,
                   caches=caches)

    new_k = jnp.stack([k.reshape(batch, seq, N_KV, HEAD_DIM) for k, _ in kvs], axis=1)
    new_v = jnp.stack([v.reshape(batch, seq, N_KV, HEAD_DIM) for _, v in kvs], axis=1)
    return (yp.reshape(batch, seq, D_MODEL), ys.reshape(dbatch, dseq, D_MODEL), new_k, new_v)
```

```python
import functools
import math

import numpy as np
import jax
import jax.numpy as jnp
from jax import lax
from jax.experimental import pallas as pl
from jax.experimental.pallas import tpu as pltpu

f32 = jnp.float32
bf16 = jnp.bfloat16

D_MODEL = 1024
DEPTH = 2
GRID_W = 64
BLOCK = 128
N_HEADS = 8
N_KV = 2
HEAD_DIM = 64
Q_GROUP = N_HEADS // N_KV
ATTN_W = N_HEADS * HEAD_DIM
KV_W = N_KV * HEAD_DIM
QKV_W = ATTN_W + 2 * KV_W
HY_W = 256
GM_W = 256
GM_HEADS = 4
GM_HEAD_DIM = GM_W // GM_HEADS
GM_CHUNK = 128
MIX_W = ATTN_W + HY_W + GM_W
IN_W = ATTN_W + 2 * KV_W + 3 * HY_W + 2 * GM_W
FILTER_EMB = 33
FILTER_HID = 64
HY_DECAY_HI_PCT = 0.3
HY_DECAY_LO_PCT = 1.5
HY_DECAY_TARGET = 1e-2
N_EXPERTS = 16
EC_CAPACITY = 2
D_EXPERT = 1024
ROPE_THETA = 10000.0
EPS = 1e-6
NEG = -1e30

LANES = 128
SUBLANES = 8
ROW_CHUNKS = D_MODEL // LANES
VMEM_LIMIT = 56 * 1024 * 1024
TM = 512
MOE_F_TILE = 256


def _params(*sem):
    return pltpu.CompilerParams(dimension_semantics=sem, vmem_limit_bytes=VMEM_LIMIT)


def _rms(x):
    return x * lax.rsqrt(jnp.mean(x * x, axis=-1, keepdims=True) + EPS)


def _sigmoid(x):
    return 1.0 / (1.0 + jnp.exp(-x))


def _ada_kernel(c_ref, w_ref, b_ref, o_ref):
    c = c_ref[...]
    s = (c * _sigmoid(c)).astype(bf16)
    o_ref[0] = jnp.dot(s, w_ref[0].astype(bf16), preferred_element_type=f32) + b_ref[0]


def _ada(cvec, ada_w, ada_b):
    nt = 6
    return pl.pallas_call(
        _ada_kernel,
        grid=(DEPTH, nt),
        in_specs=[
            pl.BlockSpec((SUBLANES, D_MODEL), lambda l, j: (0, 0)),
            pl.BlockSpec((1, D_MODEL, D_MODEL), lambda l, j: (l, 0, j)),
            pl.BlockSpec((1, 1, D_MODEL), lambda l, j: (l, 0, j)),
        ],
        out_specs=pl.BlockSpec((1, SUBLANES, D_MODEL), lambda l, j: (l, 0, j)),
        out_shape=jax.ShapeDtypeStruct((DEPTH, SUBLANES, 6 * D_MODEL), f32),
        compiler_params=_params("arbitrary", "arbitrary"),
        name="ada",
    )(cvec, ada_w, ada_b.reshape(DEPTH, 1, 6 * D_MODEL))


def _rope_swap(x):
    w = x.shape[-1]
    lane = lax.broadcasted_iota(jnp.int32, x.shape, 1)
    first = (lane % 32) < 16
    return jnp.where(first, pltpu.roll(x, w - 16, axis=1), pltpu.roll(x, 16, axis=1))


def _in_kernel(rope, x_ref, mod_ref, g_ref, w_ref, *refs):
    if rope:
        cos_ref, sin_ref, qkv_ref, zh_ref, zg_ref = refs
    else:
        qkv_ref, zh_ref, zg_ref = refs
    m = mod_ref[0]
    h = _rms(x_ref[...]) * g_ref[...] * (1.0 + m[1:2]) + m[0:1]
    z = jnp.dot(h.astype(bf16), w_ref[...], preferred_element_type=f32)
    if rope:
        reps = (ATTN_W + KV_W) // LANES
        cos = jnp.concatenate([cos_ref[...]] * reps, axis=1)
        sin = jnp.concatenate([sin_ref[...]] * reps, axis=1)
        qk = z[:, :ATTN_W + KV_W]
        qkv_ref[:, :ATTN_W + KV_W] = qk * cos + _rope_swap(qk) * sin
        qkv_ref[:, ATTN_W + KV_W:] = z[:, ATTN_W + KV_W:QKV_W]
    else:
        qkv_ref[...] = z[:, :QKV_W]
    zh_ref[...] = z[:, QKV_W:QKV_W + 3 * HY_W].astype(bf16)
    zg_ref[...] = z[:, QKV_W + 3 * HY_W:].astype(bf16)


def _in_proj(x, mod, g1, w_in_bf, seq_len, rope_tabs=None):
    n = x.shape[0]
    rope = rope_tabs is not None
    per_mod = n // mod.shape[0]
    in_specs = [pl.BlockSpec((TM, D_MODEL), lambda i: (i, 0)),
                pl.BlockSpec((1, 6, D_MODEL), lambda i: (i * TM // per_mod, 0, 0)),
                pl.BlockSpec((1, D_MODEL), lambda i: (0, 0)),
                pl.BlockSpec((D_MODEL, IN_W), lambda i: (0, 0))]
    args = [x, mod, g1.reshape(1, D_MODEL), w_in_bf]
    if rope:
        nt = seq_len // TM
        tab = pl.BlockSpec((TM, LANES), lambda i: (i % nt, 0))
        in_specs += [tab, tab]
        args += list(rope_tabs)
    return pl.pallas_call(
        functools.partial(_in_kernel, rope),
        grid=(n // TM,),
        in_specs=in_specs,
        out_specs=[pl.BlockSpec((TM, w), lambda i: (i, 0)) for w in (QKV_W, 3 * HY_W, 2 * GM_W)],
        out_shape=[jax.ShapeDtypeStruct((n, QKV_W), f32), jax.ShapeDtypeStruct((n, 3 * HY_W), bf16),
                   jax.ShapeDtypeStruct((n, 2 * GM_W), bf16)],
        compiler_params=_params("parallel"),
        name="in_proj",
    )(*args)


def _rope_tables(seq_len):
    nf = HEAD_DIM // 4
    t = np.arange(seq_len)
    inv = (ROPE_THETA ** (-np.arange(nf, dtype=np.float32) / nf)).astype(np.float32)
    d = np.arange(HEAD_DIM)
    pos = np.where((d // 32)[None, :] == 0, (t // GRID_W)[:, None], (t % GRID_W)[:, None]).astype(np.float32)
    ang = (pos * inv[d % nf][None, :]).astype(np.float32)
    cos = np.cos(ang).astype(np.float32)
    sin = np.sin(ang).astype(np.float32) * np.where((d % 32) < 16, -1.0, 1.0)[None, :].astype(np.float32)
    return jnp.asarray(np.tile(cos, (1, 2))), jnp.asarray(np.tile(sin, (1, 2)))


LOG2E = math.log2(math.e)


def _stack_queries(z, sink_ref, kv):
    rows = z.shape[0]
    heads = range(kv * Q_GROUP, (kv + 1) * Q_GROUP)
    qs = jnp.concatenate([z[:, h * HEAD_DIM:(h + 1) * HEAD_DIM] for h in heads], axis=0)
    sink = jnp.concatenate([jnp.full((1, rows), sink_ref[h] * LOG2E, f32) for h in heads], axis=1)
    return (qs * (LOG2E / math.sqrt(HEAD_DIM))).astype(bf16), sink


def _scores_t(k, qs_bf):
    return lax.dot_general(k.astype(bf16), qs_bf, (((1,), (1,)), ((), ())), preferred_element_type=f32)


def _softmax_pv_t(st, sink, vals, rows):
    m = jnp.maximum(jnp.max(st, axis=0, keepdims=True), sink)
    pt = jnp.exp2(st - m).astype(bf16)
    v_bf = vals.astype(bf16)
    v_ext = jnp.concatenate([v_bf, jnp.ones_like(v_bf)], axis=1)
    ovt = lax.dot_general(v_ext, pt, (((0,), (0,)), ((), ())), preferred_element_type=f32)
    ot = ovt[:HEAD_DIM] / (ovt[HEAD_DIM:HEAD_DIM + 1] + jnp.exp2(sink - m))
    return [ot[:, g * rows:(g + 1) * rows].T for g in range(Q_GROUP)]


CTX_BATCHES = 2
LAT_QBLOCKS = 4


def _ctx_attn_kernel(seq_len, sink_ref, z_ref, o_ref):
    for s in range(CTX_BATCHES):
        z = z_ref[s * seq_len:(s + 1) * seq_len, :]
        outs = []
        for kv in range(N_KV):
            k = z[:, ATTN_W + kv * HEAD_DIM:ATTN_W + (kv + 1) * HEAD_DIM]
            v = z[:, ATTN_W + KV_W + kv * HEAD_DIM:ATTN_W + KV_W + (kv + 1) * HEAD_DIM]
            qs, sink = _stack_queries(z, sink_ref, kv)
            outs += _softmax_pv_t(_scores_t(k, qs), sink, v, seq_len)
        o_ref[s * seq_len:(s + 1) * seq_len, :] = jnp.concatenate(outs, axis=1)


def _ctx_attention(z, sink, batch, seq_len):
    rows = CTX_BATCHES * seq_len
    return pl.pallas_call(
        functools.partial(_ctx_attn_kernel, seq_len),
        grid=(batch // CTX_BATCHES,),
        in_specs=[pl.BlockSpec(memory_space=pltpu.SMEM),
                  pl.BlockSpec((rows, QKV_W), lambda b: (b, 0))],
        out_specs=pl.BlockSpec((rows, ATTN_W), lambda b: (b, 0)),
        out_shape=jax.ShapeDtypeStruct((batch * seq_len, ATTN_W), f32),
        compiler_params=_params("parallel"),
        name="ctx_attn",
    )(sink, z)


def _lat_attn_kernel(nb, sink_ref, zp_ref, zc_ref, zn_ref, ck_ref, cv_ref, o_ref):
    step = pl.program_id(1)
    zc_all = zc_ref[...]
    blocks = [zp_ref[...]] + [zc_all[q * BLOCK:(q + 1) * BLOCK] for q in range(LAT_QBLOCKS)] + [zn_ref[...]]
    ck, cv = ck_ref[0], cv_ref[0]
    width = Q_GROUP * BLOCK
    j = lax.broadcasted_iota(jnp.int32, (BLOCK, width), 0)
    r = lax.broadcasted_iota(jnp.int32, (BLOCK, width), 1) % BLOCK
    for q in range(LAT_QBLOCKS):
        i = step * LAT_QBLOCKS + q
        zp, zc, zn = blocks[q], blocks[q + 1], blocks[q + 2]
        ok_prev = j >= r + jnp.where(i > 0, 0, BLOCK)
        ok_next = j <= r - jnp.where(i < nb - 1, 0, BLOCK)
        outs = []
        for kv in range(N_KV):
            ks = slice(ATTN_W + kv * HEAD_DIM, ATTN_W + (kv + 1) * HEAD_DIM)
            vs = slice(ATTN_W + KV_W + kv * HEAD_DIM, ATTN_W + KV_W + (kv + 1) * HEAD_DIM)
            cs = slice(kv * HEAD_DIM, (kv + 1) * HEAD_DIM)
            qs, sink = _stack_queries(zc, sink_ref, kv)
            st = jnp.concatenate([
                jnp.where(ok_prev, _scores_t(zp[:, ks], qs), NEG),
                _scores_t(zc[:, ks], qs),
                jnp.where(ok_next, _scores_t(zn[:, ks], qs), NEG),
                _scores_t(ck[:, cs], qs)], axis=0)
            vals = jnp.concatenate([zp[:, vs], zc[:, vs], zn[:, vs], cv[:, cs]], axis=0)
            outs += _softmax_pv_t(st, sink, vals, BLOCK)
        o_ref[q * BLOCK:(q + 1) * BLOCK, :] = jnp.concatenate(outs, axis=1)


def _lat_attention(z, sink, ck, cv, batch, seq_len):
    nb = seq_len // BLOCK
    ns = nb // LAT_QBLOCKS
    blk = lambda f: pl.BlockSpec((BLOCK, QKV_W), f)
    past = ck.shape[1]
    cache = pl.BlockSpec((1, past, KV_W), lambda b, i: (b, 0, 0))
    return pl.pallas_call(
        functools.partial(_lat_attn_kernel, nb),
        grid=(batch, ns),
        in_specs=[pl.BlockSpec(memory_space=pltpu.SMEM),
                  blk(lambda b, i: (b * nb + jnp.maximum(i * LAT_QBLOCKS - 1, 0), 0)),
                  pl.BlockSpec((LAT_QBLOCKS * BLOCK, QKV_W), lambda b, i: (b * ns + i, 0)),
                  blk(lambda b, i: (b * nb + jnp.minimum((i + 1) * LAT_QBLOCKS, nb - 1), 0)),
                  cache, cache],
        out_specs=pl.BlockSpec((LAT_QBLOCKS * BLOCK, ATTN_W), lambda b, i: (b * ns + i, 0)),
        out_shape=jax.ShapeDtypeStruct((batch * seq_len, ATTN_W), f32),
        compiler_params=_params("parallel", "parallel"),
        name="lat_attn",
    )(sink, z, z, z, ck, cv)


HY_TL = 256
HALO_ROWS = 2 * SUBLANES


def _short_conv(z, prev_tile, next_tile, has_prev, has_next, w, b):
    z = z.astype(f32)
    rows = z.shape[0]
    row = lax.broadcasted_iota(jnp.int32, z.shape, 0)
    prev_row = jnp.where(has_prev, prev_tile.astype(f32)[HALO_ROWS - 1:HALO_ROWS, :], 0.0)
    next_row = jnp.where(has_next, next_tile.astype(f32)[0:1, :], 0.0)
    z_prev = jnp.where(row == 0, prev_row, pltpu.roll(z, 1, axis=0))
    z_next = jnp.where(row == rows - 1, next_row, pltpu.roll(z, rows - 1, axis=0))
    zc = z_prev * w[0:1] + z * w[1:2] + z_next * w[2:3] + b
    return zc[:, :HY_W], zc[:, HY_W:2 * HY_W] * zc[:, 2 * HY_W:]


def _hy_filter_kernel(seq_len, feats_ref, win_ref, w1_ref, b1_ref, w2_ref, b2_ref, w3_ref, fr_ref, o_ref):
    hi = lax.Precision.HIGHEST
    fr = fr_ref[...]
    h = jnp.sin(fr * (jnp.dot(w1_ref[...], feats_ref[...], precision=hi, preferred_element_type=f32) + b1_ref[...]))
    h = jnp.sin(fr * (jnp.dot(w2_ref[...], h, precision=hi, preferred_element_type=f32) + b2_ref[...]))
    h = jnp.dot(h.T, w3_ref[...], precision=hi, preferred_element_type=f32)
    win = win_ref[...]
    row = lax.broadcasted_iota(jnp.int32, (seq_len, HY_W), 0)
    o_ref[0:seq_len, :] = h[:, :HY_W] * win
    o_ref[seq_len:, :] = jnp.where(row == 0, 0.0, h[:, HY_W:] * win)


def _hy_filter(seq_len, consts, w1, b1, w2, b2, w3, freq):
    w1t = jnp.zeros((FILTER_HID, LANES), f32).at[:, :FILTER_EMB].set(w1.T)
    args = (consts["feats"], consts["window"], w1t, b1.reshape(-1, 1), w2.T, b2.reshape(-1, 1), w3, freq.reshape(-1, 1))
    return pl.pallas_call(
        functools.partial(_hy_filter_kernel, seq_len),
        in_specs=[pl.BlockSpec(memory_space=pltpu.VMEM)] * len(args),
        out_specs=pl.BlockSpec(memory_space=pltpu.VMEM),
        out_shape=jax.ShapeDtypeStruct((2 * seq_len, HY_W), f32),
        compiler_params=pltpu.CompilerParams(vmem_limit_bytes=VMEM_LIMIT),
        name="hy_filter",
    )(*args)


def _hy_tiles(seq_len):
    return min(TM, seq_len // 2), min(TM, seq_len)


def _hy_fwd_kernel(nb, tk, seq_len, m_ref, zh_ref, cw_ref, cb_ref, hb_ref, r_ref, z_ref, rhs_ref):
    i = pl.program_id(0)
    nc = seq_len // HY_TL

    @pl.when(i == 0)
    def _():
        w, bias = cw_ref[...], cb_ref[...]
        for b in range(nb):
            def chunk(c, carry, b=b):
                r0 = pl.multiple_of(c * HY_TL, HY_TL)
                prev = zh_ref[b, pl.ds(pl.multiple_of(jnp.maximum(r0 - HALO_ROWS, 0), HALO_ROWS), HALO_ROWS), :]
                nxt = zh_ref[b, pl.ds(pl.multiple_of(jnp.minimum(r0 + HY_TL, seq_len - HALO_ROWS), HALO_ROWS),
                                      HALO_ROWS), :]
                _, u = _short_conv(zh_ref[b, pl.ds(r0, HY_TL), :], prev, nxt, c > 0, c < nc - 1, w, bias)
                rhs_ref[pl.ds(r0, HY_TL), b * HY_W:(b + 1) * HY_W] = u.astype(bf16)
                return carry
            lax.fori_loop(0, nc, chunk, 0)
        rhs_ref[:, nb * HY_W:(nb + 1) * HY_W] = hb_ref[0].astype(bf16)
        rhs_ref[:, (nb + 1) * HY_W:] = hb_ref[1].astype(bf16)

    acc = jnp.dot(m_ref[...], rhs_ref[...], preferred_element_type=f32)
    p, w = acc[:tk], acc[tk:]
    c0 = nb * HY_W
    pf, pb = p[:, c0:c0 + HY_W], p[:, c0 + HY_W:]
    qf, qb = w[:, c0:c0 + HY_W], w[:, c0 + HY_W:]
    hr = pf + pb
    qs = qf + qb
    hi = qb - qf
    first = (lax.broadcasted_iota(jnp.int32, (tk, HY_W), 0) == 0) & (i == 0)
    for b in range(nb):
        pu, wu = p[:, b * HY_W:(b + 1) * HY_W], w[:, b * HY_W:(b + 1) * HY_W]
        r_ref[b] = jnp.where(first, pu * hr, 2.0 * (pu * hr + wu * hi)).astype(bf16)
        z_ref[b] = jnp.where(first, wu * qs, 2.0 * (wu * hr - pu * hi)).astype(bf16)


def _hy_fwd(mat, zh, conv_w, conv_b, hb, batch, seq_len):
    tk, _ = _hy_tiles(seq_len)
    out = pl.BlockSpec((batch, tk, HY_W), lambda i: (0, i, 0))
    return pl.pallas_call(
        functools.partial(_hy_fwd_kernel, batch, tk, seq_len),
        grid=(seq_len // tk,),
        in_specs=[pl.BlockSpec((2 * tk, seq_len), lambda i: (i, 0)),
                  pl.BlockSpec((batch, seq_len, 3 * HY_W), lambda i: (0, 0, 0)),
                  pl.BlockSpec((3, 3 * HY_W), lambda i: (0, 0)),
                  pl.BlockSpec((1, 3 * HY_W), lambda i: (0, 0)),
                  pl.BlockSpec((2, seq_len, HY_W), lambda i: (0, 0, 0))],
        out_specs=[out, out],
        out_shape=[jax.ShapeDtypeStruct((batch, seq_len, HY_W), bf16)] * 2,
        scratch_shapes=[pltpu.VMEM((seq_len, (batch + 2) * HY_W), bf16)],
        compiler_params=_params("arbitrary"),
        name="hy_fwd",
    )(mat, zh, conv_w, conv_b.reshape(1, 3 * HY_W), hb)


def _hy_inv_kernel(nb, seq_len, nt, m_ref, r_ref, z_ref, zp_ref, zc_ref, zn_ref, cw_ref, cb_ref, d_ref, o_ref,
                   rhs_ref):
    i = pl.program_id(0)

    @pl.when(i == 0)
    def _():
        for b in range(nb):
            rhs_ref[:seq_len, b * HY_W:(b + 1) * HY_W] = r_ref[b]
            rhs_ref[seq_len:, b * HY_W:(b + 1) * HY_W] = z_ref[b]

    acc = jnp.dot(m_ref[...], rhs_ref[...], preferred_element_type=f32)
    d, w, bias = d_ref[...], cw_ref[...], cb_ref[...]
    for b in range(nb):
        x0, u = _short_conv(zc_ref[b], zp_ref[b], zn_ref[b], i > 0, i < nt - 1, w, bias)
        y = acc[:, b * HY_W:(b + 1) * HY_W] * (1.0 / (2 * seq_len)) + u * d
        o_ref[b] = x0 * y


def _hy_inv(mat, r, zz, zh, conv_w, conv_b, d_bias, batch, seq_len):
    _, tm = _hy_tiles(seq_len)
    nt = seq_len // tm
    rh = tm // HALO_ROWS
    last = seq_len // HALO_ROWS - 1
    full = pl.BlockSpec((batch, seq_len, HY_W), lambda i: (0, 0, 0))
    halo = lambda f: pl.BlockSpec((batch, HALO_ROWS, 3 * HY_W), f)
    return pl.pallas_call(
        functools.partial(_hy_inv_kernel, batch, seq_len, nt),
        grid=(nt,),
        in_specs=[pl.BlockSpec((tm, 2 * seq_len), lambda i: (i, 0)), full, full,
                  halo(lambda i: (0, jnp.maximum(i * rh - 1, 0), 0)),
                  pl.BlockSpec((batch, tm, 3 * HY_W), lambda i: (0, i, 0)),
                  halo(lambda i: (0, jnp.minimum((i + 1) * rh, last), 0)),
                  pl.BlockSpec((3, 3 * HY_W), lambda i: (0, 0)),
                  pl.BlockSpec((1, 3 * HY_W), lambda i: (0, 0)),
                  pl.BlockSpec((1, HY_W), lambda i: (0, 0))],
        out_specs=pl.BlockSpec((batch, tm, HY_W), lambda i: (0, i, 0)),
        out_shape=jax.ShapeDtypeStruct((batch, seq_len, HY_W), f32),
        scratch_shapes=[pltpu.VMEM((2 * seq_len, batch * HY_W), bf16)],
        compiler_params=_params("arbitrary"),
        name="hy_inv",
    )(mat, r, zz, zh, zh, zh, conv_w, conv_b.reshape(1, 3 * HY_W), d_bias.reshape(1, HY_W))


@functools.lru_cache(maxsize=None)
def _hy_consts_np(seq_len):
    n = 2 * seq_len
    k = np.arange(seq_len, dtype=np.int64)
    ang = (2.0 * np.pi / n) * ((k[:, None] * k[None, :]) % n).astype(np.float64)
    cosm = np.cos(ang)
    sinm = np.sin(ang)
    sinm[0, :] = np.where(k % 2 == 0, 1.0, -1.0)
    tk, _ = _hy_tiles(seq_len)
    fwd = np.concatenate([np.concatenate([cosm[i:i + tk], sinm[i:i + tk]], axis=0)
                          for i in range(0, seq_len, tk)], axis=0)
    inv = np.concatenate([cosm, sinm.T], axis=1)
    t = np.linspace(0.0, 1.0, seq_len, dtype=np.float32)[:, None]
    bands = (FILTER_EMB - 1) // 2
    fb = np.linspace(1e-4, bands - 1, bands, dtype=np.float32)[None, :]
    w = (np.float32(2.0 * math.pi) * np.arange(seq_len, dtype=np.float32)[:, None] / np.float32(seq_len)).astype(np.float32)
    feats = np.concatenate([t, np.cos(fb * w), -np.sin(fb * w)], axis=-1).astype(np.float32)
    feats = np.ascontiguousarray(np.pad(feats, ((0, 0), (0, LANES - FILTER_EMB))).T)
    decay_hi = math.log(HY_DECAY_TARGET) / HY_DECAY_HI_PCT
    decay_lo = math.log(HY_DECAY_TARGET) / HY_DECAY_LO_PCT
    deltas = np.abs(np.linspace(decay_lo, decay_hi, HY_W, dtype=np.float32))
    window = np.exp(-t * deltas[None, :]).astype(np.float32)
    return fwd, inv, feats, window


def _hy_consts(seq_len):
    fwd, inv, feats, window = _hy_consts_np(seq_len)
    as_bf = lambda m: jnp.asarray(m, dtype=f32).astype(bf16)
    return {"fwd": as_bf(fwd), "inv": as_bf(inv), "feats": jnp.asarray(feats), "window": jnp.asarray(window)}


def _hyena(zh, lp, consts, batch, seq_len):
    zh = zh.reshape(batch, seq_len, 3 * HY_W)
    hb = _hy_filter(seq_len, consts, lp["hy_f_w1"], lp["hy_f_b1"], lp["hy_f_w2"], lp["hy_f_b2"],
                    lp["hy_f_w3"], lp["hy_freq"]).reshape(2, seq_len, HY_W)
    r, zz = _hy_fwd(consts["fwd"], zh, lp["hy_conv_w"], lp["hy_conv_b"], hb, batch, seq_len)
    yh = _hy_inv(consts["inv"], r, zz, zh, lp["hy_conv_w"], lp["hy_conv_b"], lp["hy_bias"], batch, seq_len)
    return yh.reshape(batch * seq_len, HY_W)


def _gelu(x):
    return 0.5 * x * (1.0 + jnp.tanh(math.sqrt(2.0 / math.pi) * (x + 0.044715 * (x * x * x))))


def _gmlp(z, ws_ref, bias):
    g = _gelu(z)
    outs = []
    for c in range(TM // GM_CHUNK):
        rows = slice(c * GM_CHUNK, (c + 1) * GM_CHUNK)
        u = g[rows, :GM_W]
        v = g[rows, GM_W:].astype(bf16)
        s = jnp.concatenate(
            [jnp.dot(ws_ref[h], v[:, h * GM_HEAD_DIM:(h + 1) * GM_HEAD_DIM], preferred_element_type=f32)
             for h in range(GM_HEADS)], axis=1)
        outs.append(u * (s + bias))
    return jnp.concatenate(outs, axis=0)


def _out_kernel(a_ref, yh_ref, zg_ref, ws_ref, gb_ref, x_ref, mod_ref, og_ref, w_ref, g2_ref, rw_ref,
                x1_ref, h2_ref, aff_ref):
    og = og_ref[...]
    yg = _gmlp(zg_ref[...].astype(f32), ws_ref, gb_ref[...])
    mixed_in = jnp.concatenate([
        _rms(a_ref[...]) * og[:, :ATTN_W],
        _rms(yh_ref[...]) * og[:, ATTN_W:ATTN_W + HY_W],
        _rms(yg) * og[:, ATTN_W + HY_W:]], axis=1)
    mixed = jnp.dot(mixed_in.astype(bf16), w_ref[...], preferred_element_type=f32)
    m = mod_ref[0]
    x1 = x_ref[...] + m[2:3] * mixed
    x1_ref[...] = x1
    h2 = _rms(x1) * g2_ref[...] * (1.0 + m[4:5]) + m[3:4]
    for j in range(ROW_CHUNKS):
        h2_ref[pl.ds(j, TM, stride=ROW_CHUNKS), :] = h2[:, j * LANES:(j + 1) * LANES]
    nt = (((1,), (1,)), ((), ()))
    h_hi = h2.astype(bf16)
    h_lo = (h2 - h_hi.astype(f32)).astype(bf16)
    rw = rw_ref[...]
    t = lax.dot_general(rw, h_hi, nt, preferred_element_type=f32)
    logits = (t[:N_EXPERTS] + t[N_EXPERTS:]) + lax.dot_general(rw[:N_EXPERTS], h_lo, nt, preferred_element_type=f32)
    e = jnp.exp(logits - jnp.max(logits, axis=0, keepdims=True))
    aff_ref[...] = e / jnp.sum(e, axis=0, keepdims=True)


def _out_proj(a, yh, zg, ws_bf, gm_b, x, mod, out_g, w_out_bf, g2, router_wt):
    n = x.shape[0]
    per_mod = n // mod.shape[0]
    row = lambda w: pl.BlockSpec((TM, w), lambda i: (i, 0))
    const = lambda s: pl.BlockSpec(s, lambda i: (0,) * len(s))
    gm_bias = jnp.repeat(gm_b.T, GM_HEAD_DIM, axis=1)
    return pl.pallas_call(
        _out_kernel,
        grid=(n // TM,),
        in_specs=[row(ATTN_W), row(HY_W), row(2 * GM_W), const((GM_HEADS, GM_CHUNK, GM_CHUNK)),
                  const((GM_CHUNK, GM_W)), row(D_MODEL),
                  pl.BlockSpec((1, 6, D_MODEL), lambda i: (i * TM // per_mod, 0, 0)),
                  const((1, MIX_W)), const((MIX_W, D_MODEL)), const((1, D_MODEL)), const((2 * N_EXPERTS, D_MODEL))],
        out_specs=[row(D_MODEL), pl.BlockSpec((TM * ROW_CHUNKS, LANES), lambda i: (i, 0)),
                   pl.BlockSpec((N_EXPERTS, TM), lambda i: (0, i))],
        out_shape=[jax.ShapeDtypeStruct((n, D_MODEL), f32), jax.ShapeDtypeStruct((n * ROW_CHUNKS, LANES), f32),
                   jax.ShapeDtypeStruct((N_EXPERTS, n), f32)],
        compiler_params=_params("parallel"),
        name="out_proj",
    )(a, yh, zg, ws_bf, gm_bias, x, mod, out_g.reshape(1, MIX_W), w_out_bf, g2.reshape(1, D_MODEL), router_wt)


RT_CHUNK = 512


def _prefix_incl(x01, tri):
    n = x01.shape[1]
    carry = jnp.zeros((x01.shape[0], 1), f32)
    parts = []
    for c in range(n // RT_CHUNK):
        piece = x01[:, c * RT_CHUNK:(c + 1) * RT_CHUNK]
        parts.append(jnp.dot(piece.astype(bf16), tri, preferred_element_type=f32) + carry)
        carry = carry + jnp.sum(piece, axis=1, keepdims=True)
    return jnp.concatenate(parts, axis=1)


def _route_kernel(cap, aff_ref, idx_ref, gate_ref):
    aff = aff_ref[...]
    n = aff.shape[1]

    def step(it, t):
        cand = t | (jnp.int32(1) << (30 - it))
        cnt = jnp.sum(jnp.where(aff >= lax.bitcast_convert_type(cand, f32), 1.0, 0.0), axis=1, keepdims=True)
        return jnp.where(cnt >= cap, cand, t)

    thr_bits = lax.fori_loop(0, 31, step, jnp.zeros((aff.shape[0], 1), jnp.int32))
    thr = lax.bitcast_convert_type(thr_bits, f32)
    gt = jnp.where(aff > thr, 1.0, 0.0)
    eq = jnp.where(aff == thr, 1.0, 0.0)
    room = cap - jnp.sum(gt, axis=1, keepdims=True)
    r = lax.broadcasted_iota(jnp.int32, (RT_CHUNK, RT_CHUNK), 0)
    c = lax.broadcasted_iota(jnp.int32, (RT_CHUNK, RT_CHUNK), 1)
    tri = jnp.where(r <= c, 1.0, 0.0).astype(bf16)
    sel = jnp.maximum(gt, jnp.where(_prefix_incl(eq, tri) <= room, eq, 0.0))
    slot = _prefix_incl(sel, tri) - 1.0

    tok = lax.broadcasted_iota(jnp.int32, aff.shape, 1)
    dist = jnp.where(sel > 0.0, tok - slot.astype(jnp.int32), 0)
    g = aff
    for b in range(max(1, (n - 1).bit_length())):
        sh = 1 << b
        dist_s = pltpu.roll(dist, n - sh, axis=1)
        take = (dist_s & sh) != 0
        leave = (dist & sh) != 0
        tok = jnp.where(take, pltpu.roll(tok, n - sh, axis=1), tok)
        g = jnp.where(take, pltpu.roll(g, n - sh, axis=1), g)
        dist = jnp.where(take, dist_s, jnp.where(leave, 0, dist))
    idx_ref[...] = tok[:, :cap]
    gate_ref[...] = g[:, :cap]


def _route(aff_t, cap):
    return pl.pallas_call(
        functools.partial(_route_kernel, cap),
        in_specs=[pl.BlockSpec(memory_space=pltpu.VMEM)],
        out_specs=[pl.BlockSpec(memory_space=pltpu.VMEM)] * 2,
        out_shape=[jax.ShapeDtypeStruct((N_EXPERTS, cap), jnp.int32), jax.ShapeDtypeStruct((N_EXPERTS, cap), f32)],
        compiler_params=pltpu.CompilerParams(vmem_limit_bytes=VMEM_LIMIT),
        name="route",
    )(aff_t)


GATHER_UNROLL = 8
SCATTER_UNROLL = 8


def _moe_ffn_kernel(cap, stride, idx_ref, h_ref, g_ref, w1_ref, w3_ref, w2_ref, y_ref, tile_ref, xb_ref, acc_ref):
    base = pl.program_id(0) * cap

    def gather(q, carry):
        for t in range(GATHER_UNROLL):
            r = q * GATHER_UNROLL + t
            src = pl.multiple_of(idx_ref[base + r] * ROW_CHUNKS, ROW_CHUNKS)
            tile_ref[pl.ds(r, ROW_CHUNKS, stride=stride), :] = h_ref[pl.ds(src, ROW_CHUNKS), :]
        return carry
    lax.fori_loop(0, cap // GATHER_UNROLL, gather, 0)
    xb_ref[...] = jnp.concatenate(
        [tile_ref[j * stride:j * stride + cap, :] for j in range(ROW_CHUNKS)], axis=1).astype(bf16)

    xb = xb_ref[...]
    for f in range(D_EXPERT // MOE_F_TILE):
        cols = slice(f * MOE_F_TILE, (f + 1) * MOE_F_TILE)
        a = jnp.dot(xb, w1_ref[0, 0, :, cols].astype(bf16), preferred_element_type=f32)
        b = jnp.dot(xb, w3_ref[0, 0, :, cols].astype(bf16), preferred_element_type=f32)
        he = (a * _sigmoid(a) * b).astype(bf16)
        part = jnp.dot(he, w2_ref[0, 0, cols, :].astype(bf16), preferred_element_type=f32)
        if f == 0:
            acc_ref[...] = part
        else:
            acc_ref[...] += part

    y = acc_ref[...] * g_ref[0]
    for j in range(ROW_CHUNKS):
        y_ref[0, j * stride:j * stride + cap, :] = y[:, j * LANES:(j + 1) * LANES]
        y_ref[0, j * stride + cap:(j + 1) * stride, :] = jnp.zeros((stride - cap, LANES), f32)


def _moe_ffn(layer, h3, idx, gate, w1, w3, w2):
    cap = idx.shape[1]
    stride = cap + SUBLANES
    grid_spec = pltpu.PrefetchScalarGridSpec(
        num_scalar_prefetch=1,
        grid=(N_EXPERTS,),
        in_specs=[
            pl.BlockSpec(memory_space=pltpu.VMEM),
            pl.BlockSpec((1, cap, 1), lambda e, idx: (e, 0, 0)),
            pl.BlockSpec((1, 1, D_MODEL, D_EXPERT), lambda e, idx: (layer, e, 0, 0)),
            pl.BlockSpec((1, 1, D_MODEL, D_EXPERT), lambda e, idx: (layer, e, 0, 0)),
            pl.BlockSpec((1, 1, D_EXPERT, D_MODEL), lambda e, idx: (layer, e, 0, 0)),
        ],
        out_specs=pl.BlockSpec((1, ROW_CHUNKS * stride, LANES), lambda e, idx: (e, 0, 0)),
        scratch_shapes=[pltpu.VMEM((ROW_CHUNKS * stride, LANES), f32),
                        pltpu.VMEM((cap, D_MODEL), bf16),
                        pltpu.VMEM((cap, D_MODEL), f32)],
    )
    return pl.pallas_call(
        functools.partial(_moe_ffn_kernel, cap, stride),
        grid_spec=grid_spec,
        out_shape=jax.ShapeDtypeStruct((N_EXPERTS, ROW_CHUNKS * stride, LANES), f32),
        compiler_params=_params("arbitrary"),
        name="moe_ffn",
    )(idx.reshape(-1), h3, gate.reshape(N_EXPERTS, cap, 1), w1, w3, w2)


def _moe_combine_kernel(cap, stride, final, idx_ref, y_ref, x_ref, mod_ref, g_ref, o_ref, acc_ref):
    s = pl.program_id(0)

    @pl.when(s == 0)
    def _():
        acc_ref[...] = jnp.zeros_like(acc_ref)

    @pl.when(s < N_EXPERTS)
    def _():
        base = s * cap

        def scatter(q, carry):
            rows = [pl.multiple_of(idx_ref[base + q * SCATTER_UNROLL + t] * ROW_CHUNKS, ROW_CHUNKS)
                    for t in range(SCATTER_UNROLL)]
            new = [acc_ref[pl.ds(rows[t], ROW_CHUNKS), :]
                   + y_ref[0, pl.ds(q * SCATTER_UNROLL + t, ROW_CHUNKS, stride=stride), :]
                   for t in range(SCATTER_UNROLL)]
            for t in range(SCATTER_UNROLL):
                acc_ref[pl.ds(rows[t], ROW_CHUNKS), :] = new[t]
            return carry
        lax.fori_loop(0, cap // SCATTER_UNROLL, scatter, 0)

    @pl.when(s >= N_EXPERTS)
    def _():
        first = pl.multiple_of((s - N_EXPERTS) * (TM * ROW_CHUNKS), TM * ROW_CHUNKS)
        moe = jnp.concatenate([acc_ref[pl.ds(first + j, TM, stride=ROW_CHUNKS), :] for j in range(ROW_CHUNKS)],
                              axis=1)
        x = x_ref[...] + mod_ref[0][5:6] * moe
        o_ref[...] = _rms(x) * g_ref[...] if final else x


def _moe_combine(idx, y_cm, x1, mod, final_g, final):
    n = x1.shape[0]
    cap = idx.shape[1]
    stride = y_cm.shape[1] // ROW_CHUNKS
    per_mod = n // mod.shape[0]
    tile = lambda s: jnp.maximum(s - N_EXPERTS, 0)
    row = pl.BlockSpec((TM, D_MODEL), lambda s, idx: (tile(s), 0))
    grid_spec = pltpu.PrefetchScalarGridSpec(
        num_scalar_prefetch=1,
        grid=(N_EXPERTS + n // TM,),
        in_specs=[pl.BlockSpec((1, ROW_CHUNKS * stride, LANES), lambda s, idx: (jnp.minimum(s, N_EXPERTS - 1), 0, 0)),
                  row,
                  pl.BlockSpec((1, 6, D_MODEL), lambda s, idx: (tile(s) * TM // per_mod, 0, 0)),
                  pl.BlockSpec((1, D_MODEL), lambda s, idx: (0, 0))],
        out_specs=row,
        scratch_shapes=[pltpu.VMEM((n * ROW_CHUNKS, LANES), f32)],
    )
    return pl.pallas_call(
        functools.partial(_moe_combine_kernel, cap, stride, final),
        grid_spec=grid_spec,
        out_shape=jax.ShapeDtypeStruct((n, D_MODEL), f32),
        compiler_params=_params("arbitrary"),
        name="moe_combine",
    )(idx.reshape(-1), y_cm, x1, mod, final_g.reshape(1, D_MODEL))


def _expert_choice(layer, h3, aff_t, x1, mod, final_g, final, w1, w3, w2):
    n = x1.shape[0]
    cap = max(1, EC_CAPACITY * n // N_EXPERTS)
    idx, gate = _route(aff_t, cap)
    y_cm = _moe_ffn(layer, h3, idx, gate, w1, w3, w2)
    return _moe_combine(idx, y_cm, x1, mod, final_g, final)


def _split_bf16(w):
    hi = w.astype(bf16)
    lo = (w - hi.astype(f32)).astype(bf16)
    return jnp.concatenate([hi, lo], axis=0)


def _stream(x, mods, layers, experts, batch, seq_len, final_g, caches=None):
    consts = _hy_consts(seq_len)
    rope_tabs = _rope_tables(seq_len) if caches is not None else None
    kvs = []
    for l, lp in enumerate(layers):
        qkv, zh, zg = _in_proj(x, mods[l], lp["norm1_g"], lp["w_in"], seq_len, rope_tabs=rope_tabs)
        if caches is None:
            a = _ctx_attention(qkv, lp["attn_sink"], batch, seq_len)
            kvs.append((qkv[:, ATTN_W:ATTN_W + KV_W], qkv[:, ATTN_W + KV_W:]))
        else:
            a = _lat_attention(qkv, lp["attn_sink"], caches[0][:, l], caches[1][:, l], batch, seq_len)
        yh = _hyena(zh, lp, consts, batch, seq_len)
        x1, h3, aff_t = _out_proj(a, yh, zg, lp["gm_ws"], lp["gm_b"], x, mods[l], lp["out_norm_g"], lp["w_out"],
                                  lp["norm2_g"], lp["router_wt"])
        x = _expert_choice(l, h3, aff_t, x1, mods[l], final_g, l == len(layers) - 1, *experts)
    return x, kvs


def kernel(x_prompt, x_sample, c, cache_k, cache_v, c_ctx, norm1_g, norm2_g, ada_w, ada_b, w_in, attn_sink,
           hy_conv_w, hy_conv_b, hy_f_w1, hy_f_b1, hy_f_w2, hy_f_b2, hy_f_w3, hy_freq, hy_bias, gm_ws, gm_b,
           out_norm_g, w_out, router_w, exp_w1, exp_w3, exp_w2, final_g):
    batch, seq, _ = x_prompt.shape
    dbatch, dseq, _ = x_sample.shape
    past = cache_k.shape[2]

    cvec = jnp.zeros((SUBLANES, D_MODEL), f32).at[0].set(c_ctx).at[1:1 + dbatch].set(c)
    mod = _ada(cvec, ada_w, ada_b)
    mods_ctx = [mod[l, 0:1].reshape(1, 6, D_MODEL) for l in range(DEPTH)]
    mods_lat = [mod[l, 1:1 + dbatch].reshape(dbatch, 6, D_MODEL) for l in range(DEPTH)]

    w_in_bf, w_out_bf, gm_ws_bf = w_in.astype(bf16), w_out.astype(bf16), gm_ws.astype(bf16)
    layers = []
    for l in range(DEPTH):
        layers.append({
            "norm1_g": norm1_g[l], "norm2_g": norm2_g[l], "w_in": w_in_bf[l], "attn_sink": attn_sink[l],
            "hy_conv_w": hy_conv_w[l], "hy_conv_b": hy_conv_b[l], "hy_f_w1": hy_f_w1[l], "hy_f_b1": hy_f_b1[l],
            "hy_f_w2": hy_f_w2[l], "hy_f_b2": hy_f_b2[l], "hy_f_w3": hy_f_w3[l], "hy_freq": hy_freq[l],
            "hy_bias": hy_bias[l], "gm_ws": gm_ws_bf[l], "gm_b": gm_b[l], "out_norm_g": out_norm_g[l],
            "w_out": w_out_bf[l], "router_wt": _split_bf16(router_w[l].T)})
    experts = (exp_w1, exp_w3, exp_w2)

    yp, kvs = _stream(x_prompt.reshape(batch * seq, D_MODEL), mods_ctx, layers, experts, batch, seq, final_g)
    caches = (cache_k.reshape(dbatch, DEPTH, past, KV_W), cache_v.reshape(dbatch, DEPTH, past, KV_W))
    ys, _ = _stream(x_sample.reshape(dbatch * dseq, D_MODEL), mods_lat, layers, experts, dbatch, dseq, final_g,
                   caches=caches)

    new_k = jnp.stack([k.reshape(batch, seq, N_KV, HEAD_DIM) for k, _ in kvs], axis=1)
    new_v = jnp.stack([v.reshape(batch, seq, N_KV, HEAD_DIM) for _, v in kvs], axis=1)
    return (yp.reshape(batch, seq, D_MODEL), ys.reshape(dbatch, dseq, D_MODEL), new_k, new_v)
```

```python
import functools
import math

import numpy as np
import jax
import jax.numpy as jnp
from jax import lax
from jax.experimental import pallas as pl
from jax.experimental.pallas import tpu as pltpu

f32 = jnp.float32
bf16 = jnp.bfloat16

D_MODEL = 1024
DEPTH = 2
GRID_W = 64
BLOCK = 128
N_HEADS = 8
N_KV = 2
HEAD_DIM = 64
Q_GROUP = N_HEADS // N_KV
ATTN_W = N_HEADS * HEAD_DIM
KV_W = N_KV * HEAD_DIM
QKV_W = ATTN_W + 2 * KV_W
HY_W = 256
GM_W = 256
GM_HEADS = 4
GM_HEAD_DIM = GM_W // GM_HEADS
GM_CHUNK = 128
MIX_W = ATTN_W + HY_W + GM_W
IN_W = ATTN_W + 2 * KV_W + 3 * HY_W + 2 * GM_W
FILTER_EMB = 33
FILTER_HID = 64
HY_DECAY_HI_PCT = 0.3
HY_DECAY_LO_PCT = 1.5
HY_DECAY_TARGET = 1e-2
N_EXPERTS = 16
EC_CAPACITY = 2
D_EXPERT = 1024
ROPE_THETA = 10000.0
EPS = 1e-6
NEG = -1e30

LANES = 128
SUBLANES = 8
ROW_CHUNKS = D_MODEL // LANES
VMEM_LIMIT = 56 * 1024 * 1024
TM = 512
MOE_F_TILE = 256


def _params(*sem):
    return pltpu.CompilerParams(dimension_semantics=sem, vmem_limit_bytes=VMEM_LIMIT)


def _rms(x):
    return x * lax.rsqrt(jnp.mean(x * x, axis=-1, keepdims=True) + EPS)


def _sigmoid(x):
    return 1.0 / (1.0 + jnp.exp(-x))


def _ada_kernel(c_ref, w_ref, b_ref, o_ref):
    c = c_ref[...]
    s = (c * _sigmoid(c)).astype(bf16)
    o_ref[0] = jnp.dot(s, w_ref[0].astype(bf16), preferred_element_type=f32) + b_ref[0]


def _ada(cvec, ada_w, ada_b):
    nt = 6
    return pl.pallas_call(
        _ada_kernel,
        grid=(DEPTH, nt),
        in_specs=[
            pl.BlockSpec((SUBLANES, D_MODEL), lambda l, j: (0, 0)),
            pl.BlockSpec((1, D_MODEL, D_MODEL), lambda l, j: (l, 0, j)),
            pl.BlockSpec((1, 1, D_MODEL), lambda l, j: (l, 0, j)),
        ],
        out_specs=pl.BlockSpec((1, SUBLANES, D_MODEL), lambda l, j: (l, 0, j)),
        out_shape=jax.ShapeDtypeStruct((DEPTH, SUBLANES, 6 * D_MODEL), f32),
        compiler_params=_params("arbitrary", "arbitrary"),
        name="ada",
    )(cvec, ada_w, ada_b.reshape(DEPTH, 1, 6 * D_MODEL))


def _rope_swap(x):
    w = x.shape[-1]
    lane = lax.broadcasted_iota(jnp.int32, x.shape, 1)
    first = (lane % 32) < 16
    return jnp.where(first, pltpu.roll(x, w - 16, axis=1), pltpu.roll(x, 16, axis=1))


def _in_kernel(rope, x_ref, mod_ref, g_ref, w_ref, *refs):
    if rope:
        cos_ref, sin_ref, qkv_ref, zh_ref, zg_ref = refs
    else:
        qkv_ref, zh_ref, zg_ref = refs
    m = mod_ref[0]
    h = _rms(x_ref[...]) * g_ref[...] * (1.0 + m[1:2]) + m[0:1]
    z = jnp.dot(h.astype(bf16), w_ref[...], preferred_element_type=f32)
    if rope:
        reps = (ATTN_W + KV_W) // LANES
        cos = jnp.concatenate([cos_ref[...]] * reps, axis=1)
        sin = jnp.concatenate([sin_ref[...]] * reps, axis=1)
        qk = z[:, :ATTN_W + KV_W]
        qkv_ref[:, :ATTN_W + KV_W] = qk * cos + _rope_swap(qk) * sin
        qkv_ref[:, ATTN_W + KV_W:] = z[:, ATTN_W + KV_W:QKV_W]
    else:
        qkv_ref[...] = z[:, :QKV_W]
    zh_ref[...] = z[:, QKV_W:QKV_W + 3 * HY_W].astype(bf16)
    zg_ref[...] = z[:, QKV_W + 3 * HY_W:].astype(bf16)


def _in_proj(x, mod, g1, w_in_bf, seq_len, rope_tabs=None):
    n = x.shape[0]
    rope = rope_tabs is not None
    per_mod = n // mod.shape[0]
    in_specs = [pl.BlockSpec((TM, D_MODEL), lambda i: (i, 0)),
                pl.BlockSpec((1, 6, D_MODEL), lambda i: (i * TM // per_mod, 0, 0)),
                pl.BlockSpec((1, D_MODEL), lambda i: (0, 0)),
                pl.BlockSpec((D_MODEL, IN_W), lambda i: (0, 0))]
    args = [x, mod, g1.reshape(1, D_MODEL), w_in_bf]
    if rope:
        nt = seq_len // TM
        tab = pl.BlockSpec((TM, LANES), lambda i: (i % nt, 0))
        in_specs += [tab, tab]
        args += list(rope_tabs)
    return pl.pallas_call(
        functools.partial(_in_kernel, rope),
        grid=(n // TM,),
        in_specs=in_specs,
        out_specs=[pl.BlockSpec((TM, w), lambda i: (i, 0)) for w in (QKV_W, 3 * HY_W, 2 * GM_W)],
        out_shape=[jax.ShapeDtypeStruct((n, QKV_W), f32), jax.ShapeDtypeStruct((n, 3 * HY_W), bf16),
                   jax.ShapeDtypeStruct((n, 2 * GM_W), bf16)],
        compiler_params=_params("parallel"),
        name="in_proj",
    )(*args)


def _rope_tables(seq_len):
    nf = HEAD_DIM // 4
    t = np.arange(seq_len)
    inv = (ROPE_THETA ** (-np.arange(nf, dtype=np.float32) / nf)).astype(np.float32)
    d = np.arange(HEAD_DIM)
    pos = np.where((d // 32)[None, :] == 0, (t // GRID_W)[:, None], (t % GRID_W)[:, None]).astype(np.float32)
    ang = (pos * inv[d % nf][None, :]).astype(np.float32)
    cos = np.cos(ang).astype(np.float32)
    sin = np.sin(ang).astype(np.float32) * np.where((d % 32) < 16, -1.0, 1.0)[None, :].astype(np.float32)
    return jnp.asarray(np.tile(cos, (1, 2))), jnp.asarray(np.tile(sin, (1, 2)))


LOG2E = math.log2(math.e)


def _stack_queries(z, sink_ref, kv):
    rows = z.shape[0]
    heads = range(kv * Q_GROUP, (kv + 1) * Q_GROUP)
    qs = jnp.concatenate([z[:, h * HEAD_DIM:(h + 1) * HEAD_DIM] for h in heads], axis=0)
    sink = jnp.concatenate([jnp.full((1, rows), sink_ref[h] * LOG2E, f32) for h in heads], axis=1)
    return (qs * (LOG2E / math.sqrt(HEAD_DIM))).astype(bf16), sink


def _scores_t(k, qs_bf):
    return lax.dot_general(k.astype(bf16), qs_bf, (((1,), (1,)), ((), ())), preferred_element_type=f32)


def _softmax_pv_t(st, sink, vals, rows):
    m = jnp.maximum(jnp.max(st, axis=0, keepdims=True), sink)
    pt = jnp.exp2(st - m).astype(bf16)
    v_bf = vals.astype(bf16)
    v_ext = jnp.concatenate([v_bf, jnp.ones_like(v_bf)], axis=1)
    ovt = lax.dot_general(v_ext, pt, (((0,), (0,)), ((), ())), preferred_element_type=f32)
    ot = ovt[:HEAD_DIM] / (ovt[HEAD_DIM:HEAD_DIM + 1] + jnp.exp2(sink - m))
    return [ot[:, g * rows:(g + 1) * rows].T for g in range(Q_GROUP)]


CTX_BATCHES = 2
LAT_QBLOCKS = 4


def _ctx_attn_kernel(seq_len, sink_ref, z_ref, o_ref):
    for s in range(CTX_BATCHES):
        z = z_ref[s * seq_len:(s + 1) * seq_len, :]
        outs = []
        for kv in range(N_KV):
            k = z[:, ATTN_W + kv * HEAD_DIM:ATTN_W + (kv + 1) * HEAD_DIM]
            v = z[:, ATTN_W + KV_W + kv * HEAD_DIM:ATTN_W + KV_W + (kv + 1) * HEAD_DIM]
            qs, sink = _stack_queries(z, sink_ref, kv)
            outs += _softmax_pv_t(_scores_t(k, qs), sink, v, seq_len)
        o_ref[s * seq_len:(s + 1) * seq_len, :] = jnp.concatenate(outs, axis=1).astype(o_ref.dtype)


def _ctx_attention(z, sink, batch, seq_len):
    rows = CTX_BATCHES * seq_len
    return pl.pallas_call(
        functools.partial(_ctx_attn_kernel, seq_len),
        grid=(batch // CTX_BATCHES,),
        in_specs=[pl.BlockSpec(memory_space=pltpu.SMEM),
                  pl.BlockSpec((rows, QKV_W), lambda b: (b, 0))],
        out_specs=pl.BlockSpec((rows, ATTN_W), lambda b: (b, 0)),
        out_shape=jax.ShapeDtypeStruct((batch * seq_len, ATTN_W), bf16),
        compiler_params=_params("parallel"),
        name="ctx_attn",
    )(sink, z)


def _lat_attn_kernel(nb, sink_ref, zp_ref, zc_ref, zn_ref, ck_ref, cv_ref, o_ref):
    step = pl.program_id(1)
    zc_all = zc_ref[...]
    blocks = [zp_ref[...]] + [zc_all[q * BLOCK:(q + 1) * BLOCK] for q in range(LAT_QBLOCKS)] + [zn_ref[...]]
    ck, cv = ck_ref[0], cv_ref[0]
    width = Q_GROUP * BLOCK
    j = lax.broadcasted_iota(jnp.int32, (BLOCK, width), 0)
    r = lax.broadcasted_iota(jnp.int32, (BLOCK, width), 1) % BLOCK
    for q in range(LAT_QBLOCKS):
        i = step * LAT_QBLOCKS + q
        zp, zc, zn = blocks[q], blocks[q + 1], blocks[q + 2]
        ok_prev = j >= r + jnp.where(i > 0, 0, BLOCK)
        ok_next = j <= r - jnp.where(i < nb - 1, 0, BLOCK)
        outs = []
        for kv in range(N_KV):
            ks = slice(ATTN_W + kv * HEAD_DIM, ATTN_W + (kv + 1) * HEAD_DIM)
            vs = slice(ATTN_W + KV_W + kv * HEAD_DIM, ATTN_W + KV_W + (kv + 1) * HEAD_DIM)
            cs = slice(kv * HEAD_DIM, (kv + 1) * HEAD_DIM)
            qs, sink = _stack_queries(zc, sink_ref, kv)
            st = jnp.concatenate([
                jnp.where(ok_prev, _scores_t(zp[:, ks], qs), NEG),
                _scores_t(zc[:, ks], qs),
                jnp.where(ok_next, _scores_t(zn[:, ks], qs), NEG),
                _scores_t(ck[:, cs], qs)], axis=0)
            vals = jnp.concatenate([zp[:, vs], zc[:, vs], zn[:, vs], cv[:, cs]], axis=0)
            outs += _softmax_pv_t(st, sink, vals, BLOCK)
        o_ref[q * BLOCK:(q + 1) * BLOCK, :] = jnp.concatenate(outs, axis=1).astype(o_ref.dtype)


def _lat_attention(z, sink, ck, cv, batch, seq_len):
    nb = seq_len // BLOCK
    ns = nb // LAT_QBLOCKS
    blk = lambda f: pl.BlockSpec((BLOCK, QKV_W), f)
    past = ck.shape[1]
    cache = pl.BlockSpec((1, past, KV_W), lambda b, i: (b, 0, 0))
    return pl.pallas_call(
        functools.partial(_lat_attn_kernel, nb),
        grid=(batch, ns),
        in_specs=[pl.BlockSpec(memory_space=pltpu.SMEM),
                  blk(lambda b, i: (b * nb + jnp.maximum(i * LAT_QBLOCKS - 1, 0), 0)),
                  pl.BlockSpec((LAT_QBLOCKS * BLOCK, QKV_W), lambda b, i: (b * ns + i, 0)),
                  blk(lambda b, i: (b * nb + jnp.minimum((i + 1) * LAT_QBLOCKS, nb - 1), 0)),
                  cache, cache],
        out_specs=pl.BlockSpec((LAT_QBLOCKS * BLOCK, ATTN_W), lambda b, i: (b * ns + i, 0)),
        out_shape=jax.ShapeDtypeStruct((batch * seq_len, ATTN_W), bf16),
        compiler_params=_params("parallel", "parallel"),
        name="lat_attn",
    )(sink, z, z, z, ck, cv)


HY_TL = 256
HALO_ROWS = 2 * SUBLANES


def _short_conv(z, prev_tile, next_tile, has_prev, has_next, w, b):
    z = z.astype(f32)
    rows = z.shape[0]
    row = lax.broadcasted_iota(jnp.int32, z.shape, 0)
    prev_row = jnp.where(has_prev, prev_tile.astype(f32)[HALO_ROWS - 1:HALO_ROWS, :], 0.0)
    next_row = jnp.where(has_next, next_tile.astype(f32)[0:1, :], 0.0)
    z_prev = jnp.where(row == 0, prev_row, pltpu.roll(z, 1, axis=0))
    z_next = jnp.where(row == rows - 1, next_row, pltpu.roll(z, rows - 1, axis=0))
    zc = z_prev * w[0:1] + z * w[1:2] + z_next * w[2:3] + b
    return zc[:, :HY_W], zc[:, HY_W:2 * HY_W] * zc[:, 2 * HY_W:]


def _hy_filter_kernel(seq_len, feats_ref, win_ref, w1_ref, b1_ref, w2_ref, b2_ref, w3_ref, fr_ref, o_ref):
    hi = lax.Precision.HIGHEST
    fr = fr_ref[...]
    h = jnp.sin(fr * (jnp.dot(w1_ref[...], feats_ref[...], precision=hi, preferred_element_type=f32) + b1_ref[...]))
    h = jnp.sin(fr * (jnp.dot(w2_ref[...], h, precision=hi, preferred_element_type=f32) + b2_ref[...]))
    h = jnp.dot(h.T, w3_ref[...], precision=hi, preferred_element_type=f32)
    win = win_ref[...]
    row = lax.broadcasted_iota(jnp.int32, (seq_len, HY_W), 0)
    o_ref[0:seq_len, :] = h[:, :HY_W] * win
    o_ref[seq_len:, :] = jnp.where(row == 0, 0.0, h[:, HY_W:] * win)


def _hy_filter(seq_len, consts, w1, b1, w2, b2, w3, freq):
    w1t = jnp.zeros((FILTER_HID, LANES), f32).at[:, :FILTER_EMB].set(w1.T)
    args = (consts["feats"], consts["window"], w1t, b1.reshape(-1, 1), w2.T, b2.reshape(-1, 1), w3, freq.reshape(-1, 1))
    return pl.pallas_call(
        functools.partial(_hy_filter_kernel, seq_len),
        in_specs=[pl.BlockSpec(memory_space=pltpu.VMEM)] * len(args),
        out_specs=pl.BlockSpec(memory_space=pltpu.VMEM),
        out_shape=jax.ShapeDtypeStruct((2 * seq_len, HY_W), f32),
        compiler_params=pltpu.CompilerParams(vmem_limit_bytes=VMEM_LIMIT),
        name="hy_filter",
    )(*args)


def _hy_tiles(seq_len):
    return min(TM, seq_len // 2), min(TM, seq_len)


def _hy_fwd_kernel(nb, tk, seq_len, m_ref, zh_ref, cw_ref, cb_ref, hb_ref, r_ref, z_ref, rhs_ref):
    i = pl.program_id(0)
    nc = seq_len // HY_TL

    @pl.when(i == 0)
    def _():
        w, bias = cw_ref[...], cb_ref[...]
        for b in range(nb):
            def chunk(c, carry, b=b):
                r0 = pl.multiple_of(c * HY_TL, HY_TL)
                prev = zh_ref[b, pl.ds(pl.multiple_of(jnp.maximum(r0 - HALO_ROWS, 0), HALO_ROWS), HALO_ROWS), :]
                nxt = zh_ref[b, pl.ds(pl.multiple_of(jnp.minimum(r0 + HY_TL, seq_len - HALO_ROWS), HALO_ROWS),
                                      HALO_ROWS), :]
                _, u = _short_conv(zh_ref[b, pl.ds(r0, HY_TL), :], prev, nxt, c > 0, c < nc - 1, w, bias)
                rhs_ref[pl.ds(r0, HY_TL), b * HY_W:(b + 1) * HY_W] = u.astype(bf16)
                return carry
            lax.fori_loop(0, nc, chunk, 0)
        rhs_ref[:, nb * HY_W:(nb + 1) * HY_W] = hb_ref[0].astype(bf16)
        rhs_ref[:, (nb + 1) * HY_W:] = hb_ref[1].astype(bf16)

    acc = jnp.dot(m_ref[...], rhs_ref[...], preferred_element_type=f32)
    p, w = acc[:tk], acc[tk:]
    c0 = nb * HY_W
    pf, pb = p[:, c0:c0 + HY_W], p[:, c0 + HY_W:]
    qf, qb = w[:, c0:c0 + HY_W], w[:, c0 + HY_W:]
    hr = pf + pb
    qs = qf + qb
    hi = qb - qf
    first = (lax.broadcasted_iota(jnp.int32, (tk, HY_W), 0) == 0) & (i == 0)
    for b in range(nb):
        pu, wu = p[:, b * HY_W:(b + 1) * HY_W], w[:, b * HY_W:(b + 1) * HY_W]
        r_ref[b] = jnp.where(first, pu * hr, 2.0 * (pu * hr + wu * hi)).astype(bf16)
        z_ref[b] = jnp.where(first, wu * qs, 2.0 * (wu * hr - pu * hi)).astype(bf16)


def _hy_fwd(mat, zh, conv_w, conv_b, hb, batch, seq_len):
    tk, _ = _hy_tiles(seq_len)
    out = pl.BlockSpec((batch, tk, HY_W), lambda i: (0, i, 0))
    return pl.pallas_call(
        functools.partial(_hy_fwd_kernel, batch, tk, seq_len),
        grid=(seq_len // tk,),
        in_specs=[pl.BlockSpec((2 * tk, seq_len), lambda i: (i, 0)),
                  pl.BlockSpec((batch, seq_len, 3 * HY_W), lambda i: (0, 0, 0)),
                  pl.BlockSpec((3, 3 * HY_W), lambda i: (0, 0)),
                  pl.BlockSpec((1, 3 * HY_W), lambda i: (0, 0)),
                  pl.BlockSpec((2, seq_len, HY_W), lambda i: (0, 0, 0))],
        out_specs=[out, out],
        out_shape=[jax.ShapeDtypeStruct((batch, seq_len, HY_W), bf16)] * 2,
        scratch_shapes=[pltpu.VMEM((seq_len, (batch + 2) * HY_W), bf16)],
        compiler_params=_params("arbitrary"),
        name="hy_fwd",
    )(mat, zh, conv_w, conv_b.reshape(1, 3 * HY_W), hb)


def _hy_inv_kernel(nb, seq_len, nt, m_ref, r_ref, z_ref, zp_ref, zc_ref, zn_ref, cw_ref, cb_ref, d_ref, o_ref,
                   rhs_ref):
    i = pl.program_id(0)

    @pl.when(i == 0)
    def _():
        for b in range(nb):
            rhs_ref[:seq_len, b * HY_W:(b + 1) * HY_W] = r_ref[b]
            rhs_ref[seq_len:, b * HY_W:(b + 1) * HY_W] = z_ref[b]

    acc = jnp.dot(m_ref[...], rhs_ref[...], preferred_element_type=f32)
    d, w, bias = d_ref[...], cw_ref[...], cb_ref[...]
    for b in range(nb):
        x0, u = _short_conv(zc_ref[b], zp_ref[b], zn_ref[b], i > 0, i < nt - 1, w, bias)
        y = acc[:, b * HY_W:(b + 1) * HY_W] * (1.0 / (2 * seq_len)) + u * d
        o_ref[b] = (x0 * y).astype(o_ref.dtype)


def _hy_inv(mat, r, zz, zh, conv_w, conv_b, d_bias, batch, seq_len):
    _, tm = _hy_tiles(seq_len)
    nt = seq_len // tm
    rh = tm // HALO_ROWS
    last = seq_len // HALO_ROWS - 1
    full = pl.BlockSpec((batch, seq_len, HY_W), lambda i: (0, 0, 0))
    halo = lambda f: pl.BlockSpec((batch, HALO_ROWS, 3 * HY_W), f)
    return pl.pallas_call(
        functools.partial(_hy_inv_kernel, batch, seq_len, nt),
        grid=(nt,),
        in_specs=[pl.BlockSpec((tm, 2 * seq_len), lambda i: (i, 0)), full, full,
                  halo(lambda i: (0, jnp.maximum(i * rh - 1, 0), 0)),
                  pl.BlockSpec((batch, tm, 3 * HY_W), lambda i: (0, i, 0)),
                  halo(lambda i: (0, jnp.minimum((i + 1) * rh, last), 0)),
                  pl.BlockSpec((3, 3 * HY_W), lambda i: (0, 0)),
                  pl.BlockSpec((1, 3 * HY_W), lambda i: (0, 0)),
                  pl.BlockSpec((1, HY_W), lambda i: (0, 0))],
        out_specs=pl.BlockSpec((batch, tm, HY_W), lambda i: (0, i, 0)),
        out_shape=jax.ShapeDtypeStruct((batch, seq_len, HY_W), bf16),
        scratch_shapes=[pltpu.VMEM((2 * seq_len, batch * HY_W), bf16)],
        compiler_params=_params("arbitrary"),
        name="hy_inv",
    )(mat, r, zz, zh, zh, zh, conv_w, conv_b.reshape(1, 3 * HY_W), d_bias.reshape(1, HY_W))


@functools.lru_cache(maxsize=None)
def _hy_consts_np(seq_len):
    n = 2 * seq_len
    k = np.arange(seq_len, dtype=np.int64)
    ang = (2.0 * np.pi / n) * ((k[:, None] * k[None, :]) % n).astype(np.float64)
    cosm = np.cos(ang)
    sinm = np.sin(ang)
    sinm[0, :] = np.where(k % 2 == 0, 1.0, -1.0)
    tk, _ = _hy_tiles(seq_len)
    fwd = np.concatenate([np.concatenate([cosm[i:i + tk], sinm[i:i + tk]], axis=0)
                          for i in range(0, seq_len, tk)], axis=0)
    inv = np.concatenate([cosm, sinm.T], axis=1)
    t = np.linspace(0.0, 1.0, seq_len, dtype=np.float32)[:, None]
    bands = (FILTER_EMB - 1) // 2
    fb = np.linspace(1e-4, bands - 1, bands, dtype=np.float32)[None, :]
    w = (np.float32(2.0 * math.pi) * np.arange(seq_len, dtype=np.float32)[:, None] / np.float32(seq_len)).astype(np.float32)
    feats = np.concatenate([t, np.cos(fb * w), -np.sin(fb * w)], axis=-1).astype(np.float32)
    feats = np.ascontiguousarray(np.pad(feats, ((0, 0), (0, LANES - FILTER_EMB))).T)
    decay_hi = math.log(HY_DECAY_TARGET) / HY_DECAY_HI_PCT
    decay_lo = math.log(HY_DECAY_TARGET) / HY_DECAY_LO_PCT
    deltas = np.abs(np.linspace(decay_lo, decay_hi, HY_W, dtype=np.float32))
    window = np.exp(-t * deltas[None, :]).astype(np.float32)
    return fwd, inv, feats, window


def _hy_consts(seq_len):
    fwd, inv, feats, window = _hy_consts_np(seq_len)
    as_bf = lambda m: jnp.asarray(m, dtype=f32).astype(bf16)
    return {"fwd": as_bf(fwd), "inv": as_bf(inv), "feats": jnp.asarray(feats), "window": jnp.asarray(window)}


def _hyena(zh, lp, consts, batch, seq_len):
    zh = zh.reshape(batch, seq_len, 3 * HY_W)
    hb = _hy_filter(seq_len, consts, lp["hy_f_w1"], lp["hy_f_b1"], lp["hy_f_w2"], lp["hy_f_b2"],
                    lp["hy_f_w3"], lp["hy_freq"]).reshape(2, seq_len, HY_W)
    r, zz = _hy_fwd(consts["fwd"], zh, lp["hy_conv_w"], lp["hy_conv_b"], hb, batch, seq_len)
    yh = _hy_inv(consts["inv"], r, zz, zh, lp["hy_conv_w"], lp["hy_conv_b"], lp["hy_bias"], batch, seq_len)
    return yh.reshape(batch * seq_len, HY_W)


def _gelu(x):
    return 0.5 * x * (1.0 + jnp.tanh(math.sqrt(2.0 / math.pi) * (x + 0.044715 * (x * x * x))))


def _gmlp(z, ws_ref, bias):
    g = _gelu(z)
    outs = []
    for c in range(TM // GM_CHUNK):
        rows = slice(c * GM_CHUNK, (c + 1) * GM_CHUNK)
        u = g[rows, :GM_W]
        v = g[rows, GM_W:].astype(bf16)
        s = jnp.concatenate(
            [jnp.dot(ws_ref[h], v[:, h * GM_HEAD_DIM:(h + 1) * GM_HEAD_DIM], preferred_element_type=f32)
             for h in range(GM_HEADS)], axis=1)
        outs.append(u * (s + bias))
    return jnp.concatenate(outs, axis=0)


def _out_kernel(a_ref, yh_ref, zg_ref, ws_ref, gb_ref, x_ref, mod_ref, og_ref, w_ref, g2_ref, rw_ref,
                x1_ref, h2_ref, aff_ref):
    og = og_ref[...]
    yg = _gmlp(zg_ref[...].astype(f32), ws_ref, gb_ref[...])
    mixed_in = jnp.concatenate([
        _rms(a_ref[...].astype(f32)) * og[:, :ATTN_W],
        _rms(yh_ref[...].astype(f32)) * og[:, ATTN_W:ATTN_W + HY_W],
        _rms(yg) * og[:, ATTN_W + HY_W:]], axis=1)
    mixed = jnp.dot(mixed_in.astype(bf16), w_ref[...], preferred_element_type=f32)
    m = mod_ref[0]
    x1 = x_ref[...] + m[2:3] * mixed
    x1_ref[...] = x1
    h2 = _rms(x1) * g2_ref[...] * (1.0 + m[4:5]) + m[3:4]
    for j in range(ROW_CHUNKS):
        h2_ref[pl.ds(j, TM, stride=ROW_CHUNKS), :] = h2[:, j * LANES:(j + 1) * LANES]
    nt = (((1,), (1,)), ((), ()))
    h_hi = h2.astype(bf16)
    h_lo = (h2 - h_hi.astype(f32)).astype(bf16)
    rw = rw_ref[...]
    t = lax.dot_general(rw, h_hi, nt, preferred_element_type=f32)
    logits = (t[:N_EXPERTS] + t[N_EXPERTS:]) + lax.dot_general(rw[:N_EXPERTS], h_lo, nt, preferred_element_type=f32)
    e = jnp.exp(logits - jnp.max(logits, axis=0, keepdims=True))
    aff_ref[...] = e / jnp.sum(e, axis=0, keepdims=True)


def _out_proj(a, yh, zg, ws_bf, gm_b, x, mod, out_g, w_out_bf, g2, router_wt):
    n = x.shape[0]
    per_mod = n // mod.shape[0]
    row = lambda w: pl.BlockSpec((TM, w), lambda i: (i, 0))
    const = lambda s: pl.BlockSpec(s, lambda i: (0,) * len(s))
    gm_bias = jnp.repeat(gm_b.T, GM_HEAD_DIM, axis=1)
    return pl.pallas_call(
        _out_kernel,
        grid=(n // TM,),
        in_specs=[row(ATTN_W), row(HY_W), row(2 * GM_W), const((GM_HEADS, GM_CHUNK, GM_CHUNK)),
                  const((GM_CHUNK, GM_W)), row(D_MODEL),
                  pl.BlockSpec((1, 6, D_MODEL), lambda i: (i * TM // per_mod, 0, 0)),
                  const((1, MIX_W)), const((MIX_W, D_MODEL)), const((1, D_MODEL)), const((2 * N_EXPERTS, D_MODEL))],
        out_specs=[row(D_MODEL), pl.BlockSpec((TM * ROW_CHUNKS, LANES), lambda i: (i, 0)),
                   pl.BlockSpec((N_EXPERTS, TM), lambda i: (0, i))],
        out_shape=[jax.ShapeDtypeStruct((n, D_MODEL), f32), jax.ShapeDtypeStruct((n * ROW_CHUNKS, LANES), f32),
                   jax.ShapeDtypeStruct((N_EXPERTS, n), f32)],
        compiler_params=_params("parallel"),
        name="out_proj",
    )(a, yh, zg, ws_bf, gm_bias, x, mod, out_g.reshape(1, MIX_W), w_out_bf, g2.reshape(1, D_MODEL), router_wt)


RT_CHUNK = 512


def _prefix_incl(x01, tri):
    n = x01.shape[1]
    carry = jnp.zeros((x01.shape[0], 1), f32)
    parts = []
    for c in range(n // RT_CHUNK):
        piece = x01[:, c * RT_CHUNK:(c + 1) * RT_CHUNK]
        parts.append(jnp.dot(piece.astype(bf16), tri, preferred_element_type=f32) + carry)
        carry = carry + jnp.sum(piece, axis=1, keepdims=True)
    return jnp.concatenate(parts, axis=1)


def _route_kernel(cap, aff_ref, idx_ref, gate_ref):
    aff = aff_ref[...]
    n = aff.shape[1]

    def step(it, t):
        cand = t | (jnp.int32(1) << (30 - it))
        cnt = jnp.sum(jnp.where(aff >= lax.bitcast_convert_type(cand, f32), 1.0, 0.0), axis=1, keepdims=True)
        return jnp.where(cnt >= cap, cand, t)

    thr_bits = lax.fori_loop(0, 31, step, jnp.zeros((aff.shape[0], 1), jnp.int32))
    thr = lax.bitcast_convert_type(thr_bits, f32)
    gt = jnp.where(aff > thr, 1.0, 0.0)
    eq = jnp.where(aff == thr, 1.0, 0.0)
    room = cap - jnp.sum(gt, axis=1, keepdims=True)
    r = lax.broadcasted_iota(jnp.int32, (RT_CHUNK, RT_CHUNK), 0)
    c = lax.broadcasted_iota(jnp.int32, (RT_CHUNK, RT_CHUNK), 1)
    tri = jnp.where(r <= c, 1.0, 0.0).astype(bf16)
    sel = jnp.maximum(gt, jnp.where(_prefix_incl(eq, tri) <= room, eq, 0.0))
    slot = _prefix_incl(sel, tri) - 1.0

    tok = lax.broadcasted_iota(jnp.int32, aff.shape, 1)
    dist = jnp.where(sel > 0.0, tok - slot.astype(jnp.int32), 0)
    g = aff
    for b in range(max(1, (n - 1).bit_length())):
        sh = 1 << b
        dist_s = pltpu.roll(dist, n - sh, axis=1)
        take = (dist_s & sh) != 0
        leave = (dist & sh) != 0
        tok = jnp.where(take, pltpu.roll(tok, n - sh, axis=1), tok)
        g = jnp.where(take, pltpu.roll(g, n - sh, axis=1), g)
        dist = jnp.where(take, dist_s, jnp.where(leave, 0, dist))
    idx_ref[...] = tok[:, :cap]
    gate_ref[...] = g[:, :cap]


def _route(aff_t, cap):
    return pl.pallas_call(
        functools.partial(_route_kernel, cap),
        in_specs=[pl.BlockSpec(memory_space=pltpu.VMEM)],
        out_specs=[pl.BlockSpec(memory_space=pltpu.VMEM)] * 2,
        out_shape=[jax.ShapeDtypeStruct((N_EXPERTS, cap), jnp.int32), jax.ShapeDtypeStruct((N_EXPERTS, cap), f32)],
        compiler_params=pltpu.CompilerParams(vmem_limit_bytes=VMEM_LIMIT),
        name="route",
    )(aff_t)


GATHER_UNROLL = 8
SCATTER_UNROLL = 8


def _moe_ffn_kernel(cap, stride, idx_ref, h_ref, w1_ref, w3_ref, w2_ref, y_ref, tile_ref, xb_ref, acc_ref):
    base = pl.program_id(0) * cap

    def gather(q, carry):
        for t in range(GATHER_UNROLL):
            r = q * GATHER_UNROLL + t
            src = pl.multiple_of(idx_ref[base + r] * ROW_CHUNKS, ROW_CHUNKS)
            tile_ref[pl.ds(r, ROW_CHUNKS, stride=stride), :] = h_ref[pl.ds(src, ROW_CHUNKS), :]
        return carry
    lax.fori_loop(0, cap // GATHER_UNROLL, gather, 0)
    xb_ref[...] = jnp.concatenate(
        [tile_ref[j * stride:j * stride + cap, :] for j in range(ROW_CHUNKS)], axis=1).astype(bf16)

    xb = xb_ref[...]
    for f in range(D_EXPERT // MOE_F_TILE):
        cols = slice(f * MOE_F_TILE, (f + 1) * MOE_F_TILE)
        a = jnp.dot(xb, w1_ref[0, 0, :, cols].astype(bf16), preferred_element_type=f32)
        b = jnp.dot(xb, w3_ref[0, 0, :, cols].astype(bf16), preferred_element_type=f32)
        he = (a * _sigmoid(a) * b).astype(bf16)
        part = jnp.dot(he, w2_ref[0, 0, cols, :].astype(bf16), preferred_element_type=f32)
        if f == 0:
            acc_ref[...] = part
        else:
            acc_ref[...] += part

    y = acc_ref[...]
    for j in range(ROW_CHUNKS):
        y_ref[0, j * stride:j * stride + cap, :] = y[:, j * LANES:(j + 1) * LANES]
        y_ref[0, j * stride + cap:(j + 1) * stride, :] = jnp.zeros((stride - cap, LANES), f32)


def _moe_ffn(layer, h3, idx, w1, w3, w2):
    cap = idx.shape[1]
    stride = cap + SUBLANES
    grid_spec = pltpu.PrefetchScalarGridSpec(
        num_scalar_prefetch=1,
        grid=(N_EXPERTS,),
        in_specs=[
            pl.BlockSpec(memory_space=pltpu.VMEM),
            pl.BlockSpec((1, 1, D_MODEL, D_EXPERT), lambda e, idx: (layer, e, 0, 0)),
            pl.BlockSpec((1, 1, D_MODEL, D_EXPERT), lambda e, idx: (layer, e, 0, 0)),
            pl.BlockSpec((1, 1, D_EXPERT, D_MODEL), lambda e, idx: (layer, e, 0, 0)),
        ],
        out_specs=pl.BlockSpec((1, ROW_CHUNKS * stride, LANES), lambda e, idx: (e, 0, 0)),
        scratch_shapes=[pltpu.VMEM((ROW_CHUNKS * stride, LANES), f32),
                        pltpu.VMEM((cap, D_MODEL), bf16),
                        pltpu.VMEM((cap, D_MODEL), f32)],
    )
    return pl.pallas_call(
        functools.partial(_moe_ffn_kernel, cap, stride),
        grid_spec=grid_spec,
        out_shape=jax.ShapeDtypeStruct((N_EXPERTS, ROW_CHUNKS * stride, LANES), f32),
        compiler_params=_params("arbitrary"),
        name="moe_ffn",
    )(idx.reshape(-1), h3, w1, w3, w2)


def _moe_combine_kernel(cap, stride, final, idx_ref, gate_ref, y_ref, x_ref, mod_ref, g_ref, o_ref, acc_ref):
    s = pl.program_id(0)

    @pl.when(s == 0)
    def _():
        acc_ref[...] = jnp.zeros_like(acc_ref)

    @pl.when(s < N_EXPERTS)
    def _():
        base = s * cap

        def scatter(q, carry):
            rows = [pl.multiple_of(idx_ref[base + q * SCATTER_UNROLL + t] * ROW_CHUNKS, ROW_CHUNKS)
                    for t in range(SCATTER_UNROLL)]
            new = [acc_ref[pl.ds(rows[t], ROW_CHUNKS), :]
                   + gate_ref[base + q * SCATTER_UNROLL + t]
                   * y_ref[0, pl.ds(q * SCATTER_UNROLL + t, ROW_CHUNKS, stride=stride), :]
                   for t in range(SCATTER_UNROLL)]
            for t in range(SCATTER_UNROLL):
                acc_ref[pl.ds(rows[t], ROW_CHUNKS), :] = new[t]
            return carry
        lax.fori_loop(0, cap // SCATTER_UNROLL, scatter, 0)

    @pl.when(s >= N_EXPERTS)
    def _():
        first = pl.multiple_of((s - N_EXPERTS) * (TM * ROW_CHUNKS), TM * ROW_CHUNKS)
        moe = jnp.concatenate([acc_ref[pl.ds(first + j, TM, stride=ROW_CHUNKS), :] for j in range(ROW_CHUNKS)],
                              axis=1)
        x = x_ref[...] + mod_ref[0][5:6] * moe
        o_ref[...] = _rms(x) * g_ref[...] if final else x


def _moe_combine(idx, gate, y_cm, x1, mod, final_g, final):
    n = x1.shape[0]
    cap = idx.shape[1]
    stride = y_cm.shape[1] // ROW_CHUNKS
    per_mod = n // mod.shape[0]
    tile = lambda s: jnp.maximum(s - N_EXPERTS, 0)
    row = pl.BlockSpec((TM, D_MODEL), lambda s, *_: (tile(s), 0))
    grid_spec = pltpu.PrefetchScalarGridSpec(
        num_scalar_prefetch=2,
        grid=(N_EXPERTS + n // TM,),
        in_specs=[pl.BlockSpec((1, ROW_CHUNKS * stride, LANES), lambda s, *_: (jnp.minimum(s, N_EXPERTS - 1), 0, 0)),
                  row,
                  pl.BlockSpec((1, 6, D_MODEL), lambda s, *_: (tile(s) * TM // per_mod, 0, 0)),
                  pl.BlockSpec((1, D_MODEL), lambda s, *_: (0, 0))],
        out_specs=row,
        scratch_shapes=[pltpu.VMEM((n * ROW_CHUNKS, LANES), f32)],
    )
    return pl.pallas_call(
        functools.partial(_moe_combine_kernel, cap, stride, final),
        grid_spec=grid_spec,
        out_shape=jax.ShapeDtypeStruct((n, D_MODEL), f32),
        compiler_params=_params("arbitrary"),
        name="moe_combine",
    )(idx.reshape(-1), gate.reshape(-1), y_cm, x1, mod, final_g.reshape(1, D_MODEL))


def _expert_choice(layer, h3, aff_t, x1, mod, final_g, final, w1, w3, w2):
    n = x1.shape[0]
    cap = max(1, EC_CAPACITY * n // N_EXPERTS)
    idx, gate = _route(aff_t, cap)
    y_cm = _moe_ffn(layer, h3, idx, w1, w3, w2)
    return _moe_combine(idx, gate, y_cm, x1, mod, final_g, final)


def _split_bf16(w):
    hi = w.astype(bf16)
    lo = (w - hi.astype(f32)).astype(bf16)
    return jnp.concatenate([hi, lo], axis=0)


def _stream(x, mods, layers, experts, batch, seq_len, final_g, caches=None):
    consts = _hy_consts(seq_len)
    rope_tabs = _rope_tables(seq_len) if caches is not None else None
    kvs = []
    for l, lp in enumerate(layers):
        qkv, zh, zg = _in_proj(x, mods[l], lp["norm1_g"], lp["w_in"], seq_len, rope_tabs=rope_tabs)
        if caches is None:
            a = _ctx_attention(qkv, lp["attn_sink"], batch, seq_len)
            kvs.append((qkv[:, ATTN_W:ATTN_W + KV_W], qkv[:, ATTN_W + KV_W:]))
        else:
            a = _lat_attention(qkv, lp["attn_sink"], caches[0][:, l], caches[1][:, l], batch, seq_len)
        yh = _hyena(zh, lp, consts, batch, seq_len)
        x1, h3, aff_t = _out_proj(a, yh, zg, lp["gm_ws"], lp["gm_b"], x, mods[l], lp["out_norm_g"], lp["w_out"],
                                  lp["norm2_g"], lp["router_wt"])
        x = _expert_choice(l, h3, aff_t, x1, mods[l], final_g, l == len(layers) - 1, *experts)
    return x, kvs


def kernel(x_prompt, x_sample, c, cache_k, cache_v, c_ctx, norm1_g, norm2_g, ada_w, ada_b, w_in, attn_sink,
           hy_conv_w, hy_conv_b, hy_f_w1, hy_f_b1, hy_f_w2, hy_f_b2, hy_f_w3, hy_freq, hy_bias, gm_ws, gm_b,
           out_norm_g, w_out, router_w, exp_w1, exp_w3, exp_w2, final_g):
    batch, seq, _ = x_prompt.shape
    dbatch, dseq, _ = x_sample.shape
    past = cache_k.shape[2]

    cvec = jnp.zeros((SUBLANES, D_MODEL), f32).at[0].set(c_ctx).at[1:1 + dbatch].set(c)
    mod = _ada(cvec, ada_w, ada_b)
    mods_ctx = [mod[l, 0:1].reshape(1, 6, D_MODEL) for l in range(DEPTH)]
    mods_lat = [mod[l, 1:1 + dbatch].reshape(dbatch, 6, D_MODEL) for l in range(DEPTH)]

    w_in_bf, w_out_bf, gm_ws_bf = w_in.astype(bf16), w_out.astype(bf16), gm_ws.astype(bf16)
    layers = []
    for l in range(DEPTH):
        layers.append({
            "norm1_g": norm1_g[l], "norm2_g": norm2_g[l], "w_in": w_in_bf[l], "attn_sink": attn_sink[l],
            "hy_conv_w": hy_conv_w[l], "hy_conv_b": hy_conv_b[l], "hy_f_w1": hy_f_w1[l], "hy_f_b1": hy_f_b1[l],
            "hy_f_w2": hy_f_w2[l], "hy_f_b2": hy_f_b2[l], "hy_f_w3": hy_f_w3[l], "hy_freq": hy_freq[l],
            "hy_bias": hy_bias[l], "gm_ws": gm_ws_bf[l], "gm_b": gm_b[l], "out_norm_g": out_norm_g[l],
            "w_out": w_out_bf[l], "router_wt": _split_bf16(router_w[l].T)})
    experts = (exp_w1, exp_w3, exp_w2)

    yp, kvs = _stream(x_prompt.reshape(batch * seq, D_MODEL), mods_ctx, layers, experts, batch, seq, final_g)
    caches = (cache_k.reshape(dbatch, DEPTH, past, KV_W), cache_v.reshape(dbatch, DEPTH, past, KV_W))
    ys, _ = _stream(x_sample.reshape(dbatch * dseq, D_MODEL), mods_lat, layers, experts, dbatch, dseq, final_g,
                   caches=caches)

    new_k = jnp.stack([k.reshape(batch, seq, N_KV, HEAD_DIM) for k, _ in kvs], axis=1)
    new_v = jnp.stack([v.reshape(batch, seq, N_KV, HEAD_DIM) for _, v in kvs], axis=1)
    return (yp.reshape(batch, seq, D_MODEL), ys.reshape(dbatch, dseq, D_MODEL), new_k, new_v)
```

```python
import functools
import math

import numpy as np
import jax
import jax.numpy as jnp
from jax import lax
from jax.experimental import pallas as pl
from jax.experimental.pallas import tpu as pltpu

f32 = jnp.float32
bf16 = jnp.bfloat16

D_MODEL = 1024
DEPTH = 2
GRID_W = 64
BLOCK = 128
N_HEADS = 8
N_KV = 2
HEAD_DIM = 64
Q_GROUP = N_HEADS // N_KV
ATTN_W = N_HEADS * HEAD_DIM
KV_W = N_KV * HEAD_DIM
QKV_W = ATTN_W + 2 * KV_W
HY_W = 256
GM_W = 256
GM_HEADS = 4
GM_HEAD_DIM = GM_W // GM_HEADS
GM_CHUNK = 128
MIX_W = ATTN_W + HY_W + GM_W
IN_W = ATTN_W + 2 * KV_W + 3 * HY_W + 2 * GM_W
FILTER_EMB = 33
FILTER_HID = 64
HY_DECAY_HI_PCT = 0.3
HY_DECAY_LO_PCT = 1.5
HY_DECAY_TARGET = 1e-2
N_EXPERTS = 16
EC_CAPACITY = 2
D_EXPERT = 1024
ROPE_THETA = 10000.0
EPS = 1e-6
NEG = -1e30

LANES = 128
SUBLANES = 8
ROW_CHUNKS = D_MODEL // LANES
VMEM_LIMIT = 56 * 1024 * 1024
TM = 512
IN_TM = 1024
MOE_F_TILE = 256


def _params(*sem):
    return pltpu.CompilerParams(dimension_semantics=sem, vmem_limit_bytes=VMEM_LIMIT)


def _rms(x):
    return x * lax.rsqrt(jnp.mean(x * x, axis=-1, keepdims=True) + EPS)


def _sigmoid(x):
    return 1.0 / (1.0 + jnp.exp(-x))


def _ada_kernel(c_ref, w_ref, b_ref, o_ref):
    c = c_ref[...]
    s = (c * _sigmoid(c)).astype(bf16)
    o_ref[0] = jnp.dot(s, w_ref[0].astype(bf16), preferred_element_type=f32) + b_ref[0]


def _ada(cvec, ada_w, ada_b):
    nt = 6
    return pl.pallas_call(
        _ada_kernel,
        grid=(DEPTH, nt),
        in_specs=[
            pl.BlockSpec((SUBLANES, D_MODEL), lambda l, j: (0, 0)),
            pl.BlockSpec((1, D_MODEL, D_MODEL), lambda l, j: (l, 0, j)),
            pl.BlockSpec((1, 1, D_MODEL), lambda l, j: (l, 0, j)),
        ],
        out_specs=pl.BlockSpec((1, SUBLANES, D_MODEL), lambda l, j: (l, 0, j)),
        out_shape=jax.ShapeDtypeStruct((DEPTH, SUBLANES, 6 * D_MODEL), f32),
        compiler_params=_params("arbitrary", "arbitrary"),
        name="ada",
    )(cvec, ada_w, ada_b.reshape(DEPTH, 1, 6 * D_MODEL))


def _rope_swap(x):
    w = x.shape[-1]
    lane = lax.broadcasted_iota(jnp.int32, x.shape, 1)
    first = (lane % 32) < 16
    return jnp.where(first, pltpu.roll(x, w - 16, axis=1), pltpu.roll(x, 16, axis=1))


def _in_kernel(rope, x_ref, mod_ref, g_ref, w_ref, *refs):
    if rope:
        cos_ref, sin_ref, qkv_ref, zh_ref, zg_ref = refs
    else:
        qkv_ref, zh_ref, zg_ref = refs
    m = mod_ref[0]
    h = _rms(x_ref[...]) * g_ref[...] * (1.0 + m[1:2]) + m[0:1]
    z = jnp.dot(h.astype(bf16), w_ref[...], preferred_element_type=f32)
    if rope:
        reps = (ATTN_W + KV_W) // LANES
        cos = jnp.concatenate([cos_ref[...]] * reps, axis=1)
        sin = jnp.concatenate([sin_ref[...]] * reps, axis=1)
        qk = z[:, :ATTN_W + KV_W]
        qkv_ref[:, :ATTN_W + KV_W] = qk * cos + _rope_swap(qk) * sin
        qkv_ref[:, ATTN_W + KV_W:] = z[:, ATTN_W + KV_W:QKV_W]
    else:
        qkv_ref[...] = z[:, :QKV_W]
    zh_ref[...] = z[:, QKV_W:QKV_W + 3 * HY_W].astype(bf16)
    zg_ref[...] = z[:, QKV_W + 3 * HY_W:].astype(bf16)


def _in_proj(x, mod, g1, w_in_bf, seq_len, rope_tabs=None):
    n = x.shape[0]
    rope = rope_tabs is not None
    per_mod = n // mod.shape[0]
    tm = IN_TM
    in_specs = [pl.BlockSpec((tm, D_MODEL), lambda i: (i, 0)),
                pl.BlockSpec((1, 6, D_MODEL), lambda i: (i * tm // per_mod, 0, 0)),
                pl.BlockSpec((1, D_MODEL), lambda i: (0, 0)),
                pl.BlockSpec((D_MODEL, IN_W), lambda i: (0, 0))]
    args = [x, mod, g1.reshape(1, D_MODEL), w_in_bf]
    if rope:
        nt = seq_len // tm
        tab = pl.BlockSpec((tm, LANES), lambda i: (i % nt, 0))
        in_specs += [tab, tab]
        args += list(rope_tabs)
    return pl.pallas_call(
        functools.partial(_in_kernel, rope),
        grid=(n // tm,),
        in_specs=in_specs,
        out_specs=[pl.BlockSpec((tm, w), lambda i: (i, 0)) for w in (QKV_W, 3 * HY_W, 2 * GM_W)],
        out_shape=[jax.ShapeDtypeStruct((n, QKV_W), f32), jax.ShapeDtypeStruct((n, 3 * HY_W), bf16),
                   jax.ShapeDtypeStruct((n, 2 * GM_W), bf16)],
        compiler_params=_params("parallel"),
        name="in_proj",
    )(*args)


def _rope_tables(seq_len):
    nf = HEAD_DIM // 4
    t = np.arange(seq_len)
    inv = (ROPE_THETA ** (-np.arange(nf, dtype=np.float32) / nf)).astype(np.float32)
    d = np.arange(HEAD_DIM)
    pos = np.where((d // 32)[None, :] == 0, (t // GRID_W)[:, None], (t % GRID_W)[:, None]).astype(np.float32)
    ang = (pos * inv[d % nf][None, :]).astype(np.float32)
    cos = np.cos(ang).astype(np.float32)
    sin = np.sin(ang).astype(np.float32) * np.where((d % 32) < 16, -1.0, 1.0)[None, :].astype(np.float32)
    return jnp.asarray(np.tile(cos, (1, 2))), jnp.asarray(np.tile(sin, (1, 2)))


LOG2E = math.log2(math.e)


def _stack_queries(z, sink_ref, kv):
    rows = z.shape[0]
    heads = range(kv * Q_GROUP, (kv + 1) * Q_GROUP)
    qs = jnp.concatenate([z[:, h * HEAD_DIM:(h + 1) * HEAD_DIM] for h in heads], axis=0)
    sink = jnp.concatenate([jnp.full((1, rows), sink_ref[h] * LOG2E, f32) for h in heads], axis=1)
    return (qs * (LOG2E / math.sqrt(HEAD_DIM))).astype(bf16), sink


def _scores_t(k, qs_bf):
    return lax.dot_general(k.astype(bf16), qs_bf, (((1,), (1,)), ((), ())), preferred_element_type=f32)


def _softmax_pv_t(st, sink, vals, rows):
    m = jnp.maximum(jnp.max(st, axis=0, keepdims=True), sink)
    pt = jnp.exp2(st - m).astype(bf16)
    v_bf = vals.astype(bf16)
    v_ext = jnp.concatenate([v_bf, jnp.ones_like(v_bf)], axis=1)
    ovt = lax.dot_general(v_ext, pt, (((0,), (0,)), ((), ())), preferred_element_type=f32)
    ot = ovt[:HEAD_DIM] / (ovt[HEAD_DIM:HEAD_DIM + 1] + jnp.exp2(sink - m))
    return [ot[:, g * rows:(g + 1) * rows].T for g in range(Q_GROUP)]


CTX_BATCHES = 2
LAT_QBLOCKS = 4


def _ctx_attn_kernel(seq_len, sink_ref, z_ref, o_ref):
    for s in range(CTX_BATCHES):
        z = z_ref[s * seq_len:(s + 1) * seq_len, :]
        outs = []
        for kv in range(N_KV):
            k = z[:, ATTN_W + kv * HEAD_DIM:ATTN_W + (kv + 1) * HEAD_DIM]
            v = z[:, ATTN_W + KV_W + kv * HEAD_DIM:ATTN_W + KV_W + (kv + 1) * HEAD_DIM]
            qs, sink = _stack_queries(z, sink_ref, kv)
            outs += _softmax_pv_t(_scores_t(k, qs), sink, v, seq_len)
        o_ref[s * seq_len:(s + 1) * seq_len, :] = jnp.concatenate(outs, axis=1).astype(o_ref.dtype)


def _ctx_attention(z, sink, batch, seq_len):
    rows = CTX_BATCHES * seq_len
    return pl.pallas_call(
        functools.partial(_ctx_attn_kernel, seq_len),
        grid=(batch // CTX_BATCHES,),
        in_specs=[pl.BlockSpec(memory_space=pltpu.SMEM),
                  pl.BlockSpec((rows, QKV_W), lambda b: (b, 0))],
        out_specs=pl.BlockSpec((rows, ATTN_W), lambda b: (b, 0)),
        out_shape=jax.ShapeDtypeStruct((batch * seq_len, ATTN_W), bf16),
        compiler_params=_params("parallel"),
        name="ctx_attn",
    )(sink, z)


def _lat_attn_kernel(nb, sink_ref, zp_ref, zc_ref, zn_ref, ck_ref, cv_ref, o_ref):
    step = pl.program_id(1)
    zc_all = zc_ref[...]
    blocks = [zp_ref[...]] + [zc_all[q * BLOCK:(q + 1) * BLOCK] for q in range(LAT_QBLOCKS)] + [zn_ref[...]]
    ck, cv = ck_ref[0], cv_ref[0]
    width = Q_GROUP * BLOCK
    j = lax.broadcasted_iota(jnp.int32, (BLOCK, width), 0)
    r = lax.broadcasted_iota(jnp.int32, (BLOCK, width), 1) % BLOCK
    for q in range(LAT_QBLOCKS):
        i = step * LAT_QBLOCKS + q
        zp, zc, zn = blocks[q], blocks[q + 1], blocks[q + 2]
        ok_prev = j >= r + jnp.where(i > 0, 0, BLOCK)
        ok_next = j <= r - jnp.where(i < nb - 1, 0, BLOCK)
        outs = []
        for kv in range(N_KV):
            ks = slice(ATTN_W + kv * HEAD_DIM, ATTN_W + (kv + 1) * HEAD_DIM)
            vs = slice(ATTN_W + KV_W + kv * HEAD_DIM, ATTN_W + KV_W + (kv + 1) * HEAD_DIM)
            cs = slice(kv * HEAD_DIM, (kv + 1) * HEAD_DIM)
            qs, sink = _stack_queries(zc, sink_ref, kv)
            st = jnp.concatenate([
                jnp.where(ok_prev, _scores_t(zp[:, ks], qs), NEG),
                _scores_t(zc[:, ks], qs),
                jnp.where(ok_next, _scores_t(zn[:, ks], qs), NEG),
                _scores_t(ck[:, cs], qs)], axis=0)
            vals = jnp.concatenate([zp[:, vs], zc[:, vs], zn[:, vs], cv[:, cs]], axis=0)
            outs += _softmax_pv_t(st, sink, vals, BLOCK)
        o_ref[q * BLOCK:(q + 1) * BLOCK, :] = jnp.concatenate(outs, axis=1).astype(o_ref.dtype)


def _lat_attention(z, sink, ck, cv, batch, seq_len):
    nb = seq_len // BLOCK
    ns = nb // LAT_QBLOCKS
    blk = lambda f: pl.BlockSpec((BLOCK, QKV_W), f)
    past = ck.shape[1]
    cache = pl.BlockSpec((1, past, KV_W), lambda b, i: (b, 0, 0))
    return pl.pallas_call(
        functools.partial(_lat_attn_kernel, nb),
        grid=(batch, ns),
        in_specs=[pl.BlockSpec(memory_space=pltpu.SMEM),
                  blk(lambda b, i: (b * nb + jnp.maximum(i * LAT_QBLOCKS - 1, 0), 0)),
                  pl.BlockSpec((LAT_QBLOCKS * BLOCK, QKV_W), lambda b, i: (b * ns + i, 0)),
                  blk(lambda b, i: (b * nb + jnp.minimum((i + 1) * LAT_QBLOCKS, nb - 1), 0)),
                  cache, cache],
        out_specs=pl.BlockSpec((LAT_QBLOCKS * BLOCK, ATTN_W), lambda b, i: (b * ns + i, 0)),
        out_shape=jax.ShapeDtypeStruct((batch * seq_len, ATTN_W), bf16),
        compiler_params=_params("parallel", "parallel"),
        name="lat_attn",
    )(sink, z, z, z, ck, cv)


HY_TL = 256
HALO_ROWS = 2 * SUBLANES


def _short_conv(z, prev_tile, next_tile, has_prev, has_next, w, b):
    z = z.astype(f32)
    rows = z.shape[0]
    row = lax.broadcasted_iota(jnp.int32, z.shape, 0)
    prev_row = jnp.where(has_prev, prev_tile.astype(f32)[HALO_ROWS - 1:HALO_ROWS, :], 0.0)
    next_row = jnp.where(has_next, next_tile.astype(f32)[0:1, :], 0.0)
    z_prev = jnp.where(row == 0, prev_row, pltpu.roll(z, 1, axis=0))
    z_next = jnp.where(row == rows - 1, next_row, pltpu.roll(z, rows - 1, axis=0))
    zc = z_prev * w[0:1] + z * w[1:2] + z_next * w[2:3] + b
    return zc[:, :HY_W], zc[:, HY_W:2 * HY_W] * zc[:, 2 * HY_W:]


def _hy_filter_kernel(seq_len, feats_ref, win_ref, w1_ref, b1_ref, w2_ref, b2_ref, w3_ref, fr_ref, o_ref):
    hi = lax.Precision.HIGHEST
    fr = fr_ref[...]
    h = jnp.sin(fr * (jnp.dot(w1_ref[...], feats_ref[...], precision=hi, preferred_element_type=f32) + b1_ref[...]))
    h = jnp.sin(fr * (jnp.dot(w2_ref[...], h, precision=hi, preferred_element_type=f32) + b2_ref[...]))
    h = jnp.dot(h.T, w3_ref[...], precision=hi, preferred_element_type=f32)
    win = win_ref[...]
    row = lax.broadcasted_iota(jnp.int32, (seq_len, HY_W), 0)
    o_ref[0:seq_len, :] = h[:, :HY_W] * win
    o_ref[seq_len:, :] = jnp.where(row == 0, 0.0, h[:, HY_W:] * win)


def _hy_filter(seq_len, consts, w1, b1, w2, b2, w3, freq):
    w1t = jnp.zeros((FILTER_HID, LANES), f32).at[:, :FILTER_EMB].set(w1.T)
    args = (consts["feats"], consts["window"], w1t, b1.reshape(-1, 1), w2.T, b2.reshape(-1, 1), w3, freq.reshape(-1, 1))
    return pl.pallas_call(
        functools.partial(_hy_filter_kernel, seq_len),
        in_specs=[pl.BlockSpec(memory_space=pltpu.VMEM)] * len(args),
        out_specs=pl.BlockSpec(memory_space=pltpu.VMEM),
        out_shape=jax.ShapeDtypeStruct((2 * seq_len, HY_W), f32),
        compiler_params=pltpu.CompilerParams(vmem_limit_bytes=VMEM_LIMIT),
        name="hy_filter",
    )(*args)


def _hy_tiles(seq_len):
    return min(TM, seq_len // 2), min(TM, seq_len)


def _hy_fwd_kernel(nb, tk, seq_len, m_ref, zh_ref, cw_ref, cb_ref, hb_ref, r_ref, z_ref, rhs_ref):
    i = pl.program_id(0)
    nc = seq_len // HY_TL

    @pl.when(i == 0)
    def _():
        w, bias = cw_ref[...], cb_ref[...]
        for b in range(nb):
            def chunk(c, carry, b=b):
                r0 = pl.multiple_of(c * HY_TL, HY_TL)
                prev = zh_ref[b, pl.ds(pl.multiple_of(jnp.maximum(r0 - HALO_ROWS, 0), HALO_ROWS), HALO_ROWS), :]
                nxt = zh_ref[b, pl.ds(pl.multiple_of(jnp.minimum(r0 + HY_TL, seq_len - HALO_ROWS), HALO_ROWS),
                                      HALO_ROWS), :]
                _, u = _short_conv(zh_ref[b, pl.ds(r0, HY_TL), :], prev, nxt, c > 0, c < nc - 1, w, bias)
                rhs_ref[pl.ds(r0, HY_TL), b * HY_W:(b + 1) * HY_W] = u.astype(bf16)
                return carry
            lax.fori_loop(0, nc, chunk, 0)
        rhs_ref[:, nb * HY_W:(nb + 1) * HY_W] = hb_ref[0].astype(bf16)
        rhs_ref[:, (nb + 1) * HY_W:] = hb_ref[1].astype(bf16)

    acc = jnp.dot(m_ref[...], rhs_ref[...], preferred_element_type=f32)
    p, w = acc[:tk], acc[tk:]
    c0 = nb * HY_W
    pf, pb = p[:, c0:c0 + HY_W], p[:, c0 + HY_W:]
    qf, qb = w[:, c0:c0 + HY_W], w[:, c0 + HY_W:]
    hr = pf + pb
    qs = qf + qb
    hi = qb - qf
    first = (lax.broadcasted_iota(jnp.int32, (tk, HY_W), 0) == 0) & (i == 0)
    for b in range(nb):
        pu, wu = p[:, b * HY_W:(b + 1) * HY_W], w[:, b * HY_W:(b + 1) * HY_W]
        r_ref[b] = jnp.where(first, pu * hr, 2.0 * (pu * hr + wu * hi)).astype(bf16)
        z_ref[b] = jnp.where(first, wu * qs, 2.0 * (wu * hr - pu * hi)).astype(bf16)


def _hy_fwd(mat, zh, conv_w, conv_b, hb, batch, seq_len):
    tk, _ = _hy_tiles(seq_len)
    out = pl.BlockSpec((batch, tk, HY_W), lambda i: (0, i, 0))
    return pl.pallas_call(
        functools.partial(_hy_fwd_kernel, batch, tk, seq_len),
        grid=(seq_len // tk,),
        in_specs=[pl.BlockSpec((2 * tk, seq_len), lambda i: (i, 0)),
                  pl.BlockSpec((batch, seq_len, 3 * HY_W), lambda i: (0, 0, 0)),
                  pl.BlockSpec((3, 3 * HY_W), lambda i: (0, 0)),
                  pl.BlockSpec((1, 3 * HY_W), lambda i: (0, 0)),
                  pl.BlockSpec((2, seq_len, HY_W), lambda i: (0, 0, 0))],
        out_specs=[out, out],
        out_shape=[jax.ShapeDtypeStruct((batch, seq_len, HY_W), bf16)] * 2,
        scratch_shapes=[pltpu.VMEM((seq_len, (batch + 2) * HY_W), bf16)],
        compiler_params=_params("arbitrary"),
        name="hy_fwd",
    )(mat, zh, conv_w, conv_b.reshape(1, 3 * HY_W), hb)


def _hy_inv_kernel(nb, seq_len, nt, m_ref, r_ref, z_ref, zp_ref, zc_ref, zn_ref, cw_ref, cb_ref, d_ref, o_ref,
                   rhs_ref):
    i = pl.program_id(0)

    @pl.when(i == 0)
    def _():
        for b in range(nb):
            rhs_ref[:seq_len, b * HY_W:(b + 1) * HY_W] = r_ref[b]
            rhs_ref[seq_len:, b * HY_W:(b + 1) * HY_W] = z_ref[b]

    acc = jnp.dot(m_ref[...], rhs_ref[...], preferred_element_type=f32)
    d, w, bias = d_ref[...], cw_ref[...], cb_ref[...]
    for b in range(nb):
        x0, u = _short_conv(zc_ref[b], zp_ref[b], zn_ref[b], i > 0, i < nt - 1, w, bias)
        y = acc[:, b * HY_W:(b + 1) * HY_W] * (1.0 / (2 * seq_len)) + u * d
        o_ref[b] = (x0 * y).astype(o_ref.dtype)


def _hy_inv(mat, r, zz, zh, conv_w, conv_b, d_bias, batch, seq_len):
    _, tm = _hy_tiles(seq_len)
    nt = seq_len // tm
    rh = tm // HALO_ROWS
    last = seq_len // HALO_ROWS - 1
    full = pl.BlockSpec((batch, seq_len, HY_W), lambda i: (0, 0, 0))
    halo = lambda f: pl.BlockSpec((batch, HALO_ROWS, 3 * HY_W), f)
    return pl.pallas_call(
        functools.partial(_hy_inv_kernel, batch, seq_len, nt),
        grid=(nt,),
        in_specs=[pl.BlockSpec((tm, 2 * seq_len), lambda i: (i, 0)), full, full,
                  halo(lambda i: (0, jnp.maximum(i * rh - 1, 0), 0)),
                  pl.BlockSpec((batch, tm, 3 * HY_W), lambda i: (0, i, 0)),
                  halo(lambda i: (0, jnp.minimum((i + 1) * rh, last), 0)),
                  pl.BlockSpec((3, 3 * HY_W), lambda i: (0, 0)),
                  pl.BlockSpec((1, 3 * HY_W), lambda i: (0, 0)),
                  pl.BlockSpec((1, HY_W), lambda i: (0, 0))],
        out_specs=pl.BlockSpec((batch, tm, HY_W), lambda i: (0, i, 0)),
        out_shape=jax.ShapeDtypeStruct((batch, seq_len, HY_W), bf16),
        scratch_shapes=[pltpu.VMEM((2 * seq_len, batch * HY_W), bf16)],
        compiler_params=_params("arbitrary"),
        name="hy_inv",
    )(mat, r, zz, zh, zh, zh, conv_w, conv_b.reshape(1, 3 * HY_W), d_bias.reshape(1, HY_W))


@functools.lru_cache(maxsize=None)
def _hy_consts_np(seq_len):
    n = 2 * seq_len
    k = np.arange(seq_len, dtype=np.int64)
    ang = (2.0 * np.pi / n) * ((k[:, None] * k[None, :]) % n).astype(np.float64)
    cosm = np.cos(ang)
    sinm = np.sin(ang)
    sinm[0, :] = np.where(k % 2 == 0, 1.0, -1.0)
    tk, _ = _hy_tiles(seq_len)
    fwd = np.concatenate([np.concatenate([cosm[i:i + tk], sinm[i:i + tk]], axis=0)
                          for i in range(0, seq_len, tk)], axis=0)
    inv = np.concatenate([cosm, sinm.T], axis=1)
    t = np.linspace(0.0, 1.0, seq_len, dtype=np.float32)[:, None]
    bands = (FILTER_EMB - 1) // 2
    fb = np.linspace(1e-4, bands - 1, bands, dtype=np.float32)[None, :]
    w = (np.float32(2.0 * math.pi) * np.arange(seq_len, dtype=np.float32)[:, None] / np.float32(seq_len)).astype(np.float32)
    feats = np.concatenate([t, np.cos(fb * w), -np.sin(fb * w)], axis=-1).astype(np.float32)
    feats = np.ascontiguousarray(np.pad(feats, ((0, 0), (0, LANES - FILTER_EMB))).T)
    decay_hi = math.log(HY_DECAY_TARGET) / HY_DECAY_HI_PCT
    decay_lo = math.log(HY_DECAY_TARGET) / HY_DECAY_LO_PCT
    deltas = np.abs(np.linspace(decay_lo, decay_hi, HY_W, dtype=np.float32))
    window = np.exp(-t * deltas[None, :]).astype(np.float32)
    return fwd, inv, feats, window


def _hy_consts(seq_len):
    fwd, inv, feats, window = _hy_consts_np(seq_len)
    as_bf = lambda m: jnp.asarray(m, dtype=f32).astype(bf16)
    return {"fwd": as_bf(fwd), "inv": as_bf(inv), "feats": jnp.asarray(feats), "window": jnp.asarray(window)}


def _hyena(zh, lp, consts, batch, seq_len):
    zh = zh.reshape(batch, seq_len, 3 * HY_W)
    hb = _hy_filter(seq_len, consts, lp["hy_f_w1"], lp["hy_f_b1"], lp["hy_f_w2"], lp["hy_f_b2"],
                    lp["hy_f_w3"], lp["hy_freq"]).reshape(2, seq_len, HY_W)
    r, zz = _hy_fwd(consts["fwd"], zh, lp["hy_conv_w"], lp["hy_conv_b"], hb, batch, seq_len)
    yh = _hy_inv(consts["inv"], r, zz, zh, lp["hy_conv_w"], lp["hy_conv_b"], lp["hy_bias"], batch, seq_len)
    return yh.reshape(batch * seq_len, HY_W)


def _gelu(x):
    return 0.5 * x * (1.0 + jnp.tanh(math.sqrt(2.0 / math.pi) * (x + 0.044715 * (x * x * x))))


def _gmlp(z, ws_ref, bias):
    g = _gelu(z)
    outs = []
    for c in range(TM // GM_CHUNK):
        rows = slice(c * GM_CHUNK, (c + 1) * GM_CHUNK)
        u = g[rows, :GM_W]
        v = g[rows, GM_W:].astype(bf16)
        s = jnp.concatenate(
            [jnp.dot(ws_ref[h], v[:, h * GM_HEAD_DIM:(h + 1) * GM_HEAD_DIM], preferred_element_type=f32)
             for h in range(GM_HEADS)], axis=1)
        outs.append(u * (s + bias))
    return jnp.concatenate(outs, axis=0)


def _out_kernel(a_ref, yh_ref, zg_ref, ws_ref, gb_ref, x_ref, mod_ref, og_ref, w_ref, g2_ref, rw_ref,
                x1_ref, h2_ref, aff_ref):
    og = og_ref[...]
    yg = _gmlp(zg_ref[...].astype(f32), ws_ref, gb_ref[...])
    mixed_in = jnp.concatenate([
        _rms(a_ref[...].astype(f32)) * og[:, :ATTN_W],
        _rms(yh_ref[...].astype(f32)) * og[:, ATTN_W:ATTN_W + HY_W],
        _rms(yg) * og[:, ATTN_W + HY_W:]], axis=1)
    mixed = jnp.dot(mixed_in.astype(bf16), w_ref[...], preferred_element_type=f32)
    m = mod_ref[0]
    x1 = x_ref[...] + m[2:3] * mixed
    x1_ref[...] = x1
    h2 = _rms(x1) * g2_ref[...] * (1.0 + m[4:5]) + m[3:4]
    for j in range(ROW_CHUNKS):
        h2_ref[pl.ds(j, TM, stride=ROW_CHUNKS), :] = h2[:, j * LANES:(j + 1) * LANES]
    nt = (((1,), (1,)), ((), ()))
    h_hi = h2.astype(bf16)
    h_lo = (h2 - h_hi.astype(f32)).astype(bf16)
    rw = rw_ref[...]
    t = lax.dot_general(rw, h_hi, nt, preferred_element_type=f32)
    logits = (t[:N_EXPERTS] + t[N_EXPERTS:]) + lax.dot_general(rw[:N_EXPERTS], h_lo, nt, preferred_element_type=f32)
    e = jnp.exp(logits - jnp.max(logits, axis=0, keepdims=True))
    aff_ref[...] = e / jnp.sum(e, axis=0, keepdims=True)


def _out_proj(a, yh, zg, ws_bf, gm_b, x, mod, out_g, w_out_bf, g2, router_wt):
    n = x.shape[0]
    per_mod = n // mod.shape[0]
    row = lambda w: pl.BlockSpec((TM, w), lambda i: (i, 0))
    const = lambda s: pl.BlockSpec(s, lambda i: (0,) * len(s))
    gm_bias = jnp.repeat(gm_b.T, GM_HEAD_DIM, axis=1)
    return pl.pallas_call(
        _out_kernel,
        grid=(n // TM,),
        in_specs=[row(ATTN_W), row(HY_W), row(2 * GM_W), const((GM_HEADS, GM_CHUNK, GM_CHUNK)),
                  const((GM_CHUNK, GM_W)), row(D_MODEL),
                  pl.BlockSpec((1, 6, D_MODEL), lambda i: (i * TM // per_mod, 0, 0)),
                  const((1, MIX_W)), const((MIX_W, D_MODEL)), const((1, D_MODEL)), const((2 * N_EXPERTS, D_MODEL))],
        out_specs=[row(D_MODEL), pl.BlockSpec((TM * ROW_CHUNKS, LANES), lambda i: (i, 0)),
                   pl.BlockSpec((N_EXPERTS, TM), lambda i: (0, i))],
        out_shape=[jax.ShapeDtypeStruct((n, D_MODEL), f32), jax.ShapeDtypeStruct((n * ROW_CHUNKS, LANES), f32),
                   jax.ShapeDtypeStruct((N_EXPERTS, n), f32)],
        compiler_params=_params("parallel"),
        name="out_proj",
    )(a, yh, zg, ws_bf, gm_bias, x, mod, out_g.reshape(1, MIX_W), w_out_bf, g2.reshape(1, D_MODEL), router_wt)


RT_CHUNK = 512


def _prefix_incl(x01, tri):
    n = x01.shape[1]
    carry = jnp.zeros((x01.shape[0], 1), f32)
    parts = []
    for c in range(n // RT_CHUNK):
        piece = x01[:, c * RT_CHUNK:(c + 1) * RT_CHUNK]
        parts.append(jnp.dot(piece.astype(bf16), tri, preferred_element_type=f32) + carry)
        carry = carry + jnp.sum(piece, axis=1, keepdims=True)
    return jnp.concatenate(parts, axis=1)


def _route_kernel(cap, aff_ref, idx_ref, gate_ref):
    aff = aff_ref[...]
    n = aff.shape[1]

    def step(it, t):
        cand = t | (jnp.int32(1) << (30 - it))
        cnt = jnp.sum(jnp.where(aff >= lax.bitcast_convert_type(cand, f32), 1.0, 0.0), axis=1, keepdims=True)
        return jnp.where(cnt >= cap, cand, t)

    thr_bits = lax.fori_loop(0, 31, step, jnp.zeros((aff.shape[0], 1), jnp.int32))
    thr = lax.bitcast_convert_type(thr_bits, f32)
    gt = jnp.where(aff > thr, 1.0, 0.0)
    eq = jnp.where(aff == thr, 1.0, 0.0)
    room = cap - jnp.sum(gt, axis=1, keepdims=True)
    r = lax.broadcasted_iota(jnp.int32, (RT_CHUNK, RT_CHUNK), 0)
    c = lax.broadcasted_iota(jnp.int32, (RT_CHUNK, RT_CHUNK), 1)
    tri = jnp.where(r <= c, 1.0, 0.0).astype(bf16)
    sel = jnp.maximum(gt, jnp.where(_prefix_incl(eq, tri) <= room, eq, 0.0))
    slot = _prefix_incl(sel, tri) - 1.0

    tok = lax.broadcasted_iota(jnp.int32, aff.shape, 1)
    dist = jnp.where(sel > 0.0, tok - slot.astype(jnp.int32), 0)
    g = aff
    for b in range(max(1, (n - 1).bit_length())):
        sh = 1 << b
        dist_s = pltpu.roll(dist, n - sh, axis=1)
        take = (dist_s & sh) != 0
        leave = (dist & sh) != 0
        tok = jnp.where(take, pltpu.roll(tok, n - sh, axis=1), tok)
        g = jnp.where(take, pltpu.roll(g, n - sh, axis=1), g)
        dist = jnp.where(take, dist_s, jnp.where(leave, 0, dist))
    idx_ref[...] = tok[:, :cap]
    gate_ref[...] = g[:, :cap]


def _route(aff_t, cap):
    return pl.pallas_call(
        functools.partial(_route_kernel, cap),
        in_specs=[pl.BlockSpec(memory_space=pltpu.VMEM)],
        out_specs=[pl.BlockSpec(memory_space=pltpu.VMEM)] * 2,
        out_shape=[jax.ShapeDtypeStruct((N_EXPERTS, cap), jnp.int32), jax.ShapeDtypeStruct((N_EXPERTS, cap), f32)],
        compiler_params=pltpu.CompilerParams(vmem_limit_bytes=VMEM_LIMIT),
        name="route",
    )(aff_t)


GATHER_UNROLL = 8
SCATTER_UNROLL = 8
CMB_EXPERTS = 2
CMB_TM = 1024


def _moe_ffn_kernel(cap, stride, idx_ref, h_ref, w1_ref, w3_ref, w2_ref, y_ref, tile_ref, xb_ref, acc_ref):
    base = pl.program_id(0) * cap

    def gather(q, carry):
        for t in range(GATHER_UNROLL):
            r = q * GATHER_UNROLL + t
            src = pl.multiple_of(idx_ref[base + r] * ROW_CHUNKS, ROW_CHUNKS)
            tile_ref[pl.ds(r, ROW_CHUNKS, stride=stride), :] = h_ref[pl.ds(src, ROW_CHUNKS), :]
        return carry
    lax.fori_loop(0, cap // GATHER_UNROLL, gather, 0)
    xb_ref[...] = jnp.concatenate(
        [tile_ref[j * stride:j * stride + cap, :] for j in range(ROW_CHUNKS)], axis=1).astype(bf16)

    xb = xb_ref[...]
    for f in range(D_EXPERT // MOE_F_TILE):
        cols = slice(f * MOE_F_TILE, (f + 1) * MOE_F_TILE)
        a = jnp.dot(xb, w1_ref[0, 0, :, cols].astype(bf16), preferred_element_type=f32)
        b = jnp.dot(xb, w3_ref[0, 0, :, cols].astype(bf16), preferred_element_type=f32)
        he = (a * _sigmoid(a) * b).astype(bf16)
        part = jnp.dot(he, w2_ref[0, 0, cols, :].astype(bf16), preferred_element_type=f32)
        if f == 0:
            acc_ref[...] = part
        else:
            acc_ref[...] += part

    y = acc_ref[...]
    for j in range(ROW_CHUNKS):
        y_ref[0, j * stride:j * stride + cap, :] = y[:, j * LANES:(j + 1) * LANES]
        y_ref[0, j * stride + cap:(j + 1) * stride, :] = jnp.zeros((stride - cap, LANES), f32)


def _moe_ffn(layer, h3, idx, w1, w3, w2):
    cap = idx.shape[1]
    stride = cap + SUBLANES
    grid_spec = pltpu.PrefetchScalarGridSpec(
        num_scalar_prefetch=1,
        grid=(N_EXPERTS,),
        in_specs=[
            pl.BlockSpec(memory_space=pltpu.VMEM),
            pl.BlockSpec((1, 1, D_MODEL, D_EXPERT), lambda e, idx: (layer, e, 0, 0)),
            pl.BlockSpec((1, 1, D_MODEL, D_EXPERT), lambda e, idx: (layer, e, 0, 0)),
            pl.BlockSpec((1, 1, D_EXPERT, D_MODEL), lambda e, idx: (layer, e, 0, 0)),
        ],
        out_specs=pl.BlockSpec((1, ROW_CHUNKS * stride, LANES), lambda e, idx: (e, 0, 0)),
        scratch_shapes=[pltpu.VMEM((ROW_CHUNKS * stride, LANES), f32),
                        pltpu.VMEM((cap, D_MODEL), bf16),
                        pltpu.VMEM((cap, D_MODEL), f32)],
    )
    return pl.pallas_call(
        functools.partial(_moe_ffn_kernel, cap, stride),
        grid_spec=grid_spec,
        out_shape=jax.ShapeDtypeStruct((N_EXPERTS, ROW_CHUNKS * stride, LANES), f32),
        compiler_params=_params("arbitrary"),
        name="moe_ffn",
    )(idx.reshape(-1), h3, w1, w3, w2)


def _moe_combine_kernel(cap, stride, final, idx_ref, gate_ref, y_ref, x_ref, mod_ref, g_ref, o_ref, acc_ref):
    s = pl.program_id(0)
    scatter_steps = N_EXPERTS // CMB_EXPERTS

    @pl.when(s == 0)
    def _():
        acc_ref[...] = jnp.zeros_like(acc_ref)

    @pl.when(s < scatter_steps)
    def _():
        for k in range(CMB_EXPERTS):
            base = (s * CMB_EXPERTS + k) * cap

            def scatter(q, carry, k=k, base=base):
                rows = [pl.multiple_of(idx_ref[base + q * SCATTER_UNROLL + t] * ROW_CHUNKS, ROW_CHUNKS)
                        for t in range(SCATTER_UNROLL)]
                new = [acc_ref[pl.ds(rows[t], ROW_CHUNKS), :]
                       + gate_ref[base + q * SCATTER_UNROLL + t]
                       * y_ref[k, pl.ds(q * SCATTER_UNROLL + t, ROW_CHUNKS, stride=stride), :]
                       for t in range(SCATTER_UNROLL)]
                for t in range(SCATTER_UNROLL):
                    acc_ref[pl.ds(rows[t], ROW_CHUNKS), :] = new[t]
                return carry
            lax.fori_loop(0, cap // SCATTER_UNROLL, scatter, 0)

    @pl.when(s >= scatter_steps)
    def _():
        first = pl.multiple_of((s - scatter_steps) * (CMB_TM * ROW_CHUNKS), CMB_TM * ROW_CHUNKS)
        moe = jnp.concatenate([acc_ref[pl.ds(first + j, CMB_TM, stride=ROW_CHUNKS), :] for j in range(ROW_CHUNKS)],
                              axis=1)
        x = x_ref[...] + mod_ref[0][5:6] * moe
        o_ref[...] = _rms(x) * g_ref[...] if final else x


def _moe_combine(idx, gate, y_cm, x1, mod, final_g, final):
    n = x1.shape[0]
    cap = idx.shape[1]
    stride = y_cm.shape[1] // ROW_CHUNKS
    per_mod = n // mod.shape[0]
    scatter_steps = N_EXPERTS // CMB_EXPERTS
    tile = lambda s: jnp.maximum(s - scatter_steps, 0)
    row = pl.BlockSpec((CMB_TM, D_MODEL), lambda s, *_: (tile(s), 0))
    grid_spec = pltpu.PrefetchScalarGridSpec(
        num_scalar_prefetch=2,
        grid=(scatter_steps + n // CMB_TM,),
        in_specs=[pl.BlockSpec((CMB_EXPERTS, ROW_CHUNKS * stride, LANES),
                               lambda s, *_: (jnp.minimum(s, scatter_steps - 1), 0, 0)),
                  row,
                  pl.BlockSpec((1, 6, D_MODEL), lambda s, *_: (tile(s) * CMB_TM // per_mod, 0, 0)),
                  pl.BlockSpec((1, D_MODEL), lambda s, *_: (0, 0))],
        out_specs=row,
        scratch_shapes=[pltpu.VMEM((n * ROW_CHUNKS, LANES), f32)],
    )
    return pl.pallas_call(
        functools.partial(_moe_combine_kernel, cap, stride, final),
        grid_spec=grid_spec,
        out_shape=jax.ShapeDtypeStruct((n, D_MODEL), f32),
        compiler_params=_params("arbitrary"),
        name="moe_combine",
    )(idx.reshape(-1), gate.reshape(-1), y_cm, x1, mod, final_g.reshape(1, D_MODEL))


def _expert_choice(layer, h3, aff_t, x1, mod, final_g, final, w1, w3, w2):
    n = x1.shape[0]
    cap = max(1, EC_CAPACITY * n // N_EXPERTS)
    idx, gate = _route(aff_t, cap)
    y_cm = _moe_ffn(layer, h3, idx, w1, w3, w2)
    return _moe_combine(idx, gate, y_cm, x1, mod, final_g, final)


def _split_bf16(w):
    hi = w.astype(bf16)
    lo = (w - hi.astype(f32)).astype(bf16)
    return jnp.concatenate([hi, lo], axis=0)


def _stream(x, mods, layers, experts, batch, seq_len, final_g, caches=None):
    consts = _hy_consts(seq_len)
    rope_tabs = _rope_tables(seq_len) if caches is not None else None
    kvs = []
    for l, lp in enumerate(layers):
        qkv, zh, zg = _in_proj(x, mods[l], lp["norm1_g"], lp["w_in"], seq_len, rope_tabs=rope_tabs)
        if caches is None:
            a = _ctx_attention(qkv, lp["attn_sink"], batch, seq_len)
            kvs.append((qkv[:, ATTN_W:ATTN_W + KV_W], qkv[:, ATTN_W + KV_W:]))
        else:
            a = _lat_attention(qkv, lp["attn_sink"], caches[0][:, l], caches[1][:, l], batch, seq_len)
        yh = _hyena(zh, lp, consts, batch, seq_len)
        x1, h3, aff_t = _out_proj(a, yh, zg, lp["gm_ws"], lp["gm_b"], x, mods[l], lp["out_norm_g"], lp["w_out"],
                                  lp["norm2_g"], lp["router_wt"])
        x = _expert_choice(l, h3, aff_t, x1, mods[l], final_g, l == len(layers) - 1, *experts)
    return x, kvs


def kernel(x_prompt, x_sample, c, cache_k, cache_v, c_ctx, norm1_g, norm2_g, ada_w, ada_b, w_in, attn_sink,
           hy_conv_w, hy_conv_b, hy_f_w1, hy_f_b1, hy_f_w2, hy_f_b2, hy_f_w3, hy_freq, hy_bias, gm_ws, gm_b,
           out_norm_g, w_out, router_w, exp_w1, exp_w3, exp_w2, final_g):
    batch, seq, _ = x_prompt.shape
    dbatch, dseq, _ = x_sample.shape
    past = cache_k.shape[2]

    cvec = jnp.zeros((SUBLANES, D_MODEL), f32).at[0].set(c_ctx).at[1:1 + dbatch].set(c)
    mod = _ada(cvec, ada_w, ada_b)
    mods_ctx = [mod[l, 0:1].reshape(1, 6, D_MODEL) for l in range(DEPTH)]
    mods_lat = [mod[l, 1:1 + dbatch].reshape(dbatch, 6, D_MODEL) for l in range(DEPTH)]

    w_in_bf, w_out_bf, gm_ws_bf = w_in.astype(bf16), w_out.astype(bf16), gm_ws.astype(bf16)
    layers = []
    for l in range(DEPTH):
        layers.append({
            "norm1_g": norm1_g[l], "norm2_g": norm2_g[l], "w_in": w_in_bf[l], "attn_sink": attn_sink[l],
            "hy_conv_w": hy_conv_w[l], "hy_conv_b": hy_conv_b[l], "hy_f_w1": hy_f_w1[l], "hy_f_b1": hy_f_b1[l],
            "hy_f_w2": hy_f_w2[l], "hy_f_b2": hy_f_b2[l], "hy_f_w3": hy_f_w3[l], "hy_freq": hy_freq[l],
            "hy_bias": hy_bias[l], "gm_ws": gm_ws_bf[l], "gm_b": gm_b[l], "out_norm_g": out_norm_g[l],
            "w_out": w_out_bf[l], "router_wt": _split_bf16(router_w[l].T)})
    experts = (exp_w1, exp_w3, exp_w2)

    yp, kvs = _stream(x_prompt.reshape(batch * seq, D_MODEL), mods_ctx, layers, experts, batch, seq, final_g)
    caches = (cache_k.reshape(dbatch, DEPTH, past, KV_W), cache_v.reshape(dbatch, DEPTH, past, KV_W))
    ys, _ = _stream(x_sample.reshape(dbatch * dseq, D_MODEL), mods_lat, layers, experts, dbatch, dseq, final_g,
                   caches=caches)

    new_k = jnp.stack([k.reshape(batch, seq, N_KV, HEAD_DIM) for k, _ in kvs], axis=1)
    new_v = jnp.stack([v.reshape(batch, seq, N_KV, HEAD_DIM) for _, v in kvs], axis=1)
    return (yp.reshape(batch, seq, D_MODEL), ys.reshape(dbatch, dseq, D_MODEL), new_k, new_v)
```

```python
import functools
import math

import numpy as np
import jax
import jax.numpy as jnp
from jax import lax
from jax.experimental import pallas as pl
from jax.experimental.pallas import tpu as pltpu

f32 = jnp.float32
bf16 = jnp.bfloat16

D_MODEL = 1024
DEPTH = 2
GRID_W = 64
BLOCK = 128
N_HEADS = 8
N_KV = 2
HEAD_DIM = 64
Q_GROUP = N_HEADS // N_KV
ATTN_W = N_HEADS * HEAD_DIM
KV_W = N_KV * HEAD_DIM
QKV_W = ATTN_W + 2 * KV_W
HY_W = 256
GM_W = 256
GM_HEADS = 4
GM_HEAD_DIM = GM_W // GM_HEADS
GM_CHUNK = 128
MIX_W = ATTN_W + HY_W + GM_W
IN_W = ATTN_W + 2 * KV_W + 3 * HY_W + 2 * GM_W
FILTER_EMB = 33
FILTER_HID = 64
HY_DECAY_HI_PCT = 0.3
HY_DECAY_LO_PCT = 1.5
HY_DECAY_TARGET = 1e-2
N_EXPERTS = 16
EC_CAPACITY = 2
D_EXPERT = 1024
ROPE_THETA = 10000.0
EPS = 1e-6
NEG = -1e30

LANES = 128
SUBLANES = 8
ROW_CHUNKS = D_MODEL // LANES
VMEM_LIMIT = 56 * 1024 * 1024
TM = 512
IN_TM = 512
MOE_F_TILE = 256


def _params(*sem):
    return pltpu.CompilerParams(dimension_semantics=sem, vmem_limit_bytes=VMEM_LIMIT)


def _rms(x):
    return x * lax.rsqrt(jnp.mean(x * x, axis=-1, keepdims=True) + EPS)


def _sigmoid(x):
    return 1.0 / (1.0 + jnp.exp(-x))


def _ada_kernel(c_ref, w_ref, b_ref, o_ref):
    c = c_ref[...]
    s = (c * _sigmoid(c)).astype(bf16)
    o_ref[0] = jnp.dot(s, w_ref[0].astype(bf16), preferred_element_type=f32) + b_ref[0]


def _ada(cvec, ada_w, ada_b):
    nt = 6
    return pl.pallas_call(
        _ada_kernel,
        grid=(DEPTH, nt),
        in_specs=[
            pl.BlockSpec((SUBLANES, D_MODEL), lambda l, j: (0, 0)),
            pl.BlockSpec((1, D_MODEL, D_MODEL), lambda l, j: (l, 0, j)),
            pl.BlockSpec((1, 1, D_MODEL), lambda l, j: (l, 0, j)),
        ],
        out_specs=pl.BlockSpec((1, SUBLANES, D_MODEL), lambda l, j: (l, 0, j)),
        out_shape=jax.ShapeDtypeStruct((DEPTH, SUBLANES, 6 * D_MODEL), f32),
        compiler_params=_params("arbitrary", "arbitrary"),
        name="ada",
    )(cvec, ada_w, ada_b.reshape(DEPTH, 1, 6 * D_MODEL))


def _rope_swap(x):
    w = x.shape[-1]
    lane = lax.broadcasted_iota(jnp.int32, x.shape, 1)
    first = (lane % 32) < 16
    return jnp.where(first, pltpu.roll(x, w - 16, axis=1), pltpu.roll(x, 16, axis=1))


def _in_kernel(rope, x_ref, mod_ref, g_ref, w_ref, *refs):
    if rope:
        cos_ref, sin_ref, qkv_ref, zh_ref, zg_ref = refs
    else:
        qkv_ref, zh_ref, zg_ref = refs
    m = mod_ref[0]
    h = _rms(x_ref[...]) * g_ref[...] * (1.0 + m[1:2]) + m[0:1]
    z = jnp.dot(h.astype(bf16), w_ref[...], preferred_element_type=f32)
    if rope:
        reps = (ATTN_W + KV_W) // LANES
        cos = jnp.concatenate([cos_ref[...]] * reps, axis=1)
        sin = jnp.concatenate([sin_ref[...]] * reps, axis=1)
        qk = z[:, :ATTN_W + KV_W]
        qkv_ref[:, :ATTN_W + KV_W] = qk * cos + _rope_swap(qk) * sin
        qkv_ref[:, ATTN_W + KV_W:] = z[:, ATTN_W + KV_W:QKV_W]
    else:
        qkv_ref[...] = z[:, :QKV_W]
    zh_ref[...] = z[:, QKV_W:QKV_W + 3 * HY_W].astype(bf16)
    zg_ref[...] = z[:, QKV_W + 3 * HY_W:].astype(bf16)


def _in_proj(x, mod, g1, w_in_bf, seq_len, rope_tabs=None):
    n = x.shape[0]
    rope = rope_tabs is not None
    per_mod = n // mod.shape[0]
    tm = IN_TM
    in_specs = [pl.BlockSpec((tm, D_MODEL), lambda i: (i, 0)),
                pl.BlockSpec((1, 6, D_MODEL), lambda i: (i * tm // per_mod, 0, 0)),
                pl.BlockSpec((1, D_MODEL), lambda i: (0, 0)),
                pl.BlockSpec((D_MODEL, IN_W), lambda i: (0, 0))]
    args = [x, mod, g1.reshape(1, D_MODEL), w_in_bf]
    if rope:
        nt = seq_len // tm
        tab = pl.BlockSpec((tm, LANES), lambda i: (i % nt, 0))
        in_specs += [tab, tab]
        args += list(rope_tabs)
    return pl.pallas_call(
        functools.partial(_in_kernel, rope),
        grid=(n // tm,),
        in_specs=in_specs,
        out_specs=[pl.BlockSpec((tm, w), lambda i: (i, 0)) for w in (QKV_W, 3 * HY_W, 2 * GM_W)],
        out_shape=[jax.ShapeDtypeStruct((n, QKV_W), f32), jax.ShapeDtypeStruct((n, 3 * HY_W), bf16),
                   jax.ShapeDtypeStruct((n, 2 * GM_W), bf16)],
        compiler_params=_params("parallel"),
        name="in_proj",
    )(*args)


def _rope_tables(seq_len):
    nf = HEAD_DIM // 4
    t = np.arange(seq_len)
    inv = (ROPE_THETA ** (-np.arange(nf, dtype=np.float32) / nf)).astype(np.float32)
    d = np.arange(HEAD_DIM)
    pos = np.where((d // 32)[None, :] == 0, (t // GRID_W)[:, None], (t % GRID_W)[:, None]).astype(np.float32)
    ang = (pos * inv[d % nf][None, :]).astype(np.float32)
    cos = np.cos(ang).astype(np.float32)
    sin = np.sin(ang).astype(np.float32) * np.where((d % 32) < 16, -1.0, 1.0)[None, :].astype(np.float32)
    return jnp.asarray(np.tile(cos, (1, 2))), jnp.asarray(np.tile(sin, (1, 2)))


LOG2E = math.log2(math.e)


def _stack_queries(z, sink_ref, kv):
    rows = z.shape[0]
    heads = range(kv * Q_GROUP, (kv + 1) * Q_GROUP)
    qs = jnp.concatenate([z[:, h * HEAD_DIM:(h + 1) * HEAD_DIM] for h in heads], axis=0)
    sink = jnp.concatenate([jnp.full((1, rows), sink_ref[h] * LOG2E, f32) for h in heads], axis=1)
    return (qs * (LOG2E / math.sqrt(HEAD_DIM))).astype(bf16), sink


def _scores_t(k, qs_bf):
    return lax.dot_general(k.astype(bf16), qs_bf, (((1,), (1,)), ((), ())), preferred_element_type=f32)


def _softmax_pv_t(st, sink, vals, rows):
    m = jnp.maximum(jnp.max(st, axis=0, keepdims=True), sink)
    pt = jnp.exp2(st - m).astype(bf16)
    v_bf = vals.astype(bf16)
    v_ext = jnp.concatenate([v_bf, jnp.ones_like(v_bf)], axis=1)
    ovt = lax.dot_general(v_ext, pt, (((0,), (0,)), ((), ())), preferred_element_type=f32)
    ot = ovt[:HEAD_DIM] / (ovt[HEAD_DIM:HEAD_DIM + 1] + jnp.exp2(sink - m))
    return [ot[:, g * rows:(g + 1) * rows].T for g in range(Q_GROUP)]


CTX_BATCHES = 2
LAT_QBLOCKS = 4


def _ctx_attn_kernel(seq_len, sink_ref, z_ref, o_ref):
    for s in range(CTX_BATCHES):
        z = z_ref[s * seq_len:(s + 1) * seq_len, :]
        outs = []
        for kv in range(N_KV):
            k = z[:, ATTN_W + kv * HEAD_DIM:ATTN_W + (kv + 1) * HEAD_DIM]
            v = z[:, ATTN_W + KV_W + kv * HEAD_DIM:ATTN_W + KV_W + (kv + 1) * HEAD_DIM]
            qs, sink = _stack_queries(z, sink_ref, kv)
            outs += _softmax_pv_t(_scores_t(k, qs), sink, v, seq_len)
        o_ref[s * seq_len:(s + 1) * seq_len, :] = jnp.concatenate(outs, axis=1).astype(o_ref.dtype)


def _ctx_attention(z, sink, batch, seq_len):
    rows = CTX_BATCHES * seq_len
    return pl.pallas_call(
        functools.partial(_ctx_attn_kernel, seq_len),
        grid=(batch // CTX_BATCHES,),
        in_specs=[pl.BlockSpec(memory_space=pltpu.SMEM),
                  pl.BlockSpec((rows, QKV_W), lambda b: (b, 0))],
        out_specs=pl.BlockSpec((rows, ATTN_W), lambda b: (b, 0)),
        out_shape=jax.ShapeDtypeStruct((batch * seq_len, ATTN_W), bf16),
        compiler_params=_params("parallel"),
        name="ctx_attn",
    )(sink, z)


def _lat_attn_kernel(nb, sink_ref, zp_ref, zc_ref, zn_ref, ck_ref, cv_ref, o_ref):
    step = pl.program_id(1)
    zc_all = zc_ref[...]
    blocks = [zp_ref[...]] + [zc_all[q * BLOCK:(q + 1) * BLOCK] for q in range(LAT_QBLOCKS)] + [zn_ref[...]]
    ck, cv = ck_ref[0], cv_ref[0]
    width = Q_GROUP * BLOCK
    j = lax.broadcasted_iota(jnp.int32, (BLOCK, width), 0)
    r = lax.broadcasted_iota(jnp.int32, (BLOCK, width), 1) % BLOCK
    for q in range(LAT_QBLOCKS):
        i = step * LAT_QBLOCKS + q
        zp, zc, zn = blocks[q], blocks[q + 1], blocks[q + 2]
        ok_prev = j >= r + jnp.where(i > 0, 0, BLOCK)
        ok_next = j <= r - jnp.where(i < nb - 1, 0, BLOCK)
        outs = []
        for kv in range(N_KV):
            ks = slice(ATTN_W + kv * HEAD_DIM, ATTN_W + (kv + 1) * HEAD_DIM)
            vs = slice(ATTN_W + KV_W + kv * HEAD_DIM, ATTN_W + KV_W + (kv + 1) * HEAD_DIM)
            cs = slice(kv * HEAD_DIM, (kv + 1) * HEAD_DIM)
            qs, sink = _stack_queries(zc, sink_ref, kv)
            st = jnp.concatenate([
                jnp.where(ok_prev, _scores_t(zp[:, ks], qs), NEG),
                _scores_t(zc[:, ks], qs),
                jnp.where(ok_next, _scores_t(zn[:, ks], qs), NEG),
                _scores_t(ck[:, cs], qs)], axis=0)
            vals = jnp.concatenate([zp[:, vs], zc[:, vs], zn[:, vs], cv[:, cs]], axis=0)
            outs += _softmax_pv_t(st, sink, vals, BLOCK)
        o_ref[q * BLOCK:(q + 1) * BLOCK, :] = jnp.concatenate(outs, axis=1).astype(o_ref.dtype)


def _lat_attention(z, sink, ck, cv, batch, seq_len):
    nb = seq_len // BLOCK
    ns = nb // LAT_QBLOCKS
    blk = lambda f: pl.BlockSpec((BLOCK, QKV_W), f)
    past = ck.shape[1]
    cache = pl.BlockSpec((1, past, KV_W), lambda b, i: (b, 0, 0))
    return pl.pallas_call(
        functools.partial(_lat_attn_kernel, nb),
        grid=(batch, ns),
        in_specs=[pl.BlockSpec(memory_space=pltpu.SMEM),
                  blk(lambda b, i: (b * nb + jnp.maximum(i * LAT_QBLOCKS - 1, 0), 0)),
                  pl.BlockSpec((LAT_QBLOCKS * BLOCK, QKV_W), lambda b, i: (b * ns + i, 0)),
                  blk(lambda b, i: (b * nb + jnp.minimum((i + 1) * LAT_QBLOCKS, nb - 1), 0)),
                  cache, cache],
        out_specs=pl.BlockSpec((LAT_QBLOCKS * BLOCK, ATTN_W), lambda b, i: (b * ns + i, 0)),
        out_shape=jax.ShapeDtypeStruct((batch * seq_len, ATTN_W), bf16),
        compiler_params=_params("parallel", "parallel"),
        name="lat_attn",
    )(sink, z, z, z, ck, cv)


HY_TL = 256
HALO_ROWS = 2 * SUBLANES


def _short_conv(z, prev_tile, next_tile, has_prev, has_next, w, b):
    z = z.astype(f32)
    rows = z.shape[0]
    row = lax.broadcasted_iota(jnp.int32, z.shape, 0)
    prev_row = jnp.where(has_prev, prev_tile.astype(f32)[HALO_ROWS - 1:HALO_ROWS, :], 0.0)
    next_row = jnp.where(has_next, next_tile.astype(f32)[0:1, :], 0.0)
    z_prev = jnp.where(row == 0, prev_row, pltpu.roll(z, 1, axis=0))
    z_next = jnp.where(row == rows - 1, next_row, pltpu.roll(z, rows - 1, axis=0))
    zc = z_prev * w[0:1] + z * w[1:2] + z_next * w[2:3] + b
    return zc[:, :HY_W], zc[:, HY_W:2 * HY_W] * zc[:, 2 * HY_W:]


def _hy_filter_kernel(seq_len, feats_ref, win_ref, w1_ref, b1_ref, w2_ref, b2_ref, w3_ref, fr_ref, o_ref):
    hi = lax.Precision.HIGHEST
    fr = fr_ref[...]
    h = jnp.sin(fr * (jnp.dot(w1_ref[...], feats_ref[...], precision=hi, preferred_element_type=f32) + b1_ref[...]))
    h = jnp.sin(fr * (jnp.dot(w2_ref[...], h, precision=hi, preferred_element_type=f32) + b2_ref[...]))
    h = jnp.dot(h.T, w3_ref[...], precision=hi, preferred_element_type=f32)
    win = win_ref[...]
    row = lax.broadcasted_iota(jnp.int32, (seq_len, HY_W), 0)
    hf = h[:, :HY_W] * win
    hb = jnp.where(row == 0, 0.0, h[:, HY_W:] * win)
    o_ref[0:seq_len, :] = hf + hb
    o_ref[seq_len:, :] = hb - hf


def _hy_filter(seq_len, consts, w1, b1, w2, b2, w3, freq):
    w1t = jnp.zeros((FILTER_HID, LANES), f32).at[:, :FILTER_EMB].set(w1.T)
    args = (consts["feats"], consts["window"], w1t, b1.reshape(-1, 1), w2.T, b2.reshape(-1, 1), w3, freq.reshape(-1, 1))
    return pl.pallas_call(
        functools.partial(_hy_filter_kernel, seq_len),
        in_specs=[pl.BlockSpec(memory_space=pltpu.VMEM)] * len(args),
        out_specs=pl.BlockSpec(memory_space=pltpu.VMEM),
        out_shape=jax.ShapeDtypeStruct((2 * seq_len, HY_W), f32),
        compiler_params=pltpu.CompilerParams(vmem_limit_bytes=VMEM_LIMIT),
        name="hy_filter",
    )(*args)


def _hy_tiles(seq_len):
    return min(TM, seq_len // 2), min(TM, seq_len)


def _hy_fwd_kernel(nb, tk, seq_len, m_ref, zh_ref, cw_ref, cb_ref, hb_ref, r_ref, z_ref, rhs_ref):
    i = pl.program_id(0)
    nc = seq_len // HY_TL

    @pl.when(i == 0)
    def _():
        w, bias = cw_ref[...], cb_ref[...]
        for b in range(nb):
            def chunk(c, carry, b=b):
                r0 = pl.multiple_of(c * HY_TL, HY_TL)
                prev = zh_ref[b, pl.ds(pl.multiple_of(jnp.maximum(r0 - HALO_ROWS, 0), HALO_ROWS), HALO_ROWS), :]
                nxt = zh_ref[b, pl.ds(pl.multiple_of(jnp.minimum(r0 + HY_TL, seq_len - HALO_ROWS), HALO_ROWS),
                                      HALO_ROWS), :]
                _, u = _short_conv(zh_ref[b, pl.ds(r0, HY_TL), :], prev, nxt, c > 0, c < nc - 1, w, bias)
                rhs_ref[pl.ds(r0, HY_TL), b * HY_W:(b + 1) * HY_W] = u.astype(bf16)
                return carry
            lax.fori_loop(0, nc, chunk, 0)
        rhs_ref[:, nb * HY_W:(nb + 1) * HY_W] = hb_ref[0].astype(bf16)
        rhs_ref[:, (nb + 1) * HY_W:] = hb_ref[1].astype(bf16)

    c0 = nb * HY_W
    acc = jnp.dot(m_ref[...], rhs_ref[:, :c0], preferred_element_type=f32)
    p, w = acc[:tk], acc[tk:]
    hr = jnp.dot(m_ref[:tk, :], rhs_ref[:, c0:c0 + HY_W], preferred_element_type=f32)
    hi = jnp.dot(m_ref[tk:, :], rhs_ref[:, c0 + HY_W:], preferred_element_type=f32)
    qs = jnp.dot(m_ref[tk:tk + HALO_ROWS, :], rhs_ref[:, c0:c0 + HY_W], preferred_element_type=f32)[0:1]
    first = (lax.broadcasted_iota(jnp.int32, (tk, HY_W), 0) == 0) & (i == 0)
    for b in range(nb):
        pu, wu = p[:, b * HY_W:(b + 1) * HY_W], w[:, b * HY_W:(b + 1) * HY_W]
        r_ref[b] = jnp.where(first, pu * hr, 2.0 * (pu * hr + wu * hi)).astype(bf16)
        z_ref[b] = jnp.where(first, wu * qs, 2.0 * (wu * hr - pu * hi)).astype(bf16)


def _hy_fwd(mat, zh, conv_w, conv_b, hb, batch, seq_len):
    tk, _ = _hy_tiles(seq_len)
    out = pl.BlockSpec((batch, tk, HY_W), lambda i: (0, i, 0))
    return pl.pallas_call(
        functools.partial(_hy_fwd_kernel, batch, tk, seq_len),
        grid=(seq_len // tk,),
        in_specs=[pl.BlockSpec((2 * tk, seq_len), lambda i: (i, 0)),
                  pl.BlockSpec((batch, seq_len, 3 * HY_W), lambda i: (0, 0, 0)),
                  pl.BlockSpec((3, 3 * HY_W), lambda i: (0, 0)),
                  pl.BlockSpec((1, 3 * HY_W), lambda i: (0, 0)),
                  pl.BlockSpec((2, seq_len, HY_W), lambda i: (0, 0, 0))],
        out_specs=[out, out],
        out_shape=[jax.ShapeDtypeStruct((batch, seq_len, HY_W), bf16)] * 2,
        scratch_shapes=[pltpu.VMEM((seq_len, (batch + 2) * HY_W), bf16)],
        compiler_params=_params("arbitrary"),
        name="hy_fwd",
    )(mat, zh, conv_w, conv_b.reshape(1, 3 * HY_W), hb)


def _hy_inv_kernel(nb, seq_len, nt, m_ref, r_ref, z_ref, zp_ref, zc_ref, zn_ref, cw_ref, cb_ref, d_ref, o_ref,
                   rhs_ref):
    i = pl.program_id(0)

    @pl.when(i == 0)
    def _():
        for b in range(nb):
            rhs_ref[:seq_len, b * HY_W:(b + 1) * HY_W] = r_ref[b]
            rhs_ref[seq_len:, b * HY_W:(b + 1) * HY_W] = z_ref[b]

    acc = jnp.dot(m_ref[...], rhs_ref[...], preferred_element_type=f32)
    d, w, bias = d_ref[...], cw_ref[...], cb_ref[...]
    for b in range(nb):
        x0, u = _short_conv(zc_ref[b], zp_ref[b], zn_ref[b], i > 0, i < nt - 1, w, bias)
        y = acc[:, b * HY_W:(b + 1) * HY_W] * (1.0 / (2 * seq_len)) + u * d
        o_ref[b] = (x0 * y).astype(o_ref.dtype)


def _hy_inv(mat, r, zz, zh, conv_w, conv_b, d_bias, batch, seq_len):
    _, tm = _hy_tiles(seq_len)
    nt = seq_len // tm
    rh = tm // HALO_ROWS
    last = seq_len // HALO_ROWS - 1
    full = pl.BlockSpec((batch, seq_len, HY_W), lambda i: (0, 0, 0))
    halo = lambda f: pl.BlockSpec((batch, HALO_ROWS, 3 * HY_W), f)
    return pl.pallas_call(
        functools.partial(_hy_inv_kernel, batch, seq_len, nt),
        grid=(nt,),
        in_specs=[pl.BlockSpec((tm, 2 * seq_len), lambda i: (i, 0)), full, full,
                  halo(lambda i: (0, jnp.maximum(i * rh - 1, 0), 0)),
                  pl.BlockSpec((batch, tm, 3 * HY_W), lambda i: (0, i, 0)),
                  halo(lambda i: (0, jnp.minimum((i + 1) * rh, last), 0)),
                  pl.BlockSpec((3, 3 * HY_W), lambda i: (0, 0)),
                  pl.BlockSpec((1, 3 * HY_W), lambda i: (0, 0)),
                  pl.BlockSpec((1, HY_W), lambda i: (0, 0))],
        out_specs=pl.BlockSpec((batch, tm, HY_W), lambda i: (0, i, 0)),
        out_shape=jax.ShapeDtypeStruct((batch, seq_len, HY_W), bf16),
        scratch_shapes=[pltpu.VMEM((2 * seq_len, batch * HY_W), bf16)],
        compiler_params=_params("arbitrary"),
        name="hy_inv",
    )(mat, r, zz, zh, zh, zh, conv_w, conv_b.reshape(1, 3 * HY_W), d_bias.reshape(1, HY_W))


@functools.lru_cache(maxsize=None)
def _hy_consts_np(seq_len):
    n = 2 * seq_len
    k = np.arange(seq_len, dtype=np.int64)
    ang = (2.0 * np.pi / n) * ((k[:, None] * k[None, :]) % n).astype(np.float64)
    cosm = np.cos(ang)
    sinm = np.sin(ang)
    sinm[0, :] = np.where(k % 2 == 0, 1.0, -1.0)
    tk, _ = _hy_tiles(seq_len)
    fwd = np.concatenate([np.concatenate([cosm[i:i + tk], sinm[i:i + tk]], axis=0)
                          for i in range(0, seq_len, tk)], axis=0)
    inv = np.concatenate([cosm, sinm.T], axis=1)
    t = np.linspace(0.0, 1.0, seq_len, dtype=np.float32)[:, None]
    bands = (FILTER_EMB - 1) // 2
    fb = np.linspace(1e-4, bands - 1, bands, dtype=np.float32)[None, :]
    w = (np.float32(2.0 * math.pi) * np.arange(seq_len, dtype=np.float32)[:, None] / np.float32(seq_len)).astype(np.float32)
    feats = np.concatenate([t, np.cos(fb * w), -np.sin(fb * w)], axis=-1).astype(np.float32)
    feats = np.ascontiguousarray(np.pad(feats, ((0, 0), (0, LANES - FILTER_EMB))).T)
    decay_hi = math.log(HY_DECAY_TARGET) / HY_DECAY_HI_PCT
    decay_lo = math.log(HY_DECAY_TARGET) / HY_DECAY_LO_PCT
    deltas = np.abs(np.linspace(decay_lo, decay_hi, HY_W, dtype=np.float32))
    window = np.exp(-t * deltas[None, :]).astype(np.float32)
    return fwd, inv, feats, window


def _hy_consts(seq_len):
    fwd, inv, feats, window = _hy_consts_np(seq_len)
    as_bf = lambda m: jnp.asarray(m, dtype=f32).astype(bf16)
    return {"fwd": as_bf(fwd), "inv": as_bf(inv), "feats": jnp.asarray(feats), "window": jnp.asarray(window)}


def _hyena(zh, lp, consts, batch, seq_len):
    zh = zh.reshape(batch, seq_len, 3 * HY_W)
    hb = _hy_filter(seq_len, consts, lp["hy_f_w1"], lp["hy_f_b1"], lp["hy_f_w2"], lp["hy_f_b2"],
                    lp["hy_f_w3"], lp["hy_freq"]).reshape(2, seq_len, HY_W)
    r, zz = _hy_fwd(consts["fwd"], zh, lp["hy_conv_w"], lp["hy_conv_b"], hb, batch, seq_len)
    yh = _hy_inv(consts["inv"], r, zz, zh, lp["hy_conv_w"], lp["hy_conv_b"], lp["hy_bias"], batch, seq_len)
    return yh.reshape(batch * seq_len, HY_W)


def _gelu(x):
    return 0.5 * x * (1.0 + jnp.tanh(math.sqrt(2.0 / math.pi) * (x + 0.044715 * (x * x * x))))


def _gmlp(z, ws_ref, bias):
    g = _gelu(z)
    outs = []
    for c in range(TM // GM_CHUNK):
        rows = slice(c * GM_CHUNK, (c + 1) * GM_CHUNK)
        u = g[rows, :GM_W]
        v = g[rows, GM_W:].astype(bf16)
        s = jnp.concatenate(
            [jnp.dot(ws_ref[h], v[:, h * GM_HEAD_DIM:(h + 1) * GM_HEAD_DIM], preferred_element_type=f32)
             for h in range(GM_HEADS)], axis=1)
        outs.append(u * (s + bias))
    return jnp.concatenate(outs, axis=0)


def _out_kernel(a_ref, yh_ref, zg_ref, ws_ref, gb_ref, x_ref, mod_ref, og_ref, w_ref, g2_ref, rw_ref,
                x1_ref, h2_ref, aff_ref):
    og = og_ref[...]
    yg = _gmlp(zg_ref[...].astype(f32), ws_ref, gb_ref[...])
    mixed_in = jnp.concatenate([
        _rms(a_ref[...].astype(f32)) * og[:, :ATTN_W],
        _rms(yh_ref[...].astype(f32)) * og[:, ATTN_W:ATTN_W + HY_W],
        _rms(yg) * og[:, ATTN_W + HY_W:]], axis=1)
    mixed = jnp.dot(mixed_in.astype(bf16), w_ref[...], preferred_element_type=f32)
    m = mod_ref[0]
    x1 = x_ref[...] + m[2:3] * mixed
    x1_ref[...] = x1
    h2 = _rms(x1) * g2_ref[...] * (1.0 + m[4:5]) + m[3:4]
    for j in range(ROW_CHUNKS):
        h2_ref[pl.ds(j, TM, stride=ROW_CHUNKS), :] = h2[:, j * LANES:(j + 1) * LANES]
    nt = (((1,), (1,)), ((), ()))
    h_hi = h2.astype(bf16)
    h_lo = (h2 - h_hi.astype(f32)).astype(bf16)
    rw = rw_ref[...]
    t = lax.dot_general(rw, h_hi, nt, preferred_element_type=f32)
    logits = (t[:N_EXPERTS] + t[N_EXPERTS:]) + lax.dot_general(rw[:N_EXPERTS], h_lo, nt, preferred_element_type=f32)
    e = jnp.exp(logits - jnp.max(logits, axis=0, keepdims=True))
    aff_ref[...] = e / jnp.sum(e, axis=0, keepdims=True)


def _out_proj(a, yh, zg, ws_bf, gm_b, x, mod, out_g, w_out_bf, g2, router_wt):
    n = x.shape[0]
    per_mod = n // mod.shape[0]
    row = lambda w: pl.BlockSpec((TM, w), lambda i: (i, 0))
    const = lambda s: pl.BlockSpec(s, lambda i: (0,) * len(s))
    gm_bias = jnp.repeat(gm_b.T, GM_HEAD_DIM, axis=1)
    return pl.pallas_call(
        _out_kernel,
        grid=(n // TM,),
        in_specs=[row(ATTN_W), row(HY_W), row(2 * GM_W), const((GM_HEADS, GM_CHUNK, GM_CHUNK)),
                  const((GM_CHUNK, GM_W)), row(D_MODEL),
                  pl.BlockSpec((1, 6, D_MODEL), lambda i: (i * TM // per_mod, 0, 0)),
                  const((1, MIX_W)), const((MIX_W, D_MODEL)), const((1, D_MODEL)), const((2 * N_EXPERTS, D_MODEL))],
        out_specs=[row(D_MODEL), pl.BlockSpec((TM * ROW_CHUNKS, LANES), lambda i: (i, 0)),
                   pl.BlockSpec((N_EXPERTS, TM), lambda i: (0, i))],
        out_shape=[jax.ShapeDtypeStruct((n, D_MODEL), f32), jax.ShapeDtypeStruct((n * ROW_CHUNKS, LANES), f32),
                   jax.ShapeDtypeStruct((N_EXPERTS, n), f32)],
        compiler_params=_params("parallel"),
        name="out_proj",
    )(a, yh, zg, ws_bf, gm_bias, x, mod, out_g.reshape(1, MIX_W), w_out_bf, g2.reshape(1, D_MODEL), router_wt)


RT_CHUNK = 512


def _prefix_incl(x01, tri):
    n = x01.shape[1]
    carry = jnp.zeros((x01.shape[0], 1), f32)
    parts = []
    for c in range(n // RT_CHUNK):
        piece = x01[:, c * RT_CHUNK:(c + 1) * RT_CHUNK]
        parts.append(jnp.dot(piece.astype(bf16), tri, preferred_element_type=f32) + carry)
        carry = carry + jnp.sum(piece, axis=1, keepdims=True)
    return jnp.concatenate(parts, axis=1)


def _route_kernel(cap, aff_ref, idx_ref, gate_ref):
    aff = aff_ref[...]
    n = aff.shape[1]

    def step(it, t):
        cand = t | (jnp.int32(1) << (30 - it))
        cnt = jnp.sum(jnp.where(aff >= lax.bitcast_convert_type(cand, f32), 1.0, 0.0), axis=1, keepdims=True)
        return jnp.where(cnt >= cap, cand, t)

    thr_bits = lax.fori_loop(0, 31, step, jnp.zeros((aff.shape[0], 1), jnp.int32))
    thr = lax.bitcast_convert_type(thr_bits, f32)
    gt = jnp.where(aff > thr, 1.0, 0.0)
    eq = jnp.where(aff == thr, 1.0, 0.0)
    room = cap - jnp.sum(gt, axis=1, keepdims=True)
    r = lax.broadcasted_iota(jnp.int32, (RT_CHUNK, RT_CHUNK), 0)
    c = lax.broadcasted_iota(jnp.int32, (RT_CHUNK, RT_CHUNK), 1)
    tri = jnp.where(r <= c, 1.0, 0.0).astype(bf16)
    sel = jnp.maximum(gt, jnp.where(_prefix_incl(eq, tri) <= room, eq, 0.0))
    slot = _prefix_incl(sel, tri) - 1.0

    tok = lax.broadcasted_iota(jnp.int32, aff.shape, 1)
    dist = jnp.where(sel > 0.0, tok - slot.astype(jnp.int32), 0)
    g = aff
    for b in range(max(1, (n - 1).bit_length())):
        sh = 1 << b
        dist_s = pltpu.roll(dist, n - sh, axis=1)
        take = (dist_s & sh) != 0
        leave = (dist & sh) != 0
        tok = jnp.where(take, pltpu.roll(tok, n - sh, axis=1), tok)
        g = jnp.where(take, pltpu.roll(g, n - sh, axis=1), g)
        dist = jnp.where(take, dist_s, jnp.where(leave, 0, dist))
    idx_ref[...] = tok[:, :cap]
    gate_ref[...] = g[:, :cap]


def _route(aff_t, cap):
    return pl.pallas_call(
        functools.partial(_route_kernel, cap),
        in_specs=[pl.BlockSpec(memory_space=pltpu.VMEM)],
        out_specs=[pl.BlockSpec(memory_space=pltpu.VMEM)] * 2,
        out_shape=[jax.ShapeDtypeStruct((N_EXPERTS, cap), jnp.int32), jax.ShapeDtypeStruct((N_EXPERTS, cap), f32)],
        compiler_params=pltpu.CompilerParams(vmem_limit_bytes=VMEM_LIMIT),
        name="route",
    )(aff_t)


GATHER_UNROLL = 8
SCATTER_UNROLL = 8
CMB_EXPERTS = 2
CMB_TM = 1024


def _moe_ffn_kernel(cap, stride, idx_ref, h_ref, w1_ref, w3_ref, w2_ref, y_ref, tile_ref, xb_ref, acc_ref):
    e = pl.program_id(0)
    slab = ROW_CHUNKS * stride
    cur = pl.multiple_of((e % 2) * slab, SUBLANES)
    nxt = pl.multiple_of(((e + 1) % 2) * slab, SUBLANES)

    def gather_row(expert, slab_row, r):
        src = pl.multiple_of(idx_ref[expert * cap + r] * ROW_CHUNKS, ROW_CHUNKS)
        tile_ref[pl.ds(slab_row + r, ROW_CHUNKS, stride=stride), :] = h_ref[pl.ds(src, ROW_CHUNKS), :]

    @pl.when(e == 0)
    def _():
        def gather(q, carry):
            for t in range(GATHER_UNROLL):
                gather_row(0, cur, q * GATHER_UNROLL + t)
            return carry
        lax.fori_loop(0, cap // GATHER_UNROLL, gather, 0)

    xb_ref[...] = jnp.concatenate(
        [tile_ref[pl.ds(cur + j * stride, cap), :] for j in range(ROW_CHUNKS)], axis=1).astype(bf16)

    e_next = jnp.minimum(e + 1, N_EXPERTS - 1)
    n_f = D_EXPERT // MOE_F_TILE
    xb = xb_ref[...]
    for f in range(n_f):
        cols = slice(f * MOE_F_TILE, (f + 1) * MOE_F_TILE)
        a = jnp.dot(xb, w1_ref[0, 0, :, cols].astype(bf16), preferred_element_type=f32)
        b = jnp.dot(xb, w3_ref[0, 0, :, cols].astype(bf16), preferred_element_type=f32)
        he = (a * _sigmoid(a) * b).astype(bf16)
        part = jnp.dot(he, w2_ref[0, 0, cols, :].astype(bf16), preferred_element_type=f32)
        if f == 0:
            acc_ref[...] = part
        else:
            acc_ref[...] += part
        for r in range(f * cap // n_f, (f + 1) * cap // n_f):
            gather_row(e_next, nxt, r)

    y = acc_ref[...]
    for j in range(ROW_CHUNKS):
        y_ref[0, j * stride:j * stride + cap, :] = y[:, j * LANES:(j + 1) * LANES]
        y_ref[0, j * stride + cap:(j + 1) * stride, :] = jnp.zeros((stride - cap, LANES), f32)


def _moe_ffn(layer, h3, idx, w1, w3, w2):
    cap = idx.shape[1]
    stride = cap + SUBLANES
    grid_spec = pltpu.PrefetchScalarGridSpec(
        num_scalar_prefetch=1,
        grid=(N_EXPERTS,),
        in_specs=[
            pl.BlockSpec(memory_space=pltpu.VMEM),
            pl.BlockSpec((1, 1, D_MODEL, D_EXPERT), lambda e, idx: (layer, e, 0, 0)),
            pl.BlockSpec((1, 1, D_MODEL, D_EXPERT), lambda e, idx: (layer, e, 0, 0)),
            pl.BlockSpec((1, 1, D_EXPERT, D_MODEL), lambda e, idx: (layer, e, 0, 0)),
        ],
        out_specs=pl.BlockSpec((1, ROW_CHUNKS * stride, LANES), lambda e, idx: (e, 0, 0)),
        scratch_shapes=[pltpu.VMEM((2 * ROW_CHUNKS * stride, LANES), f32),
                        pltpu.VMEM((cap, D_MODEL), bf16),
                        pltpu.VMEM((cap, D_MODEL), f32)],
    )
    return pl.pallas_call(
        functools.partial(_moe_ffn_kernel, cap, stride),
        grid_spec=grid_spec,
        out_shape=jax.ShapeDtypeStruct((N_EXPERTS, ROW_CHUNKS * stride, LANES), f32),
        compiler_params=_params("arbitrary"),
        name="moe_ffn",
    )(idx.reshape(-1), h3, w1, w3, w2)


def _moe_combine_kernel(cap, stride, final, idx_ref, gate_ref, y_ref, x_ref, mod_ref, g_ref, o_ref, acc_ref):
    s = pl.program_id(0)
    scatter_steps = N_EXPERTS // CMB_EXPERTS

    @pl.when(s == 0)
    def _():
        acc_ref[...] = jnp.zeros_like(acc_ref)

    @pl.when(s < scatter_steps)
    def _():
        for k in range(CMB_EXPERTS):
            base = (s * CMB_EXPERTS + k) * cap

            def scatter(q, carry, k=k, base=base):
                rows = [pl.multiple_of(idx_ref[base + q * SCATTER_UNROLL + t] * ROW_CHUNKS, ROW_CHUNKS)
                        for t in range(SCATTER_UNROLL)]
                new = [acc_ref[pl.ds(rows[t], ROW_CHUNKS), :]
                       + gate_ref[base + q * SCATTER_UNROLL + t]
                       * y_ref[k, pl.ds(q * SCATTER_UNROLL + t, ROW_CHUNKS, stride=stride), :]
                       for t in range(SCATTER_UNROLL)]
                for t in range(SCATTER_UNROLL):
                    acc_ref[pl.ds(rows[t], ROW_CHUNKS), :] = new[t]
                return carry
            lax.fori_loop(0, cap // SCATTER_UNROLL, scatter, 0)

    @pl.when(s >= scatter_steps)
    def _():
        first = pl.multiple_of((s - scatter_steps) * (CMB_TM * ROW_CHUNKS), CMB_TM * ROW_CHUNKS)
        moe = jnp.concatenate([acc_ref[pl.ds(first + j, CMB_TM, stride=ROW_CHUNKS), :] for j in range(ROW_CHUNKS)],
                              axis=1)
        x = x_ref[...] + mod_ref[0][5:6] * moe
        o_ref[...] = _rms(x) * g_ref[...] if final else x


def _moe_combine(idx, gate, y_cm, x1, mod, final_g, final):
    n = x1.shape[0]
    cap = idx.shape[1]
    stride = y_cm.shape[1] // ROW_CHUNKS
    per_mod = n // mod.shape[0]
    scatter_steps = N_EXPERTS // CMB_EXPERTS
    tile = lambda s: jnp.maximum(s - scatter_steps, 0)
    row = pl.BlockSpec((CMB_TM, D_MODEL), lambda s, *_: (tile(s), 0))
    grid_spec = pltpu.PrefetchScalarGridSpec(
        num_scalar_prefetch=2,
        grid=(scatter_steps + n // CMB_TM,),
        in_specs=[pl.BlockSpec((CMB_EXPERTS, ROW_CHUNKS * stride, LANES),
                               lambda s, *_: (jnp.minimum(s, scatter_steps - 1), 0, 0)),
                  row,
                  pl.BlockSpec((1, 6, D_MODEL), lambda s, *_: (tile(s) * CMB_TM // per_mod, 0, 0)),
                  pl.BlockSpec((1, D_MODEL), lambda s, *_: (0, 0))],
        out_specs=row,
        scratch_shapes=[pltpu.VMEM((n * ROW_CHUNKS, LANES), f32)],
    )
    return pl.pallas_call(
        functools.partial(_moe_combine_kernel, cap, stride, final),
        grid_spec=grid_spec,
        out_shape=jax.ShapeDtypeStruct((n, D_MODEL), f32),
        compiler_params=_params("arbitrary"),
        name="moe_combine",
    )(idx.reshape(-1), gate.reshape(-1), y_cm, x1, mod, final_g.reshape(1, D_MODEL))


def _expert_choice(layer, h3, aff_t, x1, mod, final_g, final, w1, w3, w2):
    n = x1.shape[0]
    cap = max(1, EC_CAPACITY * n // N_EXPERTS)
    idx, gate = _route(aff_t, cap)
    y_cm = _moe_ffn(layer, h3, idx, w1, w3, w2)
    return _moe_combine(idx, gate, y_cm, x1, mod, final_g, final)


def _split_bf16(w):
    hi = w.astype(bf16)
    lo = (w - hi.astype(f32)).astype(bf16)
    return jnp.concatenate([hi, lo], axis=0)


def _stream(x, mods, layers, experts, batch, seq_len, final_g, caches=None):
    consts = _hy_consts(seq_len)
    rope_tabs = _rope_tables(seq_len) if caches is not None else None
    kvs = []
    for l, lp in enumerate(layers):
        qkv, zh, zg = _in_proj(x, mods[l], lp["norm1_g"], lp["w_in"], seq_len, rope_tabs=rope_tabs)
        if caches is None:
            a = _ctx_attention(qkv, lp["attn_sink"], batch, seq_len)
            kvs.append((qkv[:, ATTN_W:ATTN_W + KV_W], qkv[:, ATTN_W + KV_W:]))
        else:
            a = _lat_attention(qkv, lp["attn_sink"], caches[0][:, l], caches[1][:, l], batch, seq_len)
        yh = _hyena(zh, lp, consts, batch, seq_len)
        x1, h3, aff_t = _out_proj(a, yh, zg, lp["gm_ws"], lp["gm_b"], x, mods[l], lp["out_norm_g"], lp["w_out"],
                                  lp["norm2_g"], lp["router_wt"])
        x = _expert_choice(l, h3, aff_t, x1, mods[l], final_g, l == len(layers) - 1, *experts)
    return x, kvs


def kernel(x_prompt, x_sample, c, cache_k, cache_v, c_ctx, norm1_g, norm2_g, ada_w, ada_b, w_in, attn_sink,
           hy_conv_w, hy_conv_b, hy_f_w1, hy_f_b1, hy_f_w2, hy_f_b2, hy_f_w3, hy_freq, hy_bias, gm_ws, gm_b,
           out_norm_g, w_out, router_w, exp_w1, exp_w3, exp_w2, final_g):
    batch, seq, _ = x_prompt.shape
    dbatch, dseq, _ = x_sample.shape
    past = cache_k.shape[2]

    cvec = jnp.zeros((SUBLANES, D_MODEL), f32).at[0].set(c_ctx).at[1:1 + dbatch].set(c)
    mod = _ada(cvec, ada_w, ada_b)
    mods_ctx = [mod[l, 0:1].reshape(1, 6, D_MODEL) for l in range(DEPTH)]
    mods_lat = [mod[l, 1:1 + dbatch].reshape(dbatch, 6, D_MODEL) for l in range(DEPTH)]

    w_in_bf, w_out_bf, gm_ws_bf = w_in.astype(bf16), w_out.astype(bf16), gm_ws.astype(bf16)
    layers = []
    for l in range(DEPTH):
        layers.append({
            "norm1_g": norm1_g[l], "norm2_g": norm2_g[l], "w_in": w_in_bf[l], "attn_sink": attn_sink[l],
            "hy_conv_w": hy_conv_w[l], "hy_conv_b": hy_conv_b[l], "hy_f_w1": hy_f_w1[l], "hy_f_b1": hy_f_b1[l],
            "hy_f_w2": hy_f_w2[l], "hy_f_b2": hy_f_b2[l], "hy_f_w3": hy_f_w3[l], "hy_freq": hy_freq[l],
            "hy_bias": hy_bias[l], "gm_ws": gm_ws_bf[l], "gm_b": gm_b[l], "out_norm_g": out_norm_g[l],
            "w_out": w_out_bf[l], "router_wt": _split_bf16(router_w[l].T)})
    experts = (exp_w1, exp_w3, exp_w2)

    yp, kvs = _stream(x_prompt.reshape(batch * seq, D_MODEL), mods_ctx, layers, experts, batch, seq, final_g)
    caches = (cache_k.reshape(dbatch, DEPTH, past, KV_W), cache_v.reshape(dbatch, DEPTH, past, KV_W))
    ys, _ = _stream(x_sample.reshape(dbatch * dseq, D_MODEL), mods_lat, layers, experts, dbatch, dseq, final_g,
                   caches=caches)

    new_k = jnp.stack([k.reshape(batch, seq, N_KV, HEAD_DIM) for k, _ in kvs], axis=1)
    new_v = jnp.stack([v.reshape(batch, seq, N_KV, HEAD_DIM) for _, v in kvs], axis=1)
    return (yp.reshape(batch, seq, D_MODEL), ys.reshape(dbatch, dseq, D_MODEL), new_k, new_v)
```

```python
import functools
import math

import numpy as np
import jax
import jax.numpy as jnp
from jax import lax
from jax.experimental import pallas as pl
from jax.experimental.pallas import tpu as pltpu

f32 = jnp.float32
bf16 = jnp.bfloat16

D_MODEL = 1024
DEPTH = 2
GRID_W = 64
BLOCK = 128
N_HEADS = 8
N_KV = 2
HEAD_DIM = 64
Q_GROUP = N_HEADS // N_KV
ATTN_W = N_HEADS * HEAD_DIM
KV_W = N_KV * HEAD_DIM
QKV_W = ATTN_W + 2 * KV_W
HY_W = 256
GM_W = 256
GM_HEADS = 4
GM_HEAD_DIM = GM_W // GM_HEADS
GM_CHUNK = 128
MIX_W = ATTN_W + HY_W + GM_W
IN_W = ATTN_W + 2 * KV_W + 3 * HY_W + 2 * GM_W
FILTER_EMB = 33
FILTER_HID = 64
HY_DECAY_HI_PCT = 0.3
HY_DECAY_LO_PCT = 1.5
HY_DECAY_TARGET = 1e-2
N_EXPERTS = 16
EC_CAPACITY = 2
D_EXPERT = 1024
ROPE_THETA = 10000.0
EPS = 1e-6
NEG = -1e30

LANES = 128
SUBLANES = 8
ROW_CHUNKS = D_MODEL // LANES
VMEM_LIMIT = 56 * 1024 * 1024
TM = 512
IN_TM = 512
MOE_F_TILE = 256


def _params(*sem):
    return pltpu.CompilerParams(dimension_semantics=sem, vmem_limit_bytes=VMEM_LIMIT)


def _rms(x):
    return x * lax.rsqrt(jnp.mean(x * x, axis=-1, keepdims=True) + EPS)


def _sigmoid(x):
    return 1.0 / (1.0 + jnp.exp(-x))


def _ada_kernel(c_ref, w_ref, b_ref, o_ref):
    c = c_ref[...]
    s = (c * _sigmoid(c)).astype(bf16)
    o_ref[0] = jnp.dot(s, w_ref[0].astype(bf16), preferred_element_type=f32) + b_ref[0]


def _ada(cvec, ada_w, ada_b):
    nt = 6
    return pl.pallas_call(
        _ada_kernel,
        grid=(DEPTH, nt),
        in_specs=[
            pl.BlockSpec((SUBLANES, D_MODEL), lambda l, j: (0, 0)),
            pl.BlockSpec((1, D_MODEL, D_MODEL), lambda l, j: (l, 0, j)),
            pl.BlockSpec((1, 1, D_MODEL), lambda l, j: (l, 0, j)),
        ],
        out_specs=pl.BlockSpec((1, SUBLANES, D_MODEL), lambda l, j: (l, 0, j)),
        out_shape=jax.ShapeDtypeStruct((DEPTH, SUBLANES, 6 * D_MODEL), f32),
        compiler_params=_params("arbitrary", "arbitrary"),
        name="ada",
    )(cvec, ada_w, ada_b.reshape(DEPTH, 1, 6 * D_MODEL))


def _rope_swap(x):
    w = x.shape[-1]
    lane = lax.broadcasted_iota(jnp.int32, x.shape, 1)
    first = (lane % 32) < 16
    return jnp.where(first, pltpu.roll(x, w - 16, axis=1), pltpu.roll(x, 16, axis=1))


def _in_kernel(rope, x_ref, mod_ref, g_ref, w_ref, *refs):
    if rope:
        cos_ref, sin_ref, qkv_ref, zh_ref, zg_ref = refs
    else:
        qkv_ref, zh_ref, zg_ref = refs
    m = mod_ref[0]
    h = _rms(x_ref[...]) * g_ref[...] * (1.0 + m[1:2]) + m[0:1]
    z = jnp.dot(h.astype(bf16), w_ref[...], preferred_element_type=f32)
    if rope:
        reps = (ATTN_W + KV_W) // LANES
        cos = jnp.concatenate([cos_ref[...]] * reps, axis=1)
        sin = jnp.concatenate([sin_ref[...]] * reps, axis=1)
        qk = z[:, :ATTN_W + KV_W]
        qkv_ref[:, :ATTN_W + KV_W] = qk * cos + _rope_swap(qk) * sin
        qkv_ref[:, ATTN_W + KV_W:] = z[:, ATTN_W + KV_W:QKV_W]
    else:
        qkv_ref[...] = z[:, :QKV_W]
    zh_ref[...] = z[:, QKV_W:QKV_W + 3 * HY_W].astype(bf16)
    zg_ref[...] = z[:, QKV_W + 3 * HY_W:].astype(bf16)


def _in_proj(x, mod, g1, w_in_bf, seq_len, rope_tabs=None):
    n = x.shape[0]
    rope = rope_tabs is not None
    per_mod = n // mod.shape[0]
    tm = IN_TM
    in_specs = [pl.BlockSpec((tm, D_MODEL), lambda i: (i, 0)),
                pl.BlockSpec((1, 6, D_MODEL), lambda i: (i * tm // per_mod, 0, 0)),
                pl.BlockSpec((1, D_MODEL), lambda i: (0, 0)),
                pl.BlockSpec((D_MODEL, IN_W), lambda i: (0, 0))]
    args = [x, mod, g1.reshape(1, D_MODEL), w_in_bf]
    if rope:
        nt = seq_len // tm
        tab = pl.BlockSpec((tm, LANES), lambda i: (i % nt, 0))
        in_specs += [tab, tab]
        args += list(rope_tabs)
    return pl.pallas_call(
        functools.partial(_in_kernel, rope),
        grid=(n // tm,),
        in_specs=in_specs,
        out_specs=[pl.BlockSpec((tm, w), lambda i: (i, 0)) for w in (QKV_W, 3 * HY_W, 2 * GM_W)],
        out_shape=[jax.ShapeDtypeStruct((n, QKV_W), f32), jax.ShapeDtypeStruct((n, 3 * HY_W), bf16),
                   jax.ShapeDtypeStruct((n, 2 * GM_W), bf16)],
        compiler_params=_params("parallel"),
        name="in_proj",
    )(*args)


def _rope_tables(seq_len):
    nf = HEAD_DIM // 4
    t = np.arange(seq_len)
    inv = (ROPE_THETA ** (-np.arange(nf, dtype=np.float32) / nf)).astype(np.float32)
    d = np.arange(HEAD_DIM)
    pos = np.where((d // 32)[None, :] == 0, (t // GRID_W)[:, None], (t % GRID_W)[:, None]).astype(np.float32)
    ang = (pos * inv[d % nf][None, :]).astype(np.float32)
    cos = np.cos(ang).astype(np.float32)
    sin = np.sin(ang).astype(np.float32) * np.where((d % 32) < 16, -1.0, 1.0)[None, :].astype(np.float32)
    return jnp.asarray(np.tile(cos, (1, 2))), jnp.asarray(np.tile(sin, (1, 2)))


LOG2E = math.log2(math.e)


def _stack_queries(z, sink_ref, kv):
    rows = z.shape[0]
    heads = range(kv * Q_GROUP, (kv + 1) * Q_GROUP)
    qs = jnp.concatenate([z[:, h * HEAD_DIM:(h + 1) * HEAD_DIM] for h in heads], axis=0)
    sink = jnp.concatenate([jnp.full((1, rows), sink_ref[h] * LOG2E, f32) for h in heads], axis=1)
    return (qs * (LOG2E / math.sqrt(HEAD_DIM))).astype(bf16), sink


def _scores_t(k, qs_bf):
    return lax.dot_general(k.astype(bf16), qs_bf, (((1,), (1,)), ((), ())), preferred_element_type=f32)


def _softmax_pv_t(st, sink, vals, rows):
    m = jnp.maximum(jnp.max(st, axis=0, keepdims=True), sink)
    pt = jnp.exp2(st - m).astype(bf16)
    v_bf = vals.astype(bf16)
    v_ext = jnp.concatenate([v_bf, jnp.ones_like(v_bf)], axis=1)
    ovt = lax.dot_general(v_ext, pt, (((0,), (0,)), ((), ())), preferred_element_type=f32)
    ot = ovt[:HEAD_DIM] / (ovt[HEAD_DIM:HEAD_DIM + 1] + jnp.exp2(sink - m))
    return [ot[:, g * rows:(g + 1) * rows].T for g in range(Q_GROUP)]


CTX_BATCHES = 2
LAT_QBLOCKS = 4


def _ctx_attn_kernel(seq_len, sink_ref, z_ref, o_ref):
    for s in range(CTX_BATCHES):
        z = z_ref[s * seq_len:(s + 1) * seq_len, :]
        outs = []
        for kv in range(N_KV):
            k = z[:, ATTN_W + kv * HEAD_DIM:ATTN_W + (kv + 1) * HEAD_DIM]
            v = z[:, ATTN_W + KV_W + kv * HEAD_DIM:ATTN_W + KV_W + (kv + 1) * HEAD_DIM]
            qs, sink = _stack_queries(z, sink_ref, kv)
            outs += _softmax_pv_t(_scores_t(k, qs), sink, v, seq_len)
        o_ref[s * seq_len:(s + 1) * seq_len, :] = jnp.concatenate(outs, axis=1).astype(o_ref.dtype)


def _ctx_attention(z, sink, batch, seq_len):
    rows = CTX_BATCHES * seq_len
    return pl.pallas_call(
        functools.partial(_ctx_attn_kernel, seq_len),
        grid=(batch // CTX_BATCHES,),
        in_specs=[pl.BlockSpec(memory_space=pltpu.SMEM),
                  pl.BlockSpec((rows, QKV_W), lambda b: (b, 0))],
        out_specs=pl.BlockSpec((rows, ATTN_W), lambda b: (b, 0)),
        out_shape=jax.ShapeDtypeStruct((batch * seq_len, ATTN_W), bf16),
        compiler_params=_params("parallel"),
        name="ctx_attn",
    )(sink, z)


def _lat_attn_kernel(nb, sink_ref, zp_ref, zc_ref, zn_ref, ck_ref, cv_ref, o_ref):
    step = pl.program_id(1)
    zc_all = zc_ref[...]
    blocks = [zp_ref[...]] + [zc_all[q * BLOCK:(q + 1) * BLOCK] for q in range(LAT_QBLOCKS)] + [zn_ref[...]]
    ck, cv = ck_ref[0], cv_ref[0]
    width = Q_GROUP * BLOCK
    j = lax.broadcasted_iota(jnp.int32, (BLOCK, width), 0)
    r = lax.broadcasted_iota(jnp.int32, (BLOCK, width), 1) % BLOCK
    for q in range(LAT_QBLOCKS):
        i = step * LAT_QBLOCKS + q
        zp, zc, zn = blocks[q], blocks[q + 1], blocks[q + 2]
        ok_prev = j >= r + jnp.where(i > 0, 0, BLOCK)
        ok_next = j <= r - jnp.where(i < nb - 1, 0, BLOCK)
        outs = []
        for kv in range(N_KV):
            ks = slice(ATTN_W + kv * HEAD_DIM, ATTN_W + (kv + 1) * HEAD_DIM)
            vs = slice(ATTN_W + KV_W + kv * HEAD_DIM, ATTN_W + KV_W + (kv + 1) * HEAD_DIM)
            cs = slice(kv * HEAD_DIM, (kv + 1) * HEAD_DIM)
            qs, sink = _stack_queries(zc, sink_ref, kv)
            st = jnp.concatenate([
                jnp.where(ok_prev, _scores_t(zp[:, ks], qs), NEG),
                _scores_t(zc[:, ks], qs),
                jnp.where(ok_next, _scores_t(zn[:, ks], qs), NEG),
                _scores_t(ck[:, cs], qs)], axis=0)
            vals = jnp.concatenate([zp[:, vs], zc[:, vs], zn[:, vs], cv[:, cs]], axis=0)
            outs += _softmax_pv_t(st, sink, vals, BLOCK)
        o_ref[q * BLOCK:(q + 1) * BLOCK, :] = jnp.concatenate(outs, axis=1).astype(o_ref.dtype)


def _lat_attention(z, sink, ck, cv, batch, seq_len):
    nb = seq_len // BLOCK
    ns = nb // LAT_QBLOCKS
    blk = lambda f: pl.BlockSpec((BLOCK, QKV_W), f)
    past = ck.shape[1]
    cache = pl.BlockSpec((1, past, KV_W), lambda b, i: (b, 0, 0))
    return pl.pallas_call(
        functools.partial(_lat_attn_kernel, nb),
        grid=(batch, ns),
        in_specs=[pl.BlockSpec(memory_space=pltpu.SMEM),
                  blk(lambda b, i: (b * nb + jnp.maximum(i * LAT_QBLOCKS - 1, 0), 0)),
                  pl.BlockSpec((LAT_QBLOCKS * BLOCK, QKV_W), lambda b, i: (b * ns + i, 0)),
                  blk(lambda b, i: (b * nb + jnp.minimum((i + 1) * LAT_QBLOCKS, nb - 1), 0)),
                  cache, cache],
        out_specs=pl.BlockSpec((LAT_QBLOCKS * BLOCK, ATTN_W), lambda b, i: (b * ns + i, 0)),
        out_shape=jax.ShapeDtypeStruct((batch * seq_len, ATTN_W), bf16),
        compiler_params=_params("parallel", "parallel"),
        name="lat_attn",
    )(sink, z, z, z, ck, cv)


HY_TL = 256
HALO_ROWS = 2 * SUBLANES


def _short_conv(z, prev_tile, next_tile, has_prev, has_next, w, b):
    z = z.astype(f32)
    rows = z.shape[0]
    row = lax.broadcasted_iota(jnp.int32, z.shape, 0)
    prev_row = jnp.where(has_prev, prev_tile.astype(f32)[HALO_ROWS - 1:HALO_ROWS, :], 0.0)
    next_row = jnp.where(has_next, next_tile.astype(f32)[0:1, :], 0.0)
    z_prev = jnp.where(row == 0, prev_row, pltpu.roll(z, 1, axis=0))
    z_next = jnp.where(row == rows - 1, next_row, pltpu.roll(z, rows - 1, axis=0))
    zc = z_prev * w[0:1] + z * w[1:2] + z_next * w[2:3] + b
    return zc[:, :HY_W], zc[:, HY_W:2 * HY_W] * zc[:, 2 * HY_W:]


def _hy_filter_kernel(seq_len, feats_ref, win_ref, w1_ref, b1_ref, w2_ref, b2_ref, w3_ref, fr_ref, o_ref):
    hi = lax.Precision.HIGHEST
    fr = fr_ref[...]
    h = jnp.sin(fr * (jnp.dot(w1_ref[...], feats_ref[...], precision=hi, preferred_element_type=f32) + b1_ref[...]))
    h = jnp.sin(fr * (jnp.dot(w2_ref[...], h, precision=hi, preferred_element_type=f32) + b2_ref[...]))
    h = jnp.dot(h.T, w3_ref[...], precision=hi, preferred_element_type=f32)
    win = win_ref[...]
    row = lax.broadcasted_iota(jnp.int32, (seq_len, HY_W), 0)
    hf = h[:, :HY_W] * win
    hb = jnp.where(row == 0, 0.0, h[:, HY_W:] * win)
    o_ref[0:seq_len, :] = hf + hb
    o_ref[seq_len:, :] = hb - hf


def _hy_filter(seq_len, consts, w1, b1, w2, b2, w3, freq):
    w1t = jnp.zeros((FILTER_HID, LANES), f32).at[:, :FILTER_EMB].set(w1.T)
    args = (consts["feats"], consts["window"], w1t, b1.reshape(-1, 1), w2.T, b2.reshape(-1, 1), w3, freq.reshape(-1, 1))
    return pl.pallas_call(
        functools.partial(_hy_filter_kernel, seq_len),
        in_specs=[pl.BlockSpec(memory_space=pltpu.VMEM)] * len(args),
        out_specs=pl.BlockSpec(memory_space=pltpu.VMEM),
        out_shape=jax.ShapeDtypeStruct((2 * seq_len, HY_W), f32),
        compiler_params=pltpu.CompilerParams(vmem_limit_bytes=VMEM_LIMIT),
        name="hy_filter",
    )(*args)


def _hy_tiles(seq_len):
    return min(TM, seq_len // 2), min(TM, seq_len)


def _hy_fwd_kernel(nb, tk, seq_len, m_ref, zh_ref, cw_ref, cb_ref, hb_ref, r_ref, z_ref, rhs_ref):
    i = pl.program_id(0)
    nc = seq_len // HY_TL

    @pl.when(i == 0)
    def _():
        w, bias = cw_ref[...], cb_ref[...]
        for b in range(nb):
            def chunk(c, carry, b=b):
                r0 = pl.multiple_of(c * HY_TL, HY_TL)
                prev = zh_ref[b, pl.ds(pl.multiple_of(jnp.maximum(r0 - HALO_ROWS, 0), HALO_ROWS), HALO_ROWS), :]
                nxt = zh_ref[b, pl.ds(pl.multiple_of(jnp.minimum(r0 + HY_TL, seq_len - HALO_ROWS), HALO_ROWS),
                                      HALO_ROWS), :]
                _, u = _short_conv(zh_ref[b, pl.ds(r0, HY_TL), :], prev, nxt, c > 0, c < nc - 1, w, bias)
                rhs_ref[pl.ds(r0, HY_TL), b * HY_W:(b + 1) * HY_W] = u.astype(bf16)
                return carry
            lax.fori_loop(0, nc, chunk, 0)
        rhs_ref[:, nb * HY_W:(nb + 1) * HY_W] = hb_ref[0].astype(bf16)
        rhs_ref[:, (nb + 1) * HY_W:] = hb_ref[1].astype(bf16)

    c0 = nb * HY_W
    acc = jnp.dot(m_ref[...], rhs_ref[:, :c0], preferred_element_type=f32)
    p, w = acc[:tk], acc[tk:]
    hr = jnp.dot(m_ref[:tk, :], rhs_ref[:, c0:c0 + HY_W], preferred_element_type=f32)
    hi = jnp.dot(m_ref[tk:, :], rhs_ref[:, c0 + HY_W:], preferred_element_type=f32)
    qs = jnp.dot(m_ref[tk:tk + HALO_ROWS, :], rhs_ref[:, c0:c0 + HY_W], preferred_element_type=f32)[0:1]
    first = (lax.broadcasted_iota(jnp.int32, (tk, HY_W), 0) == 0) & (i == 0)
    for b in range(nb):
        pu, wu = p[:, b * HY_W:(b + 1) * HY_W], w[:, b * HY_W:(b + 1) * HY_W]
        r_ref[b] = jnp.where(first, pu * hr, 2.0 * (pu * hr + wu * hi)).astype(bf16)
        z_ref[b] = jnp.where(first, wu * qs, 2.0 * (wu * hr - pu * hi)).astype(bf16)


def _hy_fwd(mat, zh, conv_w, conv_b, hb, batch, seq_len):
    tk, _ = _hy_tiles(seq_len)
    out = pl.BlockSpec((batch, tk, HY_W), lambda i: (0, i, 0))
    return pl.pallas_call(
        functools.partial(_hy_fwd_kernel, batch, tk, seq_len),
        grid=(seq_len // tk,),
        in_specs=[pl.BlockSpec((2 * tk, seq_len), lambda i: (i, 0)),
                  pl.BlockSpec((batch, seq_len, 3 * HY_W), lambda i: (0, 0, 0)),
                  pl.BlockSpec((3, 3 * HY_W), lambda i: (0, 0)),
                  pl.BlockSpec((1, 3 * HY_W), lambda i: (0, 0)),
                  pl.BlockSpec((2, seq_len, HY_W), lambda i: (0, 0, 0))],
        out_specs=[out, out],
        out_shape=[jax.ShapeDtypeStruct((batch, seq_len, HY_W), bf16)] * 2,
        scratch_shapes=[pltpu.VMEM((seq_len, (batch + 2) * HY_W), bf16)],
        compiler_params=_params("arbitrary"),
        name="hy_fwd",
    )(mat, zh, conv_w, conv_b.reshape(1, 3 * HY_W), hb)


def _hy_inv_kernel(nb, seq_len, nt, m_ref, r_ref, z_ref, zp_ref, zc_ref, zn_ref, cw_ref, cb_ref, d_ref, o_ref,
                   rhs_ref):
    i = pl.program_id(0)

    @pl.when(i == 0)
    def _():
        for b in range(nb):
            rhs_ref[:seq_len, b * HY_W:(b + 1) * HY_W] = r_ref[b]
            rhs_ref[seq_len:, b * HY_W:(b + 1) * HY_W] = z_ref[b]

    acc = jnp.dot(m_ref[...], rhs_ref[...], preferred_element_type=f32)
    d, w, bias = d_ref[...], cw_ref[...], cb_ref[...]
    for b in range(nb):
        x0, u = _short_conv(zc_ref[b], zp_ref[b], zn_ref[b], i > 0, i < nt - 1, w, bias)
        y = acc[:, b * HY_W:(b + 1) * HY_W] * (1.0 / (2 * seq_len)) + u * d
        o_ref[b] = (x0 * y).astype(o_ref.dtype)


def _hy_inv(mat, r, zz, zh, conv_w, conv_b, d_bias, batch, seq_len):
    _, tm = _hy_tiles(seq_len)
    nt = seq_len // tm
    rh = tm // HALO_ROWS
    last = seq_len // HALO_ROWS - 1
    full = pl.BlockSpec((batch, seq_len, HY_W), lambda i: (0, 0, 0))
    halo = lambda f: pl.BlockSpec((batch, HALO_ROWS, 3 * HY_W), f)
    return pl.pallas_call(
        functools.partial(_hy_inv_kernel, batch, seq_len, nt),
        grid=(nt,),
        in_specs=[pl.BlockSpec((tm, 2 * seq_len), lambda i: (i, 0)), full, full,
                  halo(lambda i: (0, jnp.maximum(i * rh - 1, 0), 0)),
                  pl.BlockSpec((batch, tm, 3 * HY_W), lambda i: (0, i, 0)),
                  halo(lambda i: (0, jnp.minimum((i + 1) * rh, last), 0)),
                  pl.BlockSpec((3, 3 * HY_W), lambda i: (0, 0)),
                  pl.BlockSpec((1, 3 * HY_W), lambda i: (0, 0)),
                  pl.BlockSpec((1, HY_W), lambda i: (0, 0))],
        out_specs=pl.BlockSpec((batch, tm, HY_W), lambda i: (0, i, 0)),
        out_shape=jax.ShapeDtypeStruct((batch, seq_len, HY_W), bf16),
        scratch_shapes=[pltpu.VMEM((2 * seq_len, batch * HY_W), bf16)],
        compiler_params=_params("arbitrary"),
        name="hy_inv",
    )(mat, r, zz, zh, zh, zh, conv_w, conv_b.reshape(1, 3 * HY_W), d_bias.reshape(1, HY_W))


@functools.lru_cache(maxsize=None)
def _hy_consts_np(seq_len):
    n = 2 * seq_len
    k = np.arange(seq_len, dtype=np.int64)
    ang = (2.0 * np.pi / n) * ((k[:, None] * k[None, :]) % n).astype(np.float64)
    cosm = np.cos(ang)
    sinm = np.sin(ang)
    sinm[0, :] = np.where(k % 2 == 0, 1.0, -1.0)
    tk, _ = _hy_tiles(seq_len)
    fwd = np.concatenate([np.concatenate([cosm[i:i + tk], sinm[i:i + tk]], axis=0)
                          for i in range(0, seq_len, tk)], axis=0)
    inv = np.concatenate([cosm, sinm.T], axis=1)
    t = np.linspace(0.0, 1.0, seq_len, dtype=np.float32)[:, None]
    bands = (FILTER_EMB - 1) // 2
    fb = np.linspace(1e-4, bands - 1, bands, dtype=np.float32)[None, :]
    w = (np.float32(2.0 * math.pi) * np.arange(seq_len, dtype=np.float32)[:, None] / np.float32(seq_len)).astype(np.float32)
    feats = np.concatenate([t, np.cos(fb * w), -np.sin(fb * w)], axis=-1).astype(np.float32)
    feats = np.ascontiguousarray(np.pad(feats, ((0, 0), (0, LANES - FILTER_EMB))).T)
    decay_hi = math.log(HY_DECAY_TARGET) / HY_DECAY_HI_PCT
    decay_lo = math.log(HY_DECAY_TARGET) / HY_DECAY_LO_PCT
    deltas = np.abs(np.linspace(decay_lo, decay_hi, HY_W, dtype=np.float32))
    window = np.exp(-t * deltas[None, :]).astype(np.float32)
    return fwd, inv, feats, window


def _hy_consts(seq_len):
    fwd, inv, feats, window = _hy_consts_np(seq_len)
    as_bf = lambda m: jnp.asarray(m, dtype=f32).astype(bf16)
    return {"fwd": as_bf(fwd), "inv": as_bf(inv), "feats": jnp.asarray(feats), "window": jnp.asarray(window)}


def _hyena(zh, lp, consts, batch, seq_len):
    zh = zh.reshape(batch, seq_len, 3 * HY_W)
    hb = _hy_filter(seq_len, consts, lp["hy_f_w1"], lp["hy_f_b1"], lp["hy_f_w2"], lp["hy_f_b2"],
                    lp["hy_f_w3"], lp["hy_freq"]).reshape(2, seq_len, HY_W)
    r, zz = _hy_fwd(consts["fwd"], zh, lp["hy_conv_w"], lp["hy_conv_b"], hb, batch, seq_len)
    yh = _hy_inv(consts["inv"], r, zz, zh, lp["hy_conv_w"], lp["hy_conv_b"], lp["hy_bias"], batch, seq_len)
    return yh.reshape(batch * seq_len, HY_W)


def _gelu(x):
    return 0.5 * x * (1.0 + jnp.tanh(math.sqrt(2.0 / math.pi) * (x + 0.044715 * (x * x * x))))


def _gmlp(z, ws_ref, bias):
    g = _gelu(z)
    outs = []
    for c in range(TM // GM_CHUNK):
        rows = slice(c * GM_CHUNK, (c + 1) * GM_CHUNK)
        u = g[rows, :GM_W]
        v = g[rows, GM_W:].astype(bf16)
        s = jnp.concatenate(
            [jnp.dot(ws_ref[h], v[:, h * GM_HEAD_DIM:(h + 1) * GM_HEAD_DIM], preferred_element_type=f32)
             for h in range(GM_HEADS)], axis=1)
        outs.append(u * (s + bias))
    return jnp.concatenate(outs, axis=0)


def _out_kernel(a_ref, yh_ref, zg_ref, ws_ref, gb_ref, x_ref, mod_ref, og_ref, w_ref, g2_ref, rw_ref,
                x1_ref, h2_ref, aff_ref):
    og = og_ref[...]
    yg = _gmlp(zg_ref[...].astype(f32), ws_ref, gb_ref[...])
    mixed_in = jnp.concatenate([
        _rms(a_ref[...].astype(f32)) * og[:, :ATTN_W],
        _rms(yh_ref[...].astype(f32)) * og[:, ATTN_W:ATTN_W + HY_W],
        _rms(yg) * og[:, ATTN_W + HY_W:]], axis=1)
    mixed = jnp.dot(mixed_in.astype(bf16), w_ref[...], preferred_element_type=f32)
    m = mod_ref[0]
    x1 = x_ref[...] + m[2:3] * mixed
    x1_ref[...] = x1
    h2 = _rms(x1) * g2_ref[...] * (1.0 + m[4:5]) + m[3:4]
    for j in range(ROW_CHUNKS):
        h2_ref[pl.ds(j, TM, stride=ROW_CHUNKS), :] = h2[:, j * LANES:(j + 1) * LANES]
    nt = (((1,), (1,)), ((), ()))
    h_hi = h2.astype(bf16)
    h_lo = (h2 - h_hi.astype(f32)).astype(bf16)
    rw = rw_ref[...]
    t = lax.dot_general(rw, h_hi, nt, preferred_element_type=f32)
    logits = (t[:N_EXPERTS] + t[N_EXPERTS:]) + lax.dot_general(rw[:N_EXPERTS], h_lo, nt, preferred_element_type=f32)
    e = jnp.exp(logits - jnp.max(logits, axis=0, keepdims=True))
    aff_ref[...] = e / jnp.sum(e, axis=0, keepdims=True)


def _out_proj(a, yh, zg, ws_bf, gm_b, x, mod, out_g, w_out_bf, g2, router_wt):
    n = x.shape[0]
    per_mod = n // mod.shape[0]
    row = lambda w: pl.BlockSpec((TM, w), lambda i: (i, 0))
    const = lambda s: pl.BlockSpec(s, lambda i: (0,) * len(s))
    gm_bias = jnp.repeat(gm_b.T, GM_HEAD_DIM, axis=1)
    return pl.pallas_call(
        _out_kernel,
        grid=(n // TM,),
        in_specs=[row(ATTN_W), row(HY_W), row(2 * GM_W), const((GM_HEADS, GM_CHUNK, GM_CHUNK)),
                  const((GM_CHUNK, GM_W)), row(D_MODEL),
                  pl.BlockSpec((1, 6, D_MODEL), lambda i: (i * TM // per_mod, 0, 0)),
                  const((1, MIX_W)), const((MIX_W, D_MODEL)), const((1, D_MODEL)), const((2 * N_EXPERTS, D_MODEL))],
        out_specs=[row(D_MODEL), pl.BlockSpec((TM * ROW_CHUNKS, LANES), lambda i: (i, 0)),
                   pl.BlockSpec((N_EXPERTS, TM), lambda i: (0, i))],
        out_shape=[jax.ShapeDtypeStruct((n, D_MODEL), f32), jax.ShapeDtypeStruct((n * ROW_CHUNKS, LANES), f32),
                   jax.ShapeDtypeStruct((N_EXPERTS, n), f32)],
        compiler_params=_params("parallel"),
        name="out_proj",
    )(a, yh, zg, ws_bf, gm_bias, x, mod, out_g.reshape(1, MIX_W), w_out_bf, g2.reshape(1, D_MODEL), router_wt)


RT_CHUNK = 512


def _prefix_incl(x01, tri):
    n = x01.shape[1]
    carry = jnp.zeros((x01.shape[0], 1), f32)
    parts = []
    for c in range(n // RT_CHUNK):
        piece = x01[:, c * RT_CHUNK:(c + 1) * RT_CHUNK]
        parts.append(jnp.dot(piece.astype(bf16), tri, preferred_element_type=f32) + carry)
        carry = carry + jnp.sum(piece, axis=1, keepdims=True)
    return jnp.concatenate(parts, axis=1)


def _route_kernel(cap, aff_ref, idx_ref, gate_ref):
    aff = aff_ref[...]
    n = aff.shape[1]

    def step(it, t):
        cand = t | (jnp.int32(1) << (30 - it))
        cnt = jnp.sum(jnp.where(aff >= lax.bitcast_convert_type(cand, f32), 1.0, 0.0), axis=1, keepdims=True)
        return jnp.where(cnt >= cap, cand, t)

    thr_bits = lax.fori_loop(0, 31, step, jnp.zeros((aff.shape[0], 1), jnp.int32))
    thr = lax.bitcast_convert_type(thr_bits, f32)
    gt = jnp.where(aff > thr, 1.0, 0.0)
    eq = jnp.where(aff == thr, 1.0, 0.0)
    room = cap - jnp.sum(gt, axis=1, keepdims=True)
    r = lax.broadcasted_iota(jnp.int32, (RT_CHUNK, RT_CHUNK), 0)
    c = lax.broadcasted_iota(jnp.int32, (RT_CHUNK, RT_CHUNK), 1)
    tri = jnp.where(r <= c, 1.0, 0.0).astype(bf16)
    sel = jnp.maximum(gt, jnp.where(_prefix_incl(eq, tri) <= room, eq, 0.0))
    slot = _prefix_incl(sel, tri) - 1.0

    tok = lax.broadcasted_iota(jnp.int32, aff.shape, 1)
    dist = jnp.where(sel > 0.0, tok - slot.astype(jnp.int32), 0)
    g = aff
    for b in range(max(1, (n - 1).bit_length())):
        sh = 1 << b
        dist_s = pltpu.roll(dist, n - sh, axis=1)
        take = (dist_s & sh) != 0
        leave = (dist & sh) != 0
        tok = jnp.where(take, pltpu.roll(tok, n - sh, axis=1), tok)
        g = jnp.where(take, pltpu.roll(g, n - sh, axis=1), g)
        dist = jnp.where(take, dist_s, jnp.where(leave, 0, dist))
    idx_ref[...] = tok[:, :cap]
    gate_ref[...] = g[:, :cap]


def _route(aff_t, cap):
    return pl.pallas_call(
        functools.partial(_route_kernel, cap),
        in_specs=[pl.BlockSpec(memory_space=pltpu.VMEM)],
        out_specs=[pl.BlockSpec(memory_space=pltpu.VMEM)] * 2,
        out_shape=[jax.ShapeDtypeStruct((N_EXPERTS, cap), jnp.int32), jax.ShapeDtypeStruct((N_EXPERTS, cap), f32)],
        compiler_params=pltpu.CompilerParams(vmem_limit_bytes=VMEM_LIMIT),
        name="route",
    )(aff_t)


GATHER_UNROLL = 8
SCATTER_UNROLL = 8
CMB_EXPERTS = 2
CMB_TM = 1024


def _moe_ffn_kernel(cap, stride, idx_ref, h_ref, w1_ref, w3_ref, w2_ref, y_ref, tile_ref, xb_ref, acc_ref):
    e = pl.program_id(0)
    slab = ROW_CHUNKS * stride
    cur = pl.multiple_of((e % 2) * slab, SUBLANES)
    nxt = pl.multiple_of(((e + 1) % 2) * slab, SUBLANES)

    def gather_row(expert, slab_row, r):
        src = pl.multiple_of(idx_ref[expert * cap + r] * ROW_CHUNKS, ROW_CHUNKS)
        tile_ref[pl.ds(slab_row + r, ROW_CHUNKS, stride=stride), :] = h_ref[pl.ds(src, ROW_CHUNKS), :]

    def staged_rows(slab_row):
        return jnp.concatenate(
            [tile_ref[pl.ds(slab_row + j * stride, cap), :] for j in range(ROW_CHUNKS)], axis=1).astype(bf16)

    @pl.when(e == 0)
    def _():
        def gather(q, carry):
            for t in range(GATHER_UNROLL):
                gather_row(0, cur, q * GATHER_UNROLL + t)
            return carry
        lax.fori_loop(0, cap // GATHER_UNROLL, gather, 0)
        xb_ref[0] = staged_rows(cur)

    e_next = jnp.minimum(e + 1, N_EXPERTS - 1)
    n_f = D_EXPERT // MOE_F_TILE
    xb = xb_ref[e % 2]
    for f in range(n_f):
        cols = slice(f * MOE_F_TILE, (f + 1) * MOE_F_TILE)
        a = jnp.dot(xb, w1_ref[0, 0, :, cols].astype(bf16), preferred_element_type=f32)
        b = jnp.dot(xb, w3_ref[0, 0, :, cols].astype(bf16), preferred_element_type=f32)
        he = (a * _sigmoid(a) * b).astype(bf16)
        part = jnp.dot(he, w2_ref[0, 0, cols, :].astype(bf16), preferred_element_type=f32)
        if f == 0:
            acc_ref[...] = part
        elif f < n_f - 1:
            acc_ref[...] += part
        else:
            y = acc_ref[...] + part
            for j in range(ROW_CHUNKS):
                y_ref[0, j * stride:j * stride + cap, :] = y[:, j * LANES:(j + 1) * LANES]
                y_ref[0, j * stride + cap:(j + 1) * stride, :] = jnp.zeros((stride - cap, LANES), f32)
        for r in range(f * cap // n_f, (f + 1) * cap // n_f):
            gather_row(e_next, nxt, r)
    xb_ref[(e + 1) % 2] = staged_rows(nxt)


def _moe_ffn(layer, h3, idx, w1, w3, w2):
    cap = idx.shape[1]
    stride = cap + SUBLANES
    grid_spec = pltpu.PrefetchScalarGridSpec(
        num_scalar_prefetch=1,
        grid=(N_EXPERTS,),
        in_specs=[
            pl.BlockSpec(memory_space=pltpu.VMEM),
            pl.BlockSpec((1, 1, D_MODEL, D_EXPERT), lambda e, idx: (layer, e, 0, 0)),
            pl.BlockSpec((1, 1, D_MODEL, D_EXPERT), lambda e, idx: (layer, e, 0, 0)),
            pl.BlockSpec((1, 1, D_EXPERT, D_MODEL), lambda e, idx: (layer, e, 0, 0)),
        ],
        out_specs=pl.BlockSpec((1, ROW_CHUNKS * stride, LANES), lambda e, idx: (e, 0, 0)),
        scratch_shapes=[pltpu.VMEM((2 * ROW_CHUNKS * stride, LANES), f32),
                        pltpu.VMEM((2, cap, D_MODEL), bf16),
                        pltpu.VMEM((cap, D_MODEL), f32)],
    )
    return pl.pallas_call(
        functools.partial(_moe_ffn_kernel, cap, stride),
        grid_spec=grid_spec,
        out_shape=jax.ShapeDtypeStruct((N_EXPERTS, ROW_CHUNKS * stride, LANES), f32),
        compiler_params=_params("arbitrary"),
        name="moe_ffn",
    )(idx.reshape(-1), h3, w1, w3, w2)


def _moe_combine_kernel(cap, stride, final, idx_ref, gate_ref, y_ref, x_ref, mod_ref, g_ref, o_ref, acc_ref):
    s = pl.program_id(0)
    scatter_steps = N_EXPERTS // CMB_EXPERTS

    @pl.when(s == 0)
    def _():
        acc_ref[...] = jnp.zeros_like(acc_ref)

    @pl.when(s < scatter_steps)
    def _():
        for k in range(CMB_EXPERTS):
            base = (s * CMB_EXPERTS + k) * cap

            def scatter(q, carry, k=k, base=base):
                rows = [pl.multiple_of(idx_ref[base + q * SCATTER_UNROLL + t] * ROW_CHUNKS, ROW_CHUNKS)
                        for t in range(SCATTER_UNROLL)]
                new = [acc_ref[pl.ds(rows[t], ROW_CHUNKS), :]
                       + gate_ref[base + q * SCATTER_UNROLL + t]
                       * y_ref[k, pl.ds(q * SCATTER_UNROLL + t, ROW_CHUNKS, stride=stride), :]
                       for t in range(SCATTER_UNROLL)]
                for t in range(SCATTER_UNROLL):
                    acc_ref[pl.ds(rows[t], ROW_CHUNKS), :] = new[t]
                return carry
            lax.fori_loop(0, cap // SCATTER_UNROLL, scatter, 0)

    @pl.when(s >= scatter_steps)
    def _():
        first = pl.multiple_of((s - scatter_steps) * (CMB_TM * ROW_CHUNKS), CMB_TM * ROW_CHUNKS)
        moe = jnp.concatenate([acc_ref[pl.ds(first + j, CMB_TM, stride=ROW_CHUNKS), :] for j in range(ROW_CHUNKS)],
                              axis=1)
        x = x_ref[...] + mod_ref[0][5:6] * moe
        o_ref[...] = _rms(x) * g_ref[...] if final else x


def _moe_combine(idx, gate, y_cm, x1, mod, final_g, final):
    n = x1.shape[0]
    cap = idx.shape[1]
    stride = y_cm.shape[1] // ROW_CHUNKS
    per_mod = n // mod.shape[0]
    scatter_steps = N_EXPERTS // CMB_EXPERTS
    tile = lambda s: jnp.maximum(s - scatter_steps, 0)
    row = pl.BlockSpec((CMB_TM, D_MODEL), lambda s, *_: (tile(s), 0))
    grid_spec = pltpu.PrefetchScalarGridSpec(
        num_scalar_prefetch=2,
        grid=(scatter_steps + n // CMB_TM,),
        in_specs=[pl.BlockSpec((CMB_EXPERTS, ROW_CHUNKS * stride, LANES),
                               lambda s, *_: (jnp.minimum(s, scatter_steps - 1), 0, 0)),
                  row,
                  pl.BlockSpec((1, 6, D_MODEL), lambda s, *_: (tile(s) * CMB_TM // per_mod, 0, 0)),
                  pl.BlockSpec((1, D_MODEL), lambda s, *_: (0, 0))],
        out_specs=row,
        scratch_shapes=[pltpu.VMEM((n * ROW_CHUNKS, LANES), f32)],
    )
    return pl.pallas_call(
        functools.partial(_moe_combine_kernel, cap, stride, final),
        grid_spec=grid_spec,
        out_shape=jax.ShapeDtypeStruct((n, D_MODEL), f32),
        compiler_params=_params("arbitrary"),
        name="moe_combine",
    )(idx.reshape(-1), gate.reshape(-1), y_cm, x1, mod, final_g.reshape(1, D_MODEL))


def _expert_choice(layer, h3, aff_t, x1, mod, final_g, final, w1, w3, w2):
    n = x1.shape[0]
    cap = max(1, EC_CAPACITY * n // N_EXPERTS)
    idx, gate = _route(aff_t, cap)
    y_cm = _moe_ffn(layer, h3, idx, w1, w3, w2)
    return _moe_combine(idx, gate, y_cm, x1, mod, final_g, final)


def _split_bf16(w):
    hi = w.astype(bf16)
    lo = (w - hi.astype(f32)).astype(bf16)
    return jnp.concatenate([hi, lo], axis=0)


def _stream(x, mods, layers, experts, batch, seq_len, final_g, caches=None):
    consts = _hy_consts(seq_len)
    rope_tabs = _rope_tables(seq_len) if caches is not None else None
    kvs = []
    for l, lp in enumerate(layers):
        qkv, zh, zg = _in_proj(x, mods[l], lp["norm1_g"], lp["w_in"], seq_len, rope_tabs=rope_tabs)
        if caches is None:
            a = _ctx_attention(qkv, lp["attn_sink"], batch, seq_len)
            kvs.append((qkv[:, ATTN_W:ATTN_W + KV_W], qkv[:, ATTN_W + KV_W:]))
        else:
            a = _lat_attention(qkv, lp["attn_sink"], caches[0][:, l], caches[1][:, l], batch, seq_len)
        yh = _hyena(zh, lp, consts, batch, seq_len)
        x1, h3, aff_t = _out_proj(a, yh, zg, lp["gm_ws"], lp["gm_b"], x, mods[l], lp["out_norm_g"], lp["w_out"],
                                  lp["norm2_g"], lp["router_wt"])
        x = _expert_choice(l, h3, aff_t, x1, mods[l], final_g, l == len(layers) - 1, *experts)
    return x, kvs


def kernel(x_prompt, x_sample, c, cache_k, cache_v, c_ctx, norm1_g, norm2_g, ada_w, ada_b, w_in, attn_sink,
           hy_conv_w, hy_conv_b, hy_f_w1, hy_f_b1, hy_f_w2, hy_f_b2, hy_f_w3, hy_freq, hy_bias, gm_ws, gm_b,
           out_norm_g, w_out, router_w, exp_w1, exp_w3, exp_w2, final_g):
    batch, seq, _ = x_prompt.shape
    dbatch, dseq, _ = x_sample.shape
    past = cache_k.shape[2]

    cvec = jnp.zeros((SUBLANES, D_MODEL), f32).at[0].set(c_ctx).at[1:1 + dbatch].set(c)
    mod = _ada(cvec, ada_w, ada_b)
    mods_ctx = [mod[l, 0:1].reshape(1, 6, D_MODEL) for l in range(DEPTH)]
    mods_lat = [mod[l, 1:1 + dbatch].reshape(dbatch, 6, D_MODEL) for l in range(DEPTH)]

    w_in_bf, w_out_bf, gm_ws_bf = w_in.astype(bf16), w_out.astype(bf16), gm_ws.astype(bf16)
    layers = []
    for l in range(DEPTH):
        layers.append({
            "norm1_g": norm1_g[l], "norm2_g": norm2_g[l], "w_in": w_in_bf[l], "attn_sink": attn_sink[l],
            "hy_conv_w": hy_conv_w[l], "hy_conv_b": hy_conv_b[l], "hy_f_w1": hy_f_w1[l], "hy_f_b1": hy_f_b1[l],
            "hy_f_w2": hy_f_w2[l], "hy_f_b2": hy_f_b2[l], "hy_f_w3": hy_f_w3[l], "hy_freq": hy_freq[l],
            "hy_bias": hy_bias[l], "gm_ws": gm_ws_bf[l], "gm_b": gm_b[l], "out_norm_g": out_norm_g[l],
            "w_out": w_out_bf[l], "router_wt": _split_bf16(router_w[l].T)})
    experts = (exp_w1, exp_w3, exp_w2)

    yp, kvs = _stream(x_prompt.reshape(batch * seq, D_MODEL), mods_ctx, layers, experts, batch, seq, final_g)
    caches = (cache_k.reshape(dbatch, DEPTH, past, KV_W), cache_v.reshape(dbatch, DEPTH, past, KV_W))
    ys, _ = _stream(x_sample.reshape(dbatch * dseq, D_MODEL), mods_lat, layers, experts, dbatch, dseq, final_g,
                   caches=caches)

    new_k = jnp.stack([k.reshape(batch, seq, N_KV, HEAD_DIM) for k, _ in kvs], axis=1)
    new_v = jnp.stack([v.reshape(batch, seq, N_KV, HEAD_DIM) for _, v in kvs], axis=1)
    return (yp.reshape(batch, seq, D_MODEL), ys.reshape(dbatch, dseq, D_MODEL), new_k, new_v)
```

```python
import functools
import math

import numpy as np
import jax
import jax.numpy as jnp
from jax import lax
from jax.experimental import pallas as pl
from jax.experimental.pallas import tpu as pltpu

f32 = jnp.float32
bf16 = jnp.bfloat16

D_MODEL = 1024
DEPTH = 2
GRID_W = 64
BLOCK = 128
N_HEADS = 8
N_KV = 2
HEAD_DIM = 64
Q_GROUP = N_HEADS // N_KV
ATTN_W = N_HEADS * HEAD_DIM
KV_W = N_KV * HEAD_DIM
QKV_W = ATTN_W + 2 * KV_W
HY_W = 256
GM_W = 256
GM_HEADS = 4
GM_HEAD_DIM = GM_W // GM_HEADS
GM_CHUNK = 128
MIX_W = ATTN_W + HY_W + GM_W
IN_W = ATTN_W + 2 * KV_W + 3 * HY_W + 2 * GM_W
FILTER_EMB = 33
FILTER_HID = 64
HY_DECAY_HI_PCT = 0.3
HY_DECAY_LO_PCT = 1.5
HY_DECAY_TARGET = 1e-2
N_EXPERTS = 16
EC_CAPACITY = 2
D_EXPERT = 1024
ROPE_THETA = 10000.0
EPS = 1e-6
NEG = -1e30

LANES = 128
SUBLANES = 8
ROW_CHUNKS = D_MODEL // LANES
VMEM_LIMIT = 56 * 1024 * 1024
TM = 512
IN_TM = 512
MOE_F_TILE = 256


def _params(*sem):
    return pltpu.CompilerParams(dimension_semantics=sem, vmem_limit_bytes=VMEM_LIMIT)


def _rms(x):
    return x * lax.rsqrt(jnp.mean(x * x, axis=-1, keepdims=True) + EPS)


def _sigmoid(x):
    return 1.0 / (1.0 + jnp.exp(-x))


def _ada_kernel(c_ref, w_ref, b_ref, o_ref):
    c = c_ref[...]
    s = (c * _sigmoid(c)).astype(bf16)
    o_ref[0] = jnp.dot(s, w_ref[0].astype(bf16), preferred_element_type=f32) + b_ref[0]


def _ada(cvec, ada_w, ada_b):
    nt = 6
    return pl.pallas_call(
        _ada_kernel,
        grid=(DEPTH, nt),
        in_specs=[
            pl.BlockSpec((SUBLANES, D_MODEL), lambda l, j: (0, 0)),
            pl.BlockSpec((1, D_MODEL, D_MODEL), lambda l, j: (l, 0, j)),
            pl.BlockSpec((1, 1, D_MODEL), lambda l, j: (l, 0, j)),
        ],
        out_specs=pl.BlockSpec((1, SUBLANES, D_MODEL), lambda l, j: (l, 0, j)),
        out_shape=jax.ShapeDtypeStruct((DEPTH, SUBLANES, 6 * D_MODEL), f32),
        compiler_params=_params("arbitrary", "arbitrary"),
        name="ada",
    )(cvec, ada_w, ada_b.reshape(DEPTH, 1, 6 * D_MODEL))


def _rope_swap(x):
    w = x.shape[-1]
    lane = lax.broadcasted_iota(jnp.int32, x.shape, 1)
    first = (lane % 32) < 16
    return jnp.where(first, pltpu.roll(x, w - 16, axis=1), pltpu.roll(x, 16, axis=1))


def _in_kernel(rope, x_ref, mod_ref, g_ref, w_ref, *refs):
    if rope:
        cos_ref, sin_ref, qkv_ref, zh_ref, zg_ref = refs
    else:
        qkv_ref, zh_ref, zg_ref = refs
    m = mod_ref[0]
    h = _rms(x_ref[...]) * g_ref[...] * (1.0 + m[1:2]) + m[0:1]
    z = jnp.dot(h.astype(bf16), w_ref[...], preferred_element_type=f32)
    if rope:
        reps = (ATTN_W + KV_W) // LANES
        cos = jnp.concatenate([cos_ref[...]] * reps, axis=1)
        sin = jnp.concatenate([sin_ref[...]] * reps, axis=1)
        qk = z[:, :ATTN_W + KV_W]
        qkv_ref[:, :ATTN_W + KV_W] = qk * cos + _rope_swap(qk) * sin
        qkv_ref[:, ATTN_W + KV_W:] = z[:, ATTN_W + KV_W:QKV_W]
    else:
        qkv_ref[...] = z[:, :QKV_W]
    zh_ref[...] = z[:, QKV_W:QKV_W + 3 * HY_W].astype(bf16)
    zg_ref[...] = z[:, QKV_W + 3 * HY_W:].astype(bf16)


def _in_proj(x, mod, g1, w_in_bf, seq_len, rope_tabs=None):
    n = x.shape[0]
    rope = rope_tabs is not None
    per_mod = n // mod.shape[0]
    tm = IN_TM
    in_specs = [pl.BlockSpec((tm, D_MODEL), lambda i: (i, 0)),
                pl.BlockSpec((1, 6, D_MODEL), lambda i: (i * tm // per_mod, 0, 0)),
                pl.BlockSpec((1, D_MODEL), lambda i: (0, 0)),
                pl.BlockSpec((D_MODEL, IN_W), lambda i: (0, 0))]
    args = [x, mod, g1.reshape(1, D_MODEL), w_in_bf]
    if rope:
        nt = seq_len // tm
        tab = pl.BlockSpec((tm, LANES), lambda i: (i % nt, 0))
        in_specs += [tab, tab]
        args += list(rope_tabs)
    return pl.pallas_call(
        functools.partial(_in_kernel, rope),
        grid=(n // tm,),
        in_specs=in_specs,
        out_specs=[pl.BlockSpec((tm, w), lambda i: (i, 0)) for w in (QKV_W, 3 * HY_W, 2 * GM_W)],
        out_shape=[jax.ShapeDtypeStruct((n, QKV_W), f32), jax.ShapeDtypeStruct((n, 3 * HY_W), bf16),
                   jax.ShapeDtypeStruct((n, 2 * GM_W), bf16)],
        compiler_params=_params("parallel"),
        name="in_proj",
    )(*args)


def _rope_tables(seq_len):
    nf = HEAD_DIM // 4
    t = np.arange(seq_len)
    inv = (ROPE_THETA ** (-np.arange(nf, dtype=np.float32) / nf)).astype(np.float32)
    d = np.arange(HEAD_DIM)
    pos = np.where((d // 32)[None, :] == 0, (t // GRID_W)[:, None], (t % GRID_W)[:, None]).astype(np.float32)
    ang = (pos * inv[d % nf][None, :]).astype(np.float32)
    cos = np.cos(ang).astype(np.float32)
    sin = np.sin(ang).astype(np.float32) * np.where((d % 32) < 16, -1.0, 1.0)[None, :].astype(np.float32)
    return jnp.asarray(np.tile(cos, (1, 2))), jnp.asarray(np.tile(sin, (1, 2)))


LOG2E = math.log2(math.e)


def _stack_queries(z, sink_ref, kv):
    rows = z.shape[0]
    heads = range(kv * Q_GROUP, (kv + 1) * Q_GROUP)
    qs = jnp.concatenate([z[:, h * HEAD_DIM:(h + 1) * HEAD_DIM] for h in heads], axis=0)
    sink = jnp.concatenate([jnp.full((1, rows), sink_ref[h] * LOG2E, f32) for h in heads], axis=1)
    return (qs * (LOG2E / math.sqrt(HEAD_DIM))).astype(bf16), sink


def _scores_t(k, qs_bf):
    return lax.dot_general(k.astype(bf16), qs_bf, (((1,), (1,)), ((), ())), preferred_element_type=f32)


def _softmax_pv_t(st, sink, vals, rows):
    m = jnp.maximum(jnp.max(st, axis=0, keepdims=True), sink)
    pt = jnp.exp2(st - m).astype(bf16)
    v_bf = vals.astype(bf16)
    v_ext = jnp.concatenate([v_bf, jnp.ones_like(v_bf)], axis=1)
    ovt = lax.dot_general(v_ext, pt, (((0,), (0,)), ((), ())), preferred_element_type=f32)
    ot = ovt[:HEAD_DIM] / (ovt[HEAD_DIM:HEAD_DIM + 1] + jnp.exp2(sink - m))
    return [ot[:, g * rows:(g + 1) * rows].T for g in range(Q_GROUP)]


CTX_BATCHES = 2
LAT_QBLOCKS = 4


def _ctx_attn_kernel(seq_len, sink_ref, z_ref, o_ref):
    for s in range(CTX_BATCHES):
        z = z_ref[s * seq_len:(s + 1) * seq_len, :]
        outs = []
        for kv in range(N_KV):
            k = z[:, ATTN_W + kv * HEAD_DIM:ATTN_W + (kv + 1) * HEAD_DIM]
            v = z[:, ATTN_W + KV_W + kv * HEAD_DIM:ATTN_W + KV_W + (kv + 1) * HEAD_DIM]
            qs, sink = _stack_queries(z, sink_ref, kv)
            outs += _softmax_pv_t(_scores_t(k, qs), sink, v, seq_len)
        o_ref[s * seq_len:(s + 1) * seq_len, :] = jnp.concatenate(outs, axis=1).astype(o_ref.dtype)


def _ctx_attention(z, sink, batch, seq_len):
    rows = CTX_BATCHES * seq_len
    return pl.pallas_call(
        functools.partial(_ctx_attn_kernel, seq_len),
        grid=(batch // CTX_BATCHES,),
        in_specs=[pl.BlockSpec(memory_space=pltpu.SMEM),
                  pl.BlockSpec((rows, QKV_W), lambda b: (b, 0))],
        out_specs=pl.BlockSpec((rows, ATTN_W), lambda b: (b, 0)),
        out_shape=jax.ShapeDtypeStruct((batch * seq_len, ATTN_W), bf16),
        compiler_params=_params("parallel"),
        name="ctx_attn",
    )(sink, z)


def _lat_attn_kernel(nb, sink_ref, zp_ref, zc_ref, zn_ref, ck_ref, cv_ref, o_ref):
    step = pl.program_id(1)
    zc_all = zc_ref[...]
    blocks = [zp_ref[...]] + [zc_all[q * BLOCK:(q + 1) * BLOCK] for q in range(LAT_QBLOCKS)] + [zn_ref[...]]
    ck, cv = ck_ref[0], cv_ref[0]
    width = Q_GROUP * BLOCK
    j = lax.broadcasted_iota(jnp.int32, (BLOCK, width), 0)
    r = lax.broadcasted_iota(jnp.int32, (BLOCK, width), 1) % BLOCK
    for q in range(LAT_QBLOCKS):
        i = step * LAT_QBLOCKS + q
        zp, zc, zn = blocks[q], blocks[q + 1], blocks[q + 2]
        ok_prev = j >= r + jnp.where(i > 0, 0, BLOCK)
        ok_next = j <= r - jnp.where(i < nb - 1, 0, BLOCK)
        outs = []
        for kv in range(N_KV):
            ks = slice(ATTN_W + kv * HEAD_DIM, ATTN_W + (kv + 1) * HEAD_DIM)
            vs = slice(ATTN_W + KV_W + kv * HEAD_DIM, ATTN_W + KV_W + (kv + 1) * HEAD_DIM)
            cs = slice(kv * HEAD_DIM, (kv + 1) * HEAD_DIM)
            qs, sink = _stack_queries(zc, sink_ref, kv)
            st = jnp.concatenate([
                jnp.where(ok_prev, _scores_t(zp[:, ks], qs), NEG),
                _scores_t(zc[:, ks], qs),
                jnp.where(ok_next, _scores_t(zn[:, ks], qs), NEG),
                _scores_t(ck[:, cs], qs)], axis=0)
            vals = jnp.concatenate([zp[:, vs], zc[:, vs], zn[:, vs], cv[:, cs]], axis=0)
            outs += _softmax_pv_t(st, sink, vals, BLOCK)
        o_ref[q * BLOCK:(q + 1) * BLOCK, :] = jnp.concatenate(outs, axis=1).astype(o_ref.dtype)


def _lat_attention(z, sink, ck, cv, batch, seq_len):
    nb = seq_len // BLOCK
    ns = nb // LAT_QBLOCKS
    blk = lambda f: pl.BlockSpec((BLOCK, QKV_W), f)
    past = ck.shape[1]
    cache = pl.BlockSpec((1, past, KV_W), lambda b, i: (b, 0, 0))
    return pl.pallas_call(
        functools.partial(_lat_attn_kernel, nb),
        grid=(batch, ns),
        in_specs=[pl.BlockSpec(memory_space=pltpu.SMEM),
                  blk(lambda b, i: (b * nb + jnp.maximum(i * LAT_QBLOCKS - 1, 0), 0)),
                  pl.BlockSpec((LAT_QBLOCKS * BLOCK, QKV_W), lambda b, i: (b * ns + i, 0)),
                  blk(lambda b, i: (b * nb + jnp.minimum((i + 1) * LAT_QBLOCKS, nb - 1), 0)),
                  cache, cache],
        out_specs=pl.BlockSpec((LAT_QBLOCKS * BLOCK, ATTN_W), lambda b, i: (b * ns + i, 0)),
        out_shape=jax.ShapeDtypeStruct((batch * seq_len, ATTN_W), bf16),
        compiler_params=_params("parallel", "parallel"),
        name="lat_attn",
    )(sink, z, z, z, ck, cv)


HY_TL = 256
HALO_ROWS = 2 * SUBLANES


def _short_conv(z, prev_tile, next_tile, has_prev, has_next, w, b):
    z = z.astype(f32)
    rows = z.shape[0]
    row = lax.broadcasted_iota(jnp.int32, z.shape, 0)
    prev_row = jnp.where(has_prev, prev_tile.astype(f32)[HALO_ROWS - 1:HALO_ROWS, :], 0.0)
    next_row = jnp.where(has_next, next_tile.astype(f32)[0:1, :], 0.0)
    z_prev = jnp.where(row == 0, prev_row, pltpu.roll(z, 1, axis=0))
    z_next = jnp.where(row == rows - 1, next_row, pltpu.roll(z, rows - 1, axis=0))
    zc = z_prev * w[0:1] + z * w[1:2] + z_next * w[2:3] + b
    return zc[:, :HY_W], zc[:, HY_W:2 * HY_W] * zc[:, 2 * HY_W:]


def _hy_filter_kernel(seq_len, feats_ref, win_ref, w1_ref, b1_ref, w2_ref, b2_ref, w3_ref, fr_ref, o_ref):
    hi = lax.Precision.HIGHEST
    fr = fr_ref[...]
    h = jnp.sin(fr * (jnp.dot(w1_ref[...], feats_ref[...], precision=hi, preferred_element_type=f32) + b1_ref[...]))
    h = jnp.sin(fr * (jnp.dot(w2_ref[...], h, precision=hi, preferred_element_type=f32) + b2_ref[...]))
    h = jnp.dot(h.T, w3_ref[...], precision=hi, preferred_element_type=f32)
    win = win_ref[...]
    row = lax.broadcasted_iota(jnp.int32, (seq_len, HY_W), 0)
    hf = h[:, :HY_W] * win
    hb = jnp.where(row == 0, 0.0, h[:, HY_W:] * win)
    o_ref[0:seq_len, :] = hf + hb
    o_ref[seq_len:, :] = hb - hf


def _hy_filter(seq_len, consts, w1, b1, w2, b2, w3, freq):
    w1t = jnp.zeros((FILTER_HID, LANES), f32).at[:, :FILTER_EMB].set(w1.T)
    args = (consts["feats"], consts["window"], w1t, b1.reshape(-1, 1), w2.T, b2.reshape(-1, 1), w3, freq.reshape(-1, 1))
    return pl.pallas_call(
        functools.partial(_hy_filter_kernel, seq_len),
        in_specs=[pl.BlockSpec(memory_space=pltpu.VMEM)] * len(args),
        out_specs=pl.BlockSpec(memory_space=pltpu.VMEM),
        out_shape=jax.ShapeDtypeStruct((2 * seq_len, HY_W), f32),
        compiler_params=pltpu.CompilerParams(vmem_limit_bytes=VMEM_LIMIT),
        name="hy_filter",
    )(*args)


def _hy_tiles(seq_len):
    return min(TM, seq_len // 2), min(TM, seq_len)


def _hy_fwd_kernel(nb, tk, seq_len, m_ref, zh_ref, cw_ref, cb_ref, hb_ref, rz_ref, rhs_ref):
    i = pl.program_id(0)
    nc = seq_len // HY_TL

    @pl.when(i == 0)
    def _():
        w, bias = cw_ref[...], cb_ref[...]
        for b in range(nb):
            def chunk(c, carry, b=b):
                r0 = pl.multiple_of(c * HY_TL, HY_TL)
                prev = zh_ref[b, pl.ds(pl.multiple_of(jnp.maximum(r0 - HALO_ROWS, 0), HALO_ROWS), HALO_ROWS), :]
                nxt = zh_ref[b, pl.ds(pl.multiple_of(jnp.minimum(r0 + HY_TL, seq_len - HALO_ROWS), HALO_ROWS),
                                      HALO_ROWS), :]
                _, u = _short_conv(zh_ref[b, pl.ds(r0, HY_TL), :], prev, nxt, c > 0, c < nc - 1, w, bias)
                rhs_ref[pl.ds(r0, HY_TL), b * HY_W:(b + 1) * HY_W] = u.astype(bf16)
                return carry
            lax.fori_loop(0, nc, chunk, 0)
        rhs_ref[:, nb * HY_W:(nb + 1) * HY_W] = hb_ref[0].astype(bf16)
        rhs_ref[:, (nb + 1) * HY_W:] = hb_ref[1].astype(bf16)

    c0 = nb * HY_W
    acc = jnp.dot(m_ref[...], rhs_ref[:, :c0], preferred_element_type=f32)
    p, w = acc[:tk], acc[tk:]
    hr = jnp.dot(m_ref[:tk, :], rhs_ref[:, c0:c0 + HY_W], preferred_element_type=f32)
    hi = jnp.dot(m_ref[tk:, :], rhs_ref[:, c0 + HY_W:], preferred_element_type=f32)
    qs = jnp.dot(m_ref[tk:tk + HALO_ROWS, :], rhs_ref[:, c0:c0 + HY_W], preferred_element_type=f32)[0:1]
    first = (lax.broadcasted_iota(jnp.int32, (tk, HY_W), 0) == 0) & (i == 0)
    for b in range(nb):
        cols = slice(b * HY_W, (b + 1) * HY_W)
        pu, wu = p[:, cols], w[:, cols]
        rz_ref[0, :, cols] = jnp.where(first, pu * hr, 2.0 * (pu * hr + wu * hi)).astype(bf16)
        rz_ref[1, :, cols] = jnp.where(first, wu * qs, 2.0 * (wu * hr - pu * hi)).astype(bf16)


def _hy_fwd(mat, zh, conv_w, conv_b, hb, batch, seq_len):
    tk, _ = _hy_tiles(seq_len)
    return pl.pallas_call(
        functools.partial(_hy_fwd_kernel, batch, tk, seq_len),
        grid=(seq_len // tk,),
        in_specs=[pl.BlockSpec((2 * tk, seq_len), lambda i: (i, 0)),
                  pl.BlockSpec((batch, seq_len, 3 * HY_W), lambda i: (0, 0, 0)),
                  pl.BlockSpec((3, 3 * HY_W), lambda i: (0, 0)),
                  pl.BlockSpec((1, 3 * HY_W), lambda i: (0, 0)),
                  pl.BlockSpec((2, seq_len, HY_W), lambda i: (0, 0, 0))],
        out_specs=pl.BlockSpec((2, tk, batch * HY_W), lambda i: (0, i, 0)),
        out_shape=jax.ShapeDtypeStruct((2, seq_len, batch * HY_W), bf16),
        scratch_shapes=[pltpu.VMEM((seq_len, (batch + 2) * HY_W), bf16)],
        compiler_params=_params("arbitrary"),
        name="hy_fwd",
    )(mat, zh, conv_w, conv_b.reshape(1, 3 * HY_W), hb)


def _hy_inv_kernel(nb, seq_len, nt, m_ref, rz_ref, zp_ref, zc_ref, zn_ref, cw_ref, cb_ref, d_ref, o_ref):
    i = pl.program_id(0)
    acc = jnp.dot(m_ref[...], rz_ref[...], preferred_element_type=f32)
    d, w, bias = d_ref[...], cw_ref[...], cb_ref[...]
    for b in range(nb):
        x0, u = _short_conv(zc_ref[b], zp_ref[b], zn_ref[b], i > 0, i < nt - 1, w, bias)
        y = acc[:, b * HY_W:(b + 1) * HY_W] * (1.0 / (2 * seq_len)) + u * d
        o_ref[b] = (x0 * y).astype(o_ref.dtype)


def _hy_inv(mat, rz, zh, conv_w, conv_b, d_bias, batch, seq_len):
    _, tm = _hy_tiles(seq_len)
    nt = seq_len // tm
    rh = tm // HALO_ROWS
    last = seq_len // HALO_ROWS - 1
    halo = lambda f: pl.BlockSpec((batch, HALO_ROWS, 3 * HY_W), f)
    return pl.pallas_call(
        functools.partial(_hy_inv_kernel, batch, seq_len, nt),
        grid=(nt,),
        in_specs=[pl.BlockSpec((tm, 2 * seq_len), lambda i: (i, 0)),
                  pl.BlockSpec((2 * seq_len, batch * HY_W), lambda i: (0, 0)),
                  halo(lambda i: (0, jnp.maximum(i * rh - 1, 0), 0)),
                  pl.BlockSpec((batch, tm, 3 * HY_W), lambda i: (0, i, 0)),
                  halo(lambda i: (0, jnp.minimum((i + 1) * rh, last), 0)),
                  pl.BlockSpec((3, 3 * HY_W), lambda i: (0, 0)),
                  pl.BlockSpec((1, 3 * HY_W), lambda i: (0, 0)),
                  pl.BlockSpec((1, HY_W), lambda i: (0, 0))],
        out_specs=pl.BlockSpec((batch, tm, HY_W), lambda i: (0, i, 0)),
        out_shape=jax.ShapeDtypeStruct((batch, seq_len, HY_W), bf16),
        compiler_params=_params("parallel"),
        name="hy_inv",
    )(mat, rz.reshape(2 * seq_len, batch * HY_W), zh, zh, zh, conv_w, conv_b.reshape(1, 3 * HY_W),
      d_bias.reshape(1, HY_W))


@functools.lru_cache(maxsize=None)
def _hy_consts_np(seq_len):
    n = 2 * seq_len
    k = np.arange(seq_len, dtype=np.int64)
    ang = (2.0 * np.pi / n) * ((k[:, None] * k[None, :]) % n).astype(np.float64)
    cosm = np.cos(ang)
    sinm = np.sin(ang)
    sinm[0, :] = np.where(k % 2 == 0, 1.0, -1.0)
    tk, _ = _hy_tiles(seq_len)
    fwd = np.concatenate([np.concatenate([cosm[i:i + tk], sinm[i:i + tk]], axis=0)
                          for i in range(0, seq_len, tk)], axis=0)
    inv = np.concatenate([cosm, sinm.T], axis=1)
    t = np.linspace(0.0, 1.0, seq_len, dtype=np.float32)[:, None]
    bands = (FILTER_EMB - 1) // 2
    fb = np.linspace(1e-4, bands - 1, bands, dtype=np.float32)[None, :]
    w = (np.float32(2.0 * math.pi) * np.arange(seq_len, dtype=np.float32)[:, None] / np.float32(seq_len)).astype(np.float32)
    feats = np.concatenate([t, np.cos(fb * w), -np.sin(fb * w)], axis=-1).astype(np.float32)
    feats = np.ascontiguousarray(np.pad(feats, ((0, 0), (0, LANES - FILTER_EMB))).T)
    decay_hi = math.log(HY_DECAY_TARGET) / HY_DECAY_HI_PCT
    decay_lo = math.log(HY_DECAY_TARGET) / HY_DECAY_LO_PCT
    deltas = np.abs(np.linspace(decay_lo, decay_hi, HY_W, dtype=np.float32))
    window = np.exp(-t * deltas[None, :]).astype(np.float32)
    return fwd, inv, feats, window


def _hy_consts(seq_len):
    fwd, inv, feats, window = _hy_consts_np(seq_len)
    as_bf = lambda m: jnp.asarray(m, dtype=f32).astype(bf16)
    return {"fwd": as_bf(fwd), "inv": as_bf(inv), "feats": jnp.asarray(feats), "window": jnp.asarray(window)}


def _hyena(zh, lp, consts, batch, seq_len):
    zh = zh.reshape(batch, seq_len, 3 * HY_W)
    hb = _hy_filter(seq_len, consts, lp["hy_f_w1"], lp["hy_f_b1"], lp["hy_f_w2"], lp["hy_f_b2"],
                    lp["hy_f_w3"], lp["hy_freq"]).reshape(2, seq_len, HY_W)
    rz = _hy_fwd(consts["fwd"], zh, lp["hy_conv_w"], lp["hy_conv_b"], hb, batch, seq_len)
    yh = _hy_inv(consts["inv"], rz, zh, lp["hy_conv_w"], lp["hy_conv_b"], lp["hy_bias"], batch, seq_len)
    return yh.reshape(batch * seq_len, HY_W)


def _gelu(x):
    return 0.5 * x * (1.0 + jnp.tanh(math.sqrt(2.0 / math.pi) * (x + 0.044715 * (x * x * x))))


def _gmlp(z, ws_ref, bias):
    g = _gelu(z)
    outs = []
    for c in range(TM // GM_CHUNK):
        rows = slice(c * GM_CHUNK, (c + 1) * GM_CHUNK)
        u = g[rows, :GM_W]
        v = g[rows, GM_W:].astype(bf16)
        s = jnp.concatenate(
            [jnp.dot(ws_ref[h], v[:, h * GM_HEAD_DIM:(h + 1) * GM_HEAD_DIM], preferred_element_type=f32)
             for h in range(GM_HEADS)], axis=1)
        outs.append(u * (s + bias))
    return jnp.concatenate(outs, axis=0)


def _out_kernel(a_ref, yh_ref, zg_ref, ws_ref, gb_ref, x_ref, mod_ref, og_ref, w_ref, g2_ref, rw_ref,
                x1_ref, h2_ref, aff_ref):
    og = og_ref[...]
    yg = _gmlp(zg_ref[...].astype(f32), ws_ref, gb_ref[...])
    mixed_in = jnp.concatenate([
        _rms(a_ref[...].astype(f32)) * og[:, :ATTN_W],
        _rms(yh_ref[...].astype(f32)) * og[:, ATTN_W:ATTN_W + HY_W],
        _rms(yg) * og[:, ATTN_W + HY_W:]], axis=1)
    mixed = jnp.dot(mixed_in.astype(bf16), w_ref[...], preferred_element_type=f32)
    m = mod_ref[0]
    x1 = x_ref[...] + m[2:3] * mixed
    x1_ref[...] = x1
    h2 = _rms(x1) * g2_ref[...] * (1.0 + m[4:5]) + m[3:4]
    for j in range(ROW_CHUNKS):
        h2_ref[pl.ds(j, TM, stride=ROW_CHUNKS), :] = h2[:, j * LANES:(j + 1) * LANES]
    nt = (((1,), (1,)), ((), ()))
    h_hi = h2.astype(bf16)
    h_lo = (h2 - h_hi.astype(f32)).astype(bf16)
    rw = rw_ref[...]
    t = lax.dot_general(rw, h_hi, nt, preferred_element_type=f32)
    logits = (t[:N_EXPERTS] + t[N_EXPERTS:]) + lax.dot_general(rw[:N_EXPERTS], h_lo, nt, preferred_element_type=f32)
    e = jnp.exp(logits - jnp.max(logits, axis=0, keepdims=True))
    aff_ref[...] = e / jnp.sum(e, axis=0, keepdims=True)


def _out_proj(a, yh, zg, ws_bf, gm_b, x, mod, out_g, w_out_bf, g2, router_wt):
    n = x.shape[0]
    per_mod = n // mod.shape[0]
    row = lambda w: pl.BlockSpec((TM, w), lambda i: (i, 0))
    const = lambda s: pl.BlockSpec(s, lambda i: (0,) * len(s))
    gm_bias = jnp.repeat(gm_b.T, GM_HEAD_DIM, axis=1)
    return pl.pallas_call(
        _out_kernel,
        grid=(n // TM,),
        in_specs=[row(ATTN_W), row(HY_W), row(2 * GM_W), const((GM_HEADS, GM_CHUNK, GM_CHUNK)),
                  const((GM_CHUNK, GM_W)), row(D_MODEL),
                  pl.BlockSpec((1, 6, D_MODEL), lambda i: (i * TM // per_mod, 0, 0)),
                  const((1, MIX_W)), const((MIX_W, D_MODEL)), const((1, D_MODEL)), const((2 * N_EXPERTS, D_MODEL))],
        out_specs=[row(D_MODEL), pl.BlockSpec((TM * ROW_CHUNKS, LANES), lambda i: (i, 0)),
                   pl.BlockSpec((N_EXPERTS, TM), lambda i: (0, i))],
        out_shape=[jax.ShapeDtypeStruct((n, D_MODEL), f32), jax.ShapeDtypeStruct((n * ROW_CHUNKS, LANES), f32),
                   jax.ShapeDtypeStruct((N_EXPERTS, n), f32)],
        compiler_params=_params("parallel"),
        name="out_proj",
    )(a, yh, zg, ws_bf, gm_bias, x, mod, out_g.reshape(1, MIX_W), w_out_bf, g2.reshape(1, D_MODEL), router_wt)


RT_CHUNK = 512


def _prefix_incl(x01, tri):
    n = x01.shape[1]
    carry = jnp.zeros((x01.shape[0], 1), f32)
    parts = []
    for c in range(n // RT_CHUNK):
        piece = x01[:, c * RT_CHUNK:(c + 1) * RT_CHUNK]
        parts.append(jnp.dot(piece.astype(bf16), tri, preferred_element_type=f32) + carry)
        carry = carry + jnp.sum(piece, axis=1, keepdims=True)
    return jnp.concatenate(parts, axis=1)


def _route_kernel(cap, aff_ref, idx_ref, gate_ref):
    aff = aff_ref[...]
    n = aff.shape[1]

    def step(it, t):
        cand = t | (jnp.int32(1) << (30 - it))
        cnt = jnp.sum(jnp.where(aff >= lax.bitcast_convert_type(cand, f32), 1.0, 0.0), axis=1, keepdims=True)
        return jnp.where(cnt >= cap, cand, t)

    thr_bits = lax.fori_loop(0, 31, step, jnp.zeros((aff.shape[0], 1), jnp.int32))
    thr = lax.bitcast_convert_type(thr_bits, f32)
    gt = jnp.where(aff > thr, 1.0, 0.0)
    eq = jnp.where(aff == thr, 1.0, 0.0)
    room = cap - jnp.sum(gt, axis=1, keepdims=True)
    r = lax.broadcasted_iota(jnp.int32, (RT_CHUNK, RT_CHUNK), 0)
    c = lax.broadcasted_iota(jnp.int32, (RT_CHUNK, RT_CHUNK), 1)
    tri = jnp.where(r <= c, 1.0, 0.0).astype(bf16)
    sel = jnp.maximum(gt, jnp.where(_prefix_incl(eq, tri) <= room, eq, 0.0))
    slot = _prefix_incl(sel, tri) - 1.0

    tok = lax.broadcasted_iota(jnp.int32, aff.shape, 1)
    dist = jnp.where(sel > 0.0, tok - slot.astype(jnp.int32), 0)
    g = aff
    for b in range(max(1, (n - 1).bit_length())):
        sh = 1 << b
        dist_s = pltpu.roll(dist, n - sh, axis=1)
        take = (dist_s & sh) != 0
        leave = (dist & sh) != 0
        tok = jnp.where(take, pltpu.roll(tok, n - sh, axis=1), tok)
        g = jnp.where(take, pltpu.roll(g, n - sh, axis=1), g)
        dist = jnp.where(take, dist_s, jnp.where(leave, 0, dist))
    idx_ref[...] = tok[:, :cap]
    gate_ref[...] = g[:, :cap]


def _route(aff_t, cap):
    return pl.pallas_call(
        functools.partial(_route_kernel, cap),
        in_specs=[pl.BlockSpec(memory_space=pltpu.VMEM)],
        out_specs=[pl.BlockSpec(memory_space=pltpu.VMEM)] * 2,
        out_shape=[jax.ShapeDtypeStruct((N_EXPERTS, cap), jnp.int32), jax.ShapeDtypeStruct((N_EXPERTS, cap), f32)],
        compiler_params=pltpu.CompilerParams(vmem_limit_bytes=VMEM_LIMIT),
        name="route",
    )(aff_t)


GATHER_UNROLL = 8
SCATTER_UNROLL = 8
CMB_EXPERTS = 2
CMB_TM = 1024


def _moe_ffn_kernel(cap, stride, idx_ref, h_ref, w1_ref, w3_ref, w2_ref, y_ref, tile_ref, xb_ref, acc_ref):
    e = pl.program_id(0)
    slab = ROW_CHUNKS * stride
    cur = pl.multiple_of((e % 2) * slab, SUBLANES)
    nxt = pl.multiple_of(((e + 1) % 2) * slab, SUBLANES)

    def gather_row(expert, slab_row, r):
        src = pl.multiple_of(idx_ref[expert * cap + r] * ROW_CHUNKS, ROW_CHUNKS)
        tile_ref[pl.ds(slab_row + r, ROW_CHUNKS, stride=stride), :] = h_ref[pl.ds(src, ROW_CHUNKS), :]

    @pl.when(e == 0)
    def _():
        def gather(q, carry):
            for t in range(GATHER_UNROLL):
                gather_row(0, cur, q * GATHER_UNROLL + t)
            return carry
        lax.fori_loop(0, cap // GATHER_UNROLL, gather, 0)

    xb_ref[...] = jnp.concatenate(
        [tile_ref[pl.ds(cur + j * stride, cap), :] for j in range(ROW_CHUNKS)], axis=1).astype(bf16)

    e_next = jnp.minimum(e + 1, N_EXPERTS - 1)
    n_f = D_EXPERT // MOE_F_TILE
    xb = xb_ref[...]
    for f in range(n_f):
        cols = slice(f * MOE_F_TILE, (f + 1) * MOE_F_TILE)
        a = jnp.dot(xb, w1_ref[0, 0, :, cols].astype(bf16), preferred_element_type=f32)
        b = jnp.dot(xb, w3_ref[0, 0, :, cols].astype(bf16), preferred_element_type=f32)
        he = (a * _sigmoid(a) * b).astype(bf16)
        part = jnp.dot(he, w2_ref[0, 0, cols, :].astype(bf16), preferred_element_type=f32)
        if f == 0:
            acc_ref[...] = part
        else:
            acc_ref[...] += part
        for r in range(f * cap // n_f, (f + 1) * cap // n_f):
            gather_row(e_next, nxt, r)

    y = acc_ref[...]
    for j in range(ROW_CHUNKS):
        y_ref[0, j * stride:j * stride + cap, :] = y[:, j * LANES:(j + 1) * LANES]
        y_ref[0, j * stride + cap:(j + 1) * stride, :] = jnp.zeros((stride - cap, LANES), f32)


def _moe_ffn(layer, h3, idx, w1, w3, w2):
    cap = idx.shape[1]
    stride = cap + SUBLANES
    grid_spec = pltpu.PrefetchScalarGridSpec(
        num_scalar_prefetch=1,
        grid=(N_EXPERTS,),
        in_specs=[
            pl.BlockSpec(memory_space=pltpu.VMEM),
            pl.BlockSpec((1, 1, D_MODEL, D_EXPERT), lambda e, idx: (layer, e, 0, 0)),
            pl.BlockSpec((1, 1, D_MODEL, D_EXPERT), lambda e, idx: (layer, e, 0, 0)),
            pl.BlockSpec((1, 1, D_EXPERT, D_MODEL), lambda e, idx: (layer, e, 0, 0)),
        ],
        out_specs=pl.BlockSpec((1, ROW_CHUNKS * stride, LANES), lambda e, idx: (e, 0, 0)),
        scratch_shapes=[pltpu.VMEM((2 * ROW_CHUNKS * stride, LANES), f32),
                        pltpu.VMEM((cap, D_MODEL), bf16),
                        pltpu.VMEM((cap, D_MODEL), f32)],
    )
    return pl.pallas_call(
        functools.partial(_moe_ffn_kernel, cap, stride),
        grid_spec=grid_spec,
        out_shape=jax.ShapeDtypeStruct((N_EXPERTS, ROW_CHUNKS * stride, LANES), f32),
        compiler_params=_params("arbitrary"),
        name="moe_ffn",
    )(idx.reshape(-1), h3, w1, w3, w2)


def _moe_combine_kernel(cap, stride, final, idx_ref, gate_ref, y_ref, x_ref, mod_ref, g_ref, o_ref, acc_ref):
    s = pl.program_id(0)
    scatter_steps = N_EXPERTS // CMB_EXPERTS

    @pl.when(s == 0)
    def _():
        acc_ref[...] = jnp.zeros_like(acc_ref)

    @pl.when(s < scatter_steps)
    def _():
        for k in range(CMB_EXPERTS):
            base = (s * CMB_EXPERTS + k) * cap

            def scatter(q, carry, k=k, base=base):
                rows = [pl.multiple_of(idx_ref[base + q * SCATTER_UNROLL + t] * ROW_CHUNKS, ROW_CHUNKS)
                        for t in range(SCATTER_UNROLL)]
                new = [acc_ref[pl.ds(rows[t], ROW_CHUNKS), :]
                       + gate_ref[base + q * SCATTER_UNROLL + t]
                       * y_ref[k, pl.ds(q * SCATTER_UNROLL + t, ROW_CHUNKS, stride=stride), :]
                       for t in range(SCATTER_UNROLL)]
                for t in range(SCATTER_UNROLL):
                    acc_ref[pl.ds(rows[t], ROW_CHUNKS), :] = new[t]
                return carry
            lax.fori_loop(0, cap // SCATTER_UNROLL, scatter, 0)

    @pl.when(s >= scatter_steps)
    def _():
        first = pl.multiple_of((s - scatter_steps) * (CMB_TM * ROW_CHUNKS), CMB_TM * ROW_CHUNKS)
        moe = jnp.concatenate([acc_ref[pl.ds(first + j, CMB_TM, stride=ROW_CHUNKS), :] for j in range(ROW_CHUNKS)],
                              axis=1)
        x = x_ref[...] + mod_ref[0][5:6] * moe
        o_ref[...] = _rms(x) * g_ref[...] if final else x


def _moe_combine(idx, gate, y_cm, x1, mod, final_g, final):
    n = x1.shape[0]
    cap = idx.shape[1]
    stride = y_cm.shape[1] // ROW_CHUNKS
    per_mod = n // mod.shape[0]
    scatter_steps = N_EXPERTS // CMB_EXPERTS
    tile = lambda s: jnp.maximum(s - scatter_steps, 0)
    row = pl.BlockSpec((CMB_TM, D_MODEL), lambda s, *_: (tile(s), 0))
    grid_spec = pltpu.PrefetchScalarGridSpec(
        num_scalar_prefetch=2,
        grid=(scatter_steps + n // CMB_TM,),
        in_specs=[pl.BlockSpec((CMB_EXPERTS, ROW_CHUNKS * stride, LANES),
                               lambda s, *_: (jnp.minimum(s, scatter_steps - 1), 0, 0)),
                  row,
                  pl.BlockSpec((1, 6, D_MODEL), lambda s, *_: (tile(s) * CMB_TM // per_mod, 0, 0)),
                  pl.BlockSpec((1, D_MODEL), lambda s, *_: (0, 0))],
        out_specs=row,
        scratch_shapes=[pltpu.VMEM((n * ROW_CHUNKS, LANES), f32)],
    )
    return pl.pallas_call(
        functools.partial(_moe_combine_kernel, cap, stride, final),
        grid_spec=grid_spec,
        out_shape=jax.ShapeDtypeStruct((n, D_MODEL), f32),
        compiler_params=_params("arbitrary"),
        name="moe_combine",
    )(idx.reshape(-1), gate.reshape(-1), y_cm, x1, mod, final_g.reshape(1, D_MODEL))


def _expert_choice(layer, h3, aff_t, x1, mod, final_g, final, w1, w3, w2):
    n = x1.shape[0]
    cap = max(1, EC_CAPACITY * n // N_EXPERTS)
    idx, gate = _route(aff_t, cap)
    y_cm = _moe_ffn(layer, h3, idx, w1, w3, w2)
    return _moe_combine(idx, gate, y_cm, x1, mod, final_g, final)


def _split_bf16(w):
    hi = w.astype(bf16)
    lo = (w - hi.astype(f32)).astype(bf16)
    return jnp.concatenate([hi, lo], axis=0)


def _stream(x, mods, layers, experts, batch, seq_len, final_g, caches=None):
    consts = _hy_consts(seq_len)
    rope_tabs = _rope_tables(seq_len) if caches is not None else None
    kvs = []
    for l, lp in enumerate(layers):
        qkv, zh, zg = _in_proj(x, mods[l], lp["norm1_g"], lp["w_in"], seq_len, rope_tabs=rope_tabs)
        if caches is None:
            a = _ctx_attention(qkv, lp["attn_sink"], batch, seq_len)
            kvs.append((qkv[:, ATTN_W:ATTN_W + KV_W], qkv[:, ATTN_W + KV_W:]))
        else:
            a = _lat_attention(qkv, lp["attn_sink"], caches[0][:, l], caches[1][:, l], batch, seq_len)
        yh = _hyena(zh, lp, consts, batch, seq_len)
        x1, h3, aff_t = _out_proj(a, yh, zg, lp["gm_ws"], lp["gm_b"], x, mods[l], lp["out_norm_g"], lp["w_out"],
                                  lp["norm2_g"], lp["router_wt"])
        x = _expert_choice(l, h3, aff_t, x1, mods[l], final_g, l == len(layers) - 1, *experts)
    return x, kvs


def kernel(x_prompt, x_sample, c, cache_k, cache_v, c_ctx, norm1_g, norm2_g, ada_w, ada_b, w_in, attn_sink,
           hy_conv_w, hy_conv_b, hy_f_w1, hy_f_b1, hy_f_w2, hy_f_b2, hy_f_w3, hy_freq, hy_bias, gm_ws, gm_b,
           out_norm_g, w_out, router_w, exp_w1, exp_w3, exp_w2, final_g):
    batch, seq, _ = x_prompt.shape
    dbatch, dseq, _ = x_sample.shape
    past = cache_k.shape[2]

    cvec = jnp.concatenate([c_ctx[None], c, jnp.zeros((SUBLANES - 1 - dbatch, D_MODEL), f32)], axis=0)
    mod = _ada(cvec, ada_w, ada_b)
    mods_ctx = [mod[l, 0:1].reshape(1, 6, D_MODEL) for l in range(DEPTH)]
    mods_lat = [mod[l, 1:1 + dbatch].reshape(dbatch, 6, D_MODEL) for l in range(DEPTH)]

    w_in_bf, w_out_bf, gm_ws_bf = w_in.astype(bf16), w_out.astype(bf16), gm_ws.astype(bf16)
    layers = []
    for l in range(DEPTH):
        layers.append({
            "norm1_g": norm1_g[l], "norm2_g": norm2_g[l], "w_in": w_in_bf[l], "attn_sink": attn_sink[l],
            "hy_conv_w": hy_conv_w[l], "hy_conv_b": hy_conv_b[l], "hy_f_w1": hy_f_w1[l], "hy_f_b1": hy_f_b1[l],
            "hy_f_w2": hy_f_w2[l], "hy_f_b2": hy_f_b2[l], "hy_f_w3": hy_f_w3[l], "hy_freq": hy_freq[l],
            "hy_bias": hy_bias[l], "gm_ws": gm_ws_bf[l], "gm_b": gm_b[l], "out_norm_g": out_norm_g[l],
            "w_out": w_out_bf[l], "router_wt": _split_bf16(router_w[l].T)})
    experts = (exp_w1, exp_w3, exp_w2)

    yp, kvs = _stream(x_prompt.reshape(batch * seq, D_MODEL), mods_ctx, layers, experts, batch, seq, final_g)
    caches = (cache_k.reshape(dbatch, DEPTH, past, KV_W), cache_v.reshape(dbatch, DEPTH, past, KV_W))
    ys, _ = _stream(x_sample.reshape(dbatch * dseq, D_MODEL), mods_lat, layers, experts, dbatch, dseq, final_g,
                   caches=caches)

    new_k = jnp.stack([k.reshape(batch, seq, N_KV, HEAD_DIM) for k, _ in kvs], axis=1)
    new_v = jnp.stack([v.reshape(batch, seq, N_KV, HEAD_DIM) for _, v in kvs], axis=1)
    return (yp.reshape(batch, seq, D_MODEL), ys.reshape(dbatch, dseq, D_MODEL), new_k, new_v)
```

```python
import functools
import math

import numpy as np
import jax
import jax.numpy as jnp
from jax import lax
from jax.experimental import pallas as pl
from jax.experimental.pallas import tpu as pltpu

f32 = jnp.float32
bf16 = jnp.bfloat16

D_MODEL = 1024
DEPTH = 2
GRID_W = 64
BLOCK = 128
N_HEADS = 8
N_KV = 2
HEAD_DIM = 64
Q_GROUP = N_HEADS // N_KV
ATTN_W = N_HEADS * HEAD_DIM
KV_W = N_KV * HEAD_DIM
QKV_W = ATTN_W + 2 * KV_W
HY_W = 256
GM_W = 256
GM_HEADS = 4
GM_HEAD_DIM = GM_W // GM_HEADS
GM_CHUNK = 128
MIX_W = ATTN_W + HY_W + GM_W
IN_W = ATTN_W + 2 * KV_W + 3 * HY_W + 2 * GM_W
FILTER_EMB = 33
FILTER_HID = 64
HY_DECAY_HI_PCT = 0.3
HY_DECAY_LO_PCT = 1.5
HY_DECAY_TARGET = 1e-2
N_EXPERTS = 16
EC_CAPACITY = 2
D_EXPERT = 1024
ROPE_THETA = 10000.0
EPS = 1e-6
NEG = -1e30

LANES = 128
SUBLANES = 8
ROW_CHUNKS = D_MODEL // LANES
VMEM_LIMIT = 56 * 1024 * 1024
TM = 512
IN_TM = 512
MOE_F_TILE = 256


def _params(*sem):
    return pltpu.CompilerParams(dimension_semantics=sem, vmem_limit_bytes=VMEM_LIMIT)


def _rms(x):
    return x * lax.rsqrt(jnp.mean(x * x, axis=-1, keepdims=True) + EPS)


def _sigmoid(x):
    return 1.0 / (1.0 + jnp.exp(-x))


def _ada_kernel(c_ref, w_ref, b_ref, o_ref):
    c = c_ref[...]
    s = (c * _sigmoid(c)).astype(bf16)
    o_ref[0] = jnp.dot(s, w_ref[0].astype(bf16), preferred_element_type=f32) + b_ref[0]


def _ada(cvec, ada_w, ada_b):
    nt = 6
    return pl.pallas_call(
        _ada_kernel,
        grid=(DEPTH, nt),
        in_specs=[
            pl.BlockSpec((SUBLANES, D_MODEL), lambda l, j: (0, 0)),
            pl.BlockSpec((1, D_MODEL, D_MODEL), lambda l, j: (l, 0, j)),
            pl.BlockSpec((1, 1, D_MODEL), lambda l, j: (l, 0, j)),
        ],
        out_specs=pl.BlockSpec((1, SUBLANES, D_MODEL), lambda l, j: (l, 0, j)),
        out_shape=jax.ShapeDtypeStruct((DEPTH, SUBLANES, 6 * D_MODEL), f32),
        compiler_params=_params("arbitrary", "arbitrary"),
        name="ada",
    )(cvec, ada_w, ada_b.reshape(DEPTH, 1, 6 * D_MODEL))


def _rope_swap(x):
    w = x.shape[-1]
    lane = lax.broadcasted_iota(jnp.int32, x.shape, 1)
    first = (lane % 32) < 16
    return jnp.where(first, pltpu.roll(x, w - 16, axis=1), pltpu.roll(x, 16, axis=1))


def _in_kernel(rope, x_ref, mod_ref, g_ref, w_ref, *refs):
    if rope:
        cos_ref, sin_ref, qkv_ref, zh_ref, zg_ref = refs
    else:
        qkv_ref, zh_ref, zg_ref = refs
    m = mod_ref[0]
    h = _rms(x_ref[...]) * g_ref[...] * (1.0 + m[1:2]) + m[0:1]
    z = jnp.dot(h.astype(bf16), w_ref[...], preferred_element_type=f32)
    if rope:
        reps = (ATTN_W + KV_W) // LANES
        cos = jnp.concatenate([cos_ref[...]] * reps, axis=1)
        sin = jnp.concatenate([sin_ref[...]] * reps, axis=1)
        qk = z[:, :ATTN_W + KV_W]
        qkv_ref[:, :ATTN_W + KV_W] = qk * cos + _rope_swap(qk) * sin
        qkv_ref[:, ATTN_W + KV_W:] = z[:, ATTN_W + KV_W:QKV_W]
    else:
        qkv_ref[...] = z[:, :QKV_W]
    zh_ref[...] = z[:, QKV_W:QKV_W + 3 * HY_W].astype(bf16)
    zg_ref[...] = z[:, QKV_W + 3 * HY_W:].astype(bf16)


def _in_proj(x, mod, g1, w_in_bf, seq_len, rope_tabs=None):
    n = x.shape[0]
    rope = rope_tabs is not None
    per_mod = n // mod.shape[0]
    tm = IN_TM
    in_specs = [pl.BlockSpec((tm, D_MODEL), lambda i: (i, 0)),
                pl.BlockSpec((1, 6, D_MODEL), lambda i: (i * tm // per_mod, 0, 0)),
                pl.BlockSpec((1, D_MODEL), lambda i: (0, 0)),
                pl.BlockSpec((D_MODEL, IN_W), lambda i: (0, 0))]
    args = [x, mod, g1.reshape(1, D_MODEL), w_in_bf]
    if rope:
        nt = seq_len // tm
        tab = pl.BlockSpec((tm, LANES), lambda i: (i % nt, 0))
        in_specs += [tab, tab]
        args += list(rope_tabs)
    return pl.pallas_call(
        functools.partial(_in_kernel, rope),
        grid=(n // tm,),
        in_specs=in_specs,
        out_specs=[pl.BlockSpec((tm, w), lambda i: (i, 0)) for w in (QKV_W, 3 * HY_W, 2 * GM_W)],
        out_shape=[jax.ShapeDtypeStruct((n, QKV_W), f32), jax.ShapeDtypeStruct((n, 3 * HY_W), bf16),
                   jax.ShapeDtypeStruct((n, 2 * GM_W), bf16)],
        compiler_params=_params("parallel"),
        name="in_proj",
    )(*args)


def _rope_tables(seq_len):
    nf = HEAD_DIM // 4
    t = np.arange(seq_len)
    inv = (ROPE_THETA ** (-np.arange(nf, dtype=np.float32) / nf)).astype(np.float32)
    d = np.arange(HEAD_DIM)
    pos = np.where((d // 32)[None, :] == 0, (t // GRID_W)[:, None], (t % GRID_W)[:, None]).astype(np.float32)
    ang = (pos * inv[d % nf][None, :]).astype(np.float32)
    cos = np.cos(ang).astype(np.float32)
    sin = np.sin(ang).astype(np.float32) * np.where((d % 32) < 16, -1.0, 1.0)[None, :].astype(np.float32)
    return jnp.asarray(np.tile(cos, (1, 2))), jnp.asarray(np.tile(sin, (1, 2)))


LOG2E = math.log2(math.e)


def _stack_queries(z, sink_ref, kv):
    rows = z.shape[0]
    heads = range(kv * Q_GROUP, (kv + 1) * Q_GROUP)
    qs = jnp.concatenate([z[:, h * HEAD_DIM:(h + 1) * HEAD_DIM] for h in heads], axis=0)
    sink = jnp.concatenate([jnp.full((1, rows), sink_ref[h] * LOG2E, f32) for h in heads], axis=1)
    return (qs * (LOG2E / math.sqrt(HEAD_DIM))).astype(bf16), sink


def _scores_t(k, qs_bf):
    return lax.dot_general(k.astype(bf16), qs_bf, (((1,), (1,)), ((), ())), preferred_element_type=f32)


def _softmax_pv_t(st, sink, vals, rows):
    m = jnp.maximum(jnp.max(st, axis=0, keepdims=True), sink)
    pt = jnp.exp2(st - m).astype(bf16)
    v_bf = vals.astype(bf16)
    v_ext = jnp.concatenate([v_bf, jnp.ones_like(v_bf)], axis=1)
    ovt = lax.dot_general(v_ext, pt, (((0,), (0,)), ((), ())), preferred_element_type=f32)
    ot = ovt[:HEAD_DIM] / (ovt[HEAD_DIM:HEAD_DIM + 1] + jnp.exp2(sink - m))
    return [ot[:, g * rows:(g + 1) * rows].T for g in range(Q_GROUP)]


CTX_BATCHES = 2
LAT_QBLOCKS = 4


def _ctx_attn_kernel(seq_len, sink_ref, z_ref, o_ref):
    for s in range(CTX_BATCHES):
        z = z_ref[s * seq_len:(s + 1) * seq_len, :]
        outs = []
        for kv in range(N_KV):
            k = z[:, ATTN_W + kv * HEAD_DIM:ATTN_W + (kv + 1) * HEAD_DIM]
            v = z[:, ATTN_W + KV_W + kv * HEAD_DIM:ATTN_W + KV_W + (kv + 1) * HEAD_DIM]
            qs, sink = _stack_queries(z, sink_ref, kv)
            outs += _softmax_pv_t(_scores_t(k, qs), sink, v, seq_len)
        o_ref[s * seq_len:(s + 1) * seq_len, :] = jnp.concatenate(outs, axis=1).astype(o_ref.dtype)


def _ctx_attention(z, sink, batch, seq_len):
    rows = CTX_BATCHES * seq_len
    return pl.pallas_call(
        functools.partial(_ctx_attn_kernel, seq_len),
        grid=(batch // CTX_BATCHES,),
        in_specs=[pl.BlockSpec(memory_space=pltpu.SMEM),
                  pl.BlockSpec((rows, QKV_W), lambda b: (b, 0))],
        out_specs=pl.BlockSpec((rows, ATTN_W), lambda b: (b, 0)),
        out_shape=jax.ShapeDtypeStruct((batch * seq_len, ATTN_W), bf16),
        compiler_params=_params("parallel"),
        name="ctx_attn",
    )(sink, z)


def _lat_attn_kernel(nb, sink_ref, zp_ref, zc_ref, zn_ref, ck_ref, cv_ref, o_ref):
    step = pl.program_id(1)
    zc_all = zc_ref[...]
    blocks = [zp_ref[...]] + [zc_all[q * BLOCK:(q + 1) * BLOCK] for q in range(LAT_QBLOCKS)] + [zn_ref[...]]
    ck, cv = ck_ref[0], cv_ref[0]
    width = Q_GROUP * BLOCK
    j = lax.broadcasted_iota(jnp.int32, (BLOCK, width), 0)
    r = lax.broadcasted_iota(jnp.int32, (BLOCK, width), 1) % BLOCK
    for q in range(LAT_QBLOCKS):
        i = step * LAT_QBLOCKS + q
        zp, zc, zn = blocks[q], blocks[q + 1], blocks[q + 2]
        ok_prev = j >= r + jnp.where(i > 0, 0, BLOCK)
        ok_next = j <= r - jnp.where(i < nb - 1, 0, BLOCK)
        outs = []
        for kv in range(N_KV):
            ks = slice(ATTN_W + kv * HEAD_DIM, ATTN_W + (kv + 1) * HEAD_DIM)
            vs = slice(ATTN_W + KV_W + kv * HEAD_DIM, ATTN_W + KV_W + (kv + 1) * HEAD_DIM)
            cs = slice(kv * HEAD_DIM, (kv + 1) * HEAD_DIM)
            qs, sink = _stack_queries(zc, sink_ref, kv)
            st = jnp.concatenate([
                jnp.where(ok_prev, _scores_t(zp[:, ks], qs), NEG),
                _scores_t(zc[:, ks], qs),
                jnp.where(ok_next, _scores_t(zn[:, ks], qs), NEG),
                _scores_t(ck[:, cs], qs)], axis=0)
            vals = jnp.concatenate([zp[:, vs], zc[:, vs], zn[:, vs], cv[:, cs]], axis=0)
            outs += _softmax_pv_t(st, sink, vals, BLOCK)
        o_ref[q * BLOCK:(q + 1) * BLOCK, :] = jnp.concatenate(outs, axis=1).astype(o_ref.dtype)


def _lat_attention(z, sink, ck, cv, batch, seq_len):
    nb = seq_len // BLOCK
    ns = nb // LAT_QBLOCKS
    blk = lambda f: pl.BlockSpec((BLOCK, QKV_W), f)
    past = ck.shape[1]
    cache = pl.BlockSpec((1, past, KV_W), lambda b, i: (b, 0, 0))
    return pl.pallas_call(
        functools.partial(_lat_attn_kernel, nb),
        grid=(batch, ns),
        in_specs=[pl.BlockSpec(memory_space=pltpu.SMEM),
                  blk(lambda b, i: (b * nb + jnp.maximum(i * LAT_QBLOCKS - 1, 0), 0)),
                  pl.BlockSpec((LAT_QBLOCKS * BLOCK, QKV_W), lambda b, i: (b * ns + i, 0)),
                  blk(lambda b, i: (b * nb + jnp.minimum((i + 1) * LAT_QBLOCKS, nb - 1), 0)),
                  cache, cache],
        out_specs=pl.BlockSpec((LAT_QBLOCKS * BLOCK, ATTN_W), lambda b, i: (b * ns + i, 0)),
        out_shape=jax.ShapeDtypeStruct((batch * seq_len, ATTN_W), bf16),
        compiler_params=_params("parallel", "parallel"),
        name="lat_attn",
    )(sink, z, z, z, ck, cv)


HY_TL = 256
HALO_ROWS = 2 * SUBLANES


def _short_conv(z, prev_tile, next_tile, has_prev, has_next, w, b):
    z = z.astype(f32)
    rows = z.shape[0]
    row = lax.broadcasted_iota(jnp.int32, z.shape, 0)
    prev_row = jnp.where(has_prev, prev_tile.astype(f32)[HALO_ROWS - 1:HALO_ROWS, :], 0.0)
    next_row = jnp.where(has_next, next_tile.astype(f32)[0:1, :], 0.0)
    z_prev = jnp.where(row == 0, prev_row, pltpu.roll(z, 1, axis=0))
    z_next = jnp.where(row == rows - 1, next_row, pltpu.roll(z, rows - 1, axis=0))
    zc = z_prev * w[0:1] + z * w[1:2] + z_next * w[2:3] + b
    return zc[:, :HY_W], zc[:, HY_W:2 * HY_W] * zc[:, 2 * HY_W:]


def _hy_filter_kernel(seq_len, feats_ref, win_ref, w1_ref, b1_ref, w2_ref, b2_ref, w3_ref, fr_ref, o_ref):
    hi = lax.Precision.HIGHEST
    fr = fr_ref[...]
    h = jnp.sin(fr * (jnp.dot(w1_ref[...], feats_ref[...], precision=hi, preferred_element_type=f32) + b1_ref[...]))
    h = jnp.sin(fr * (jnp.dot(w2_ref[...], h, precision=hi, preferred_element_type=f32) + b2_ref[...]))
    h = jnp.dot(h.T, w3_ref[...], precision=hi, preferred_element_type=f32)
    win = win_ref[...]
    row = lax.broadcasted_iota(jnp.int32, (seq_len, HY_W), 0)
    hf = h[:, :HY_W] * win
    hb = jnp.where(row == 0, 0.0, h[:, HY_W:] * win)
    o_ref[0:seq_len, :] = hf + hb
    o_ref[seq_len:, :] = hb - hf


def _hy_filter(seq_len, consts, w1, b1, w2, b2, w3, freq):
    w1t = jnp.zeros((FILTER_HID, LANES), f32).at[:, :FILTER_EMB].set(w1.T)
    args = (consts["feats"], consts["window"], w1t, b1.reshape(-1, 1), w2.T, b2.reshape(-1, 1), w3, freq.reshape(-1, 1))
    return pl.pallas_call(
        functools.partial(_hy_filter_kernel, seq_len),
        in_specs=[pl.BlockSpec(memory_space=pltpu.VMEM)] * len(args),
        out_specs=pl.BlockSpec(memory_space=pltpu.VMEM),
        out_shape=jax.ShapeDtypeStruct((2 * seq_len, HY_W), f32),
        compiler_params=pltpu.CompilerParams(vmem_limit_bytes=VMEM_LIMIT),
        name="hy_filter",
    )(*args)


def _hy_tiles(seq_len):
    return min(TM, seq_len // 2), min(TM, seq_len)


def _hy_fwd_kernel(nb, tk, seq_len, m_ref, zh_ref, cw_ref, cb_ref, hb_ref, rz_ref, rhs_ref):
    i = pl.program_id(0)
    nc = seq_len // HY_TL

    @pl.when(i == 0)
    def _():
        w, bias = cw_ref[...], cb_ref[...]
        for b in range(nb):
            def chunk(c, carry, b=b):
                r0 = pl.multiple_of(c * HY_TL, HY_TL)
                prev = zh_ref[b, pl.ds(pl.multiple_of(jnp.maximum(r0 - HALO_ROWS, 0), HALO_ROWS), HALO_ROWS), :]
                nxt = zh_ref[b, pl.ds(pl.multiple_of(jnp.minimum(r0 + HY_TL, seq_len - HALO_ROWS), HALO_ROWS),
                                      HALO_ROWS), :]
                _, u = _short_conv(zh_ref[b, pl.ds(r0, HY_TL), :], prev, nxt, c > 0, c < nc - 1, w, bias)
                rhs_ref[pl.ds(r0, HY_TL), b * HY_W:(b + 1) * HY_W] = u.astype(bf16)
                return carry
            lax.fori_loop(0, nc, chunk, 0)
        rhs_ref[:, nb * HY_W:(nb + 1) * HY_W] = hb_ref[0].astype(bf16)
        rhs_ref[:, (nb + 1) * HY_W:] = hb_ref[1].astype(bf16)

    c0 = nb * HY_W
    acc = jnp.dot(m_ref[...], rhs_ref[:, :c0], preferred_element_type=f32)
    p, w = acc[:tk], acc[tk:]
    hr = jnp.dot(m_ref[:tk, :], rhs_ref[:, c0:c0 + HY_W], preferred_element_type=f32)
    hi = jnp.dot(m_ref[tk:, :], rhs_ref[:, c0 + HY_W:], preferred_element_type=f32)
    qs = jnp.dot(m_ref[tk:tk + HALO_ROWS, :], rhs_ref[:, c0:c0 + HY_W], preferred_element_type=f32)[0:1]
    first = (lax.broadcasted_iota(jnp.int32, (tk, HY_W), 0) == 0) & (i == 0)
    for b in range(nb):
        cols = slice(b * HY_W, (b + 1) * HY_W)
        pu, wu = p[:, cols], w[:, cols]
        rz_ref[0, :, cols] = jnp.where(first, pu * hr, 2.0 * (pu * hr + wu * hi)).astype(bf16)
        rz_ref[1, :, cols] = jnp.where(first, wu * qs, 2.0 * (wu * hr - pu * hi)).astype(bf16)


def _hy_fwd(mat, zh, conv_w, conv_b, hb, batch, seq_len):
    tk, _ = _hy_tiles(seq_len)
    return pl.pallas_call(
        functools.partial(_hy_fwd_kernel, batch, tk, seq_len),
        grid=(seq_len // tk,),
        in_specs=[pl.BlockSpec((2 * tk, seq_len), lambda i: (i, 0)),
                  pl.BlockSpec((batch, seq_len, 3 * HY_W), lambda i: (0, 0, 0)),
                  pl.BlockSpec((3, 3 * HY_W), lambda i: (0, 0)),
                  pl.BlockSpec((1, 3 * HY_W), lambda i: (0, 0)),
                  pl.BlockSpec((2, seq_len, HY_W), lambda i: (0, 0, 0))],
        out_specs=pl.BlockSpec((2, tk, batch * HY_W), lambda i: (0, i, 0)),
        out_shape=jax.ShapeDtypeStruct((2, seq_len, batch * HY_W), bf16),
        scratch_shapes=[pltpu.VMEM((seq_len, (batch + 2) * HY_W), bf16)],
        compiler_params=_params("arbitrary"),
        name="hy_fwd",
    )(mat, zh, conv_w, conv_b.reshape(1, 3 * HY_W), hb)


def _hy_inv_kernel(nb, seq_len, nt, m_ref, rz_ref, zp_ref, zc_ref, zn_ref, cw_ref, cb_ref, d_ref, o_ref):
    i = pl.program_id(0)
    acc = jnp.dot(m_ref[...], rz_ref[...], preferred_element_type=f32)
    d, w, bias = d_ref[...], cw_ref[...], cb_ref[...]
    for b in range(nb):
        x0, u = _short_conv(zc_ref[b], zp_ref[b], zn_ref[b], i > 0, i < nt - 1, w, bias)
        y = acc[:, b * HY_W:(b + 1) * HY_W] * (1.0 / (2 * seq_len)) + u * d
        o_ref[b] = (x0 * y).astype(o_ref.dtype)


def _hy_inv(mat, rz, zh, conv_w, conv_b, d_bias, batch, seq_len):
    _, tm = _hy_tiles(seq_len)
    nt = seq_len // tm
    rh = tm // HALO_ROWS
    last = seq_len // HALO_ROWS - 1
    halo = lambda f: pl.BlockSpec((batch, HALO_ROWS, 3 * HY_W), f)
    return pl.pallas_call(
        functools.partial(_hy_inv_kernel, batch, seq_len, nt),
        grid=(nt,),
        in_specs=[pl.BlockSpec((tm, 2 * seq_len), lambda i: (i, 0)),
                  pl.BlockSpec((2 * seq_len, batch * HY_W), lambda i: (0, 0)),
                  halo(lambda i: (0, jnp.maximum(i * rh - 1, 0), 0)),
                  pl.BlockSpec((batch, tm, 3 * HY_W), lambda i: (0, i, 0)),
                  halo(lambda i: (0, jnp.minimum((i + 1) * rh, last), 0)),
                  pl.BlockSpec((3, 3 * HY_W), lambda i: (0, 0)),
                  pl.BlockSpec((1, 3 * HY_W), lambda i: (0, 0)),
                  pl.BlockSpec((1, HY_W), lambda i: (0, 0))],
        out_specs=pl.BlockSpec((batch, tm, HY_W), lambda i: (0, i, 0)),
        out_shape=jax.ShapeDtypeStruct((batch, seq_len, HY_W), bf16),
        compiler_params=_params("parallel"),
        name="hy_inv",
    )(mat, rz.reshape(2 * seq_len, batch * HY_W), zh, zh, zh, conv_w, conv_b.reshape(1, 3 * HY_W),
      d_bias.reshape(1, HY_W))


@functools.lru_cache(maxsize=None)
def _hy_consts_np(seq_len):
    n = 2 * seq_len
    k = np.arange(seq_len, dtype=np.int64)
    ang = (2.0 * np.pi / n) * ((k[:, None] * k[None, :]) % n).astype(np.float64)
    cosm = np.cos(ang)
    sinm = np.sin(ang)
    sinm[0, :] = np.where(k % 2 == 0, 1.0, -1.0)
    tk, _ = _hy_tiles(seq_len)
    fwd = np.concatenate([np.concatenate([cosm[i:i + tk], sinm[i:i + tk]], axis=0)
                          for i in range(0, seq_len, tk)], axis=0)
    inv = np.concatenate([cosm, sinm.T], axis=1)
    t = np.linspace(0.0, 1.0, seq_len, dtype=np.float32)[:, None]
    bands = (FILTER_EMB - 1) // 2
    fb = np.linspace(1e-4, bands - 1, bands, dtype=np.float32)[None, :]
    w = (np.float32(2.0 * math.pi) * np.arange(seq_len, dtype=np.float32)[:, None] / np.float32(seq_len)).astype(np.float32)
    feats = np.concatenate([t, np.cos(fb * w), -np.sin(fb * w)], axis=-1).astype(np.float32)
    feats = np.ascontiguousarray(np.pad(feats, ((0, 0), (0, LANES - FILTER_EMB))).T)
    decay_hi = math.log(HY_DECAY_TARGET) / HY_DECAY_HI_PCT
    decay_lo = math.log(HY_DECAY_TARGET) / HY_DECAY_LO_PCT
    deltas = np.abs(np.linspace(decay_lo, decay_hi, HY_W, dtype=np.float32))
    window = np.exp(-t * deltas[None, :]).astype(np.float32)
    return fwd, inv, feats, window


def _hy_consts(seq_len):
    fwd, inv, feats, window = _hy_consts_np(seq_len)
    as_bf = lambda m: jnp.asarray(m, dtype=f32).astype(bf16)
    return {"fwd": as_bf(fwd), "inv": as_bf(inv), "feats": jnp.asarray(feats), "window": jnp.asarray(window)}


def _hyena(zh, lp, consts, batch, seq_len):
    zh = zh.reshape(batch, seq_len, 3 * HY_W)
    hb = _hy_filter(seq_len, consts, lp["hy_f_w1"], lp["hy_f_b1"], lp["hy_f_w2"], lp["hy_f_b2"],
                    lp["hy_f_w3"], lp["hy_freq"]).reshape(2, seq_len, HY_W)
    rz = _hy_fwd(consts["fwd"], zh, lp["hy_conv_w"], lp["hy_conv_b"], hb, batch, seq_len)
    yh = _hy_inv(consts["inv"], rz, zh, lp["hy_conv_w"], lp["hy_conv_b"], lp["hy_bias"], batch, seq_len)
    return yh.reshape(batch * seq_len, HY_W)


def _gelu(x):
    return 0.5 * x * (1.0 + jnp.tanh(math.sqrt(2.0 / math.pi) * (x + 0.044715 * (x * x * x))))


def _gmlp(z, ws_ref, bias):
    g = _gelu(z)
    outs = []
    for c in range(TM // GM_CHUNK):
        rows = slice(c * GM_CHUNK, (c + 1) * GM_CHUNK)
        u = g[rows, :GM_W]
        v = g[rows, GM_W:].astype(bf16)
        s = jnp.concatenate(
            [jnp.dot(ws_ref[h], v[:, h * GM_HEAD_DIM:(h + 1) * GM_HEAD_DIM], preferred_element_type=f32)
             for h in range(GM_HEADS)], axis=1)
        outs.append(u * (s + bias))
    return jnp.concatenate(outs, axis=0)


def _out_kernel(a_ref, yh_ref, zg_ref, ws_ref, gb_ref, x_ref, mod_ref, og_ref, w_ref, g2_ref, rw_ref,
                x1_ref, h2_ref, aff_ref):
    og = og_ref[...]
    yg = _gmlp(zg_ref[...].astype(f32), ws_ref, gb_ref[...])
    mixed_in = jnp.concatenate([
        _rms(a_ref[...].astype(f32)) * og[:, :ATTN_W],
        _rms(yh_ref[...].astype(f32)) * og[:, ATTN_W:ATTN_W + HY_W],
        _rms(yg) * og[:, ATTN_W + HY_W:]], axis=1)
    mixed = jnp.dot(mixed_in.astype(bf16), w_ref[...], preferred_element_type=f32)
    m = mod_ref[0]
    x1 = x_ref[...] + m[2:3] * mixed
    x1_ref[...] = x1
    h2 = _rms(x1) * g2_ref[...] * (1.0 + m[4:5]) + m[3:4]
    for j in range(ROW_CHUNKS):
        h2_ref[pl.ds(j, TM, stride=ROW_CHUNKS), :] = h2[:, j * LANES:(j + 1) * LANES]
    nt = (((1,), (1,)), ((), ()))
    h_hi = h2.astype(bf16)
    h_lo = (h2 - h_hi.astype(f32)).astype(bf16)
    rw = rw_ref[...]
    t = lax.dot_general(rw, h_hi, nt, preferred_element_type=f32)
    logits = (t[:N_EXPERTS] + t[N_EXPERTS:]) + lax.dot_general(rw[:N_EXPERTS], h_lo, nt, preferred_element_type=f32)
    e = jnp.exp(logits - jnp.max(logits, axis=0, keepdims=True))
    aff_ref[...] = e / jnp.sum(e, axis=0, keepdims=True)


def _out_proj(a, yh, zg, ws_bf, gm_b, x, mod, out_g, w_out_bf, g2, router_wt):
    n = x.shape[0]
    per_mod = n // mod.shape[0]
    row = lambda w: pl.BlockSpec((TM, w), lambda i: (i, 0))
    const = lambda s: pl.BlockSpec(s, lambda i: (0,) * len(s))
    gm_bias = jnp.repeat(gm_b.T, GM_HEAD_DIM, axis=1)
    return pl.pallas_call(
        _out_kernel,
        grid=(n // TM,),
        in_specs=[row(ATTN_W), row(HY_W), row(2 * GM_W), const((GM_HEADS, GM_CHUNK, GM_CHUNK)),
                  const((GM_CHUNK, GM_W)), row(D_MODEL),
                  pl.BlockSpec((1, 6, D_MODEL), lambda i: (i * TM // per_mod, 0, 0)),
                  const((1, MIX_W)), const((MIX_W, D_MODEL)), const((1, D_MODEL)), const((2 * N_EXPERTS, D_MODEL))],
        out_specs=[row(D_MODEL), pl.BlockSpec((TM * ROW_CHUNKS, LANES), lambda i: (i, 0)),
                   pl.BlockSpec((N_EXPERTS, TM), lambda i: (0, i))],
        out_shape=[jax.ShapeDtypeStruct((n, D_MODEL), f32), jax.ShapeDtypeStruct((n * ROW_CHUNKS, LANES), f32),
                   jax.ShapeDtypeStruct((N_EXPERTS, n), f32)],
        compiler_params=_params("parallel"),
        name="out_proj",
    )(a, yh, zg, ws_bf, gm_bias, x, mod, out_g.reshape(1, MIX_W), w_out_bf, g2.reshape(1, D_MODEL), router_wt)


RT_CHUNK = 512


def _prefix_incl(x01, tri):
    n = x01.shape[1]
    carry = jnp.zeros((x01.shape[0], 1), f32)
    parts = []
    for c in range(n // RT_CHUNK):
        piece = x01[:, c * RT_CHUNK:(c + 1) * RT_CHUNK]
        parts.append(jnp.dot(piece.astype(bf16), tri, preferred_element_type=f32) + carry)
        carry = carry + jnp.sum(piece, axis=1, keepdims=True)
    return jnp.concatenate(parts, axis=1)


def _route_kernel(cap, aff_ref, idx_ref, gate_ref):
    aff = aff_ref[...]
    n = aff.shape[1]

    def step(it, t):
        cand = t | (jnp.int32(1) << (30 - it))
        cnt = jnp.sum(jnp.where(aff >= lax.bitcast_convert_type(cand, f32), 1.0, 0.0), axis=1, keepdims=True)
        return jnp.where(cnt >= cap, cand, t)

    thr_bits = lax.fori_loop(0, 31, step, jnp.zeros((aff.shape[0], 1), jnp.int32))
    thr = lax.bitcast_convert_type(thr_bits, f32)
    gt = jnp.where(aff > thr, 1.0, 0.0)
    eq = jnp.where(aff == thr, 1.0, 0.0)
    room = cap - jnp.sum(gt, axis=1, keepdims=True)
    r = lax.broadcasted_iota(jnp.int32, (RT_CHUNK, RT_CHUNK), 0)
    c = lax.broadcasted_iota(jnp.int32, (RT_CHUNK, RT_CHUNK), 1)
    tri = jnp.where(r <= c, 1.0, 0.0).astype(bf16)
    sel = jnp.maximum(gt, jnp.where(_prefix_incl(eq, tri) <= room, eq, 0.0))
    slot = _prefix_incl(sel, tri) - 1.0

    tok = lax.broadcasted_iota(jnp.int32, aff.shape, 1)
    dist = jnp.where(sel > 0.0, tok - slot.astype(jnp.int32), 0)
    g = aff
    for b in range(max(1, (n - 1).bit_length())):
        sh = 1 << b
        dist_s = pltpu.roll(dist, n - sh, axis=1)
        take = (dist_s & sh) != 0
        leave = (dist & sh) != 0
        tok = jnp.where(take, pltpu.roll(tok, n - sh, axis=1), tok)
        g = jnp.where(take, pltpu.roll(g, n - sh, axis=1), g)
        dist = jnp.where(take, dist_s, jnp.where(leave, 0, dist))
    idx_ref[...] = tok[:, :cap] * ROW_CHUNKS
    gate_ref[...] = g[:, :cap]


def _route(aff_t, cap):
    return pl.pallas_call(
        functools.partial(_route_kernel, cap),
        in_specs=[pl.BlockSpec(memory_space=pltpu.VMEM)],
        out_specs=[pl.BlockSpec(memory_space=pltpu.VMEM)] * 2,
        out_shape=[jax.ShapeDtypeStruct((N_EXPERTS, cap), jnp.int32), jax.ShapeDtypeStruct((N_EXPERTS, cap), f32)],
        compiler_params=pltpu.CompilerParams(vmem_limit_bytes=VMEM_LIMIT),
        name="route",
    )(aff_t)


GATHER_UNROLL = 8
SCATTER_UNROLL = 8
CMB_EXPERTS = 2
CMB_TM = 1024


def _moe_ffn_kernel(cap, stride, idx_ref, h_ref, w1_ref, w3_ref, w2_ref, y_ref, tile_ref, xb_ref, acc_ref):
    e = pl.program_id(0)
    slab = ROW_CHUNKS * stride
    cur = pl.multiple_of((e % 2) * slab, SUBLANES)
    nxt = pl.multiple_of(((e + 1) % 2) * slab, SUBLANES)

    def gather_row(expert, slab_row, r):
        src = pl.multiple_of(idx_ref[expert * cap + r], ROW_CHUNKS)
        tile_ref[pl.ds(slab_row + r, ROW_CHUNKS, stride=stride), :] = h_ref[pl.ds(src, ROW_CHUNKS), :]

    @pl.when(e == 0)
    def _():
        def gather(q, carry):
            for t in range(GATHER_UNROLL):
                gather_row(0, cur, q * GATHER_UNROLL + t)
            return carry
        lax.fori_loop(0, cap // GATHER_UNROLL, gather, 0)

    xb_ref[...] = jnp.concatenate(
        [tile_ref[pl.ds(cur + j * stride, cap), :] for j in range(ROW_CHUNKS)], axis=1).astype(bf16)

    e_next = jnp.minimum(e + 1, N_EXPERTS - 1)
    n_f = D_EXPERT // MOE_F_TILE
    xb = xb_ref[...]
    for f in range(n_f):
        cols = slice(f * MOE_F_TILE, (f + 1) * MOE_F_TILE)
        a = jnp.dot(xb, w1_ref[0, 0, :, cols].astype(bf16), preferred_element_type=f32)
        b = jnp.dot(xb, w3_ref[0, 0, :, cols].astype(bf16), preferred_element_type=f32)
        he = (a * _sigmoid(a) * b).astype(bf16)
        part = jnp.dot(he, w2_ref[0, 0, cols, :].astype(bf16), preferred_element_type=f32)
        if f == 0:
            acc_ref[...] = part
        else:
            acc_ref[...] += part
        for r in range(f * cap // n_f, (f + 1) * cap // n_f):
            gather_row(e_next, nxt, r)

    y = acc_ref[...]
    for j in range(ROW_CHUNKS):
        y_ref[0, pl.ds(j, cap, stride=ROW_CHUNKS), :] = y[:, j * LANES:(j + 1) * LANES]


def _moe_ffn(layer, h3, idx, w1, w3, w2):
    cap = idx.shape[1]
    stride = cap + SUBLANES
    grid_spec = pltpu.PrefetchScalarGridSpec(
        num_scalar_prefetch=1,
        grid=(N_EXPERTS,),
        in_specs=[
            pl.BlockSpec(memory_space=pltpu.VMEM),
            pl.BlockSpec((1, 1, D_MODEL, D_EXPERT), lambda e, idx: (layer, e, 0, 0)),
            pl.BlockSpec((1, 1, D_MODEL, D_EXPERT), lambda e, idx: (layer, e, 0, 0)),
            pl.BlockSpec((1, 1, D_EXPERT, D_MODEL), lambda e, idx: (layer, e, 0, 0)),
        ],
        out_specs=pl.BlockSpec((1, ROW_CHUNKS * cap, LANES), lambda e, idx: (e, 0, 0)),
        scratch_shapes=[pltpu.VMEM((2 * ROW_CHUNKS * stride, LANES), f32),
                        pltpu.VMEM((cap, D_MODEL), bf16),
                        pltpu.VMEM((cap, D_MODEL), f32)],
    )
    return pl.pallas_call(
        functools.partial(_moe_ffn_kernel, cap, stride),
        grid_spec=grid_spec,
        out_shape=jax.ShapeDtypeStruct((N_EXPERTS, ROW_CHUNKS * cap, LANES), f32),
        compiler_params=_params("arbitrary"),
        name="moe_ffn",
    )(idx.reshape(-1), h3, w1, w3, w2)


def _moe_combine_kernel(cap, final, idx_ref, gate_ref, y_ref, x_ref, mod_ref, g_ref, o_ref, acc_ref):
    s = pl.program_id(0)
    scatter_steps = N_EXPERTS // CMB_EXPERTS

    @pl.when(s == 0)
    def _():
        acc_ref[...] = jnp.zeros_like(acc_ref)

    @pl.when(s < scatter_steps)
    def _():
        for k in range(CMB_EXPERTS):
            base = (s * CMB_EXPERTS + k) * cap

            def scatter(q, carry, k=k, base=base):
                group = SCATTER_UNROLL * ROW_CHUNKS
                y = y_ref[k, pl.ds(pl.multiple_of(q * group, group), group), :]
                rows = [pl.multiple_of(idx_ref[base + q * SCATTER_UNROLL + t], ROW_CHUNKS)
                        for t in range(SCATTER_UNROLL)]
                new = [acc_ref[pl.ds(rows[t], ROW_CHUNKS), :]
                       + gate_ref[base + q * SCATTER_UNROLL + t] * y[t * ROW_CHUNKS:(t + 1) * ROW_CHUNKS]
                       for t in range(SCATTER_UNROLL)]
                for t in range(SCATTER_UNROLL):
                    acc_ref[pl.ds(rows[t], ROW_CHUNKS), :] = new[t]
                return carry
            lax.fori_loop(0, cap // SCATTER_UNROLL, scatter, 0)

    @pl.when(s >= scatter_steps)
    def _():
        first = pl.multiple_of((s - scatter_steps) * (CMB_TM * ROW_CHUNKS), CMB_TM * ROW_CHUNKS)
        moe = jnp.concatenate([acc_ref[pl.ds(first + j, CMB_TM, stride=ROW_CHUNKS), :] for j in range(ROW_CHUNKS)],
                              axis=1)
        x = x_ref[...] + mod_ref[0][5:6] * moe
        o_ref[...] = _rms(x) * g_ref[...] if final else x


def _moe_combine(idx, gate, y_cm, x1, mod, final_g, final):
    n = x1.shape[0]
    cap = idx.shape[1]
    per_mod = n // mod.shape[0]
    scatter_steps = N_EXPERTS // CMB_EXPERTS
    tile = lambda s: jnp.maximum(s - scatter_steps, 0)
    row = pl.BlockSpec((CMB_TM, D_MODEL), lambda s, *_: (tile(s), 0))
    grid_spec = pltpu.PrefetchScalarGridSpec(
        num_scalar_prefetch=2,
        grid=(scatter_steps + n // CMB_TM,),
        in_specs=[pl.BlockSpec((CMB_EXPERTS, ROW_CHUNKS * cap, LANES),
                               lambda s, *_: (jnp.minimum(s, scatter_steps - 1), 0, 0)),
                  row,
                  pl.BlockSpec((1, 6, D_MODEL), lambda s, *_: (tile(s) * CMB_TM // per_mod, 0, 0)),
                  pl.BlockSpec((1, D_MODEL), lambda s, *_: (0, 0))],
        out_specs=row,
        scratch_shapes=[pltpu.VMEM((n * ROW_CHUNKS, LANES), f32)],
    )
    return pl.pallas_call(
        functools.partial(_moe_combine_kernel, cap, final),
        grid_spec=grid_spec,
        out_shape=jax.ShapeDtypeStruct((n, D_MODEL), f32),
        compiler_params=_params("arbitrary"),
        name="moe_combine",
    )(idx.reshape(-1), gate.reshape(-1), y_cm, x1, mod, final_g.reshape(1, D_MODEL))


def _expert_choice(layer, h3, aff_t, x1, mod, final_g, final, w1, w3, w2):
    n = x1.shape[0]
    cap = max(1, EC_CAPACITY * n // N_EXPERTS)
    idx, gate = _route(aff_t, cap)
    y_cm = _moe_ffn(layer, h3, idx, w1, w3, w2)
    return _moe_combine(idx, gate, y_cm, x1, mod, final_g, final)


def _split_bf16(w):
    hi = w.astype(bf16)
    lo = (w - hi.astype(f32)).astype(bf16)
    return jnp.concatenate([hi, lo], axis=0)


def _stream(x, mods, layers, experts, batch, seq_len, final_g, caches=None):
    consts = _hy_consts(seq_len)
    rope_tabs = _rope_tables(seq_len) if caches is not None else None
    kvs = []
    for l, lp in enumerate(layers):
        qkv, zh, zg = _in_proj(x, mods[l], lp["norm1_g"], lp["w_in"], seq_len, rope_tabs=rope_tabs)
        if caches is None:
            a = _ctx_attention(qkv, lp["attn_sink"], batch, seq_len)
            kvs.append((qkv[:, ATTN_W:ATTN_W + KV_W], qkv[:, ATTN_W + KV_W:]))
        else:
            a = _lat_attention(qkv, lp["attn_sink"], caches[0][:, l], caches[1][:, l], batch, seq_len)
        yh = _hyena(zh, lp, consts, batch, seq_len)
        x1, h3, aff_t = _out_proj(a, yh, zg, lp["gm_ws"], lp["gm_b"], x, mods[l], lp["out_norm_g"], lp["w_out"],
                                  lp["norm2_g"], lp["router_wt"])
        x = _expert_choice(l, h3, aff_t, x1, mods[l], final_g, l == len(layers) - 1, *experts)
    return x, kvs


def kernel(x_prompt, x_sample, c, cache_k, cache_v, c_ctx, norm1_g, norm2_g, ada_w, ada_b, w_in, attn_sink,
           hy_conv_w, hy_conv_b, hy_f_w1, hy_f_b1, hy_f_w2, hy_f_b2, hy_f_w3, hy_freq, hy_bias, gm_ws, gm_b,
           out_norm_g, w_out, router_w, exp_w1, exp_w3, exp_w2, final_g):
    batch, seq, _ = x_prompt.shape
    dbatch, dseq, _ = x_sample.shape
    past = cache_k.shape[2]

    cvec = jnp.concatenate([c_ctx[None], c, jnp.zeros((SUBLANES - 1 - dbatch, D_MODEL), f32)], axis=0)
    mod = _ada(cvec, ada_w, ada_b)
    mods_ctx = [mod[l, 0:1].reshape(1, 6, D_MODEL) for l in range(DEPTH)]
    mods_lat = [mod[l, 1:1 + dbatch].reshape(dbatch, 6, D_MODEL) for l in range(DEPTH)]

    w_in_bf, w_out_bf, gm_ws_bf = w_in.astype(bf16), w_out.astype(bf16), gm_ws.astype(bf16)
    layers = []
    for l in range(DEPTH):
        layers.append({
            "norm1_g": norm1_g[l], "norm2_g": norm2_g[l], "w_in": w_in_bf[l], "attn_sink": attn_sink[l],
            "hy_conv_w": hy_conv_w[l], "hy_conv_b": hy_conv_b[l], "hy_f_w1": hy_f_w1[l], "hy_f_b1": hy_f_b1[l],
            "hy_f_w2": hy_f_w2[l], "hy_f_b2": hy_f_b2[l], "hy_f_w3": hy_f_w3[l], "hy_freq": hy_freq[l],
            "hy_bias": hy_bias[l], "gm_ws": gm_ws_bf[l], "gm_b": gm_b[l], "out_norm_g": out_norm_g[l],
            "w_out": w_out_bf[l], "router_wt": _split_bf16(router_w[l].T)})
    experts = (exp_w1, exp_w3, exp_w2)

    yp, kvs = _stream(x_prompt.reshape(batch * seq, D_MODEL), mods_ctx, layers, experts, batch, seq, final_g)
    caches = (cache_k.reshape(dbatch, DEPTH, past, KV_W), cache_v.reshape(dbatch, DEPTH, past, KV_W))
    ys, _ = _stream(x_sample.reshape(dbatch * dseq, D_MODEL), mods_lat, layers, experts, dbatch, dseq, final_g,
                   caches=caches)

    new_k = jnp.stack([k.reshape(batch, seq, N_KV, HEAD_DIM) for k, _ in kvs], axis=1)
    new_v = jnp.stack([v.reshape(batch, seq, N_KV, HEAD_DIM) for _, v in kvs], axis=1)
    return (yp.reshape(batch, seq, D_MODEL), ys.reshape(dbatch, dseq, D_MODEL), new_k, new_v)
```

```python
import functools
import math

import numpy as np
import jax
import jax.numpy as jnp
from jax import lax
from jax.experimental import pallas as pl
from jax.experimental.pallas import tpu as pltpu

f32 = jnp.float32
bf16 = jnp.bfloat16

D_MODEL = 1024
DEPTH = 2
GRID_W = 64
BLOCK = 128
N_HEADS = 8
N_KV = 2
HEAD_DIM = 64
Q_GROUP = N_HEADS // N_KV
ATTN_W = N_HEADS * HEAD_DIM
KV_W = N_KV * HEAD_DIM
QKV_W = ATTN_W + 2 * KV_W
HY_W = 256
GM_W = 256
GM_HEADS = 4
GM_HEAD_DIM = GM_W // GM_HEADS
GM_CHUNK = 128
MIX_W = ATTN_W + HY_W + GM_W
IN_W = ATTN_W + 2 * KV_W + 3 * HY_W + 2 * GM_W
FILTER_EMB = 33
FILTER_HID = 64
HY_DECAY_HI_PCT = 0.3
HY_DECAY_LO_PCT = 1.5
HY_DECAY_TARGET = 1e-2
N_EXPERTS = 16
EC_CAPACITY = 2
D_EXPERT = 1024
ROPE_THETA = 10000.0
EPS = 1e-6
NEG = -1e30

LANES = 128
SUBLANES = 8
ROW_CHUNKS = D_MODEL // LANES
VMEM_LIMIT = 56 * 1024 * 1024
TM = 512
IN_TM = 512
MOE_F_TILE = 256


def _params(*sem):
    return pltpu.CompilerParams(dimension_semantics=sem, vmem_limit_bytes=VMEM_LIMIT)


def _rms(x):
    return x * lax.rsqrt(jnp.mean(x * x, axis=-1, keepdims=True) + EPS)


def _sigmoid(x):
    return 1.0 / (1.0 + jnp.exp(-x))


def _ada_kernel(c_ref, w_ref, b_ref, o_ref):
    c = c_ref[...]
    s = (c * _sigmoid(c)).astype(bf16)
    o_ref[0] = jnp.dot(s, w_ref[0].astype(bf16), preferred_element_type=f32) + b_ref[0]


def _ada(cvec, ada_w, ada_b):
    nt = 6
    return pl.pallas_call(
        _ada_kernel,
        grid=(DEPTH, nt),
        in_specs=[
            pl.BlockSpec((SUBLANES, D_MODEL), lambda l, j: (0, 0)),
            pl.BlockSpec((1, D_MODEL, D_MODEL), lambda l, j: (l, 0, j)),
            pl.BlockSpec((1, 1, D_MODEL), lambda l, j: (l, 0, j)),
        ],
        out_specs=pl.BlockSpec((1, SUBLANES, D_MODEL), lambda l, j: (l, 0, j)),
        out_shape=jax.ShapeDtypeStruct((DEPTH, SUBLANES, 6 * D_MODEL), f32),
        compiler_params=_params("arbitrary", "arbitrary"),
        name="ada",
    )(cvec, ada_w, ada_b.reshape(DEPTH, 1, 6 * D_MODEL))


def _rope_swap(x):
    w = x.shape[-1]
    lane = lax.broadcasted_iota(jnp.int32, x.shape, 1)
    first = (lane % 32) < 16
    return jnp.where(first, pltpu.roll(x, w - 16, axis=1), pltpu.roll(x, 16, axis=1))


def _in_kernel(rope, x_ref, mod_ref, g_ref, w_ref, *refs):
    if rope:
        cos_ref, sin_ref, qkv_ref, zh_ref, zg_ref = refs
    else:
        qkv_ref, zh_ref, zg_ref = refs
    m = mod_ref[0]
    h = _rms(x_ref[...]) * g_ref[...] * (1.0 + m[1:2]) + m[0:1]
    z = jnp.dot(h.astype(bf16), w_ref[...], preferred_element_type=f32)
    if rope:
        reps = (ATTN_W + KV_W) // LANES
        cos = jnp.concatenate([cos_ref[...]] * reps, axis=1)
        sin = jnp.concatenate([sin_ref[...]] * reps, axis=1)
        qk = z[:, :ATTN_W + KV_W]
        qkv_ref[:, :ATTN_W + KV_W] = qk * cos + _rope_swap(qk) * sin
        qkv_ref[:, ATTN_W + KV_W:] = z[:, ATTN_W + KV_W:QKV_W]
    else:
        qkv_ref[...] = z[:, :QKV_W]
    zh_ref[...] = z[:, QKV_W:QKV_W + 3 * HY_W].astype(bf16)
    zg_ref[...] = z[:, QKV_W + 3 * HY_W:].astype(bf16)


def _in_proj(x, mod, g1, w_in_bf, seq_len, rope_tabs=None):
    n = x.shape[0]
    rope = rope_tabs is not None
    per_mod = n // mod.shape[0]
    tm = IN_TM
    in_specs = [pl.BlockSpec((tm, D_MODEL), lambda i: (i, 0)),
                pl.BlockSpec((1, 6, D_MODEL), lambda i: (i * tm // per_mod, 0, 0)),
                pl.BlockSpec((1, D_MODEL), lambda i: (0, 0)),
                pl.BlockSpec((D_MODEL, IN_W), lambda i: (0, 0))]
    args = [x, mod, g1.reshape(1, D_MODEL), w_in_bf]
    if rope:
        nt = seq_len // tm
        tab = pl.BlockSpec((tm, LANES), lambda i: (i % nt, 0))
        in_specs += [tab, tab]
        args += list(rope_tabs)
    return pl.pallas_call(
        functools.partial(_in_kernel, rope),
        grid=(n // tm,),
        in_specs=in_specs,
        out_specs=[pl.BlockSpec((tm, w), lambda i: (i, 0)) for w in (QKV_W, 3 * HY_W, 2 * GM_W)],
        out_shape=[jax.ShapeDtypeStruct((n, QKV_W), f32), jax.ShapeDtypeStruct((n, 3 * HY_W), bf16),
                   jax.ShapeDtypeStruct((n, 2 * GM_W), bf16)],
        compiler_params=_params("parallel"),
        name="in_proj",
    )(*args)


def _rope_tables(seq_len):
    nf = HEAD_DIM // 4
    t = np.arange(seq_len)
    inv = (ROPE_THETA ** (-np.arange(nf, dtype=np.float32) / nf)).astype(np.float32)
    d = np.arange(HEAD_DIM)
    pos = np.where((d // 32)[None, :] == 0, (t // GRID_W)[:, None], (t % GRID_W)[:, None]).astype(np.float32)
    ang = (pos * inv[d % nf][None, :]).astype(np.float32)
    cos = np.cos(ang).astype(np.float32)
    sin = np.sin(ang).astype(np.float32) * np.where((d % 32) < 16, -1.0, 1.0)[None, :].astype(np.float32)
    return jnp.asarray(np.tile(cos, (1, 2))), jnp.asarray(np.tile(sin, (1, 2)))


LOG2E = math.log2(math.e)


def _stack_queries(z, sink_ref, kv):
    rows = z.shape[0]
    heads = range(kv * Q_GROUP, (kv + 1) * Q_GROUP)
    qs = jnp.concatenate([z[:, h * HEAD_DIM:(h + 1) * HEAD_DIM] for h in heads], axis=0)
    sink = jnp.concatenate([jnp.full((1, rows), sink_ref[h] * LOG2E, f32) for h in heads], axis=1)
    return (qs * (LOG2E / math.sqrt(HEAD_DIM))).astype(bf16), sink


def _scores_t(k, qs_bf):
    return lax.dot_general(k.astype(bf16), qs_bf, (((1,), (1,)), ((), ())), preferred_element_type=f32)


def _softmax_pv_t(st, sink, vals, rows):
    m = jnp.maximum(jnp.max(st, axis=0, keepdims=True), sink)
    pt = jnp.exp2(st - m).astype(bf16)
    v_bf = vals.astype(bf16)
    v_ext = jnp.concatenate([v_bf, jnp.ones_like(v_bf)], axis=1)
    ovt = lax.dot_general(v_ext, pt, (((0,), (0,)), ((), ())), preferred_element_type=f32)
    ot = ovt[:HEAD_DIM] / (ovt[HEAD_DIM:HEAD_DIM + 1] + jnp.exp2(sink - m))
    return [ot[:, g * rows:(g + 1) * rows].T for g in range(Q_GROUP)]


CTX_BATCHES = 4
LAT_QBLOCKS = 8


def _ctx_attn_kernel(seq_len, sink_ref, z_ref, o_ref):
    for s in range(CTX_BATCHES):
        z = z_ref[s * seq_len:(s + 1) * seq_len, :]
        outs = []
        for kv in range(N_KV):
            k = z[:, ATTN_W + kv * HEAD_DIM:ATTN_W + (kv + 1) * HEAD_DIM]
            v = z[:, ATTN_W + KV_W + kv * HEAD_DIM:ATTN_W + KV_W + (kv + 1) * HEAD_DIM]
            qs, sink = _stack_queries(z, sink_ref, kv)
            outs += _softmax_pv_t(_scores_t(k, qs), sink, v, seq_len)
        o_ref[s * seq_len:(s + 1) * seq_len, :] = jnp.concatenate(outs, axis=1).astype(o_ref.dtype)


def _ctx_attention(z, sink, batch, seq_len):
    rows = CTX_BATCHES * seq_len
    return pl.pallas_call(
        functools.partial(_ctx_attn_kernel, seq_len),
        grid=(batch // CTX_BATCHES,),
        in_specs=[pl.BlockSpec(memory_space=pltpu.SMEM),
                  pl.BlockSpec((rows, QKV_W), lambda b: (b, 0))],
        out_specs=pl.BlockSpec((rows, ATTN_W), lambda b: (b, 0)),
        out_shape=jax.ShapeDtypeStruct((batch * seq_len, ATTN_W), bf16),
        compiler_params=_params("parallel"),
        name="ctx_attn",
    )(sink, z)


def _lat_attn_kernel(nb, sink_ref, zp_ref, zc_ref, zn_ref, ck_ref, cv_ref, o_ref):
    step = pl.program_id(1)
    zc_all = zc_ref[...]
    blocks = [zp_ref[...]] + [zc_all[q * BLOCK:(q + 1) * BLOCK] for q in range(LAT_QBLOCKS)] + [zn_ref[...]]
    ck, cv = ck_ref[0], cv_ref[0]
    width = Q_GROUP * BLOCK
    j = lax.broadcasted_iota(jnp.int32, (BLOCK, width), 0)
    r = lax.broadcasted_iota(jnp.int32, (BLOCK, width), 1) % BLOCK
    for q in range(LAT_QBLOCKS):
        i = step * LAT_QBLOCKS + q
        zp, zc, zn = blocks[q], blocks[q + 1], blocks[q + 2]
        ok_prev = j >= r + jnp.where(i > 0, 0, BLOCK)
        ok_next = j <= r - jnp.where(i < nb - 1, 0, BLOCK)
        outs = []
        for kv in range(N_KV):
            ks = slice(ATTN_W + kv * HEAD_DIM, ATTN_W + (kv + 1) * HEAD_DIM)
            vs = slice(ATTN_W + KV_W + kv * HEAD_DIM, ATTN_W + KV_W + (kv + 1) * HEAD_DIM)
            cs = slice(kv * HEAD_DIM, (kv + 1) * HEAD_DIM)
            qs, sink = _stack_queries(zc, sink_ref, kv)
            st = jnp.concatenate([
                jnp.where(ok_prev, _scores_t(zp[:, ks], qs), NEG),
                _scores_t(zc[:, ks], qs),
                jnp.where(ok_next, _scores_t(zn[:, ks], qs), NEG),
                _scores_t(ck[:, cs], qs)], axis=0)
            vals = jnp.concatenate([zp[:, vs], zc[:, vs], zn[:, vs], cv[:, cs]], axis=0)
            outs += _softmax_pv_t(st, sink, vals, BLOCK)
        o_ref[q * BLOCK:(q + 1) * BLOCK, :] = jnp.concatenate(outs, axis=1).astype(o_ref.dtype)


def _lat_attention(z, sink, ck, cv, batch, seq_len):
    nb = seq_len // BLOCK
    ns = nb // LAT_QBLOCKS
    blk = lambda f: pl.BlockSpec((BLOCK, QKV_W), f)
    past = ck.shape[1]
    cache = pl.BlockSpec((1, past, KV_W), lambda b, i: (b, 0, 0))
    return pl.pallas_call(
        functools.partial(_lat_attn_kernel, nb),
        grid=(batch, ns),
        in_specs=[pl.BlockSpec(memory_space=pltpu.SMEM),
                  blk(lambda b, i: (b * nb + jnp.maximum(i * LAT_QBLOCKS - 1, 0), 0)),
                  pl.BlockSpec((LAT_QBLOCKS * BLOCK, QKV_W), lambda b, i: (b * ns + i, 0)),
                  blk(lambda b, i: (b * nb + jnp.minimum((i + 1) * LAT_QBLOCKS, nb - 1), 0)),
                  cache, cache],
        out_specs=pl.BlockSpec((LAT_QBLOCKS * BLOCK, ATTN_W), lambda b, i: (b * ns + i, 0)),
        out_shape=jax.ShapeDtypeStruct((batch * seq_len, ATTN_W), bf16),
        compiler_params=_params("parallel", "parallel"),
        name="lat_attn",
    )(sink, z, z, z, ck, cv)


HY_TL = 256
HALO_ROWS = 2 * SUBLANES


def _short_conv(z, prev_tile, next_tile, has_prev, has_next, w, b):
    z = z.astype(f32)
    rows = z.shape[0]
    row = lax.broadcasted_iota(jnp.int32, (SUBLANES, z.shape[1]), 0)
    prev_row = jnp.where(has_prev, prev_tile.astype(f32)[HALO_ROWS - 1:HALO_ROWS, :], 0.0)
    next_row = jnp.where(has_next, next_tile.astype(f32)[0:1, :], 0.0)
    z_prev = pltpu.roll(z, 1, axis=0)
    z_prev = jnp.concatenate([jnp.where(row == 0, prev_row, z_prev[:SUBLANES]), z_prev[SUBLANES:]], axis=0)
    z_next = pltpu.roll(z, rows - 1, axis=0)
    z_next = jnp.concatenate([z_next[:rows - SUBLANES],
                              jnp.where(row == SUBLANES - 1, next_row, z_next[rows - SUBLANES:])], axis=0)
    zc = z_prev * w[0:1] + z * w[1:2] + z_next * w[2:3] + b
    return zc[:, :HY_W], zc[:, HY_W:2 * HY_W] * zc[:, 2 * HY_W:]


def _hy_filter_kernel(seq_len, feats_ref, win_ref, w1_ref, b1_ref, w2_ref, b2_ref, w3_ref, fr_ref, o_ref):
    hi = lax.Precision.HIGHEST
    fr = fr_ref[...]
    h = jnp.sin(fr * (jnp.dot(w1_ref[...], feats_ref[...], precision=hi, preferred_element_type=f32) + b1_ref[...]))
    h = jnp.sin(fr * (jnp.dot(w2_ref[...], h, precision=hi, preferred_element_type=f32) + b2_ref[...]))
    h = jnp.dot(h.T, w3_ref[...], precision=hi, preferred_element_type=f32)
    win = win_ref[...]
    row = lax.broadcasted_iota(jnp.int32, (seq_len, HY_W), 0)
    hf = h[:, :HY_W] * win
    hb = jnp.where(row == 0, 0.0, h[:, HY_W:] * win)
    o_ref[0:seq_len, :] = hf + hb
    o_ref[seq_len:, :] = hb - hf


def _hy_filter(seq_len, consts, w1, b1, w2, b2, w3, freq):
    w1t = jnp.zeros((FILTER_HID, LANES), f32).at[:, :FILTER_EMB].set(w1.T)
    args = (consts["feats"], consts["window"], w1t, b1.reshape(-1, 1), w2.T, b2.reshape(-1, 1), w3, freq.reshape(-1, 1))
    return pl.pallas_call(
        functools.partial(_hy_filter_kernel, seq_len),
        in_specs=[pl.BlockSpec(memory_space=pltpu.VMEM)] * len(args),
        out_specs=pl.BlockSpec(memory_space=pltpu.VMEM),
        out_shape=jax.ShapeDtypeStruct((2 * seq_len, HY_W), f32),
        compiler_params=pltpu.CompilerParams(vmem_limit_bytes=VMEM_LIMIT),
        name="hy_filter",
    )(*args)


def _hy_tiles(seq_len):
    return min(TM, seq_len // 2), min(TM, seq_len)


def _hy_fwd_kernel(nb, tk, seq_len, m_ref, zh_ref, cw_ref, cb_ref, hb_ref, rz_ref, rhs_ref):
    i = pl.program_id(0)
    nc = seq_len // HY_TL

    @pl.when(i == 0)
    def _():
        w, bias = cw_ref[...], cb_ref[...]
        for b in range(nb):
            def chunk(c, carry, b=b):
                r0 = pl.multiple_of(c * HY_TL, HY_TL)
                prev = zh_ref[b, pl.ds(pl.multiple_of(jnp.maximum(r0 - HALO_ROWS, 0), HALO_ROWS), HALO_ROWS), :]
                nxt = zh_ref[b, pl.ds(pl.multiple_of(jnp.minimum(r0 + HY_TL, seq_len - HALO_ROWS), HALO_ROWS),
                                      HALO_ROWS), :]
                _, u = _short_conv(zh_ref[b, pl.ds(r0, HY_TL), :], prev, nxt, c > 0, c < nc - 1, w, bias)
                rhs_ref[pl.ds(r0, HY_TL), b * HY_W:(b + 1) * HY_W] = u.astype(bf16)
                return carry
            lax.fori_loop(0, nc, chunk, 0)
        rhs_ref[:, nb * HY_W:(nb + 1) * HY_W] = hb_ref[0].astype(bf16)
        rhs_ref[:, (nb + 1) * HY_W:] = hb_ref[1].astype(bf16)

    c0 = nb * HY_W
    acc = jnp.dot(m_ref[...], rhs_ref[:, :c0], preferred_element_type=f32)
    p, w = acc[:tk], acc[tk:]
    hr = jnp.dot(m_ref[:tk, :], rhs_ref[:, c0:c0 + HY_W], preferred_element_type=f32)
    hi = jnp.dot(m_ref[tk:, :], rhs_ref[:, c0 + HY_W:], preferred_element_type=f32)
    qs = jnp.dot(m_ref[tk:tk + HALO_ROWS, :], rhs_ref[:, c0:c0 + HY_W], preferred_element_type=f32)[0:1]
    first = (lax.broadcasted_iota(jnp.int32, (tk, HY_W), 0) == 0) & (i == 0)
    for b in range(nb):
        cols = slice(b * HY_W, (b + 1) * HY_W)
        pu, wu = p[:, cols], w[:, cols]
        rz_ref[0, :, cols] = jnp.where(first, pu * hr, 2.0 * (pu * hr + wu * hi)).astype(bf16)
        rz_ref[1, :, cols] = jnp.where(first, wu * qs, 2.0 * (wu * hr - pu * hi)).astype(bf16)


def _hy_fwd(mat, zh, conv_w, conv_b, hb, batch, seq_len):
    tk, _ = _hy_tiles(seq_len)
    return pl.pallas_call(
        functools.partial(_hy_fwd_kernel, batch, tk, seq_len),
        grid=(seq_len // tk,),
        in_specs=[pl.BlockSpec((2 * tk, seq_len), lambda i: (i, 0)),
                  pl.BlockSpec((batch, seq_len, 3 * HY_W), lambda i: (0, 0, 0)),
                  pl.BlockSpec((3, 3 * HY_W), lambda i: (0, 0)),
                  pl.BlockSpec((1, 3 * HY_W), lambda i: (0, 0)),
                  pl.BlockSpec((2, seq_len, HY_W), lambda i: (0, 0, 0))],
        out_specs=pl.BlockSpec((2, tk, batch * HY_W), lambda i: (0, i, 0)),
        out_shape=jax.ShapeDtypeStruct((2, seq_len, batch * HY_W), bf16),
        scratch_shapes=[pltpu.VMEM((seq_len, (batch + 2) * HY_W), bf16)],
        compiler_params=_params("arbitrary"),
        name="hy_fwd",
    )(mat, zh, conv_w, conv_b.reshape(1, 3 * HY_W), hb)


def _hy_inv_kernel(nb, seq_len, nt, m_ref, rz_ref, zp_ref, zc_ref, zn_ref, cw_ref, cb_ref, d_ref, o_ref):
    i = pl.program_id(0)
    acc = jnp.dot(m_ref[...], rz_ref[...], preferred_element_type=f32)
    d, w, bias = d_ref[...], cw_ref[...], cb_ref[...]
    for b in range(nb):
        x0, u = _short_conv(zc_ref[b], zp_ref[b], zn_ref[b], i > 0, i < nt - 1, w, bias)
        y = acc[:, b * HY_W:(b + 1) * HY_W] * (1.0 / (2 * seq_len)) + u * d
        o_ref[b] = (x0 * y).astype(o_ref.dtype)


def _hy_inv(mat, rz, zh, conv_w, conv_b, d_bias, batch, seq_len):
    _, tm = _hy_tiles(seq_len)
    nt = seq_len // tm
    rh = tm // HALO_ROWS
    last = seq_len // HALO_ROWS - 1
    halo = lambda f: pl.BlockSpec((batch, HALO_ROWS, 3 * HY_W), f)
    return pl.pallas_call(
        functools.partial(_hy_inv_kernel, batch, seq_len, nt),
        grid=(nt,),
        in_specs=[pl.BlockSpec((tm, 2 * seq_len), lambda i: (i, 0)),
                  pl.BlockSpec((2 * seq_len, batch * HY_W), lambda i: (0, 0)),
                  halo(lambda i: (0, jnp.maximum(i * rh - 1, 0), 0)),
                  pl.BlockSpec((batch, tm, 3 * HY_W), lambda i: (0, i, 0)),
                  halo(lambda i: (0, jnp.minimum((i + 1) * rh, last), 0)),
                  pl.BlockSpec((3, 3 * HY_W), lambda i: (0, 0)),
                  pl.BlockSpec((1, 3 * HY_W), lambda i: (0, 0)),
                  pl.BlockSpec((1, HY_W), lambda i: (0, 0))],
        out_specs=pl.BlockSpec((batch, tm, HY_W), lambda i: (0, i, 0)),
        out_shape=jax.ShapeDtypeStruct((batch, seq_len, HY_W), bf16),
        compiler_params=_params("parallel"),
        name="hy_inv",
    )(mat, rz.reshape(2 * seq_len, batch * HY_W), zh, zh, zh, conv_w, conv_b.reshape(1, 3 * HY_W),
      d_bias.reshape(1, HY_W))


@functools.lru_cache(maxsize=None)
def _hy_consts_np(seq_len):
    n = 2 * seq_len
    k = np.arange(seq_len, dtype=np.int64)
    ang = (2.0 * np.pi / n) * ((k[:, None] * k[None, :]) % n).astype(np.float64)
    cosm = np.cos(ang)
    sinm = np.sin(ang)
    sinm[0, :] = np.where(k % 2 == 0, 1.0, -1.0)
    tk, _ = _hy_tiles(seq_len)
    fwd = np.concatenate([np.concatenate([cosm[i:i + tk], sinm[i:i + tk]], axis=0)
                          for i in range(0, seq_len, tk)], axis=0)
    inv = np.concatenate([cosm, sinm.T], axis=1)
    t = np.linspace(0.0, 1.0, seq_len, dtype=np.float32)[:, None]
    bands = (FILTER_EMB - 1) // 2
    fb = np.linspace(1e-4, bands - 1, bands, dtype=np.float32)[None, :]
    w = (np.float32(2.0 * math.pi) * np.arange(seq_len, dtype=np.float32)[:, None] / np.float32(seq_len)).astype(np.float32)
    feats = np.concatenate([t, np.cos(fb * w), -np.sin(fb * w)], axis=-1).astype(np.float32)
    feats = np.ascontiguousarray(np.pad(feats, ((0, 0), (0, LANES - FILTER_EMB))).T)
    decay_hi = math.log(HY_DECAY_TARGET) / HY_DECAY_HI_PCT
    decay_lo = math.log(HY_DECAY_TARGET) / HY_DECAY_LO_PCT
    deltas = np.abs(np.linspace(decay_lo, decay_hi, HY_W, dtype=np.float32))
    window = np.exp(-t * deltas[None, :]).astype(np.float32)
    return fwd, inv, feats, window


def _hy_consts(seq_len):
    fwd, inv, feats, window = _hy_consts_np(seq_len)
    as_bf = lambda m: jnp.asarray(m, dtype=f32).astype(bf16)
    return {"fwd": as_bf(fwd), "inv": as_bf(inv), "feats": jnp.asarray(feats), "window": jnp.asarray(window)}


def _hyena(zh, lp, consts, batch, seq_len):
    zh = zh.reshape(batch, seq_len, 3 * HY_W)
    hb = _hy_filter(seq_len, consts, lp["hy_f_w1"], lp["hy_f_b1"], lp["hy_f_w2"], lp["hy_f_b2"],
                    lp["hy_f_w3"], lp["hy_freq"]).reshape(2, seq_len, HY_W)
    rz = _hy_fwd(consts["fwd"], zh, lp["hy_conv_w"], lp["hy_conv_b"], hb, batch, seq_len)
    yh = _hy_inv(consts["inv"], rz, zh, lp["hy_conv_w"], lp["hy_conv_b"], lp["hy_bias"], batch, seq_len)
    return yh.reshape(batch * seq_len, HY_W)


def _gelu(x):
    return 0.5 * x * (1.0 + jnp.tanh(math.sqrt(2.0 / math.pi) * (x + 0.044715 * (x * x * x))))


def _gmlp(z, ws_ref, bias):
    g = _gelu(z)
    outs = []
    for c in range(TM // GM_CHUNK):
        rows = slice(c * GM_CHUNK, (c + 1) * GM_CHUNK)
        u = g[rows, :GM_W]
        v = g[rows, GM_W:].astype(bf16)
        s = jnp.concatenate(
            [jnp.dot(ws_ref[h], v[:, h * GM_HEAD_DIM:(h + 1) * GM_HEAD_DIM], preferred_element_type=f32)
             for h in range(GM_HEADS)], axis=1)
        outs.append(u * (s + bias))
    return jnp.concatenate(outs, axis=0)


def _out_kernel(a_ref, yh_ref, zg_ref, ws_ref, gb_ref, x_ref, mod_ref, og_ref, w_ref, g2_ref, rw_ref,
                x1_ref, h2_ref, aff_ref):
    og = og_ref[...]
    yg = _gmlp(zg_ref[...].astype(f32), ws_ref, gb_ref[...])
    mixed_in = jnp.concatenate([
        _rms(a_ref[...].astype(f32)) * og[:, :ATTN_W],
        _rms(yh_ref[...].astype(f32)) * og[:, ATTN_W:ATTN_W + HY_W],
        _rms(yg) * og[:, ATTN_W + HY_W:]], axis=1)
    mixed = jnp.dot(mixed_in.astype(bf16), w_ref[...], preferred_element_type=f32)
    m = mod_ref[0]
    x1 = x_ref[...] + m[2:3] * mixed
    x1_ref[...] = x1
    h2 = _rms(x1) * g2_ref[...] * (1.0 + m[4:5]) + m[3:4]
    for j in range(ROW_CHUNKS):
        h2_ref[pl.ds(j, TM, stride=ROW_CHUNKS), :] = h2[:, j * LANES:(j + 1) * LANES]
    nt = (((1,), (1,)), ((), ()))
    h_hi = h2.astype(bf16)
    h_lo = (h2 - h_hi.astype(f32)).astype(bf16)
    rw = rw_ref[...]
    t = lax.dot_general(rw, h_hi, nt, preferred_element_type=f32)
    logits = (t[:N_EXPERTS] + t[N_EXPERTS:]) + lax.dot_general(rw[:N_EXPERTS], h_lo, nt, preferred_element_type=f32)
    e = jnp.exp(logits - jnp.max(logits, axis=0, keepdims=True))
    aff_ref[...] = e / jnp.sum(e, axis=0, keepdims=True)


def _out_proj(a, yh, zg, ws_bf, gm_b, x, mod, out_g, w_out_bf, g2, router_wt):
    n = x.shape[0]
    per_mod = n // mod.shape[0]
    row = lambda w: pl.BlockSpec((TM, w), lambda i: (i, 0))
    const = lambda s: pl.BlockSpec(s, lambda i: (0,) * len(s))
    gm_bias = jnp.repeat(gm_b.T, GM_HEAD_DIM, axis=1)
    return pl.pallas_call(
        _out_kernel,
        grid=(n // TM,),
        in_specs=[row(ATTN_W), row(HY_W), row(2 * GM_W), const((GM_HEADS, GM_CHUNK, GM_CHUNK)),
                  const((GM_CHUNK, GM_W)), row(D_MODEL),
                  pl.BlockSpec((1, 6, D_MODEL), lambda i: (i * TM // per_mod, 0, 0)),
                  const((1, MIX_W)), const((MIX_W, D_MODEL)), const((1, D_MODEL)), const((2 * N_EXPERTS, D_MODEL))],
        out_specs=[row(D_MODEL), pl.BlockSpec((TM * ROW_CHUNKS, LANES), lambda i: (i, 0)),
                   pl.BlockSpec((N_EXPERTS, TM), lambda i: (0, i))],
        out_shape=[jax.ShapeDtypeStruct((n, D_MODEL), f32), jax.ShapeDtypeStruct((n * ROW_CHUNKS, LANES), f32),
                   jax.ShapeDtypeStruct((N_EXPERTS, n), f32)],
        compiler_params=_params("parallel"),
        name="out_proj",
    )(a, yh, zg, ws_bf, gm_bias, x, mod, out_g.reshape(1, MIX_W), w_out_bf, g2.reshape(1, D_MODEL), router_wt)


RT_CHUNK = 512


def _prefix_incl(x01, tri):
    n = x01.shape[1]
    carry = jnp.zeros((x01.shape[0], 1), f32)
    parts = []
    for c in range(n // RT_CHUNK):
        piece = x01[:, c * RT_CHUNK:(c + 1) * RT_CHUNK]
        parts.append(jnp.dot(piece.astype(bf16), tri, preferred_element_type=f32) + carry)
        carry = carry + jnp.sum(piece, axis=1, keepdims=True)
    return jnp.concatenate(parts, axis=1)


def _route_kernel(cap, aff_ref, idx_ref, gate_ref):
    aff = aff_ref[...]
    n = aff.shape[1]

    def step(it, t):
        cand = t | (jnp.int32(1) << (30 - it))
        cnt = jnp.sum(jnp.where(aff >= lax.bitcast_convert_type(cand, f32), 1.0, 0.0), axis=1, keepdims=True)
        return jnp.where(cnt >= cap, cand, t)

    thr_bits = lax.fori_loop(0, 31, step, jnp.zeros((aff.shape[0], 1), jnp.int32))
    thr = lax.bitcast_convert_type(thr_bits, f32)
    gt = jnp.where(aff > thr, 1.0, 0.0)
    eq = jnp.where(aff == thr, 1.0, 0.0)
    room = cap - jnp.sum(gt, axis=1, keepdims=True)
    r = lax.broadcasted_iota(jnp.int32, (RT_CHUNK, RT_CHUNK), 0)
    c = lax.broadcasted_iota(jnp.int32, (RT_CHUNK, RT_CHUNK), 1)
    tri = jnp.where(r <= c, 1.0, 0.0).astype(bf16)
    sel = jnp.maximum(gt, jnp.where(_prefix_incl(eq, tri) <= room, eq, 0.0))
    slot = _prefix_incl(sel, tri) - 1.0

    tok = lax.broadcasted_iota(jnp.int32, aff.shape, 1)
    dist = jnp.where(sel > 0.0, tok - slot.astype(jnp.int32), 0)
    g = aff
    for b in range(max(1, (n - 1).bit_length())):
        sh = 1 << b
        dist_s = pltpu.roll(dist, n - sh, axis=1)
        take = (dist_s & sh) != 0
        leave = (dist & sh) != 0
        tok = jnp.where(take, pltpu.roll(tok, n - sh, axis=1), tok)
        g = jnp.where(take, pltpu.roll(g, n - sh, axis=1), g)
        dist = jnp.where(take, dist_s, jnp.where(leave, 0, dist))
    idx_ref[...] = tok[:, :cap] * ROW_CHUNKS
    gate_ref[...] = g[:, :cap]


def _route(aff_t, cap):
    return pl.pallas_call(
        functools.partial(_route_kernel, cap),
        in_specs=[pl.BlockSpec(memory_space=pltpu.VMEM)],
        out_specs=[pl.BlockSpec(memory_space=pltpu.VMEM)] * 2,
        out_shape=[jax.ShapeDtypeStruct((N_EXPERTS, cap), jnp.int32), jax.ShapeDtypeStruct((N_EXPERTS, cap), f32)],
        compiler_params=pltpu.CompilerParams(vmem_limit_bytes=VMEM_LIMIT),
        name="route",
    )(aff_t)


GATHER_UNROLL = 8
SCATTER_UNROLL = 8
CMB_EXPERTS = 2
CMB_TM = 1024


def _moe_ffn_kernel(cap, stride, idx_ref, h_ref, w1_ref, w3_ref, w2_ref, y_ref, tile_ref, xb_ref, acc_ref):
    e = pl.program_id(0)
    slab = ROW_CHUNKS * stride
    cur = pl.multiple_of((e % 2) * slab, SUBLANES)
    nxt = pl.multiple_of(((e + 1) % 2) * slab, SUBLANES)

    def gather_row(expert, slab_row, r):
        src = pl.multiple_of(idx_ref[expert * cap + r], ROW_CHUNKS)
        tile_ref[pl.ds(slab_row + r, ROW_CHUNKS, stride=stride), :] = h_ref[pl.ds(src, ROW_CHUNKS), :]

    @pl.when(e == 0)
    def _():
        def gather(q, carry):
            for t in range(GATHER_UNROLL):
                gather_row(0, cur, q * GATHER_UNROLL + t)
            return carry
        lax.fori_loop(0, cap // GATHER_UNROLL, gather, 0)

    xb_ref[...] = jnp.concatenate(
        [tile_ref[pl.ds(cur + j * stride, cap), :] for j in range(ROW_CHUNKS)], axis=1).astype(bf16)

    e_next = jnp.minimum(e + 1, N_EXPERTS - 1)
    n_f = D_EXPERT // MOE_F_TILE
    xb = xb_ref[...]
    for f in range(n_f):
        cols = slice(f * MOE_F_TILE, (f + 1) * MOE_F_TILE)
        a = jnp.dot(xb, w1_ref[0, 0, :, cols].astype(bf16), preferred_element_type=f32)
        b = jnp.dot(xb, w3_ref[0, 0, :, cols].astype(bf16), preferred_element_type=f32)
        he = (a * _sigmoid(a) * b).astype(bf16)
        part = jnp.dot(he, w2_ref[0, 0, cols, :].astype(bf16), preferred_element_type=f32)
        if f == 0:
            acc_ref[...] = part
        else:
            acc_ref[...] += part
        for r in range(f * cap // n_f, (f + 1) * cap // n_f):
            gather_row(e_next, nxt, r)

    y = acc_ref[...]
    for j in range(ROW_CHUNKS):
        y_ref[0, pl.ds(j, cap, stride=ROW_CHUNKS), :] = y[:, j * LANES:(j + 1) * LANES]


def _moe_ffn(layer, h3, idx, w1, w3, w2):
    cap = idx.shape[1]
    stride = cap + SUBLANES
    grid_spec = pltpu.PrefetchScalarGridSpec(
        num_scalar_prefetch=1,
        grid=(N_EXPERTS,),
        in_specs=[
            pl.BlockSpec(memory_space=pltpu.VMEM),
            pl.BlockSpec((1, 1, D_MODEL, D_EXPERT), lambda e, idx: (layer, e, 0, 0)),
            pl.BlockSpec((1, 1, D_MODEL, D_EXPERT), lambda e, idx: (layer, e, 0, 0)),
            pl.BlockSpec((1, 1, D_EXPERT, D_MODEL), lambda e, idx: (layer, e, 0, 0)),
        ],
        out_specs=pl.BlockSpec((1, ROW_CHUNKS * cap, LANES), lambda e, idx: (e, 0, 0)),
        scratch_shapes=[pltpu.VMEM((2 * ROW_CHUNKS * stride, LANES), f32),
                        pltpu.VMEM((cap, D_MODEL), bf16),
                        pltpu.VMEM((cap, D_MODEL), f32)],
    )
    return pl.pallas_call(
        functools.partial(_moe_ffn_kernel, cap, stride),
        grid_spec=grid_spec,
        out_shape=jax.ShapeDtypeStruct((N_EXPERTS, ROW_CHUNKS * cap, LANES), f32),
        compiler_params=_params("arbitrary"),
        name="moe_ffn",
    )(idx.reshape(-1), h3, w1, w3, w2)


def _moe_combine_kernel(cap, final, idx_ref, gate_ref, y_ref, x_ref, mod_ref, g_ref, o_ref, acc_ref):
    s = pl.program_id(0)
    scatter_steps = N_EXPERTS // CMB_EXPERTS

    @pl.when(s == 0)
    def _():
        acc_ref[...] = jnp.zeros_like(acc_ref)

    @pl.when(s < scatter_steps)
    def _():
        for k in range(CMB_EXPERTS):
            base = (s * CMB_EXPERTS + k) * cap

            def scatter(q, carry, k=k, base=base):
                group = SCATTER_UNROLL * ROW_CHUNKS
                y = y_ref[k, pl.ds(pl.multiple_of(q * group, group), group), :]
                rows = [pl.multiple_of(idx_ref[base + q * SCATTER_UNROLL + t], ROW_CHUNKS)
                        for t in range(SCATTER_UNROLL)]
                new = [acc_ref[pl.ds(rows[t], ROW_CHUNKS), :]
                       + gate_ref[base + q * SCATTER_UNROLL + t] * y[t * ROW_CHUNKS:(t + 1) * ROW_CHUNKS]
                       for t in range(SCATTER_UNROLL)]
                for t in range(SCATTER_UNROLL):
                    acc_ref[pl.ds(rows[t], ROW_CHUNKS), :] = new[t]
                return carry
            lax.fori_loop(0, cap // SCATTER_UNROLL, scatter, 0)

    @pl.when(s >= scatter_steps)
    def _():
        first = pl.multiple_of((s - scatter_steps) * (CMB_TM * ROW_CHUNKS), CMB_TM * ROW_CHUNKS)
        moe = jnp.concatenate([acc_ref[pl.ds(first + j, CMB_TM, stride=ROW_CHUNKS), :] for j in range(ROW_CHUNKS)],
                              axis=1)
        x = x_ref[...] + mod_ref[0][5:6] * moe
        o_ref[...] = _rms(x) * g_ref[...] if final else x


def _moe_combine(idx, gate, y_cm, x1, mod, final_g, final):
    n = x1.shape[0]
    cap = idx.shape[1]
    per_mod = n // mod.shape[0]
    scatter_steps = N_EXPERTS // CMB_EXPERTS
    tile = lambda s: jnp.maximum(s - scatter_steps, 0)
    row = pl.BlockSpec((CMB_TM, D_MODEL), lambda s, *_: (tile(s), 0))
    grid_spec = pltpu.PrefetchScalarGridSpec(
        num_scalar_prefetch=2,
        grid=(scatter_steps + n // CMB_TM,),
        in_specs=[pl.BlockSpec((CMB_EXPERTS, ROW_CHUNKS * cap, LANES),
                               lambda s, *_: (jnp.minimum(s, scatter_steps - 1), 0, 0)),
                  row,
                  pl.BlockSpec((1, 6, D_MODEL), lambda s, *_: (tile(s) * CMB_TM // per_mod, 0, 0)),
                  pl.BlockSpec((1, D_MODEL), lambda s, *_: (0, 0))],
        out_specs=row,
        scratch_shapes=[pltpu.VMEM((n * ROW_CHUNKS, LANES), f32)],
    )
    return pl.pallas_call(
        functools.partial(_moe_combine_kernel, cap, final),
        grid_spec=grid_spec,
        out_shape=jax.ShapeDtypeStruct((n, D_MODEL), f32),
        compiler_params=_params("arbitrary"),
        name="moe_combine",
    )(idx.reshape(-1), gate.reshape(-1), y_cm, x1, mod, final_g.reshape(1, D_MODEL))


def _expert_choice(layer, h3, aff_t, x1, mod, final_g, final, w1, w3, w2):
    n = x1.shape[0]
    cap = max(1, EC_CAPACITY * n // N_EXPERTS)
    idx, gate = _route(aff_t, cap)
    y_cm = _moe_ffn(layer, h3, idx, w1, w3, w2)
    return _moe_combine(idx, gate, y_cm, x1, mod, final_g, final)


def _split_bf16(w):
    hi = w.astype(bf16)
    lo = (w - hi.astype(f32)).astype(bf16)
    return jnp.concatenate([hi, lo], axis=0)


def _stream(x, mods, layers, experts, batch, seq_len, final_g, caches=None):
    consts = _hy_consts(seq_len)
    rope_tabs = _rope_tables(seq_len) if caches is not None else None
    kvs = []
    for l, lp in enumerate(layers):
        qkv, zh, zg = _in_proj(x, mods[l], lp["norm1_g"], lp["w_in"], seq_len, rope_tabs=rope_tabs)
        if caches is None:
            a = _ctx_attention(qkv, lp["attn_sink"], batch, seq_len)
            kvs.append((qkv[:, ATTN_W:ATTN_W + KV_W], qkv[:, ATTN_W + KV_W:]))
        else:
            a = _lat_attention(qkv, lp["attn_sink"], caches[0][:, l], caches[1][:, l], batch, seq_len)
        yh = _hyena(zh, lp, consts, batch, seq_len)
        x1, h3, aff_t = _out_proj(a, yh, zg, lp["gm_ws"], lp["gm_b"], x, mods[l], lp["out_norm_g"], lp["w_out"],
                                  lp["norm2_g"], lp["router_wt"])
        x = _expert_choice(l, h3, aff_t, x1, mods[l], final_g, l == len(layers) - 1, *experts)
    return x, kvs


def kernel(x_prompt, x_sample, c, cache_k, cache_v, c_ctx, norm1_g, norm2_g, ada_w, ada_b, w_in, attn_sink,
           hy_conv_w, hy_conv_b, hy_f_w1, hy_f_b1, hy_f_w2, hy_f_b2, hy_f_w3, hy_freq, hy_bias, gm_ws, gm_b,
           out_norm_g, w_out, router_w, exp_w1, exp_w3, exp_w2, final_g):
    batch, seq, _ = x_prompt.shape
    dbatch, dseq, _ = x_sample.shape
    past = cache_k.shape[2]

    cvec = jnp.concatenate([c_ctx[None], c, jnp.zeros((SUBLANES - 1 - dbatch, D_MODEL), f32)], axis=0)
    mod = _ada(cvec, ada_w, ada_b)
    mods_ctx = [mod[l, 0:1].reshape(1, 6, D_MODEL) for l in range(DEPTH)]
    mods_lat = [mod[l, 1:1 + dbatch].reshape(dbatch, 6, D_MODEL) for l in range(DEPTH)]

    w_in_bf, w_out_bf, gm_ws_bf = w_in.astype(bf16), w_out.astype(bf16), gm_ws.astype(bf16)
    layers = []
    for l in range(DEPTH):
        layers.append({
            "norm1_g": norm1_g[l], "norm2_g": norm2_g[l], "w_in": w_in_bf[l], "attn_sink": attn_sink[l],
            "hy_conv_w": hy_conv_w[l], "hy_conv_b": hy_conv_b[l], "hy_f_w1": hy_f_w1[l], "hy_f_b1": hy_f_b1[l],
            "hy_f_w2": hy_f_w2[l], "hy_f_b2": hy_f_b2[l], "hy_f_w3": hy_f_w3[l], "hy_freq": hy_freq[l],
            "hy_bias": hy_bias[l], "gm_ws": gm_ws_bf[l], "gm_b": gm_b[l], "out_norm_g": out_norm_g[l],
            "w_out": w_out_bf[l], "router_wt": _split_bf16(router_w[l].T)})
    experts = (exp_w1, exp_w3, exp_w2)

    yp, kvs = _stream(x_prompt.reshape(batch * seq, D_MODEL), mods_ctx, layers, experts, batch, seq, final_g)
    caches = (cache_k.reshape(dbatch, DEPTH, past, KV_W), cache_v.reshape(dbatch, DEPTH, past, KV_W))
    ys, _ = _stream(x_sample.reshape(dbatch * dseq, D_MODEL), mods_lat, layers, experts, dbatch, dseq, final_g,
                   caches=caches)

    new_k = jnp.stack([k.reshape(batch, seq, N_KV, HEAD_DIM) for k, _ in kvs], axis=1)
    new_v = jnp.stack([v.reshape(batch, seq, N_KV, HEAD_DIM) for _, v in kvs], axis=1)
    return (yp.reshape(batch, seq, D_MODEL), ys.reshape(dbatch, dseq, D_MODEL), new_k, new_v)
```

```python
import functools
import math

import numpy as np
import jax
import jax.numpy as jnp
from jax import lax
from jax.experimental import pallas as pl
from jax.experimental.pallas import tpu as pltpu

f32 = jnp.float32
bf16 = jnp.bfloat16

D_MODEL = 1024
DEPTH = 2
GRID_W = 64
BLOCK = 128
N_HEADS = 8
N_KV = 2
HEAD_DIM = 64
Q_GROUP = N_HEADS // N_KV
ATTN_W = N_HEADS * HEAD_DIM
KV_W = N_KV * HEAD_DIM
QKV_W = ATTN_W + 2 * KV_W
HY_W = 256
GM_W = 256
GM_HEADS = 4
GM_HEAD_DIM = GM_W // GM_HEADS
GM_CHUNK = 128
MIX_W = ATTN_W + HY_W + GM_W
IN_W = ATTN_W + 2 * KV_W + 3 * HY_W + 2 * GM_W
FILTER_EMB = 33
FILTER_HID = 64
HY_DECAY_HI_PCT = 0.3
HY_DECAY_LO_PCT = 1.5
HY_DECAY_TARGET = 1e-2
N_EXPERTS = 16
EC_CAPACITY = 2
D_EXPERT = 1024
ROPE_THETA = 10000.0
EPS = 1e-6
NEG = -1e30

LANES = 128
SUBLANES = 8
ROW_CHUNKS = D_MODEL // LANES
VMEM_LIMIT = 56 * 1024 * 1024
TM = 512
IN_TM = 512
MOE_F_TILE = 256


def _params(*sem):
    return pltpu.CompilerParams(dimension_semantics=sem, vmem_limit_bytes=VMEM_LIMIT)


def _rms(x):
    return x * lax.rsqrt(jnp.mean(x * x, axis=-1, keepdims=True) + EPS)


def _sigmoid(x):
    return 1.0 / (1.0 + jnp.exp(-x))


def _ada_kernel(c_ref, w_ref, b_ref, o_ref):
    c = c_ref[...]
    s = (c * _sigmoid(c)).astype(bf16)
    o_ref[0] = jnp.dot(s, w_ref[0].astype(bf16), preferred_element_type=f32) + b_ref[0]


def _ada(cvec, ada_w, ada_b):
    nt = 6
    return pl.pallas_call(
        _ada_kernel,
        grid=(DEPTH, nt),
        in_specs=[
            pl.BlockSpec((SUBLANES, D_MODEL), lambda l, j: (0, 0)),
            pl.BlockSpec((1, D_MODEL, D_MODEL), lambda l, j: (l, 0, j)),
            pl.BlockSpec((1, 1, D_MODEL), lambda l, j: (l, 0, j)),
        ],
        out_specs=pl.BlockSpec((1, SUBLANES, D_MODEL), lambda l, j: (l, 0, j)),
        out_shape=jax.ShapeDtypeStruct((DEPTH, SUBLANES, 6 * D_MODEL), f32),
        compiler_params=_params("arbitrary", "arbitrary"),
        name="ada",
    )(cvec, ada_w, ada_b.reshape(DEPTH, 1, 6 * D_MODEL))


def _rope_swap(x):
    w = x.shape[-1]
    lane = lax.broadcasted_iota(jnp.int32, x.shape, 1)
    first = (lane % 32) < 16
    return jnp.where(first, pltpu.roll(x, w - 16, axis=1), pltpu.roll(x, 16, axis=1))


def _in_kernel(rope, cache_rows, n_alias, x_ref, mod_ref, g_ref, w_ref, *refs):
    refs = list(refs)
    if rope:
        cos_ref, sin_ref = refs[:2]
        refs = refs[2:]
    refs = refs[n_alias:]
    qkv_ref, zh_ref, zg_ref = refs[:3]
    m = mod_ref[0]
    h = _rms(x_ref[...]) * g_ref[...] * (1.0 + m[1:2]) + m[0:1]
    z = jnp.dot(h.astype(bf16), w_ref[...], preferred_element_type=f32)
    if rope:
        reps = (ATTN_W + KV_W) // LANES
        cos = jnp.concatenate([cos_ref[...]] * reps, axis=1)
        sin = jnp.concatenate([sin_ref[...]] * reps, axis=1)
        qk = z[:, :ATTN_W + KV_W]
        qkv_ref[:, :ATTN_W + KV_W] = qk * cos + _rope_swap(qk) * sin
        qkv_ref[:, ATTN_W + KV_W:] = z[:, ATTN_W + KV_W:QKV_W]
    else:
        qkv_ref[...] = z[:, :QKV_W]
    zh_ref[...] = z[:, QKV_W:QKV_W + 3 * HY_W].astype(bf16)
    zg_ref[...] = z[:, QKV_W + 3 * HY_W:].astype(bf16)
    if cache_rows:
        k_ref, v_ref = refs[3:]
        for s in range(z.shape[0] // cache_rows):
            rows = slice(s * cache_rows, (s + 1) * cache_rows)
            k_ref[s, 0] = z[rows, ATTN_W:ATTN_W + KV_W]
            v_ref[s, 0] = z[rows, ATTN_W + KV_W:QKV_W]


def _in_proj(x, mod, g1, w_in_bf, seq_len, rope_tabs=None, cache_layer=None, caches_so_far=None):
    n = x.shape[0]
    rope = rope_tabs is not None
    per_mod = n // mod.shape[0]
    tm = IN_TM
    in_specs = [pl.BlockSpec((tm, D_MODEL), lambda i: (i, 0)),
                pl.BlockSpec((1, 6, D_MODEL), lambda i: (i * tm // per_mod, 0, 0)),
                pl.BlockSpec((1, D_MODEL), lambda i: (0, 0)),
                pl.BlockSpec((D_MODEL, IN_W), lambda i: (0, 0))]
    args = [x, mod, g1.reshape(1, D_MODEL), w_in_bf]
    if rope:
        nt = seq_len // tm
        tab = pl.BlockSpec((tm, LANES), lambda i: (i % nt, 0))
        in_specs += [tab, tab]
        args += list(rope_tabs)
    out_specs = [pl.BlockSpec((tm, w), lambda i: (i, 0)) for w in (QKV_W, 3 * HY_W, 2 * GM_W)]
    out_shape = [jax.ShapeDtypeStruct((n, QKV_W), f32), jax.ShapeDtypeStruct((n, 3 * HY_W), bf16),
                 jax.ShapeDtypeStruct((n, 2 * GM_W), bf16)]
    aliases, n_alias = {}, 0
    if cache_layer is not None:
        seqs = tm // seq_len
        cache = pl.BlockSpec((seqs, 1, seq_len, KV_W), lambda i: (i, cache_layer, 0, 0))
        out_specs += [cache, cache]
        out_shape += [jax.ShapeDtypeStruct((n // seq_len, DEPTH, seq_len, KV_W), f32)] * 2
        if caches_so_far is not None:
            n_alias = 2
            aliases = {len(args): 3, len(args) + 1: 4}
            in_specs += [pl.BlockSpec(memory_space=pl.ANY)] * 2
            args += list(caches_so_far)
    return pl.pallas_call(
        functools.partial(_in_kernel, rope, seq_len if cache_layer is not None else 0, n_alias),
        grid=(n // tm,),
        in_specs=in_specs,
        out_specs=out_specs,
        out_shape=out_shape,
        input_output_aliases=aliases,
        compiler_params=_params("parallel"),
        name="in_proj",
    )(*args)


def _rope_tables(seq_len):
    nf = HEAD_DIM // 4
    t = np.arange(seq_len)
    inv = (ROPE_THETA ** (-np.arange(nf, dtype=np.float32) / nf)).astype(np.float32)
    d = np.arange(HEAD_DIM)
    pos = np.where((d // 32)[None, :] == 0, (t // GRID_W)[:, None], (t % GRID_W)[:, None]).astype(np.float32)
    ang = (pos * inv[d % nf][None, :]).astype(np.float32)
    cos = np.cos(ang).astype(np.float32)
    sin = np.sin(ang).astype(np.float32) * np.where((d % 32) < 16, -1.0, 1.0)[None, :].astype(np.float32)
    return jnp.asarray(np.tile(cos, (1, 2))), jnp.asarray(np.tile(sin, (1, 2)))


LOG2E = math.log2(math.e)


def _stack_queries(z, sink_ref, kv):
    rows = z.shape[0]
    heads = range(kv * Q_GROUP, (kv + 1) * Q_GROUP)
    qs = jnp.concatenate([z[:, h * HEAD_DIM:(h + 1) * HEAD_DIM] for h in heads], axis=0)
    sink = jnp.concatenate([jnp.full((1, rows), sink_ref[h] * LOG2E, f32) for h in heads], axis=1)
    return (qs * (LOG2E / math.sqrt(HEAD_DIM))).astype(bf16), sink


def _scores_t(k, qs_bf):
    return lax.dot_general(k.astype(bf16), qs_bf, (((1,), (1,)), ((), ())), preferred_element_type=f32)


def _softmax_pv_t(st, sink, vals, rows):
    m = jnp.maximum(jnp.max(st, axis=0, keepdims=True), sink)
    pt = jnp.exp2(st - m).astype(bf16)
    v_bf = vals.astype(bf16)
    v_ext = jnp.concatenate([v_bf, jnp.ones_like(v_bf)], axis=1)
    ovt = lax.dot_general(v_ext, pt, (((0,), (0,)), ((), ())), preferred_element_type=f32)
    ot = ovt[:HEAD_DIM] / (ovt[HEAD_DIM:HEAD_DIM + 1] + jnp.exp2(sink - m))
    return [ot[:, g * rows:(g + 1) * rows].T for g in range(Q_GROUP)]


CTX_BATCHES = 4
LAT_QBLOCKS = 8


def _ctx_attn_kernel(seq_len, sink_ref, z_ref, o_ref):
    for s in range(CTX_BATCHES):
        z = z_ref[s * seq_len:(s + 1) * seq_len, :]
        outs = []
        for kv in range(N_KV):
            k = z[:, ATTN_W + kv * HEAD_DIM:ATTN_W + (kv + 1) * HEAD_DIM]
            v = z[:, ATTN_W + KV_W + kv * HEAD_DIM:ATTN_W + KV_W + (kv + 1) * HEAD_DIM]
            qs, sink = _stack_queries(z, sink_ref, kv)
            outs += _softmax_pv_t(_scores_t(k, qs), sink, v, seq_len)
        o_ref[s * seq_len:(s + 1) * seq_len, :] = jnp.concatenate(outs, axis=1).astype(o_ref.dtype)


def _ctx_attention(z, sink, batch, seq_len):
    rows = CTX_BATCHES * seq_len
    return pl.pallas_call(
        functools.partial(_ctx_attn_kernel, seq_len),
        grid=(batch // CTX_BATCHES,),
        in_specs=[pl.BlockSpec(memory_space=pltpu.SMEM),
                  pl.BlockSpec((rows, QKV_W), lambda b: (b, 0))],
        out_specs=pl.BlockSpec((rows, ATTN_W), lambda b: (b, 0)),
        out_shape=jax.ShapeDtypeStruct((batch * seq_len, ATTN_W), bf16),
        compiler_params=_params("parallel"),
        name="ctx_attn",
    )(sink, z)


def _lat_attn_kernel(nb, sink_ref, zp_ref, zc_ref, zn_ref, ck_ref, cv_ref, o_ref):
    step = pl.program_id(1)
    zc_all = zc_ref[...]
    blocks = [zp_ref[...]] + [zc_all[q * BLOCK:(q + 1) * BLOCK] for q in range(LAT_QBLOCKS)] + [zn_ref[...]]
    ck, cv = ck_ref[0], cv_ref[0]
    width = Q_GROUP * BLOCK
    j = lax.broadcasted_iota(jnp.int32, (BLOCK, width), 0)
    r = lax.broadcasted_iota(jnp.int32, (BLOCK, width), 1) % BLOCK
    for q in range(LAT_QBLOCKS):
        i = step * LAT_QBLOCKS + q
        zp, zc, zn = blocks[q], blocks[q + 1], blocks[q + 2]
        ok_prev = j >= r + jnp.where(i > 0, 0, BLOCK)
        ok_next = j <= r - jnp.where(i < nb - 1, 0, BLOCK)
        outs = []
        for kv in range(N_KV):
            ks = slice(ATTN_W + kv * HEAD_DIM, ATTN_W + (kv + 1) * HEAD_DIM)
            vs = slice(ATTN_W + KV_W + kv * HEAD_DIM, ATTN_W + KV_W + (kv + 1) * HEAD_DIM)
            cs = slice(kv * HEAD_DIM, (kv + 1) * HEAD_DIM)
            qs, sink = _stack_queries(zc, sink_ref, kv)
            st = jnp.concatenate([
                jnp.where(ok_prev, _scores_t(zp[:, ks], qs), NEG),
                _scores_t(zc[:, ks], qs),
                jnp.where(ok_next, _scores_t(zn[:, ks], qs), NEG),
                _scores_t(ck[:, cs], qs)], axis=0)
            vals = jnp.concatenate([zp[:, vs], zc[:, vs], zn[:, vs], cv[:, cs]], axis=0)
            outs += _softmax_pv_t(st, sink, vals, BLOCK)
        o_ref[q * BLOCK:(q + 1) * BLOCK, :] = jnp.concatenate(outs, axis=1).astype(o_ref.dtype)


def _lat_attention(z, sink, ck, cv, batch, seq_len):
    nb = seq_len // BLOCK
    ns = nb // LAT_QBLOCKS
    blk = lambda f: pl.BlockSpec((BLOCK, QKV_W), f)
    past = ck.shape[1]
    cache = pl.BlockSpec((1, past, KV_W), lambda b, i: (b, 0, 0))
    return pl.pallas_call(
        functools.partial(_lat_attn_kernel, nb),
        grid=(batch, ns),
        in_specs=[pl.BlockSpec(memory_space=pltpu.SMEM),
                  blk(lambda b, i: (b * nb + jnp.maximum(i * LAT_QBLOCKS - 1, 0), 0)),
                  pl.BlockSpec((LAT_QBLOCKS * BLOCK, QKV_W), lambda b, i: (b * ns + i, 0)),
                  blk(lambda b, i: (b * nb + jnp.minimum((i + 1) * LAT_QBLOCKS, nb - 1), 0)),
                  cache, cache],
        out_specs=pl.BlockSpec((LAT_QBLOCKS * BLOCK, ATTN_W), lambda b, i: (b * ns + i, 0)),
        out_shape=jax.ShapeDtypeStruct((batch * seq_len, ATTN_W), bf16),
        compiler_params=_params("parallel", "parallel"),
        name="lat_attn",
    )(sink, z, z, z, ck, cv)


HY_TL = 256
HALO_ROWS = 2 * SUBLANES


def _short_conv(z, prev_tile, next_tile, has_prev, has_next, w, b):
    z = z.astype(f32)
    rows = z.shape[0]
    row = lax.broadcasted_iota(jnp.int32, (SUBLANES, z.shape[1]), 0)
    prev_row = jnp.where(has_prev, prev_tile.astype(f32)[HALO_ROWS - 1:HALO_ROWS, :], 0.0)
    next_row = jnp.where(has_next, next_tile.astype(f32)[0:1, :], 0.0)
    z_prev = pltpu.roll(z, 1, axis=0)
    z_prev = jnp.concatenate([jnp.where(row == 0, prev_row, z_prev[:SUBLANES]), z_prev[SUBLANES:]], axis=0)
    z_next = pltpu.roll(z, rows - 1, axis=0)
    z_next = jnp.concatenate([z_next[:rows - SUBLANES],
                              jnp.where(row == SUBLANES - 1, next_row, z_next[rows - SUBLANES:])], axis=0)
    zc = z_prev * w[0:1] + z * w[1:2] + z_next * w[2:3] + b
    return zc[:, :HY_W], zc[:, HY_W:2 * HY_W] * zc[:, 2 * HY_W:]


def _hy_filter_kernel(seq_len, feats_ref, win_ref, w1_ref, b1_ref, w2_ref, b2_ref, w3_ref, fr_ref, o_ref):
    hi = lax.Precision.HIGHEST
    fr = fr_ref[...]
    h = jnp.sin(fr * (jnp.dot(w1_ref[...], feats_ref[...], precision=hi, preferred_element_type=f32) + b1_ref[...]))
    h = jnp.sin(fr * (jnp.dot(w2_ref[...], h, precision=hi, preferred_element_type=f32) + b2_ref[...]))
    h = jnp.dot(h.T, w3_ref[...], precision=hi, preferred_element_type=f32)
    win = win_ref[...]
    row = lax.broadcasted_iota(jnp.int32, (seq_len, HY_W), 0)
    hf = h[:, :HY_W] * win
    hb = jnp.where(row == 0, 0.0, h[:, HY_W:] * win)
    o_ref[0:seq_len, :] = hf + hb
    o_ref[seq_len:, :] = hb - hf


def _hy_filter(seq_len, consts, w1, b1, w2, b2, w3, freq):
    w1t = jnp.zeros((FILTER_HID, LANES), f32).at[:, :FILTER_EMB].set(w1.T)
    args = (consts["feats"], consts["window"], w1t, b1.reshape(-1, 1), w2.T, b2.reshape(-1, 1), w3, freq.reshape(-1, 1))
    return pl.pallas_call(
        functools.partial(_hy_filter_kernel, seq_len),
        in_specs=[pl.BlockSpec(memory_space=pltpu.VMEM)] * len(args),
        out_specs=pl.BlockSpec(memory_space=pltpu.VMEM),
        out_shape=jax.ShapeDtypeStruct((2 * seq_len, HY_W), f32),
        compiler_params=pltpu.CompilerParams(vmem_limit_bytes=VMEM_LIMIT),
        name="hy_filter",
    )(*args)


def _hy_tiles(seq_len):
    return min(TM, seq_len // 2), min(TM, seq_len)


def _hy_fwd_kernel(nb, tk, seq_len, m_ref, zh_ref, cw_ref, cb_ref, hb_ref, rz_ref, rhs_ref):
    i = pl.program_id(0)
    nc = seq_len // HY_TL

    @pl.when(i == 0)
    def _():
        w, bias = cw_ref[...], cb_ref[...]
        for b in range(nb):
            def chunk(c, carry, b=b):
                r0 = pl.multiple_of(c * HY_TL, HY_TL)
                prev = zh_ref[b, pl.ds(pl.multiple_of(jnp.maximum(r0 - HALO_ROWS, 0), HALO_ROWS), HALO_ROWS), :]
                nxt = zh_ref[b, pl.ds(pl.multiple_of(jnp.minimum(r0 + HY_TL, seq_len - HALO_ROWS), HALO_ROWS),
                                      HALO_ROWS), :]
                _, u = _short_conv(zh_ref[b, pl.ds(r0, HY_TL), :], prev, nxt, c > 0, c < nc - 1, w, bias)
                rhs_ref[pl.ds(r0, HY_TL), b * HY_W:(b + 1) * HY_W] = u.astype(bf16)
                return carry
            lax.fori_loop(0, nc, chunk, 0)
        rhs_ref[:, nb * HY_W:(nb + 1) * HY_W] = hb_ref[0].astype(bf16)
        rhs_ref[:, (nb + 1) * HY_W:] = hb_ref[1].astype(bf16)

    c0 = nb * HY_W
    acc = jnp.dot(m_ref[...], rhs_ref[:, :c0], preferred_element_type=f32)
    p, w = acc[:tk], acc[tk:]
    hr = jnp.dot(m_ref[:tk, :], rhs_ref[:, c0:c0 + HY_W], preferred_element_type=f32)
    hi = jnp.dot(m_ref[tk:, :], rhs_ref[:, c0 + HY_W:], preferred_element_type=f32)
    qs = jnp.dot(m_ref[tk:tk + HALO_ROWS, :], rhs_ref[:, c0:c0 + HY_W], preferred_element_type=f32)[0:1]
    first = (lax.broadcasted_iota(jnp.int32, (tk, HY_W), 0) == 0) & (i == 0)
    for b in range(nb):
        cols = slice(b * HY_W, (b + 1) * HY_W)
        pu, wu = p[:, cols], w[:, cols]
        rz_ref[0, :, cols] = jnp.where(first, pu * hr, 2.0 * (pu * hr + wu * hi)).astype(bf16)
        rz_ref[1, :, cols] = jnp.where(first, wu * qs, 2.0 * (wu * hr - pu * hi)).astype(bf16)


def _hy_fwd(mat, zh, conv_w, conv_b, hb, batch, seq_len):
    tk, _ = _hy_tiles(seq_len)
    return pl.pallas_call(
        functools.partial(_hy_fwd_kernel, batch, tk, seq_len),
        grid=(seq_len // tk,),
        in_specs=[pl.BlockSpec((2 * tk, seq_len), lambda i: (i, 0)),
                  pl.BlockSpec((batch, seq_len, 3 * HY_W), lambda i: (0, 0, 0)),
                  pl.BlockSpec((3, 3 * HY_W), lambda i: (0, 0)),
                  pl.BlockSpec((1, 3 * HY_W), lambda i: (0, 0)),
                  pl.BlockSpec((2, seq_len, HY_W), lambda i: (0, 0, 0))],
        out_specs=pl.BlockSpec((2, tk, batch * HY_W), lambda i: (0, i, 0)),
        out_shape=jax.ShapeDtypeStruct((2, seq_len, batch * HY_W), bf16),
        scratch_shapes=[pltpu.VMEM((seq_len, (batch + 2) * HY_W), bf16)],
        compiler_params=_params("arbitrary"),
        name="hy_fwd",
    )(mat, zh, conv_w, conv_b.reshape(1, 3 * HY_W), hb)


def _hy_inv_kernel(nb, seq_len, nt, m_ref, rz_ref, zp_ref, zc_ref, zn_ref, cw_ref, cb_ref, d_ref, o_ref):
    i = pl.program_id(0)
    acc = jnp.dot(m_ref[...], rz_ref[...], preferred_element_type=f32)
    d, w, bias = d_ref[...], cw_ref[...], cb_ref[...]
    for b in range(nb):
        x0, u = _short_conv(zc_ref[b], zp_ref[b], zn_ref[b], i > 0, i < nt - 1, w, bias)
        y = acc[:, b * HY_W:(b + 1) * HY_W] * (1.0 / (2 * seq_len)) + u * d
        o_ref[b] = (x0 * y).astype(o_ref.dtype)


def _hy_inv(mat, rz, zh, conv_w, conv_b, d_bias, batch, seq_len):
    _, tm = _hy_tiles(seq_len)
    nt = seq_len // tm
    rh = tm // HALO_ROWS
    last = seq_len // HALO_ROWS - 1
    halo = lambda f: pl.BlockSpec((batch, HALO_ROWS, 3 * HY_W), f)
    return pl.pallas_call(
        functools.partial(_hy_inv_kernel, batch, seq_len, nt),
        grid=(nt,),
        in_specs=[pl.BlockSpec((tm, 2 * seq_len), lambda i: (i, 0)),
                  pl.BlockSpec((2 * seq_len, batch * HY_W), lambda i: (0, 0)),
                  halo(lambda i: (0, jnp.maximum(i * rh - 1, 0), 0)),
                  pl.BlockSpec((batch, tm, 3 * HY_W), lambda i: (0, i, 0)),
                  halo(lambda i: (0, jnp.minimum((i + 1) * rh, last), 0)),
                  pl.BlockSpec((3, 3 * HY_W), lambda i: (0, 0)),
                  pl.BlockSpec((1, 3 * HY_W), lambda i: (0, 0)),
                  pl.BlockSpec((1, HY_W), lambda i: (0, 0))],
        out_specs=pl.BlockSpec((batch, tm, HY_W), lambda i: (0, i, 0)),
        out_shape=jax.ShapeDtypeStruct((batch, seq_len, HY_W), bf16),
        compiler_params=_params("parallel"),
        name="hy_inv",
    )(mat, rz.reshape(2 * seq_len, batch * HY_W), zh, zh, zh, conv_w, conv_b.reshape(1, 3 * HY_W),
      d_bias.reshape(1, HY_W))


@functools.lru_cache(maxsize=None)
def _hy_consts_np(seq_len):
    n = 2 * seq_len
    k = np.arange(seq_len, dtype=np.int64)
    ang = (2.0 * np.pi / n) * ((k[:, None] * k[None, :]) % n).astype(np.float64)
    cosm = np.cos(ang)
    sinm = np.sin(ang)
    sinm[0, :] = np.where(k % 2 == 0, 1.0, -1.0)
    tk, _ = _hy_tiles(seq_len)
    fwd = np.concatenate([np.concatenate([cosm[i:i + tk], sinm[i:i + tk]], axis=0)
                          for i in range(0, seq_len, tk)], axis=0)
    inv = np.concatenate([cosm, sinm.T], axis=1)
    t = np.linspace(0.0, 1.0, seq_len, dtype=np.float32)[:, None]
    bands = (FILTER_EMB - 1) // 2
    fb = np.linspace(1e-4, bands - 1, bands, dtype=np.float32)[None, :]
    w = (np.float32(2.0 * math.pi) * np.arange(seq_len, dtype=np.float32)[:, None] / np.float32(seq_len)).astype(np.float32)
    feats = np.concatenate([t, np.cos(fb * w), -np.sin(fb * w)], axis=-1).astype(np.float32)
    feats = np.ascontiguousarray(np.pad(feats, ((0, 0), (0, LANES - FILTER_EMB))).T)
    decay_hi = math.log(HY_DECAY_TARGET) / HY_DECAY_HI_PCT
    decay_lo = math.log(HY_DECAY_TARGET) / HY_DECAY_LO_PCT
    deltas = np.abs(np.linspace(decay_lo, decay_hi, HY_W, dtype=np.float32))
    window = np.exp(-t * deltas[None, :]).astype(np.float32)
    return fwd, inv, feats, window


def _hy_consts(seq_len):
    fwd, inv, feats, window = _hy_consts_np(seq_len)
    as_bf = lambda m: jnp.asarray(m, dtype=f32).astype(bf16)
    return {"fwd": as_bf(fwd), "inv": as_bf(inv), "feats": jnp.asarray(feats), "window": jnp.asarray(window)}


def _hyena(zh, lp, consts, batch, seq_len):
    zh = zh.reshape(batch, seq_len, 3 * HY_W)
    hb = _hy_filter(seq_len, consts, lp["hy_f_w1"], lp["hy_f_b1"], lp["hy_f_w2"], lp["hy_f_b2"],
                    lp["hy_f_w3"], lp["hy_freq"]).reshape(2, seq_len, HY_W)
    rz = _hy_fwd(consts["fwd"], zh, lp["hy_conv_w"], lp["hy_conv_b"], hb, batch, seq_len)
    yh = _hy_inv(consts["inv"], rz, zh, lp["hy_conv_w"], lp["hy_conv_b"], lp["hy_bias"], batch, seq_len)
    return yh.reshape(batch * seq_len, HY_W)


def _gelu(x):
    return 0.5 * x * (1.0 + jnp.tanh(math.sqrt(2.0 / math.pi) * (x + 0.044715 * (x * x * x))))


def _gmlp(z, ws_ref, bias):
    g = _gelu(z)
    outs = []
    for c in range(TM // GM_CHUNK):
        rows = slice(c * GM_CHUNK, (c + 1) * GM_CHUNK)
        u = g[rows, :GM_W]
        v = g[rows, GM_W:].astype(bf16)
        s = jnp.concatenate(
            [jnp.dot(ws_ref[h], v[:, h * GM_HEAD_DIM:(h + 1) * GM_HEAD_DIM], preferred_element_type=f32)
             for h in range(GM_HEADS)], axis=1)
        outs.append(u * (s + bias))
    return jnp.concatenate(outs, axis=0)


def _out_kernel(a_ref, yh_ref, zg_ref, ws_ref, gb_ref, x_ref, mod_ref, og_ref, w_ref, g2_ref, rw_ref,
                x1_ref, h2_ref, aff_ref):
    og = og_ref[...]
    yg = _gmlp(zg_ref[...].astype(f32), ws_ref, gb_ref[...])
    mixed_in = jnp.concatenate([
        _rms(a_ref[...].astype(f32)) * og[:, :ATTN_W],
        _rms(yh_ref[...].astype(f32)) * og[:, ATTN_W:ATTN_W + HY_W],
        _rms(yg) * og[:, ATTN_W + HY_W:]], axis=1)
    mixed = jnp.dot(mixed_in.astype(bf16), w_ref[...], preferred_element_type=f32)
    m = mod_ref[0]
    x1 = x_ref[...] + m[2:3] * mixed
    x1_ref[...] = x1
    h2 = _rms(x1) * g2_ref[...] * (1.0 + m[4:5]) + m[3:4]
    for j in range(ROW_CHUNKS):
        h2_ref[pl.ds(j, TM, stride=ROW_CHUNKS), :] = h2[:, j * LANES:(j + 1) * LANES]
    nt = (((1,), (1,)), ((), ()))
    h_hi = h2.astype(bf16)
    h_lo = (h2 - h_hi.astype(f32)).astype(bf16)
    rw = rw_ref[...]
    t = lax.dot_general(rw, h_hi, nt, preferred_element_type=f32)
    logits = (t[:N_EXPERTS] + t[N_EXPERTS:]) + lax.dot_general(rw[:N_EXPERTS], h_lo, nt, preferred_element_type=f32)
    e = jnp.exp(logits - jnp.max(logits, axis=0, keepdims=True))
    aff_ref[...] = e / jnp.sum(e, axis=0, keepdims=True)


def _out_proj(a, yh, zg, ws_bf, gm_b, x, mod, out_g, w_out_bf, g2, router_wt):
    n = x.shape[0]
    per_mod = n // mod.shape[0]
    row = lambda w: pl.BlockSpec((TM, w), lambda i: (i, 0))
    const = lambda s: pl.BlockSpec(s, lambda i: (0,) * len(s))
    gm_bias = jnp.repeat(gm_b.T, GM_HEAD_DIM, axis=1)
    return pl.pallas_call(
        _out_kernel,
        grid=(n // TM,),
        in_specs=[row(ATTN_W), row(HY_W), row(2 * GM_W), const((GM_HEADS, GM_CHUNK, GM_CHUNK)),
                  const((GM_CHUNK, GM_W)), row(D_MODEL),
                  pl.BlockSpec((1, 6, D_MODEL), lambda i: (i * TM // per_mod, 0, 0)),
                  const((1, MIX_W)), const((MIX_W, D_MODEL)), const((1, D_MODEL)), const((2 * N_EXPERTS, D_MODEL))],
        out_specs=[row(D_MODEL), pl.BlockSpec((TM * ROW_CHUNKS, LANES), lambda i: (i, 0)),
                   pl.BlockSpec((N_EXPERTS, TM), lambda i: (0, i))],
        out_shape=[jax.ShapeDtypeStruct((n, D_MODEL), f32), jax.ShapeDtypeStruct((n * ROW_CHUNKS, LANES), f32),
                   jax.ShapeDtypeStruct((N_EXPERTS, n), f32)],
        compiler_params=_params("parallel"),
        name="out_proj",
    )(a, yh, zg, ws_bf, gm_bias, x, mod, out_g.reshape(1, MIX_W), w_out_bf, g2.reshape(1, D_MODEL), router_wt)


RT_CHUNK = 512


def _prefix_incl(x01, tri):
    n = x01.shape[1]
    carry = jnp.zeros((x01.shape[0], 1), f32)
    parts = []
    for c in range(n // RT_CHUNK):
        piece = x01[:, c * RT_CHUNK:(c + 1) * RT_CHUNK]
        parts.append(jnp.dot(piece.astype(bf16), tri, preferred_element_type=f32) + carry)
        carry = carry + jnp.sum(piece, axis=1, keepdims=True)
    return jnp.concatenate(parts, axis=1)


def _route_kernel(cap, aff_ref, idx_ref, gate_ref):
    aff = aff_ref[...]
    n = aff.shape[1]

    def step(it, t):
        cand = t | (jnp.int32(1) << (30 - it))
        cnt = jnp.sum(jnp.where(aff >= lax.bitcast_convert_type(cand, f32), 1.0, 0.0), axis=1, keepdims=True)
        return jnp.where(cnt >= cap, cand, t)

    thr_bits = lax.fori_loop(0, 31, step, jnp.zeros((aff.shape[0], 1), jnp.int32))
    thr = lax.bitcast_convert_type(thr_bits, f32)
    gt = jnp.where(aff > thr, 1.0, 0.0)
    eq = jnp.where(aff == thr, 1.0, 0.0)
    room = cap - jnp.sum(gt, axis=1, keepdims=True)
    r = lax.broadcasted_iota(jnp.int32, (RT_CHUNK, RT_CHUNK), 0)
    c = lax.broadcasted_iota(jnp.int32, (RT_CHUNK, RT_CHUNK), 1)
    tri = jnp.where(r <= c, 1.0, 0.0).astype(bf16)
    sel = jnp.maximum(gt, jnp.where(_prefix_incl(eq, tri) <= room, eq, 0.0))
    slot = _prefix_incl(sel, tri) - 1.0

    tok = lax.broadcasted_iota(jnp.int32, aff.shape, 1)
    dist = jnp.where(sel > 0.0, tok - slot.astype(jnp.int32), 0)
    g = aff
    for b in range(max(1, (n - 1).bit_length())):
        sh = 1 << b
        dist_s = pltpu.roll(dist, n - sh, axis=1)
        take = (dist_s & sh) != 0
        leave = (dist & sh) != 0
        tok = jnp.where(take, pltpu.roll(tok, n - sh, axis=1), tok)
        g = jnp.where(take, pltpu.roll(g, n - sh, axis=1), g)
        dist = jnp.where(take, dist_s, jnp.where(leave, 0, dist))
    idx_ref[...] = tok[:, :cap] * ROW_CHUNKS
    gate_ref[...] = g[:, :cap]


def _route(aff_t, cap):
    return pl.pallas_call(
        functools.partial(_route_kernel, cap),
        in_specs=[pl.BlockSpec(memory_space=pltpu.VMEM)],
        out_specs=[pl.BlockSpec(memory_space=pltpu.VMEM)] * 2,
        out_shape=[jax.ShapeDtypeStruct((N_EXPERTS, cap), jnp.int32), jax.ShapeDtypeStruct((N_EXPERTS, cap), f32)],
        compiler_params=pltpu.CompilerParams(vmem_limit_bytes=VMEM_LIMIT),
        name="route",
    )(aff_t)


GATHER_UNROLL = 8
SCATTER_UNROLL = 8
CMB_EXPERTS = 2
CMB_TM = 1024


def _moe_ffn_kernel(cap, stride, idx_ref, h_ref, w1_ref, w3_ref, w2_ref, y_ref, tile_ref, xb_ref, acc_ref):
    e = pl.program_id(0)
    slab = ROW_CHUNKS * stride
    cur = pl.multiple_of((e % 2) * slab, SUBLANES)
    nxt = pl.multiple_of(((e + 1) % 2) * slab, SUBLANES)

    def gather_row(expert, slab_row, r):
        src = pl.multiple_of(idx_ref[expert * cap + r], ROW_CHUNKS)
        tile_ref[pl.ds(slab_row + r, ROW_CHUNKS, stride=stride), :] = h_ref[pl.ds(src, ROW_CHUNKS), :]

    @pl.when(e == 0)
    def _():
        def gather(q, carry):
            for t in range(GATHER_UNROLL):
                gather_row(0, cur, q * GATHER_UNROLL + t)
            return carry
        lax.fori_loop(0, cap // GATHER_UNROLL, gather, 0)

    xb_ref[...] = jnp.concatenate(
        [tile_ref[pl.ds(cur + j * stride, cap), :] for j in range(ROW_CHUNKS)], axis=1).astype(bf16)

    e_next = jnp.minimum(e + 1, N_EXPERTS - 1)
    n_f = D_EXPERT // MOE_F_TILE
    xb = xb_ref[...]
    for f in range(n_f):
        cols = slice(f * MOE_F_TILE, (f + 1) * MOE_F_TILE)
        a = jnp.dot(xb, w1_ref[0, 0, :, cols].astype(bf16), preferred_element_type=f32)
        b = jnp.dot(xb, w3_ref[0, 0, :, cols].astype(bf16), preferred_element_type=f32)
        he = (a * _sigmoid(a) * b).astype(bf16)
        part = jnp.dot(he, w2_ref[0, 0, cols, :].astype(bf16), preferred_element_type=f32)
        if f == 0:
            acc_ref[...] = part
        else:
            acc_ref[...] += part
        for r in range(f * cap // n_f, (f + 1) * cap // n_f):
            gather_row(e_next, nxt, r)

    y = acc_ref[...]
    for j in range(ROW_CHUNKS):
        y_ref[0, pl.ds(j, cap, stride=ROW_CHUNKS), :] = y[:, j * LANES:(j + 1) * LANES]


def _moe_ffn(layer, h3, idx, w1, w3, w2):
    cap = idx.shape[1]
    stride = cap + SUBLANES
    grid_spec = pltpu.PrefetchScalarGridSpec(
        num_scalar_prefetch=1,
        grid=(N_EXPERTS,),
        in_specs=[
            pl.BlockSpec(memory_space=pltpu.VMEM),
            pl.BlockSpec((1, 1, D_MODEL, D_EXPERT), lambda e, idx: (layer, e, 0, 0)),
            pl.BlockSpec((1, 1, D_MODEL, D_EXPERT), lambda e, idx: (layer, e, 0, 0)),
            pl.BlockSpec((1, 1, D_EXPERT, D_MODEL), lambda e, idx: (layer, e, 0, 0)),
        ],
        out_specs=pl.BlockSpec((1, ROW_CHUNKS * cap, LANES), lambda e, idx: (e, 0, 0)),
        scratch_shapes=[pltpu.VMEM((2 * ROW_CHUNKS * stride, LANES), f32),
                        pltpu.VMEM((cap, D_MODEL), bf16),
                        pltpu.VMEM((cap, D_MODEL), f32)],
    )
    return pl.pallas_call(
        functools.partial(_moe_ffn_kernel, cap, stride),
        grid_spec=grid_spec,
        out_shape=jax.ShapeDtypeStruct((N_EXPERTS, ROW_CHUNKS * cap, LANES), f32),
        compiler_params=_params("arbitrary"),
        name="moe_ffn",
    )(idx.reshape(-1), h3, w1, w3, w2)


def _moe_combine_kernel(cap, final, idx_ref, gate_ref, y_ref, x_ref, mod_ref, g_ref, o_ref, acc_ref):
    s = pl.program_id(0)
    scatter_steps = N_EXPERTS // CMB_EXPERTS

    @pl.when(s == 0)
    def _():
        acc_ref[...] = jnp.zeros_like(acc_ref)

    @pl.when(s < scatter_steps)
    def _():
        for k in range(CMB_EXPERTS):
            base = (s * CMB_EXPERTS + k) * cap

            def scatter(q, carry, k=k, base=base):
                group = SCATTER_UNROLL * ROW_CHUNKS
                y = y_ref[k, pl.ds(pl.multiple_of(q * group, group), group), :]
                rows = [pl.multiple_of(idx_ref[base + q * SCATTER_UNROLL + t], ROW_CHUNKS)
                        for t in range(SCATTER_UNROLL)]
                new = [acc_ref[pl.ds(rows[t], ROW_CHUNKS), :]
                       + gate_ref[base + q * SCATTER_UNROLL + t] * y[t * ROW_CHUNKS:(t + 1) * ROW_CHUNKS]
                       for t in range(SCATTER_UNROLL)]
                for t in range(SCATTER_UNROLL):
                    acc_ref[pl.ds(rows[t], ROW_CHUNKS), :] = new[t]
                return carry
            lax.fori_loop(0, cap // SCATTER_UNROLL, scatter, 0)

    @pl.when(s >= scatter_steps)
    def _():
        first = pl.multiple_of((s - scatter_steps) * (CMB_TM * ROW_CHUNKS), CMB_TM * ROW_CHUNKS)
        moe = jnp.concatenate([acc_ref[pl.ds(first + j, CMB_TM, stride=ROW_CHUNKS), :] for j in range(ROW_CHUNKS)],
                              axis=1)
        x = x_ref[...] + mod_ref[0][5:6] * moe
        o_ref[...] = _rms(x) * g_ref[...] if final else x


def _moe_combine(idx, gate, y_cm, x1, mod, final_g, final):
    n = x1.shape[0]
    cap = idx.shape[1]
    per_mod = n // mod.shape[0]
    scatter_steps = N_EXPERTS // CMB_EXPERTS
    tile = lambda s: jnp.maximum(s - scatter_steps, 0)
    row = pl.BlockSpec((CMB_TM, D_MODEL), lambda s, *_: (tile(s), 0))
    grid_spec = pltpu.PrefetchScalarGridSpec(
        num_scalar_prefetch=2,
        grid=(scatter_steps + n // CMB_TM,),
        in_specs=[pl.BlockSpec((CMB_EXPERTS, ROW_CHUNKS * cap, LANES),
                               lambda s, *_: (jnp.minimum(s, scatter_steps - 1), 0, 0)),
                  row,
                  pl.BlockSpec((1, 6, D_MODEL), lambda s, *_: (tile(s) * CMB_TM // per_mod, 0, 0)),
                  pl.BlockSpec((1, D_MODEL), lambda s, *_: (0, 0))],
        out_specs=row,
        scratch_shapes=[pltpu.VMEM((n * ROW_CHUNKS, LANES), f32)],
    )
    return pl.pallas_call(
        functools.partial(_moe_combine_kernel, cap, final),
        grid_spec=grid_spec,
        out_shape=jax.ShapeDtypeStruct((n, D_MODEL), f32),
        compiler_params=_params("arbitrary"),
        name="moe_combine",
    )(idx.reshape(-1), gate.reshape(-1), y_cm, x1, mod, final_g.reshape(1, D_MODEL))


def _expert_choice(layer, h3, aff_t, x1, mod, final_g, final, w1, w3, w2):
    n = x1.shape[0]
    cap = max(1, EC_CAPACITY * n // N_EXPERTS)
    idx, gate = _route(aff_t, cap)
    y_cm = _moe_ffn(layer, h3, idx, w1, w3, w2)
    return _moe_combine(idx, gate, y_cm, x1, mod, final_g, final)


def _split_bf16(w):
    hi = w.astype(bf16)
    lo = (w - hi.astype(f32)).astype(bf16)
    return jnp.concatenate([hi, lo], axis=0)


def _stream(x, mods, layers, experts, batch, seq_len, final_g, caches=None):
    consts = _hy_consts(seq_len)
    rope_tabs = _rope_tables(seq_len) if caches is not None else None
    new_caches = None
    for l, lp in enumerate(layers):
        if caches is None:
            qkv, zh, zg, *new_caches = _in_proj(x, mods[l], lp["norm1_g"], lp["w_in"], seq_len, cache_layer=l,
                                                caches_so_far=new_caches)
            a = _ctx_attention(qkv, lp["attn_sink"], batch, seq_len)
        else:
            qkv, zh, zg = _in_proj(x, mods[l], lp["norm1_g"], lp["w_in"], seq_len, rope_tabs=rope_tabs)
            a = _lat_attention(qkv, lp["attn_sink"], caches[0][:, l], caches[1][:, l], batch, seq_len)
        yh = _hyena(zh, lp, consts, batch, seq_len)
        x1, h3, aff_t = _out_proj(a, yh, zg, lp["gm_ws"], lp["gm_b"], x, mods[l], lp["out_norm_g"], lp["w_out"],
                                  lp["norm2_g"], lp["router_wt"])
        x = _expert_choice(l, h3, aff_t, x1, mods[l], final_g, l == len(layers) - 1, *experts)
    return x, new_caches


def kernel(x_prompt, x_sample, c, cache_k, cache_v, c_ctx, norm1_g, norm2_g, ada_w, ada_b, w_in, attn_sink,
           hy_conv_w, hy_conv_b, hy_f_w1, hy_f_b1, hy_f_w2, hy_f_b2, hy_f_w3, hy_freq, hy_bias, gm_ws, gm_b,
           out_norm_g, w_out, router_w, exp_w1, exp_w3, exp_w2, final_g):
    batch, seq, _ = x_prompt.shape
    dbatch, dseq, _ = x_sample.shape
    past = cache_k.shape[2]

    cvec = jnp.concatenate([c_ctx[None], c, jnp.zeros((SUBLANES - 1 - dbatch, D_MODEL), f32)], axis=0)
    mod = _ada(cvec, ada_w, ada_b)
    mods_ctx = [mod[l, 0:1].reshape(1, 6, D_MODEL) for l in range(DEPTH)]
    mods_lat = [mod[l, 1:1 + dbatch].reshape(dbatch, 6, D_MODEL) for l in range(DEPTH)]

    w_in_bf, w_out_bf, gm_ws_bf = w_in.astype(bf16), w_out.astype(bf16), gm_ws.astype(bf16)
    layers = []
    for l in range(DEPTH):
        layers.append({
            "norm1_g": norm1_g[l], "norm2_g": norm2_g[l], "w_in": w_in_bf[l], "attn_sink": attn_sink[l],
            "hy_conv_w": hy_conv_w[l], "hy_conv_b": hy_conv_b[l], "hy_f_w1": hy_f_w1[l], "hy_f_b1": hy_f_b1[l],
            "hy_f_w2": hy_f_w2[l], "hy_f_b2": hy_f_b2[l], "hy_f_w3": hy_f_w3[l], "hy_freq": hy_freq[l],
            "hy_bias": hy_bias[l], "gm_ws": gm_ws_bf[l], "gm_b": gm_b[l], "out_norm_g": out_norm_g[l],
            "w_out": w_out_bf[l], "router_wt": _split_bf16(router_w[l].T)})
    experts = (exp_w1, exp_w3, exp_w2)

    yp, kvs = _stream(x_prompt.reshape(batch * seq, D_MODEL), mods_ctx, layers, experts, batch, seq, final_g)
    caches = (cache_k.reshape(dbatch, DEPTH, past, KV_W), cache_v.reshape(dbatch, DEPTH, past, KV_W))
    ys, _ = _stream(x_sample.reshape(dbatch * dseq, D_MODEL), mods_lat, layers, experts, dbatch, dseq, final_g,
                   caches=caches)

    new_k, new_v = (a.reshape(batch, DEPTH, seq, N_KV, HEAD_DIM) for a in kvs)
    return (yp.reshape(batch, seq, D_MODEL), ys.reshape(dbatch, dseq, D_MODEL), new_k, new_v)
```

```python
import functools
import math

import numpy as np
import jax
import jax.numpy as jnp
from jax import lax
from jax.experimental import pallas as pl
from jax.experimental.pallas import tpu as pltpu

f32 = jnp.float32
bf16 = jnp.bfloat16

D_MODEL = 1024
DEPTH = 2
GRID_W = 64
BLOCK = 128
N_HEADS = 8
N_KV = 2
HEAD_DIM = 64
Q_GROUP = N_HEADS // N_KV
ATTN_W = N_HEADS * HEAD_DIM
KV_W = N_KV * HEAD_DIM
QKV_W = ATTN_W + 2 * KV_W
HY_W = 256
GM_W = 256
GM_HEADS = 4
GM_HEAD_DIM = GM_W // GM_HEADS
GM_CHUNK = 128
MIX_W = ATTN_W + HY_W + GM_W
IN_W = ATTN_W + 2 * KV_W + 3 * HY_W + 2 * GM_W
FILTER_EMB = 33
FILTER_HID = 64
HY_DECAY_HI_PCT = 0.3
HY_DECAY_LO_PCT = 1.5
HY_DECAY_TARGET = 1e-2
N_EXPERTS = 16
EC_CAPACITY = 2
D_EXPERT = 1024
ROPE_THETA = 10000.0
EPS = 1e-6
NEG = -1e30

LANES = 128
SUBLANES = 8
ROW_CHUNKS = D_MODEL // LANES
VMEM_LIMIT = 56 * 1024 * 1024
TM = 512
IN_TM = 512
MOE_F_TILE = 256


def _params(*sem):
    return pltpu.CompilerParams(dimension_semantics=sem, vmem_limit_bytes=VMEM_LIMIT)


def _rms(x):
    return x * lax.rsqrt(jnp.mean(x * x, axis=-1, keepdims=True) + EPS)


def _sigmoid(x):
    return 1.0 / (1.0 + jnp.exp(-x))


def _ada_kernel(c_ref, w_ref, b_ref, o_ref):
    c = c_ref[...]
    s = (c * _sigmoid(c)).astype(bf16)
    o_ref[0] = jnp.dot(s, w_ref[0].astype(bf16), preferred_element_type=f32) + b_ref[0]


def _ada(cvec, ada_w, ada_b):
    nt = 6
    return pl.pallas_call(
        _ada_kernel,
        grid=(DEPTH, nt),
        in_specs=[
            pl.BlockSpec((SUBLANES, D_MODEL), lambda l, j: (0, 0)),
            pl.BlockSpec((1, D_MODEL, D_MODEL), lambda l, j: (l, 0, j)),
            pl.BlockSpec((1, 1, D_MODEL), lambda l, j: (l, 0, j)),
        ],
        out_specs=pl.BlockSpec((1, SUBLANES, D_MODEL), lambda l, j: (l, 0, j)),
        out_shape=jax.ShapeDtypeStruct((DEPTH, SUBLANES, 6 * D_MODEL), f32),
        compiler_params=_params("arbitrary", "arbitrary"),
        name="ada",
    )(cvec, ada_w, ada_b.reshape(DEPTH, 1, 6 * D_MODEL))


def _rope_swap(x):
    w = x.shape[-1]
    lane = lax.broadcasted_iota(jnp.int32, x.shape, 1)
    first = (lane % 32) < 16
    return jnp.where(first, pltpu.roll(x, w - 16, axis=1), pltpu.roll(x, 16, axis=1))


def _in_kernel(rope, cache_rows, n_alias, x_ref, mod_ref, g_ref, w_ref, *refs):
    refs = list(refs)
    if rope:
        cos_ref, sin_ref = refs[:2]
        refs = refs[2:]
    refs = refs[n_alias:]
    qkv_ref, zh_ref, zg_ref = refs[:3]
    m = mod_ref[0]
    h = _rms(x_ref[...]) * g_ref[...] * (1.0 + m[1:2]) + m[0:1]
    z = jnp.dot(h.astype(bf16), w_ref[...], preferred_element_type=f32)
    if rope:
        reps = (ATTN_W + KV_W) // LANES
        cos = jnp.concatenate([cos_ref[...]] * reps, axis=1)
        sin = jnp.concatenate([sin_ref[...]] * reps, axis=1)
        qk = z[:, :ATTN_W + KV_W]
        qkv_ref[:, :ATTN_W + KV_W] = qk * cos + _rope_swap(qk) * sin
        qkv_ref[:, ATTN_W + KV_W:] = z[:, ATTN_W + KV_W:QKV_W]
    else:
        qkv_ref[...] = z[:, :QKV_W]
    zh_ref[...] = z[:, QKV_W:QKV_W + 3 * HY_W].astype(bf16)
    zg_ref[...] = z[:, QKV_W + 3 * HY_W:].astype(bf16)
    if cache_rows:
        k_ref, v_ref = refs[3:]
        for s in range(z.shape[0] // cache_rows):
            rows = slice(s * cache_rows, (s + 1) * cache_rows)
            k_ref[s, 0] = z[rows, ATTN_W:ATTN_W + KV_W]
            v_ref[s, 0] = z[rows, ATTN_W + KV_W:QKV_W]


def _in_proj(x, mod, g1, w_in_bf, seq_len, rope_tabs=None, cache_layer=None, caches_so_far=None):
    n = x.shape[0]
    rope = rope_tabs is not None
    per_mod = n // mod.shape[0]
    tm = IN_TM
    in_specs = [pl.BlockSpec((tm, D_MODEL), lambda i: (i, 0)),
                pl.BlockSpec((1, 6, D_MODEL), lambda i: (i * tm // per_mod, 0, 0)),
                pl.BlockSpec((1, D_MODEL), lambda i: (0, 0)),
                pl.BlockSpec((D_MODEL, IN_W), lambda i: (0, 0))]
    args = [x, mod, g1.reshape(1, D_MODEL), w_in_bf]
    if rope:
        nt = seq_len // tm
        tab = pl.BlockSpec((tm, LANES), lambda i: (i % nt, 0))
        in_specs += [tab, tab]
        args += list(rope_tabs)
    out_specs = [pl.BlockSpec((tm, w), lambda i: (i, 0)) for w in (QKV_W, 3 * HY_W, 2 * GM_W)]
    out_shape = [jax.ShapeDtypeStruct((n, QKV_W), f32), jax.ShapeDtypeStruct((n, 3 * HY_W), bf16),
                 jax.ShapeDtypeStruct((n, 2 * GM_W), bf16)]
    aliases, n_alias = {}, 0
    if cache_layer is not None:
        seqs = tm // seq_len
        cache = pl.BlockSpec((seqs, 1, seq_len, KV_W), lambda i: (i, cache_layer, 0, 0))
        out_specs += [cache, cache]
        out_shape += [jax.ShapeDtypeStruct((n // seq_len, DEPTH, seq_len, KV_W), f32)] * 2
        if caches_so_far is not None:
            n_alias = 2
            aliases = {len(args): 3, len(args) + 1: 4}
            in_specs += [pl.BlockSpec(memory_space=pl.ANY)] * 2
            args += list(caches_so_far)
    return pl.pallas_call(
        functools.partial(_in_kernel, rope, seq_len if cache_layer is not None else 0, n_alias),
        grid=(n // tm,),
        in_specs=in_specs,
        out_specs=out_specs,
        out_shape=out_shape,
        input_output_aliases=aliases,
        compiler_params=_params("parallel"),
        name="in_proj",
    )(*args)


def _rope_tables(seq_len):
    nf = HEAD_DIM // 4
    t = np.arange(seq_len)
    inv = (ROPE_THETA ** (-np.arange(nf, dtype=np.float32) / nf)).astype(np.float32)
    d = np.arange(HEAD_DIM)
    pos = np.where((d // 32)[None, :] == 0, (t // GRID_W)[:, None], (t % GRID_W)[:, None]).astype(np.float32)
    ang = (pos * inv[d % nf][None, :]).astype(np.float32)
    cos = np.cos(ang).astype(np.float32)
    sin = np.sin(ang).astype(np.float32) * np.where((d % 32) < 16, -1.0, 1.0)[None, :].astype(np.float32)
    return jnp.asarray(np.tile(cos, (1, 2))), jnp.asarray(np.tile(sin, (1, 2)))


LOG2E = math.log2(math.e)


def _stack_queries(z, sink_ref, kv):
    rows = z.shape[0]
    heads = range(kv * Q_GROUP, (kv + 1) * Q_GROUP)
    qs = jnp.concatenate([z[:, h * HEAD_DIM:(h + 1) * HEAD_DIM] for h in heads], axis=0)
    sink = jnp.concatenate([jnp.full((1, rows), sink_ref[h] * LOG2E, f32) for h in heads], axis=1)
    return (qs * (LOG2E / math.sqrt(HEAD_DIM))).astype(bf16), sink


def _scores_t(k, qs_bf):
    return lax.dot_general(k.astype(bf16), qs_bf, (((1,), (1,)), ((), ())), preferred_element_type=f32)


def _softmax_pv_t(st, sink, vals, rows):
    m = jnp.maximum(jnp.max(st, axis=0, keepdims=True), sink)
    pt = jnp.exp2(st - m).astype(bf16)
    v_bf = vals.astype(bf16)
    v_ext = jnp.concatenate([v_bf, jnp.ones_like(v_bf)], axis=1)
    ovt = lax.dot_general(v_ext, pt, (((0,), (0,)), ((), ())), preferred_element_type=f32)
    ot = ovt[:HEAD_DIM] / (ovt[HEAD_DIM:HEAD_DIM + 1] + jnp.exp2(sink - m))
    return [ot[:, g * rows:(g + 1) * rows].T for g in range(Q_GROUP)]


CTX_BATCHES = 4
LAT_QBLOCKS = 8


def _ctx_attn_kernel(seq_len, sink_ref, z_ref, o_ref):
    for s in range(CTX_BATCHES):
        z = z_ref[s * seq_len:(s + 1) * seq_len, :]
        outs = []
        for kv in range(N_KV):
            k = z[:, ATTN_W + kv * HEAD_DIM:ATTN_W + (kv + 1) * HEAD_DIM]
            v = z[:, ATTN_W + KV_W + kv * HEAD_DIM:ATTN_W + KV_W + (kv + 1) * HEAD_DIM]
            qs, sink = _stack_queries(z, sink_ref, kv)
            outs += _softmax_pv_t(_scores_t(k, qs), sink, v, seq_len)
        o_ref[s * seq_len:(s + 1) * seq_len, :] = jnp.concatenate(outs, axis=1).astype(o_ref.dtype)


def _ctx_attention(z, sink, batch, seq_len):
    rows = CTX_BATCHES * seq_len
    return pl.pallas_call(
        functools.partial(_ctx_attn_kernel, seq_len),
        grid=(batch // CTX_BATCHES,),
        in_specs=[pl.BlockSpec(memory_space=pltpu.SMEM),
                  pl.BlockSpec((rows, QKV_W), lambda b: (b, 0))],
        out_specs=pl.BlockSpec((rows, ATTN_W), lambda b: (b, 0)),
        out_shape=jax.ShapeDtypeStruct((batch * seq_len, ATTN_W), bf16),
        compiler_params=_params("parallel"),
        name="ctx_attn",
    )(sink, z)


def _lat_attn_kernel(nb, sink_ref, zp_ref, zc_ref, zn_ref, ck_ref, cv_ref, o_ref):
    step = pl.program_id(1)
    zc_all = zc_ref[...]
    blocks = [zp_ref[...]] + [zc_all[q * BLOCK:(q + 1) * BLOCK] for q in range(LAT_QBLOCKS)] + [zn_ref[...]]
    ck, cv = ck_ref[0, 0], cv_ref[0, 0]
    width = Q_GROUP * BLOCK
    j = lax.broadcasted_iota(jnp.int32, (BLOCK, width), 0)
    r = lax.broadcasted_iota(jnp.int32, (BLOCK, width), 1) % BLOCK
    for q in range(LAT_QBLOCKS):
        i = step * LAT_QBLOCKS + q
        zp, zc, zn = blocks[q], blocks[q + 1], blocks[q + 2]
        ok_prev = j >= r + jnp.where(i > 0, 0, BLOCK)
        ok_next = j <= r - jnp.where(i < nb - 1, 0, BLOCK)
        outs = []
        for kv in range(N_KV):
            ks = slice(ATTN_W + kv * HEAD_DIM, ATTN_W + (kv + 1) * HEAD_DIM)
            vs = slice(ATTN_W + KV_W + kv * HEAD_DIM, ATTN_W + KV_W + (kv + 1) * HEAD_DIM)
            cs = slice(kv * HEAD_DIM, (kv + 1) * HEAD_DIM)
            qs, sink = _stack_queries(zc, sink_ref, kv)
            st = jnp.concatenate([
                jnp.where(ok_prev, _scores_t(zp[:, ks], qs), NEG),
                _scores_t(zc[:, ks], qs),
                jnp.where(ok_next, _scores_t(zn[:, ks], qs), NEG),
                _scores_t(ck[:, cs], qs)], axis=0)
            vals = jnp.concatenate([zp[:, vs], zc[:, vs], zn[:, vs], cv[:, cs]], axis=0)
            outs += _softmax_pv_t(st, sink, vals, BLOCK)
        o_ref[q * BLOCK:(q + 1) * BLOCK, :] = jnp.concatenate(outs, axis=1).astype(o_ref.dtype)


def _lat_attention(z, sink, ck, cv, layer, batch, seq_len):
    nb = seq_len // BLOCK
    ns = nb // LAT_QBLOCKS
    blk = lambda f: pl.BlockSpec((BLOCK, QKV_W), f)
    past = ck.shape[2]
    cache = pl.BlockSpec((1, 1, past, KV_W), lambda b, i: (b, layer, 0, 0))
    return pl.pallas_call(
        functools.partial(_lat_attn_kernel, nb),
        grid=(batch, ns),
        in_specs=[pl.BlockSpec(memory_space=pltpu.SMEM),
                  blk(lambda b, i: (b * nb + jnp.maximum(i * LAT_QBLOCKS - 1, 0), 0)),
                  pl.BlockSpec((LAT_QBLOCKS * BLOCK, QKV_W), lambda b, i: (b * ns + i, 0)),
                  blk(lambda b, i: (b * nb + jnp.minimum((i + 1) * LAT_QBLOCKS, nb - 1), 0)),
                  cache, cache],
        out_specs=pl.BlockSpec((LAT_QBLOCKS * BLOCK, ATTN_W), lambda b, i: (b * ns + i, 0)),
        out_shape=jax.ShapeDtypeStruct((batch * seq_len, ATTN_W), bf16),
        compiler_params=_params("parallel", "parallel"),
        name="lat_attn",
    )(sink, z, z, z, ck, cv)


HY_TL = 256
HALO_ROWS = 2 * SUBLANES


def _short_conv(z, prev_tile, next_tile, has_prev, has_next, w, b):
    z = z.astype(f32)
    rows = z.shape[0]
    row = lax.broadcasted_iota(jnp.int32, (SUBLANES, z.shape[1]), 0)
    prev_row = jnp.where(has_prev, prev_tile.astype(f32)[HALO_ROWS - 1:HALO_ROWS, :], 0.0)
    next_row = jnp.where(has_next, next_tile.astype(f32)[0:1, :], 0.0)
    z_prev = pltpu.roll(z, 1, axis=0)
    z_prev = jnp.concatenate([jnp.where(row == 0, prev_row, z_prev[:SUBLANES]), z_prev[SUBLANES:]], axis=0)
    z_next = pltpu.roll(z, rows - 1, axis=0)
    z_next = jnp.concatenate([z_next[:rows - SUBLANES],
                              jnp.where(row == SUBLANES - 1, next_row, z_next[rows - SUBLANES:])], axis=0)
    zc = z_prev * w[0:1] + z * w[1:2] + z_next * w[2:3] + b
    return zc[:, :HY_W], zc[:, HY_W:2 * HY_W] * zc[:, 2 * HY_W:]


def _hy_filter_kernel(seq_len, feats_ref, win_ref, w1_ref, b1_ref, w2_ref, b2_ref, w3_ref, fr_ref, o_ref):
    hi = lax.Precision.HIGHEST
    fr = fr_ref[...]
    h = jnp.sin(fr * (jnp.dot(w1_ref[...], feats_ref[...], precision=hi, preferred_element_type=f32) + b1_ref[...]))
    h = jnp.sin(fr * (jnp.dot(w2_ref[...], h, precision=hi, preferred_element_type=f32) + b2_ref[...]))
    h = jnp.dot(h.T, w3_ref[...], precision=hi, preferred_element_type=f32)
    win = win_ref[...]
    row = lax.broadcasted_iota(jnp.int32, (seq_len, HY_W), 0)
    hf = h[:, :HY_W] * win
    hb = jnp.where(row == 0, 0.0, h[:, HY_W:] * win)
    o_ref[0:seq_len, :] = hf + hb
    o_ref[seq_len:, :] = hb - hf


def _hy_filter(seq_len, consts, w1, b1, w2, b2, w3, freq):
    w1t = jnp.zeros((FILTER_HID, LANES), f32).at[:, :FILTER_EMB].set(w1.T)
    args = (consts["feats"], consts["window"], w1t, b1.reshape(-1, 1), w2.T, b2.reshape(-1, 1), w3, freq.reshape(-1, 1))
    return pl.pallas_call(
        functools.partial(_hy_filter_kernel, seq_len),
        in_specs=[pl.BlockSpec(memory_space=pltpu.VMEM)] * len(args),
        out_specs=pl.BlockSpec(memory_space=pltpu.VMEM),
        out_shape=jax.ShapeDtypeStruct((2 * seq_len, HY_W), f32),
        compiler_params=pltpu.CompilerParams(vmem_limit_bytes=VMEM_LIMIT),
        name="hy_filter",
    )(*args)


def _hy_tiles(seq_len):
    return min(TM, seq_len // 2), min(TM, seq_len)


def _hy_fwd_kernel(nb, tk, seq_len, m_ref, zh_ref, cw_ref, cb_ref, hb_ref, rz_ref, rhs_ref):
    i = pl.program_id(0)
    nc = seq_len // HY_TL

    @pl.when(i == 0)
    def _():
        w, bias = cw_ref[...], cb_ref[...]
        for b in range(nb):
            def chunk(c, carry, b=b):
                r0 = pl.multiple_of(c * HY_TL, HY_TL)
                prev = zh_ref[b, pl.ds(pl.multiple_of(jnp.maximum(r0 - HALO_ROWS, 0), HALO_ROWS), HALO_ROWS), :]
                nxt = zh_ref[b, pl.ds(pl.multiple_of(jnp.minimum(r0 + HY_TL, seq_len - HALO_ROWS), HALO_ROWS),
                                      HALO_ROWS), :]
                _, u = _short_conv(zh_ref[b, pl.ds(r0, HY_TL), :], prev, nxt, c > 0, c < nc - 1, w, bias)
                rhs_ref[pl.ds(r0, HY_TL), b * HY_W:(b + 1) * HY_W] = u.astype(bf16)
                return carry
            lax.fori_loop(0, nc, chunk, 0)
        rhs_ref[:, nb * HY_W:(nb + 1) * HY_W] = hb_ref[0].astype(bf16)
        rhs_ref[:, (nb + 1) * HY_W:] = hb_ref[1].astype(bf16)

    c0 = nb * HY_W
    acc = jnp.dot(m_ref[...], rhs_ref[:, :c0], preferred_element_type=f32)
    p, w = acc[:tk], acc[tk:]
    hr = jnp.dot(m_ref[:tk, :], rhs_ref[:, c0:c0 + HY_W], preferred_element_type=f32)
    hi = jnp.dot(m_ref[tk:, :], rhs_ref[:, c0 + HY_W:], preferred_element_type=f32)
    qs = jnp.dot(m_ref[tk:tk + HALO_ROWS, :], rhs_ref[:, c0:c0 + HY_W], preferred_element_type=f32)[0:1]
    first = (lax.broadcasted_iota(jnp.int32, (tk, HY_W), 0) == 0) & (i == 0)
    for b in range(nb):
        cols = slice(b * HY_W, (b + 1) * HY_W)
        pu, wu = p[:, cols], w[:, cols]
        rz_ref[0, :, cols] = jnp.where(first, pu * hr, 2.0 * (pu * hr + wu * hi)).astype(bf16)
        rz_ref[1, :, cols] = jnp.where(first, wu * qs, 2.0 * (wu * hr - pu * hi)).astype(bf16)


def _hy_fwd(mat, zh, conv_w, conv_b, hb, batch, seq_len):
    tk, _ = _hy_tiles(seq_len)
    return pl.pallas_call(
        functools.partial(_hy_fwd_kernel, batch, tk, seq_len),
        grid=(seq_len // tk,),
        in_specs=[pl.BlockSpec((2 * tk, seq_len), lambda i: (i, 0)),
                  pl.BlockSpec((batch, seq_len, 3 * HY_W), lambda i: (0, 0, 0)),
                  pl.BlockSpec((3, 3 * HY_W), lambda i: (0, 0)),
                  pl.BlockSpec((1, 3 * HY_W), lambda i: (0, 0)),
                  pl.BlockSpec((2, seq_len, HY_W), lambda i: (0, 0, 0))],
        out_specs=pl.BlockSpec((2, tk, batch * HY_W), lambda i: (0, i, 0)),
        out_shape=jax.ShapeDtypeStruct((2, seq_len, batch * HY_W), bf16),
        scratch_shapes=[pltpu.VMEM((seq_len, (batch + 2) * HY_W), bf16)],
        compiler_params=_params("arbitrary"),
        name="hy_fwd",
    )(mat, zh, conv_w, conv_b.reshape(1, 3 * HY_W), hb)


def _hy_inv_kernel(nb, seq_len, nt, m_ref, rz_ref, zp_ref, zc_ref, zn_ref, cw_ref, cb_ref, d_ref, o_ref):
    i = pl.program_id(0)
    acc = jnp.dot(m_ref[...], rz_ref[...], preferred_element_type=f32)
    d, w, bias = d_ref[...], cw_ref[...], cb_ref[...]
    for b in range(nb):
        x0, u = _short_conv(zc_ref[b], zp_ref[b], zn_ref[b], i > 0, i < nt - 1, w, bias)
        y = acc[:, b * HY_W:(b + 1) * HY_W] * (1.0 / (2 * seq_len)) + u * d
        o_ref[b] = (x0 * y).astype(o_ref.dtype)


def _hy_inv(mat, rz, zh, conv_w, conv_b, d_bias, batch, seq_len):
    _, tm = _hy_tiles(seq_len)
    nt = seq_len // tm
    rh = tm // HALO_ROWS
    last = seq_len // HALO_ROWS - 1
    halo = lambda f: pl.BlockSpec((batch, HALO_ROWS, 3 * HY_W), f)
    return pl.pallas_call(
        functools.partial(_hy_inv_kernel, batch, seq_len, nt),
        grid=(nt,),
        in_specs=[pl.BlockSpec((tm, 2 * seq_len), lambda i: (i, 0)),
                  pl.BlockSpec((2 * seq_len, batch * HY_W), lambda i: (0, 0)),
                  halo(lambda i: (0, jnp.maximum(i * rh - 1, 0), 0)),
                  pl.BlockSpec((batch, tm, 3 * HY_W), lambda i: (0, i, 0)),
                  halo(lambda i: (0, jnp.minimum((i + 1) * rh, last), 0)),
                  pl.BlockSpec((3, 3 * HY_W), lambda i: (0, 0)),
                  pl.BlockSpec((1, 3 * HY_W), lambda i: (0, 0)),
                  pl.BlockSpec((1, HY_W), lambda i: (0, 0))],
        out_specs=pl.BlockSpec((batch, tm, HY_W), lambda i: (0, i, 0)),
        out_shape=jax.ShapeDtypeStruct((batch, seq_len, HY_W), bf16),
        compiler_params=_params("parallel"),
        name="hy_inv",
    )(mat, rz.reshape(2 * seq_len, batch * HY_W), zh, zh, zh, conv_w, conv_b.reshape(1, 3 * HY_W),
      d_bias.reshape(1, HY_W))


@functools.lru_cache(maxsize=None)
def _hy_consts_np(seq_len):
    n = 2 * seq_len
    k = np.arange(seq_len, dtype=np.int64)
    ang = (2.0 * np.pi / n) * ((k[:, None] * k[None, :]) % n).astype(np.float64)
    cosm = np.cos(ang)
    sinm = np.sin(ang)
    sinm[0, :] = np.where(k % 2 == 0, 1.0, -1.0)
    tk, _ = _hy_tiles(seq_len)
    fwd = np.concatenate([np.concatenate([cosm[i:i + tk], sinm[i:i + tk]], axis=0)
                          for i in range(0, seq_len, tk)], axis=0)
    inv = np.concatenate([cosm, sinm.T], axis=1)
    t = np.linspace(0.0, 1.0, seq_len, dtype=np.float32)[:, None]
    bands = (FILTER_EMB - 1) // 2
    fb = np.linspace(1e-4, bands - 1, bands, dtype=np.float32)[None, :]
    w = (np.float32(2.0 * math.pi) * np.arange(seq_len, dtype=np.float32)[:, None] / np.float32(seq_len)).astype(np.float32)
    feats = np.concatenate([t, np.cos(fb * w), -np.sin(fb * w)], axis=-1).astype(np.float32)
    feats = np.ascontiguousarray(np.pad(feats, ((0, 0), (0, LANES - FILTER_EMB))).T)
    decay_hi = math.log(HY_DECAY_TARGET) / HY_DECAY_HI_PCT
    decay_lo = math.log(HY_DECAY_TARGET) / HY_DECAY_LO_PCT
    deltas = np.abs(np.linspace(decay_lo, decay_hi, HY_W, dtype=np.float32))
    window = np.exp(-t * deltas[None, :]).astype(np.float32)
    return fwd, inv, feats, window


def _hy_consts(seq_len):
    fwd, inv, feats, window = _hy_consts_np(seq_len)
    as_bf = lambda m: jnp.asarray(m, dtype=f32).astype(bf16)
    return {"fwd": as_bf(fwd), "inv": as_bf(inv), "feats": jnp.asarray(feats), "window": jnp.asarray(window)}


def _hyena(zh, lp, consts, batch, seq_len):
    zh = zh.reshape(batch, seq_len, 3 * HY_W)
    hb = _hy_filter(seq_len, consts, lp["hy_f_w1"], lp["hy_f_b1"], lp["hy_f_w2"], lp["hy_f_b2"],
                    lp["hy_f_w3"], lp["hy_freq"]).reshape(2, seq_len, HY_W)
    rz = _hy_fwd(consts["fwd"], zh, lp["hy_conv_w"], lp["hy_conv_b"], hb, batch, seq_len)
    yh = _hy_inv(consts["inv"], rz, zh, lp["hy_conv_w"], lp["hy_conv_b"], lp["hy_bias"], batch, seq_len)
    return yh.reshape(batch * seq_len, HY_W)


def _gelu(x):
    return 0.5 * x * (1.0 + jnp.tanh(math.sqrt(2.0 / math.pi) * (x + 0.044715 * (x * x * x))))


def _gmlp(z, ws_ref, bias):
    g = _gelu(z)
    outs = []
    for c in range(TM // GM_CHUNK):
        rows = slice(c * GM_CHUNK, (c + 1) * GM_CHUNK)
        u = g[rows, :GM_W]
        v = g[rows, GM_W:].astype(bf16)
        s = jnp.concatenate(
            [jnp.dot(ws_ref[h], v[:, h * GM_HEAD_DIM:(h + 1) * GM_HEAD_DIM], preferred_element_type=f32)
             for h in range(GM_HEADS)], axis=1)
        outs.append(u * (s + bias))
    return jnp.concatenate(outs, axis=0)


def _out_kernel(a_ref, yh_ref, zg_ref, ws_ref, gb_ref, x_ref, mod_ref, og_ref, w_ref, g2_ref, rw_ref,
                x1_ref, h2_ref, aff_ref):
    og = og_ref[...]
    yg = _gmlp(zg_ref[...].astype(f32), ws_ref, gb_ref[...])
    mixed_in = jnp.concatenate([
        _rms(a_ref[...].astype(f32)) * og[:, :ATTN_W],
        _rms(yh_ref[...].astype(f32)) * og[:, ATTN_W:ATTN_W + HY_W],
        _rms(yg) * og[:, ATTN_W + HY_W:]], axis=1)
    mixed = jnp.dot(mixed_in.astype(bf16), w_ref[...], preferred_element_type=f32)
    m = mod_ref[0]
    x1 = x_ref[...] + m[2:3] * mixed
    x1_ref[...] = x1
    h2 = _rms(x1) * g2_ref[...] * (1.0 + m[4:5]) + m[3:4]
    for j in range(ROW_CHUNKS):
        h2_ref[pl.ds(j, TM, stride=ROW_CHUNKS), :] = h2[:, j * LANES:(j + 1) * LANES]
    nt = (((1,), (1,)), ((), ()))
    h_hi = h2.astype(bf16)
    h_lo = (h2 - h_hi.astype(f32)).astype(bf16)
    rw = rw_ref[...]
    t = lax.dot_general(rw, h_hi, nt, preferred_element_type=f32)
    logits = (t[:N_EXPERTS] + t[N_EXPERTS:]) + lax.dot_general(rw[:N_EXPERTS], h_lo, nt, preferred_element_type=f32)
    e = jnp.exp(logits - jnp.max(logits, axis=0, keepdims=True))
    aff_ref[...] = e / jnp.sum(e, axis=0, keepdims=True)


def _out_proj(a, yh, zg, ws_bf, gm_b, x, mod, out_g, w_out_bf, g2, router_wt):
    n = x.shape[0]
    per_mod = n // mod.shape[0]
    row = lambda w: pl.BlockSpec((TM, w), lambda i: (i, 0))
    const = lambda s: pl.BlockSpec(s, lambda i: (0,) * len(s))
    gm_bias = jnp.repeat(gm_b.T, GM_HEAD_DIM, axis=1)
    return pl.pallas_call(
        _out_kernel,
        grid=(n // TM,),
        in_specs=[row(ATTN_W), row(HY_W), row(2 * GM_W), const((GM_HEADS, GM_CHUNK, GM_CHUNK)),
                  const((GM_CHUNK, GM_W)), row(D_MODEL),
                  pl.BlockSpec((1, 6, D_MODEL), lambda i: (i * TM // per_mod, 0, 0)),
                  const((1, MIX_W)), const((MIX_W, D_MODEL)), const((1, D_MODEL)), const((2 * N_EXPERTS, D_MODEL))],
        out_specs=[row(D_MODEL), pl.BlockSpec((TM * ROW_CHUNKS, LANES), lambda i: (i, 0)),
                   pl.BlockSpec((N_EXPERTS, TM), lambda i: (0, i))],
        out_shape=[jax.ShapeDtypeStruct((n, D_MODEL), f32), jax.ShapeDtypeStruct((n * ROW_CHUNKS, LANES), f32),
                   jax.ShapeDtypeStruct((N_EXPERTS, n), f32)],
        compiler_params=_params("parallel"),
        name="out_proj",
    )(a, yh, zg, ws_bf, gm_bias, x, mod, out_g.reshape(1, MIX_W), w_out_bf, g2.reshape(1, D_MODEL), router_wt)


RT_CHUNK = 512


def _prefix_incl(x01, tri):
    n = x01.shape[1]
    carry = jnp.zeros((x01.shape[0], 1), f32)
    parts = []
    for c in range(n // RT_CHUNK):
        piece = x01[:, c * RT_CHUNK:(c + 1) * RT_CHUNK]
        parts.append(jnp.dot(piece.astype(bf16), tri, preferred_element_type=f32) + carry)
        carry = carry + jnp.sum(piece, axis=1, keepdims=True)
    return jnp.concatenate(parts, axis=1)


def _route_kernel(cap, aff_ref, idx_ref, gate_ref):
    aff = aff_ref[...]
    n = aff.shape[1]

    def step(it, t):
        cand = t | (jnp.int32(1) << (30 - it))
        cnt = jnp.sum(jnp.where(aff >= lax.bitcast_convert_type(cand, f32), 1.0, 0.0), axis=1, keepdims=True)
        return jnp.where(cnt >= cap, cand, t)

    thr_bits = lax.fori_loop(0, 31, step, jnp.zeros((aff.shape[0], 1), jnp.int32))
    thr = lax.bitcast_convert_type(thr_bits, f32)
    gt = jnp.where(aff > thr, 1.0, 0.0)
    eq = jnp.where(aff == thr, 1.0, 0.0)
    room = cap - jnp.sum(gt, axis=1, keepdims=True)
    r = lax.broadcasted_iota(jnp.int32, (RT_CHUNK, RT_CHUNK), 0)
    c = lax.broadcasted_iota(jnp.int32, (RT_CHUNK, RT_CHUNK), 1)
    tri = jnp.where(r <= c, 1.0, 0.0).astype(bf16)
    sel = jnp.maximum(gt, jnp.where(_prefix_incl(eq, tri) <= room, eq, 0.0))
    slot = _prefix_incl(sel, tri) - 1.0

    tok = lax.broadcasted_iota(jnp.int32, aff.shape, 1)
    dist = jnp.where(sel > 0.0, tok - slot.astype(jnp.int32), 0)
    g = aff
    for b in range(max(1, (n - 1).bit_length())):
        sh = 1 << b
        dist_s = pltpu.roll(dist, n - sh, axis=1)
        take = (dist_s & sh) != 0
        leave = (dist & sh) != 0
        tok = jnp.where(take, pltpu.roll(tok, n - sh, axis=1), tok)
        g = jnp.where(take, pltpu.roll(g, n - sh, axis=1), g)
        dist = jnp.where(take, dist_s, jnp.where(leave, 0, dist))
    idx_ref[...] = tok[:, :cap] * ROW_CHUNKS
    gate_ref[...] = g[:, :cap]


def _route(aff_t, cap):
    return pl.pallas_call(
        functools.partial(_route_kernel, cap),
        in_specs=[pl.BlockSpec(memory_space=pltpu.VMEM)],
        out_specs=[pl.BlockSpec(memory_space=pltpu.VMEM)] * 2,
        out_shape=[jax.ShapeDtypeStruct((N_EXPERTS, cap), jnp.int32), jax.ShapeDtypeStruct((N_EXPERTS, cap), f32)],
        compiler_params=pltpu.CompilerParams(vmem_limit_bytes=VMEM_LIMIT),
        name="route",
    )(aff_t)


GATHER_UNROLL = 8
SCATTER_UNROLL = 8
CMB_EXPERTS = 2
CMB_TM = 1024


def _moe_ffn_kernel(cap, stride, idx_ref, h_ref, w1_ref, w3_ref, w2_ref, y_ref, tile_ref, xb_ref, acc_ref):
    e = pl.program_id(0)
    slab = ROW_CHUNKS * stride
    cur = pl.multiple_of((e % 2) * slab, SUBLANES)
    nxt = pl.multiple_of(((e + 1) % 2) * slab, SUBLANES)

    def gather_row(expert, slab_row, r):
        src = pl.multiple_of(idx_ref[expert * cap + r], ROW_CHUNKS)
        tile_ref[pl.ds(slab_row + r, ROW_CHUNKS, stride=stride), :] = h_ref[pl.ds(src, ROW_CHUNKS), :]

    @pl.when(e == 0)
    def _():
        def gather(q, carry):
            for t in range(GATHER_UNROLL):
                gather_row(0, cur, q * GATHER_UNROLL + t)
            return carry
        lax.fori_loop(0, cap // GATHER_UNROLL, gather, 0)

    xb_ref[...] = jnp.concatenate(
        [tile_ref[pl.ds(cur + j * stride, cap), :] for j in range(ROW_CHUNKS)], axis=1).astype(bf16)

    e_next = jnp.minimum(e + 1, N_EXPERTS - 1)
    n_f = D_EXPERT // MOE_F_TILE
    xb = xb_ref[...]
    for f in range(n_f):
        cols = slice(f * MOE_F_TILE, (f + 1) * MOE_F_TILE)
        a = jnp.dot(xb, w1_ref[0, 0, :, cols].astype(bf16), preferred_element_type=f32)
        b = jnp.dot(xb, w3_ref[0, 0, :, cols].astype(bf16), preferred_element_type=f32)
        he = (a * _sigmoid(a) * b).astype(bf16)
        part = jnp.dot(he, w2_ref[0, 0, cols, :].astype(bf16), preferred_element_type=f32)
        if f == 0:
            acc_ref[...] = part
        else:
            acc_ref[...] += part
        for r in range(f * cap // n_f, (f + 1) * cap // n_f):
            gather_row(e_next, nxt, r)

    y = acc_ref[...]
    for j in range(ROW_CHUNKS):
        y_ref[0, pl.ds(j, cap, stride=ROW_CHUNKS), :] = y[:, j * LANES:(j + 1) * LANES]


def _moe_ffn(layer, h3, idx, w1, w3, w2):
    cap = idx.shape[1]
    stride = cap + SUBLANES
    grid_spec = pltpu.PrefetchScalarGridSpec(
        num_scalar_prefetch=1,
        grid=(N_EXPERTS,),
        in_specs=[
            pl.BlockSpec(memory_space=pltpu.VMEM),
            pl.BlockSpec((1, 1, D_MODEL, D_EXPERT), lambda e, idx: (layer, e, 0, 0)),
            pl.BlockSpec((1, 1, D_MODEL, D_EXPERT), lambda e, idx: (layer, e, 0, 0)),
            pl.BlockSpec((1, 1, D_EXPERT, D_MODEL), lambda e, idx: (layer, e, 0, 0)),
        ],
        out_specs=pl.BlockSpec((1, ROW_CHUNKS * cap, LANES), lambda e, idx: (e, 0, 0)),
        scratch_shapes=[pltpu.VMEM((2 * ROW_CHUNKS * stride, LANES), f32),
                        pltpu.VMEM((cap, D_MODEL), bf16),
                        pltpu.VMEM((cap, D_MODEL), f32)],
    )
    return pl.pallas_call(
        functools.partial(_moe_ffn_kernel, cap, stride),
        grid_spec=grid_spec,
        out_shape=jax.ShapeDtypeStruct((N_EXPERTS, ROW_CHUNKS * cap, LANES), f32),
        compiler_params=_params("arbitrary"),
        name="moe_ffn",
    )(idx.reshape(-1), h3, w1, w3, w2)


def _moe_combine_kernel(cap, final, idx_ref, gate_ref, y_ref, x_ref, mod_ref, g_ref, o_ref, acc_ref):
    s = pl.program_id(0)
    scatter_steps = N_EXPERTS // CMB_EXPERTS

    @pl.when(s == 0)
    def _():
        acc_ref[...] = jnp.zeros_like(acc_ref)

    @pl.when(s < scatter_steps)
    def _():
        for k in range(CMB_EXPERTS):
            base = (s * CMB_EXPERTS + k) * cap

            def scatter(q, carry, k=k, base=base):
                group = SCATTER_UNROLL * ROW_CHUNKS
                y = y_ref[k, pl.ds(pl.multiple_of(q * group, group), group), :]
                rows = [pl.multiple_of(idx_ref[base + q * SCATTER_UNROLL + t], ROW_CHUNKS)
                        for t in range(SCATTER_UNROLL)]
                new = [acc_ref[pl.ds(rows[t], ROW_CHUNKS), :]
                       + gate_ref[base + q * SCATTER_UNROLL + t] * y[t * ROW_CHUNKS:(t + 1) * ROW_CHUNKS]
                       for t in range(SCATTER_UNROLL)]
                for t in range(SCATTER_UNROLL):
                    acc_ref[pl.ds(rows[t], ROW_CHUNKS), :] = new[t]
                return carry
            lax.fori_loop(0, cap // SCATTER_UNROLL, scatter, 0)

    @pl.when(s >= scatter_steps)
    def _():
        first = pl.multiple_of((s - scatter_steps) * (CMB_TM * ROW_CHUNKS), CMB_TM * ROW_CHUNKS)
        moe = jnp.concatenate([acc_ref[pl.ds(first + j, CMB_TM, stride=ROW_CHUNKS), :] for j in range(ROW_CHUNKS)],
                              axis=1)
        x = x_ref[...] + mod_ref[0][5:6] * moe
        o_ref[...] = _rms(x) * g_ref[...] if final else x


def _moe_combine(idx, gate, y_cm, x1, mod, final_g, final):
    n = x1.shape[0]
    cap = idx.shape[1]
    per_mod = n // mod.shape[0]
    scatter_steps = N_EXPERTS // CMB_EXPERTS
    tile = lambda s: jnp.maximum(s - scatter_steps, 0)
    row = pl.BlockSpec((CMB_TM, D_MODEL), lambda s, *_: (tile(s), 0))
    grid_spec = pltpu.PrefetchScalarGridSpec(
        num_scalar_prefetch=2,
        grid=(scatter_steps + n // CMB_TM,),
        in_specs=[pl.BlockSpec((CMB_EXPERTS, ROW_CHUNKS * cap, LANES),
                               lambda s, *_: (jnp.minimum(s, scatter_steps - 1), 0, 0)),
                  row,
                  pl.BlockSpec((1, 6, D_MODEL), lambda s, *_: (tile(s) * CMB_TM // per_mod, 0, 0)),
                  pl.BlockSpec((1, D_MODEL), lambda s, *_: (0, 0))],
        out_specs=row,
        scratch_shapes=[pltpu.VMEM((n * ROW_CHUNKS, LANES), f32)],
    )
    return pl.pallas_call(
        functools.partial(_moe_combine_kernel, cap, final),
        grid_spec=grid_spec,
        out_shape=jax.ShapeDtypeStruct((n, D_MODEL), f32),
        compiler_params=_params("arbitrary"),
        name="moe_combine",
    )(idx.reshape(-1), gate.reshape(-1), y_cm, x1, mod, final_g.reshape(1, D_MODEL))


def _expert_choice(layer, h3, aff_t, x1, mod, final_g, final, w1, w3, w2):
    n = x1.shape[0]
    cap = max(1, EC_CAPACITY * n // N_EXPERTS)
    idx, gate = _route(aff_t, cap)
    y_cm = _moe_ffn(layer, h3, idx, w1, w3, w2)
    return _moe_combine(idx, gate, y_cm, x1, mod, final_g, final)


def _split_bf16(w):
    hi = w.astype(bf16)
    lo = (w - hi.astype(f32)).astype(bf16)
    return jnp.concatenate([hi, lo], axis=0)


def _stream(x, mods, layers, experts, batch, seq_len, final_g, caches=None):
    consts = _hy_consts(seq_len)
    rope_tabs = _rope_tables(seq_len) if caches is not None else None
    new_caches = None
    for l, lp in enumerate(layers):
        if caches is None:
            qkv, zh, zg, *new_caches = _in_proj(x, mods[l], lp["norm1_g"], lp["w_in"], seq_len, cache_layer=l,
                                                caches_so_far=new_caches)
            a = _ctx_attention(qkv, lp["attn_sink"], batch, seq_len)
        else:
            qkv, zh, zg = _in_proj(x, mods[l], lp["norm1_g"], lp["w_in"], seq_len, rope_tabs=rope_tabs)
            a = _lat_attention(qkv, lp["attn_sink"], caches[0], caches[1], l, batch, seq_len)
        yh = _hyena(zh, lp, consts, batch, seq_len)
        x1, h3, aff_t = _out_proj(a, yh, zg, lp["gm_ws"], lp["gm_b"], x, mods[l], lp["out_norm_g"], lp["w_out"],
                                  lp["norm2_g"], lp["router_wt"])
        x = _expert_choice(l, h3, aff_t, x1, mods[l], final_g, l == len(layers) - 1, *experts)
    return x, new_caches


def kernel(x_prompt, x_sample, c, cache_k, cache_v, c_ctx, norm1_g, norm2_g, ada_w, ada_b, w_in, attn_sink,
           hy_conv_w, hy_conv_b, hy_f_w1, hy_f_b1, hy_f_w2, hy_f_b2, hy_f_w3, hy_freq, hy_bias, gm_ws, gm_b,
           out_norm_g, w_out, router_w, exp_w1, exp_w3, exp_w2, final_g):
    batch, seq, _ = x_prompt.shape
    dbatch, dseq, _ = x_sample.shape
    past = cache_k.shape[2]

    cvec = jnp.concatenate([c_ctx[None], c, jnp.zeros((SUBLANES - 1 - dbatch, D_MODEL), f32)], axis=0)
    mod = _ada(cvec, ada_w, ada_b)
    mods_ctx = [mod[l, 0:1].reshape(1, 6, D_MODEL) for l in range(DEPTH)]
    mods_lat = [mod[l, 1:1 + dbatch].reshape(dbatch, 6, D_MODEL) for l in range(DEPTH)]

    w_in_bf, w_out_bf, gm_ws_bf = w_in.astype(bf16), w_out.astype(bf16), gm_ws.astype(bf16)
    layers = []
    for l in range(DEPTH):
        layers.append({
            "norm1_g": norm1_g[l], "norm2_g": norm2_g[l], "w_in": w_in_bf[l], "attn_sink": attn_sink[l],
            "hy_conv_w": hy_conv_w[l], "hy_conv_b": hy_conv_b[l], "hy_f_w1": hy_f_w1[l], "hy_f_b1": hy_f_b1[l],
            "hy_f_w2": hy_f_w2[l], "hy_f_b2": hy_f_b2[l], "hy_f_w3": hy_f_w3[l], "hy_freq": hy_freq[l],
            "hy_bias": hy_bias[l], "gm_ws": gm_ws_bf[l], "gm_b": gm_b[l], "out_norm_g": out_norm_g[l],
            "w_out": w_out_bf[l], "router_wt": _split_bf16(router_w[l].T)})
    experts = (exp_w1, exp_w3, exp_w2)

    yp, kvs = _stream(x_prompt.reshape(batch * seq, D_MODEL), mods_ctx, layers, experts, batch, seq, final_g)
    caches = (cache_k.reshape(dbatch, DEPTH, past, KV_W), cache_v.reshape(dbatch, DEPTH, past, KV_W))
    ys, _ = _stream(x_sample.reshape(dbatch * dseq, D_MODEL), mods_lat, layers, experts, dbatch, dseq, final_g,
                   caches=caches)

    new_k, new_v = (a.reshape(batch, DEPTH, seq, N_KV, HEAD_DIM) for a in kvs)
    return (yp.reshape(batch, seq, D_MODEL), ys.reshape(dbatch, dseq, D_MODEL), new_k, new_v)
```

```python
import functools
import math

import numpy as np
import jax
import jax.numpy as jnp
from jax import lax
from jax.experimental import pallas as pl
from jax.experimental.pallas import tpu as pltpu

f32 = jnp.float32
bf16 = jnp.bfloat16

D_MODEL = 1024
DEPTH = 2
GRID_W = 64
BLOCK = 128
N_HEADS = 8
N_KV = 2
HEAD_DIM = 64
Q_GROUP = N_HEADS // N_KV
ATTN_W = N_HEADS * HEAD_DIM
KV_W = N_KV * HEAD_DIM
QKV_W = ATTN_W + 2 * KV_W
HY_W = 256
GM_W = 256
GM_HEADS = 4
GM_HEAD_DIM = GM_W // GM_HEADS
GM_CHUNK = 128
MIX_W = ATTN_W + HY_W + GM_W
IN_W = ATTN_W + 2 * KV_W + 3 * HY_W + 2 * GM_W
FILTER_EMB = 33
FILTER_HID = 64
HY_DECAY_HI_PCT = 0.3
HY_DECAY_LO_PCT = 1.5
HY_DECAY_TARGET = 1e-2
N_EXPERTS = 16
EC_CAPACITY = 2
D_EXPERT = 1024
ROPE_THETA = 10000.0
EPS = 1e-6
NEG = -1e30

LANES = 128
SUBLANES = 8
ROW_CHUNKS = D_MODEL // LANES
VMEM_LIMIT = 56 * 1024 * 1024
TM = 512
IN_TM = 512
MOE_F_TILE = 256


def _params(*sem):
    return pltpu.CompilerParams(dimension_semantics=sem, vmem_limit_bytes=VMEM_LIMIT)


def _rms(x):
    return x * lax.rsqrt(jnp.mean(x * x, axis=-1, keepdims=True) + EPS)


def _sigmoid(x):
    return 1.0 / (1.0 + jnp.exp(-x))


def _ada_kernel(c_ref, w_ref, b_ref, o_ref):
    c = c_ref[...]
    s = (c * _sigmoid(c)).astype(bf16)
    o_ref[0] = jnp.dot(s, w_ref[0].astype(bf16), preferred_element_type=f32) + b_ref[0]


def _ada(cvec, ada_w, ada_b):
    nt = 6
    return pl.pallas_call(
        _ada_kernel,
        grid=(DEPTH, nt),
        in_specs=[
            pl.BlockSpec((SUBLANES, D_MODEL), lambda l, j: (0, 0)),
            pl.BlockSpec((1, D_MODEL, D_MODEL), lambda l, j: (l, 0, j)),
            pl.BlockSpec((1, 1, D_MODEL), lambda l, j: (l, 0, j)),
        ],
        out_specs=pl.BlockSpec((1, SUBLANES, D_MODEL), lambda l, j: (l, 0, j)),
        out_shape=jax.ShapeDtypeStruct((DEPTH, SUBLANES, 6 * D_MODEL), f32),
        compiler_params=_params("arbitrary", "arbitrary"),
        name="ada",
    )(cvec, ada_w, ada_b.reshape(DEPTH, 1, 6 * D_MODEL))


def _rope_swap(x):
    w = x.shape[-1]
    lane = lax.broadcasted_iota(jnp.int32, x.shape, 1)
    first = (lane % 32) < 16
    return jnp.where(first, pltpu.roll(x, w - 16, axis=1), pltpu.roll(x, 16, axis=1))


def _in_kernel(rope, cache_rows, cache_slot, n_alias, x_ref, mod_ref, g_ref, w_ref, *refs):
    refs = list(refs)
    if rope:
        cos_ref, sin_ref = refs[:2]
        refs = refs[2:]
    refs = refs[n_alias:]
    qkv_ref, zh_ref, zg_ref = refs[:3]
    m = mod_ref[0]
    h = _rms(x_ref[...]) * g_ref[...] * (1.0 + m[1:2]) + m[0:1]
    z = jnp.dot(h.astype(bf16), w_ref[...], preferred_element_type=f32)
    if rope:
        reps = (ATTN_W + KV_W) // LANES
        cos = jnp.concatenate([cos_ref[...]] * reps, axis=1)
        sin = jnp.concatenate([sin_ref[...]] * reps, axis=1)
        qk = z[:, :ATTN_W + KV_W]
        qkv_ref[:, :ATTN_W + KV_W] = qk * cos + _rope_swap(qk) * sin
        qkv_ref[:, ATTN_W + KV_W:] = z[:, ATTN_W + KV_W:QKV_W]
    else:
        qkv_ref[...] = z[:, :QKV_W]
    zh_ref[...] = z[:, QKV_W:QKV_W + 3 * HY_W].astype(bf16)
    zg_ref[...] = z[:, QKV_W + 3 * HY_W:].astype(bf16)
    if cache_rows:
        k_ref, v_ref = refs[3:]
        for s in range(z.shape[0] // cache_rows):
            rows = slice(s * cache_rows, (s + 1) * cache_rows)
            for d in range(k_ref.shape[1]):
                if d == cache_slot:
                    k_ref[s, d] = z[rows, ATTN_W:ATTN_W + KV_W]
                    v_ref[s, d] = z[rows, ATTN_W + KV_W:QKV_W]
                else:
                    k_ref[s, d] = jnp.zeros((cache_rows, KV_W), f32)
                    v_ref[s, d] = jnp.zeros((cache_rows, KV_W), f32)


def _in_proj(x, mod, g1, w_in_bf, seq_len, rope_tabs=None, cache_layer=None, caches_so_far=None):
    n = x.shape[0]
    rope = rope_tabs is not None
    per_mod = n // mod.shape[0]
    tm = IN_TM
    in_specs = [pl.BlockSpec((tm, D_MODEL), lambda i: (i, 0)),
                pl.BlockSpec((1, 6, D_MODEL), lambda i: (i * tm // per_mod, 0, 0)),
                pl.BlockSpec((1, D_MODEL), lambda i: (0, 0)),
                pl.BlockSpec((D_MODEL, IN_W), lambda i: (0, 0))]
    args = [x, mod, g1.reshape(1, D_MODEL), w_in_bf]
    if rope:
        nt = seq_len // tm
        tab = pl.BlockSpec((tm, LANES), lambda i: (i % nt, 0))
        in_specs += [tab, tab]
        args += list(rope_tabs)
    out_specs = [pl.BlockSpec((tm, w), lambda i: (i, 0)) for w in (QKV_W, 3 * HY_W, 2 * GM_W)]
    out_shape = [jax.ShapeDtypeStruct((n, QKV_W), f32), jax.ShapeDtypeStruct((n, 3 * HY_W), bf16),
                 jax.ShapeDtypeStruct((n, 2 * GM_W), bf16)]
    aliases, n_alias, cache_slot = {}, 0, 0
    if cache_layer is not None:
        seqs = tm // seq_len
        if caches_so_far is None:
            cache = pl.BlockSpec((seqs, DEPTH, seq_len, KV_W), lambda i: (i, 0, 0, 0))
            cache_slot = cache_layer
        else:
            cache = pl.BlockSpec((seqs, 1, seq_len, KV_W), lambda i: (i, cache_layer, 0, 0))
            n_alias = 2
            aliases = {len(args): 3, len(args) + 1: 4}
            in_specs += [pl.BlockSpec(memory_space=pl.ANY)] * 2
            args += list(caches_so_far)
        out_specs += [cache, cache]
        out_shape += [jax.ShapeDtypeStruct((n // seq_len, DEPTH, seq_len, KV_W), f32)] * 2
    return pl.pallas_call(
        functools.partial(_in_kernel, rope, seq_len if cache_layer is not None else 0, cache_slot, n_alias),
        grid=(n // tm,),
        in_specs=in_specs,
        out_specs=out_specs,
        out_shape=out_shape,
        input_output_aliases=aliases,
        compiler_params=_params("parallel"),
        name="in_proj",
    )(*args)


def _rope_tables(seq_len):
    nf = HEAD_DIM // 4
    t = np.arange(seq_len)
    inv = (ROPE_THETA ** (-np.arange(nf, dtype=np.float32) / nf)).astype(np.float32)
    d = np.arange(HEAD_DIM)
    pos = np.where((d // 32)[None, :] == 0, (t // GRID_W)[:, None], (t % GRID_W)[:, None]).astype(np.float32)
    ang = (pos * inv[d % nf][None, :]).astype(np.float32)
    cos = np.cos(ang).astype(np.float32)
    sin = np.sin(ang).astype(np.float32) * np.where((d % 32) < 16, -1.0, 1.0)[None, :].astype(np.float32)
    return jnp.asarray(np.tile(cos, (1, 2))), jnp.asarray(np.tile(sin, (1, 2)))


LOG2E = math.log2(math.e)


def _stack_queries(z, sink_ref, kv):
    rows = z.shape[0]
    heads = range(kv * Q_GROUP, (kv + 1) * Q_GROUP)
    qs = jnp.concatenate([z[:, h * HEAD_DIM:(h + 1) * HEAD_DIM] for h in heads], axis=0)
    sink = jnp.concatenate([jnp.full((1, rows), sink_ref[h] * LOG2E, f32) for h in heads], axis=1)
    return (qs * (LOG2E / math.sqrt(HEAD_DIM))).astype(bf16), sink


def _scores_t(k, qs_bf):
    return lax.dot_general(k.astype(bf16), qs_bf, (((1,), (1,)), ((), ())), preferred_element_type=f32)


def _softmax_pv_t(st, sink, vals, rows):
    m = jnp.maximum(jnp.max(st, axis=0, keepdims=True), sink)
    pt = jnp.exp2(st - m).astype(bf16)
    v_bf = vals.astype(bf16)
    v_ext = jnp.concatenate([v_bf, jnp.ones_like(v_bf)], axis=1)
    ovt = lax.dot_general(v_ext, pt, (((0,), (0,)), ((), ())), preferred_element_type=f32)
    ot = ovt[:HEAD_DIM] / (ovt[HEAD_DIM:HEAD_DIM + 1] + jnp.exp2(sink - m))
    return [ot[:, g * rows:(g + 1) * rows].T for g in range(Q_GROUP)]


CTX_BATCHES = 4
LAT_QBLOCKS = 8


def _ctx_attn_kernel(seq_len, sink_ref, z_ref, o_ref):
    for s in range(CTX_BATCHES):
        z = z_ref[s * seq_len:(s + 1) * seq_len, :]
        outs = []
        for kv in range(N_KV):
            k = z[:, ATTN_W + kv * HEAD_DIM:ATTN_W + (kv + 1) * HEAD_DIM]
            v = z[:, ATTN_W + KV_W + kv * HEAD_DIM:ATTN_W + KV_W + (kv + 1) * HEAD_DIM]
            qs, sink = _stack_queries(z, sink_ref, kv)
            outs += _softmax_pv_t(_scores_t(k, qs), sink, v, seq_len)
        o_ref[s * seq_len:(s + 1) * seq_len, :] = jnp.concatenate(outs, axis=1).astype(o_ref.dtype)


def _ctx_attention(z, sink, batch, seq_len):
    rows = CTX_BATCHES * seq_len
    return pl.pallas_call(
        functools.partial(_ctx_attn_kernel, seq_len),
        grid=(batch // CTX_BATCHES,),
        in_specs=[pl.BlockSpec(memory_space=pltpu.SMEM),
                  pl.BlockSpec((rows, QKV_W), lambda b: (b, 0))],
        out_specs=pl.BlockSpec((rows, ATTN_W), lambda b: (b, 0)),
        out_shape=jax.ShapeDtypeStruct((batch * seq_len, ATTN_W), bf16),
        compiler_params=_params("parallel"),
        name="ctx_attn",
    )(sink, z)


def _lat_attn_kernel(nb, sink_ref, zp_ref, zc_ref, zn_ref, ck_ref, cv_ref, o_ref):
    step = pl.program_id(1)
    zc_all = zc_ref[...]
    blocks = [zp_ref[...]] + [zc_all[q * BLOCK:(q + 1) * BLOCK] for q in range(LAT_QBLOCKS)] + [zn_ref[...]]
    ck, cv = ck_ref[0, 0], cv_ref[0, 0]
    width = Q_GROUP * BLOCK
    j = lax.broadcasted_iota(jnp.int32, (BLOCK, width), 0)
    r = lax.broadcasted_iota(jnp.int32, (BLOCK, width), 1) % BLOCK
    for q in range(LAT_QBLOCKS):
        i = step * LAT_QBLOCKS + q
        zp, zc, zn = blocks[q], blocks[q + 1], blocks[q + 2]
        ok_prev = j >= r + jnp.where(i > 0, 0, BLOCK)
        ok_next = j <= r - jnp.where(i < nb - 1, 0, BLOCK)
        outs = []
        for kv in range(N_KV):
            ks = slice(ATTN_W + kv * HEAD_DIM, ATTN_W + (kv + 1) * HEAD_DIM)
            vs = slice(ATTN_W + KV_W + kv * HEAD_DIM, ATTN_W + KV_W + (kv + 1) * HEAD_DIM)
            cs = slice(kv * HEAD_DIM, (kv + 1) * HEAD_DIM)
            qs, sink = _stack_queries(zc, sink_ref, kv)
            st = jnp.concatenate([
                jnp.where(ok_prev, _scores_t(zp[:, ks], qs), NEG),
                _scores_t(zc[:, ks], qs),
                jnp.where(ok_next, _scores_t(zn[:, ks], qs), NEG),
                _scores_t(ck[:, cs], qs)], axis=0)
            vals = jnp.concatenate([zp[:, vs], zc[:, vs], zn[:, vs], cv[:, cs]], axis=0)
            outs += _softmax_pv_t(st, sink, vals, BLOCK)
        o_ref[q * BLOCK:(q + 1) * BLOCK, :] = jnp.concatenate(outs, axis=1).astype(o_ref.dtype)


def _lat_attention(z, sink, ck, cv, layer, batch, seq_len):
    nb = seq_len // BLOCK
    ns = nb // LAT_QBLOCKS
    blk = lambda f: pl.BlockSpec((BLOCK, QKV_W), f)
    past = ck.shape[2]
    cache = pl.BlockSpec((1, 1, past, KV_W), lambda b, i: (b, layer, 0, 0))
    return pl.pallas_call(
        functools.partial(_lat_attn_kernel, nb),
        grid=(batch, ns),
        in_specs=[pl.BlockSpec(memory_space=pltpu.SMEM),
                  blk(lambda b, i: (b * nb + jnp.maximum(i * LAT_QBLOCKS - 1, 0), 0)),
                  pl.BlockSpec((LAT_QBLOCKS * BLOCK, QKV_W), lambda b, i: (b * ns + i, 0)),
                  blk(lambda b, i: (b * nb + jnp.minimum((i + 1) * LAT_QBLOCKS, nb - 1), 0)),
                  cache, cache],
        out_specs=pl.BlockSpec((LAT_QBLOCKS * BLOCK, ATTN_W), lambda b, i: (b * ns + i, 0)),
        out_shape=jax.ShapeDtypeStruct((batch * seq_len, ATTN_W), bf16),
        compiler_params=_params("parallel", "parallel"),
        name="lat_attn",
    )(sink, z, z, z, ck, cv)


HY_TL = 256
HALO_ROWS = 2 * SUBLANES


def _short_conv(z, prev_tile, next_tile, has_prev, has_next, w, b):
    z = z.astype(f32)
    rows = z.shape[0]
    row = lax.broadcasted_iota(jnp.int32, (SUBLANES, z.shape[1]), 0)
    prev_row = jnp.where(has_prev, prev_tile.astype(f32)[HALO_ROWS - 1:HALO_ROWS, :], 0.0)
    next_row = jnp.where(has_next, next_tile.astype(f32)[0:1, :], 0.0)
    z_prev = pltpu.roll(z, 1, axis=0)
    z_prev = jnp.concatenate([jnp.where(row == 0, prev_row, z_prev[:SUBLANES]), z_prev[SUBLANES:]], axis=0)
    z_next = pltpu.roll(z, rows - 1, axis=0)
    z_next = jnp.concatenate([z_next[:rows - SUBLANES],
                              jnp.where(row == SUBLANES - 1, next_row, z_next[rows - SUBLANES:])], axis=0)
    zc = z_prev * w[0:1] + z * w[1:2] + z_next * w[2:3] + b
    return zc[:, :HY_W], zc[:, HY_W:2 * HY_W] * zc[:, 2 * HY_W:]


def _hy_filter_kernel(seq_len, feats_ref, win_ref, w1_ref, b1_ref, w2_ref, b2_ref, w3_ref, fr_ref, o_ref):
    hi = lax.Precision.HIGHEST
    fr = fr_ref[...]
    h = jnp.sin(fr * (jnp.dot(w1_ref[...], feats_ref[...], precision=hi, preferred_element_type=f32) + b1_ref[...]))
    h = jnp.sin(fr * (jnp.dot(w2_ref[...], h, precision=hi, preferred_element_type=f32) + b2_ref[...]))
    h = jnp.dot(h.T, w3_ref[...], precision=hi, preferred_element_type=f32)
    win = win_ref[...]
    row = lax.broadcasted_iota(jnp.int32, (seq_len, HY_W), 0)
    hf = h[:, :HY_W] * win
    hb = jnp.where(row == 0, 0.0, h[:, HY_W:] * win)
    o_ref[0:seq_len, :] = hf + hb
    o_ref[seq_len:, :] = hb - hf


def _hy_filter(seq_len, consts, w1, b1, w2, b2, w3, freq):
    w1t = jnp.zeros((FILTER_HID, LANES), f32).at[:, :FILTER_EMB].set(w1.T)
    args = (consts["feats"], consts["window"], w1t, b1.reshape(-1, 1), w2.T, b2.reshape(-1, 1), w3, freq.reshape(-1, 1))
    return pl.pallas_call(
        functools.partial(_hy_filter_kernel, seq_len),
        in_specs=[pl.BlockSpec(memory_space=pltpu.VMEM)] * len(args),
        out_specs=pl.BlockSpec(memory_space=pltpu.VMEM),
        out_shape=jax.ShapeDtypeStruct((2 * seq_len, HY_W), f32),
        compiler_params=pltpu.CompilerParams(vmem_limit_bytes=VMEM_LIMIT),
        name="hy_filter",
    )(*args)


def _hy_tiles(seq_len):
    return min(TM, seq_len // 2), min(TM, seq_len)


def _hy_fwd_kernel(nb, tk, seq_len, m_ref, zh_ref, cw_ref, cb_ref, hb_ref, rz_ref, rhs_ref):
    i = pl.program_id(0)
    nc = seq_len // HY_TL

    @pl.when(i == 0)
    def _():
        w, bias = cw_ref[...], cb_ref[...]
        for b in range(nb):
            def chunk(c, carry, b=b):
                r0 = pl.multiple_of(c * HY_TL, HY_TL)
                prev = zh_ref[b, pl.ds(pl.multiple_of(jnp.maximum(r0 - HALO_ROWS, 0), HALO_ROWS), HALO_ROWS), :]
                nxt = zh_ref[b, pl.ds(pl.multiple_of(jnp.minimum(r0 + HY_TL, seq_len - HALO_ROWS), HALO_ROWS),
                                      HALO_ROWS), :]
                _, u = _short_conv(zh_ref[b, pl.ds(r0, HY_TL), :], prev, nxt, c > 0, c < nc - 1, w, bias)
                rhs_ref[pl.ds(r0, HY_TL), b * HY_W:(b + 1) * HY_W] = u.astype(bf16)
                return carry
            lax.fori_loop(0, nc, chunk, 0)
        rhs_ref[:, nb * HY_W:(nb + 1) * HY_W] = hb_ref[0].astype(bf16)
        rhs_ref[:, (nb + 1) * HY_W:] = hb_ref[1].astype(bf16)

    c0 = nb * HY_W
    acc = jnp.dot(m_ref[...], rhs_ref[:, :c0], preferred_element_type=f32)
    p, w = acc[:tk], acc[tk:]
    hr = jnp.dot(m_ref[:tk, :], rhs_ref[:, c0:c0 + HY_W], preferred_element_type=f32)
    hi = jnp.dot(m_ref[tk:, :], rhs_ref[:, c0 + HY_W:], preferred_element_type=f32)
    qs = jnp.dot(m_ref[tk:tk + HALO_ROWS, :], rhs_ref[:, c0:c0 + HY_W], preferred_element_type=f32)[0:1]
    first = (lax.broadcasted_iota(jnp.int32, (tk, HY_W), 0) == 0) & (i == 0)
    for b in range(nb):
        cols = slice(b * HY_W, (b + 1) * HY_W)
        pu, wu = p[:, cols], w[:, cols]
        rz_ref[0, :, cols] = jnp.where(first, pu * hr, 2.0 * (pu * hr + wu * hi)).astype(bf16)
        rz_ref[1, :, cols] = jnp.where(first, wu * qs, 2.0 * (wu * hr - pu * hi)).astype(bf16)


def _hy_fwd(mat, zh, conv_w, conv_b, hb, batch, seq_len):
    tk, _ = _hy_tiles(seq_len)
    return pl.pallas_call(
        functools.partial(_hy_fwd_kernel, batch, tk, seq_len),
        grid=(seq_len // tk,),
        in_specs=[pl.BlockSpec((2 * tk, seq_len), lambda i: (i, 0)),
                  pl.BlockSpec((batch, seq_len, 3 * HY_W), lambda i: (0, 0, 0)),
                  pl.BlockSpec((3, 3 * HY_W), lambda i: (0, 0)),
                  pl.BlockSpec((1, 3 * HY_W), lambda i: (0, 0)),
                  pl.BlockSpec((2, seq_len, HY_W), lambda i: (0, 0, 0))],
        out_specs=pl.BlockSpec((2, tk, batch * HY_W), lambda i: (0, i, 0)),
        out_shape=jax.ShapeDtypeStruct((2, seq_len, batch * HY_W), bf16),
        scratch_shapes=[pltpu.VMEM((seq_len, (batch + 2) * HY_W), bf16)],
        compiler_params=_params("arbitrary"),
        name="hy_fwd",
    )(mat, zh, conv_w, conv_b.reshape(1, 3 * HY_W), hb)


def _hy_inv_kernel(nb, seq_len, nt, m_ref, rz_ref, zp_ref, zc_ref, zn_ref, cw_ref, cb_ref, d_ref, o_ref):
    i = pl.program_id(0)
    acc = jnp.dot(m_ref[...], rz_ref[...], preferred_element_type=f32)
    d, w, bias = d_ref[...], cw_ref[...], cb_ref[...]
    for b in range(nb):
        x0, u = _short_conv(zc_ref[b], zp_ref[b], zn_ref[b], i > 0, i < nt - 1, w, bias)
        y = acc[:, b * HY_W:(b + 1) * HY_W] * (1.0 / (2 * seq_len)) + u * d
        o_ref[b] = (x0 * y).astype(o_ref.dtype)


def _hy_inv(mat, rz, zh, conv_w, conv_b, d_bias, batch, seq_len):
    _, tm = _hy_tiles(seq_len)
    nt = seq_len // tm
    rh = tm // HALO_ROWS
    last = seq_len // HALO_ROWS - 1
    halo = lambda f: pl.BlockSpec((batch, HALO_ROWS, 3 * HY_W), f)
    return pl.pallas_call(
        functools.partial(_hy_inv_kernel, batch, seq_len, nt),
        grid=(nt,),
        in_specs=[pl.BlockSpec((tm, 2 * seq_len), lambda i: (i, 0)),
                  pl.BlockSpec((2 * seq_len, batch * HY_W), lambda i: (0, 0)),
                  halo(lambda i: (0, jnp.maximum(i * rh - 1, 0), 0)),
                  pl.BlockSpec((batch, tm, 3 * HY_W), lambda i: (0, i, 0)),
                  halo(lambda i: (0, jnp.minimum((i + 1) * rh, last), 0)),
                  pl.BlockSpec((3, 3 * HY_W), lambda i: (0, 0)),
                  pl.BlockSpec((1, 3 * HY_W), lambda i: (0, 0)),
                  pl.BlockSpec((1, HY_W), lambda i: (0, 0))],
        out_specs=pl.BlockSpec((batch, tm, HY_W), lambda i: (0, i, 0)),
        out_shape=jax.ShapeDtypeStruct((batch, seq_len, HY_W), bf16),
        compiler_params=_params("parallel"),
        name="hy_inv",
    )(mat, rz.reshape(2 * seq_len, batch * HY_W), zh, zh, zh, conv_w, conv_b.reshape(1, 3 * HY_W),
      d_bias.reshape(1, HY_W))


@functools.lru_cache(maxsize=None)
def _hy_consts_np(seq_len):
    n = 2 * seq_len
    k = np.arange(seq_len, dtype=np.int64)
    ang = (2.0 * np.pi / n) * ((k[:, None] * k[None, :]) % n).astype(np.float64)
    cosm = np.cos(ang)
    sinm = np.sin(ang)
    sinm[0, :] = np.where(k % 2 == 0, 1.0, -1.0)
    tk, _ = _hy_tiles(seq_len)
    fwd = np.concatenate([np.concatenate([cosm[i:i + tk], sinm[i:i + tk]], axis=0)
                          for i in range(0, seq_len, tk)], axis=0)
    inv = np.concatenate([cosm, sinm.T], axis=1)
    t = np.linspace(0.0, 1.0, seq_len, dtype=np.float32)[:, None]
    bands = (FILTER_EMB - 1) // 2
    fb = np.linspace(1e-4, bands - 1, bands, dtype=np.float32)[None, :]
    w = (np.float32(2.0 * math.pi) * np.arange(seq_len, dtype=np.float32)[:, None] / np.float32(seq_len)).astype(np.float32)
    feats = np.concatenate([t, np.cos(fb * w), -np.sin(fb * w)], axis=-1).astype(np.float32)
    feats = np.ascontiguousarray(np.pad(feats, ((0, 0), (0, LANES - FILTER_EMB))).T)
    decay_hi = math.log(HY_DECAY_TARGET) / HY_DECAY_HI_PCT
    decay_lo = math.log(HY_DECAY_TARGET) / HY_DECAY_LO_PCT
    deltas = np.abs(np.linspace(decay_lo, decay_hi, HY_W, dtype=np.float32))
    window = np.exp(-t * deltas[None, :]).astype(np.float32)
    return fwd, inv, feats, window


def _hy_consts(seq_len):
    fwd, inv, feats, window = _hy_consts_np(seq_len)
    as_bf = lambda m: jnp.asarray(m, dtype=f32).astype(bf16)
    return {"fwd": as_bf(fwd), "inv": as_bf(inv), "feats": jnp.asarray(feats), "window": jnp.asarray(window)}


def _hyena(zh, lp, consts, batch, seq_len):
    zh = zh.reshape(batch, seq_len, 3 * HY_W)
    hb = _hy_filter(seq_len, consts, lp["hy_f_w1"], lp["hy_f_b1"], lp["hy_f_w2"], lp["hy_f_b2"],
                    lp["hy_f_w3"], lp["hy_freq"]).reshape(2, seq_len, HY_W)
    rz = _hy_fwd(consts["fwd"], zh, lp["hy_conv_w"], lp["hy_conv_b"], hb, batch, seq_len)
    yh = _hy_inv(consts["inv"], rz, zh, lp["hy_conv_w"], lp["hy_conv_b"], lp["hy_bias"], batch, seq_len)
    return yh.reshape(batch * seq_len, HY_W)


def _gelu(x):
    return 0.5 * x * (1.0 + jnp.tanh(math.sqrt(2.0 / math.pi) * (x + 0.044715 * (x * x * x))))


def _gmlp(z, ws_ref, bias):
    g = _gelu(z)
    outs = []
    for c in range(TM // GM_CHUNK):
        rows = slice(c * GM_CHUNK, (c + 1) * GM_CHUNK)
        u = g[rows, :GM_W]
        v = g[rows, GM_W:].astype(bf16)
        s = jnp.concatenate(
            [jnp.dot(ws_ref[h], v[:, h * GM_HEAD_DIM:(h + 1) * GM_HEAD_DIM], preferred_element_type=f32)
             for h in range(GM_HEADS)], axis=1)
        outs.append(u * (s + bias))
    return jnp.concatenate(outs, axis=0)


def _out_kernel(a_ref, yh_ref, zg_ref, ws_ref, gb_ref, x_ref, mod_ref, og_ref, w_ref, g2_ref, rw_ref,
                x1_ref, h2_ref, aff_ref):
    og = og_ref[...]
    yg = _gmlp(zg_ref[...].astype(f32), ws_ref, gb_ref[...])
    mixed_in = jnp.concatenate([
        _rms(a_ref[...].astype(f32)) * og[:, :ATTN_W],
        _rms(yh_ref[...].astype(f32)) * og[:, ATTN_W:ATTN_W + HY_W],
        _rms(yg) * og[:, ATTN_W + HY_W:]], axis=1)
    mixed = jnp.dot(mixed_in.astype(bf16), w_ref[...], preferred_element_type=f32)
    m = mod_ref[0]
    x1 = x_ref[...] + m[2:3] * mixed
    x1_ref[...] = x1
    h2 = _rms(x1) * g2_ref[...] * (1.0 + m[4:5]) + m[3:4]
    for j in range(ROW_CHUNKS):
        h2_ref[pl.ds(j, TM, stride=ROW_CHUNKS), :] = h2[:, j * LANES:(j + 1) * LANES]
    nt = (((1,), (1,)), ((), ()))
    h_hi = h2.astype(bf16)
    h_lo = (h2 - h_hi.astype(f32)).astype(bf16)
    rw = rw_ref[...]
    t = lax.dot_general(rw, h_hi, nt, preferred_element_type=f32)
    logits = (t[:N_EXPERTS] + t[N_EXPERTS:]) + lax.dot_general(rw[:N_EXPERTS], h_lo, nt, preferred_element_type=f32)
    e = jnp.exp(logits - jnp.max(logits, axis=0, keepdims=True))
    aff_ref[...] = e / jnp.sum(e, axis=0, keepdims=True)


def _out_proj(a, yh, zg, ws_bf, gm_b, x, mod, out_g, w_out_bf, g2, router_wt):
    n = x.shape[0]
    per_mod = n // mod.shape[0]
    row = lambda w: pl.BlockSpec((TM, w), lambda i: (i, 0))
    const = lambda s: pl.BlockSpec(s, lambda i: (0,) * len(s))
    gm_bias = jnp.repeat(gm_b.T, GM_HEAD_DIM, axis=1)
    return pl.pallas_call(
        _out_kernel,
        grid=(n // TM,),
        in_specs=[row(ATTN_W), row(HY_W), row(2 * GM_W), const((GM_HEADS, GM_CHUNK, GM_CHUNK)),
                  const((GM_CHUNK, GM_W)), row(D_MODEL),
                  pl.BlockSpec((1, 6, D_MODEL), lambda i: (i * TM // per_mod, 0, 0)),
                  const((1, MIX_W)), const((MIX_W, D_MODEL)), const((1, D_MODEL)), const((2 * N_EXPERTS, D_MODEL))],
        out_specs=[row(D_MODEL), pl.BlockSpec((TM * ROW_CHUNKS, LANES), lambda i: (i, 0)),
                   pl.BlockSpec((N_EXPERTS, TM), lambda i: (0, i))],
        out_shape=[jax.ShapeDtypeStruct((n, D_MODEL), f32), jax.ShapeDtypeStruct((n * ROW_CHUNKS, LANES), f32),
                   jax.ShapeDtypeStruct((N_EXPERTS, n), f32)],
        compiler_params=_params("parallel"),
        name="out_proj",
    )(a, yh, zg, ws_bf, gm_bias, x, mod, out_g.reshape(1, MIX_W), w_out_bf, g2.reshape(1, D_MODEL), router_wt)


RT_CHUNK = 512


def _prefix_incl(x01, tri):
    n = x01.shape[1]
    carry = jnp.zeros((x01.shape[0], 1), f32)
    parts = []
    for c in range(n // RT_CHUNK):
        piece = x01[:, c * RT_CHUNK:(c + 1) * RT_CHUNK]
        parts.append(jnp.dot(piece.astype(bf16), tri, preferred_element_type=f32) + carry)
        carry = carry + jnp.sum(piece, axis=1, keepdims=True)
    return jnp.concatenate(parts, axis=1)


def _route_kernel(cap, aff_ref, idx_ref, gate_ref):
    aff = aff_ref[...]
    n = aff.shape[1]

    def step(it, t):
        cand = t | (jnp.int32(1) << (30 - it))
        cnt = jnp.sum(jnp.where(aff >= lax.bitcast_convert_type(cand, f32), 1.0, 0.0), axis=1, keepdims=True)
        return jnp.where(cnt >= cap, cand, t)

    thr_bits = lax.fori_loop(0, 31, step, jnp.zeros((aff.shape[0], 1), jnp.int32))
    thr = lax.bitcast_convert_type(thr_bits, f32)
    gt = jnp.where(aff > thr, 1.0, 0.0)
    eq = jnp.where(aff == thr, 1.0, 0.0)
    room = cap - jnp.sum(gt, axis=1, keepdims=True)
    r = lax.broadcasted_iota(jnp.int32, (RT_CHUNK, RT_CHUNK), 0)
    c = lax.broadcasted_iota(jnp.int32, (RT_CHUNK, RT_CHUNK), 1)
    tri = jnp.where(r <= c, 1.0, 0.0).astype(bf16)
    sel = jnp.maximum(gt, jnp.where(_prefix_incl(eq, tri) <= room, eq, 0.0))
    slot = _prefix_incl(sel, tri) - 1.0

    tok = lax.broadcasted_iota(jnp.int32, aff.shape, 1)
    dist = jnp.where(sel > 0.0, tok - slot.astype(jnp.int32), 0)
    g = aff
    for b in range(max(1, (n - 1).bit_length())):
        sh = 1 << b
        dist_s = pltpu.roll(dist, n - sh, axis=1)
        take = (dist_s & sh) != 0
        leave = (dist & sh) != 0
        tok = jnp.where(take, pltpu.roll(tok, n - sh, axis=1), tok)
        g = jnp.where(take, pltpu.roll(g, n - sh, axis=1), g)
        dist = jnp.where(take, dist_s, jnp.where(leave, 0, dist))
    idx_ref[...] = tok[:, :cap] * ROW_CHUNKS
    gate_ref[...] = g[:, :cap]


def _route(aff_t, cap):
    return pl.pallas_call(
        functools.partial(_route_kernel, cap),
        in_specs=[pl.BlockSpec(memory_space=pltpu.VMEM)],
        out_specs=[pl.BlockSpec(memory_space=pltpu.VMEM)] * 2,
        out_shape=[jax.ShapeDtypeStruct((N_EXPERTS, cap), jnp.int32), jax.ShapeDtypeStruct((N_EXPERTS, cap), f32)],
        compiler_params=pltpu.CompilerParams(vmem_limit_bytes=VMEM_LIMIT),
        name="route",
    )(aff_t)


GATHER_UNROLL = 8
SCATTER_UNROLL = 8
CMB_EXPERTS = 2
CMB_TM = 1024


def _moe_ffn_kernel(cap, stride, idx_ref, h_ref, w1_ref, w3_ref, w2_ref, y_ref, tile_ref, xb_ref, acc_ref):
    e = pl.program_id(0)
    slab = ROW_CHUNKS * stride
    cur = pl.multiple_of((e % 2) * slab, SUBLANES)
    nxt = pl.multiple_of(((e + 1) % 2) * slab, SUBLANES)

    def gather_row(expert, slab_row, r):
        src = pl.multiple_of(idx_ref[expert * cap + r], ROW_CHUNKS)
        tile_ref[pl.ds(slab_row + r, ROW_CHUNKS, stride=stride), :] = h_ref[pl.ds(src, ROW_CHUNKS), :]

    @pl.when(e == 0)
    def _():
        def gather(q, carry):
            for t in range(GATHER_UNROLL):
                gather_row(0, cur, q * GATHER_UNROLL + t)
            return carry
        lax.fori_loop(0, cap // GATHER_UNROLL, gather, 0)

    xb_ref[...] = jnp.concatenate(
        [tile_ref[pl.ds(cur + j * stride, cap), :] for j in range(ROW_CHUNKS)], axis=1).astype(bf16)

    e_next = jnp.minimum(e + 1, N_EXPERTS - 1)
    n_f = D_EXPERT // MOE_F_TILE
    xb = xb_ref[...]
    for f in range(n_f):
        cols = slice(f * MOE_F_TILE, (f + 1) * MOE_F_TILE)
        a = jnp.dot(xb, w1_ref[0, 0, :, cols].astype(bf16), preferred_element_type=f32)
        b = jnp.dot(xb, w3_ref[0, 0, :, cols].astype(bf16), preferred_element_type=f32)
        he = (a * _sigmoid(a) * b).astype(bf16)
        part = jnp.dot(he, w2_ref[0, 0, cols, :].astype(bf16), preferred_element_type=f32)
        if f == 0:
            acc_ref[...] = part
        else:
            acc_ref[...] += part
        for r in range(f * cap // n_f, (f + 1) * cap // n_f):
            gather_row(e_next, nxt, r)

    y = acc_ref[...]
    for j in range(ROW_CHUNKS):
        y_ref[0, pl.ds(j, cap, stride=ROW_CHUNKS), :] = y[:, j * LANES:(j + 1) * LANES]


def _moe_ffn(layer, h3, idx, w1, w3, w2):
    cap = idx.shape[1]
    stride = cap + SUBLANES
    grid_spec = pltpu.PrefetchScalarGridSpec(
        num_scalar_prefetch=1,
        grid=(N_EXPERTS,),
        in_specs=[
            pl.BlockSpec(memory_space=pltpu.VMEM),
            pl.BlockSpec((1, 1, D_MODEL, D_EXPERT), lambda e, idx: (layer, e, 0, 0)),
            pl.BlockSpec((1, 1, D_MODEL, D_EXPERT), lambda e, idx: (layer, e, 0, 0)),
            pl.BlockSpec((1, 1, D_EXPERT, D_MODEL), lambda e, idx: (layer, e, 0, 0)),
        ],
        out_specs=pl.BlockSpec((1, ROW_CHUNKS * cap, LANES), lambda e, idx: (e, 0, 0)),
        scratch_shapes=[pltpu.VMEM((2 * ROW_CHUNKS * stride, LANES), f32),
                        pltpu.VMEM((cap, D_MODEL), bf16),
                        pltpu.VMEM((cap, D_MODEL), f32)],
    )
    return pl.pallas_call(
        functools.partial(_moe_ffn_kernel, cap, stride),
        grid_spec=grid_spec,
        out_shape=jax.ShapeDtypeStruct((N_EXPERTS, ROW_CHUNKS * cap, LANES), f32),
        compiler_params=_params("arbitrary"),
        name="moe_ffn",
    )(idx.reshape(-1), h3, w1, w3, w2)


def _moe_combine_kernel(cap, final, idx_ref, gate_ref, y_ref, x_ref, mod_ref, g_ref, o_ref, acc_ref):
    s = pl.program_id(0)
    scatter_steps = N_EXPERTS // CMB_EXPERTS

    @pl.when(s == 0)
    def _():
        acc_ref[...] = jnp.zeros_like(acc_ref)

    @pl.when(s < scatter_steps)
    def _():
        for k in range(CMB_EXPERTS):
            base = (s * CMB_EXPERTS + k) * cap

            def scatter(q, carry, k=k, base=base):
                group = SCATTER_UNROLL * ROW_CHUNKS
                y = y_ref[k, pl.ds(pl.multiple_of(q * group, group), group), :]
                rows = [pl.multiple_of(idx_ref[base + q * SCATTER_UNROLL + t], ROW_CHUNKS)
                        for t in range(SCATTER_UNROLL)]
                new = [acc_ref[pl.ds(rows[t], ROW_CHUNKS), :]
                       + gate_ref[base + q * SCATTER_UNROLL + t] * y[t * ROW_CHUNKS:(t + 1) * ROW_CHUNKS]
                       for t in range(SCATTER_UNROLL)]
                for t in range(SCATTER_UNROLL):
                    acc_ref[pl.ds(rows[t], ROW_CHUNKS), :] = new[t]
                return carry
            lax.fori_loop(0, cap // SCATTER_UNROLL, scatter, 0)

    @pl.when(s >= scatter_steps)
    def _():
        first = pl.multiple_of((s - scatter_steps) * (CMB_TM * ROW_CHUNKS), CMB_TM * ROW_CHUNKS)
        moe = jnp.concatenate([acc_ref[pl.ds(first + j, CMB_TM, stride=ROW_CHUNKS), :] for j in range(ROW_CHUNKS)],
                              axis=1)
        x = x_ref[...] + mod_ref[0][5:6] * moe
        o_ref[...] = _rms(x) * g_ref[...] if final else x


def _moe_combine(idx, gate, y_cm, x1, mod, final_g, final):
    n = x1.shape[0]
    cap = idx.shape[1]
    per_mod = n // mod.shape[0]
    scatter_steps = N_EXPERTS // CMB_EXPERTS
    tile = lambda s: jnp.maximum(s - scatter_steps, 0)
    row = pl.BlockSpec((CMB_TM, D_MODEL), lambda s, *_: (tile(s), 0))
    grid_spec = pltpu.PrefetchScalarGridSpec(
        num_scalar_prefetch=2,
        grid=(scatter_steps + n // CMB_TM,),
        in_specs=[pl.BlockSpec((CMB_EXPERTS, ROW_CHUNKS * cap, LANES),
                               lambda s, *_: (jnp.minimum(s, scatter_steps - 1), 0, 0)),
                  row,
                  pl.BlockSpec((1, 6, D_MODEL), lambda s, *_: (tile(s) * CMB_TM // per_mod, 0, 0)),
                  pl.BlockSpec((1, D_MODEL), lambda s, *_: (0, 0))],
        out_specs=row,
        scratch_shapes=[pltpu.VMEM((n * ROW_CHUNKS, LANES), f32)],
    )
    return pl.pallas_call(
        functools.partial(_moe_combine_kernel, cap, final),
        grid_spec=grid_spec,
        out_shape=jax.ShapeDtypeStruct((n, D_MODEL), f32),
        compiler_params=_params("arbitrary"),
        name="moe_combine",
    )(idx.reshape(-1), gate.reshape(-1), y_cm, x1, mod, final_g.reshape(1, D_MODEL))


def _expert_choice(layer, h3, aff_t, x1, mod, final_g, final, w1, w3, w2):
    n = x1.shape[0]
    cap = max(1, EC_CAPACITY * n // N_EXPERTS)
    idx, gate = _route(aff_t, cap)
    y_cm = _moe_ffn(layer, h3, idx, w1, w3, w2)
    return _moe_combine(idx, gate, y_cm, x1, mod, final_g, final)


def _split_bf16(w):
    hi = w.astype(bf16)
    lo = (w - hi.astype(f32)).astype(bf16)
    return jnp.concatenate([hi, lo], axis=0)


def _stream(x, mods, layers, experts, batch, seq_len, final_g, caches=None):
    consts = _hy_consts(seq_len)
    rope_tabs = _rope_tables(seq_len) if caches is not None else None
    new_caches = None
    for l, lp in enumerate(layers):
        if caches is None:
            qkv, zh, zg, *new_caches = _in_proj(x, mods[l], lp["norm1_g"], lp["w_in"], seq_len, cache_layer=l,
                                                caches_so_far=new_caches)
            a = _ctx_attention(qkv, lp["attn_sink"], batch, seq_len)
        else:
            qkv, zh, zg = _in_proj(x, mods[l], lp["norm1_g"], lp["w_in"], seq_len, rope_tabs=rope_tabs)
            a = _lat_attention(qkv, lp["attn_sink"], caches[0], caches[1], l, batch, seq_len)
        yh = _hyena(zh, lp, consts, batch, seq_len)
        x1, h3, aff_t = _out_proj(a, yh, zg, lp["gm_ws"], lp["gm_b"], x, mods[l], lp["out_norm_g"], lp["w_out"],
                                  lp["norm2_g"], lp["router_wt"])
        x = _expert_choice(l, h3, aff_t, x1, mods[l], final_g, l == len(layers) - 1, *experts)
    return x, new_caches


def kernel(x_prompt, x_sample, c, cache_k, cache_v, c_ctx, norm1_g, norm2_g, ada_w, ada_b, w_in, attn_sink,
           hy_conv_w, hy_conv_b, hy_f_w1, hy_f_b1, hy_f_w2, hy_f_b2, hy_f_w3, hy_freq, hy_bias, gm_ws, gm_b,
           out_norm_g, w_out, router_w, exp_w1, exp_w3, exp_w2, final_g):
    batch, seq, _ = x_prompt.shape
    dbatch, dseq, _ = x_sample.shape
    past = cache_k.shape[2]

    cvec = jnp.concatenate([c_ctx[None], c, jnp.zeros((SUBLANES - 1 - dbatch, D_MODEL), f32)], axis=0)
    mod = _ada(cvec, ada_w, ada_b)
    mods_ctx = [mod[l, 0:1].reshape(1, 6, D_MODEL) for l in range(DEPTH)]
    mods_lat = [mod[l, 1:1 + dbatch].reshape(dbatch, 6, D_MODEL) for l in range(DEPTH)]

    w_in_bf, w_out_bf, gm_ws_bf = w_in.astype(bf16), w_out.astype(bf16), gm_ws.astype(bf16)
    layers = []
    for l in range(DEPTH):
        layers.append({
            "norm1_g": norm1_g[l], "norm2_g": norm2_g[l], "w_in": w_in_bf[l], "attn_sink": attn_sink[l],
            "hy_conv_w": hy_conv_w[l], "hy_conv_b": hy_conv_b[l], "hy_f_w1": hy_f_w1[l], "hy_f_b1": hy_f_b1[l],
            "hy_f_w2": hy_f_w2[l], "hy_f_b2": hy_f_b2[l], "hy_f_w3": hy_f_w3[l], "hy_freq": hy_freq[l],
            "hy_bias": hy_bias[l], "gm_ws": gm_ws_bf[l], "gm_b": gm_b[l], "out_norm_g": out_norm_g[l],
            "w_out": w_out_bf[l], "router_wt": _split_bf16(router_w[l].T)})
    experts = (exp_w1, exp_w3, exp_w2)

    yp, kvs = _stream(x_prompt.reshape(batch * seq, D_MODEL), mods_ctx, layers, experts, batch, seq, final_g)
    caches = (cache_k.reshape(dbatch, DEPTH, past, KV_W), cache_v.reshape(dbatch, DEPTH, past, KV_W))
    ys, _ = _stream(x_sample.reshape(dbatch * dseq, D_MODEL), mods_lat, layers, experts, dbatch, dseq, final_g,
                   caches=caches)

    new_k, new_v = (a.reshape(batch, DEPTH, seq, N_KV, HEAD_DIM) for a in kvs)
    return (yp.reshape(batch, seq, D_MODEL), ys.reshape(dbatch, dseq, D_MODEL), new_k, new_v)
```

```python
import functools
import math

import numpy as np
import jax
import jax.numpy as jnp
from jax import lax
from jax.experimental import pallas as pl
from jax.experimental.pallas import tpu as pltpu

f32 = jnp.float32
bf16 = jnp.bfloat16

D_MODEL = 1024
DEPTH = 2
GRID_W = 64
BLOCK = 128
N_HEADS = 8
N_KV = 2
HEAD_DIM = 64
Q_GROUP = N_HEADS // N_KV
ATTN_W = N_HEADS * HEAD_DIM
KV_W = N_KV * HEAD_DIM
QKV_W = ATTN_W + 2 * KV_W
HY_W = 256
GM_W = 256
GM_HEADS = 4
GM_HEAD_DIM = GM_W // GM_HEADS
GM_CHUNK = 128
MIX_W = ATTN_W + HY_W + GM_W
IN_W = ATTN_W + 2 * KV_W + 3 * HY_W + 2 * GM_W
FILTER_EMB = 33
FILTER_HID = 64
HY_DECAY_HI_PCT = 0.3
HY_DECAY_LO_PCT = 1.5
HY_DECAY_TARGET = 1e-2
N_EXPERTS = 16
EC_CAPACITY = 2
D_EXPERT = 1024
ROPE_THETA = 10000.0
EPS = 1e-6
NEG = -1e30

LANES = 128
SUBLANES = 8
ROW_CHUNKS = D_MODEL // LANES
VMEM_LIMIT = 56 * 1024 * 1024
TM = 512
IN_TM = 512
MOE_F_TILE = 256


def _params(*sem):
    return pltpu.CompilerParams(dimension_semantics=sem, vmem_limit_bytes=VMEM_LIMIT)


def _rms(x):
    return x * lax.rsqrt(jnp.mean(x * x, axis=-1, keepdims=True) + EPS)


def _sigmoid(x):
    return 1.0 / (1.0 + jnp.exp(-x))


def _ada_kernel(c_ref, w_ref, b_ref, o_ref):
    c = c_ref[...]
    s = (c * _sigmoid(c)).astype(bf16)
    o_ref[0] = jnp.dot(s, w_ref[0].astype(bf16), preferred_element_type=f32) + b_ref[0]


def _ada(cvec, ada_w, ada_b):
    nt = 6
    return pl.pallas_call(
        _ada_kernel,
        grid=(DEPTH, nt),
        in_specs=[
            pl.BlockSpec((SUBLANES, D_MODEL), lambda l, j: (0, 0)),
            pl.BlockSpec((1, D_MODEL, D_MODEL), lambda l, j: (l, 0, j)),
            pl.BlockSpec((1, 1, D_MODEL), lambda l, j: (l, 0, j)),
        ],
        out_specs=pl.BlockSpec((1, SUBLANES, D_MODEL), lambda l, j: (l, 0, j)),
        out_shape=jax.ShapeDtypeStruct((DEPTH, SUBLANES, 6 * D_MODEL), f32),
        compiler_params=_params("arbitrary", "arbitrary"),
        name="ada",
    )(cvec, ada_w, ada_b.reshape(DEPTH, 1, 6 * D_MODEL))


def _rope_swap(x):
    w = x.shape[-1]
    lane = lax.broadcasted_iota(jnp.int32, x.shape, 1)
    first = (lane % 32) < 16
    return jnp.where(first, pltpu.roll(x, w - 16, axis=1), pltpu.roll(x, 16, axis=1))


def _in_kernel(rope, cache_rows, cache_slot, n_alias, x_ref, mod_ref, g_ref, w_ref, *refs):
    refs = list(refs)
    if rope:
        cos_ref, sin_ref = refs[:2]
        refs = refs[2:]
    refs = refs[n_alias:]
    qkv_ref, zh_ref, zg_ref = refs[:3]
    m = mod_ref[0]
    h = _rms(x_ref[...]) * g_ref[...] * (1.0 + m[1:2]) + m[0:1]
    z = jnp.dot(h.astype(bf16), w_ref[...], preferred_element_type=f32)
    if rope:
        reps = (ATTN_W + KV_W) // LANES
        cos = jnp.concatenate([cos_ref[...]] * reps, axis=1)
        sin = jnp.concatenate([sin_ref[...]] * reps, axis=1)
        qk = z[:, :ATTN_W + KV_W]
        qkv_ref[:, :ATTN_W + KV_W] = qk * cos + _rope_swap(qk) * sin
        qkv_ref[:, ATTN_W + KV_W:] = z[:, ATTN_W + KV_W:QKV_W]
    else:
        qkv_ref[...] = z[:, :QKV_W]
    zh_ref[...] = z[:, QKV_W:QKV_W + 3 * HY_W].astype(bf16)
    zg_ref[...] = z[:, QKV_W + 3 * HY_W:].astype(bf16)
    if cache_rows:
        k_ref, v_ref = refs[3:]
        for s in range(z.shape[0] // cache_rows):
            rows = slice(s * cache_rows, (s + 1) * cache_rows)
            for d in range(k_ref.shape[1]):
                if d == cache_slot:
                    k_ref[s, d] = z[rows, ATTN_W:ATTN_W + KV_W]
                    v_ref[s, d] = z[rows, ATTN_W + KV_W:QKV_W]
                else:
                    k_ref[s, d] = jnp.zeros((cache_rows, KV_W), f32)
                    v_ref[s, d] = jnp.zeros((cache_rows, KV_W), f32)


def _in_proj(x, mod, g1, w_in_bf, seq_len, rope_tabs=None, cache_layer=None, caches_so_far=None):
    n = x.shape[0]
    rope = rope_tabs is not None
    per_mod = n // mod.shape[0]
    tm = IN_TM
    in_specs = [pl.BlockSpec((tm, D_MODEL), lambda i: (i, 0)),
                pl.BlockSpec((1, 6, D_MODEL), lambda i: (i * tm // per_mod, 0, 0)),
                pl.BlockSpec((1, D_MODEL), lambda i: (0, 0)),
                pl.BlockSpec((D_MODEL, IN_W), lambda i: (0, 0))]
    args = [x, mod, g1.reshape(1, D_MODEL), w_in_bf]
    if rope:
        nt = seq_len // tm
        tab = pl.BlockSpec((tm, LANES), lambda i: (i % nt, 0))
        in_specs += [tab, tab]
        args += list(rope_tabs)
    out_specs = [pl.BlockSpec((tm, w), lambda i: (i, 0)) for w in (QKV_W, 3 * HY_W, 2 * GM_W)]
    out_shape = [jax.ShapeDtypeStruct((n, QKV_W), f32), jax.ShapeDtypeStruct((n, 3 * HY_W), bf16),
                 jax.ShapeDtypeStruct((n, 2 * GM_W), bf16)]
    aliases, n_alias, cache_slot = {}, 0, 0
    if cache_layer is not None:
        seqs = tm // seq_len
        if caches_so_far is None:
            cache = pl.BlockSpec((seqs, DEPTH, seq_len, KV_W), lambda i: (i, 0, 0, 0))
            cache_slot = cache_layer
        else:
            cache = pl.BlockSpec((seqs, 1, seq_len, KV_W), lambda i: (i, cache_layer, 0, 0))
            n_alias = 2
            aliases = {len(args): 3, len(args) + 1: 4}
            in_specs += [pl.BlockSpec(memory_space=pl.ANY)] * 2
            args += list(caches_so_far)
        out_specs += [cache, cache]
        out_shape += [jax.ShapeDtypeStruct((n // seq_len, DEPTH, seq_len, KV_W), f32)] * 2
    return pl.pallas_call(
        functools.partial(_in_kernel, rope, seq_len if cache_layer is not None else 0, cache_slot, n_alias),
        grid=(n // tm,),
        in_specs=in_specs,
        out_specs=out_specs,
        out_shape=out_shape,
        input_output_aliases=aliases,
        compiler_params=_params("parallel"),
        name="in_proj",
    )(*args)


def _rope_tables(seq_len):
    nf = HEAD_DIM // 4
    t = np.arange(seq_len)
    inv = (ROPE_THETA ** (-np.arange(nf, dtype=np.float32) / nf)).astype(np.float32)
    d = np.arange(HEAD_DIM)
    pos = np.where((d // 32)[None, :] == 0, (t // GRID_W)[:, None], (t % GRID_W)[:, None]).astype(np.float32)
    ang = (pos * inv[d % nf][None, :]).astype(np.float32)
    cos = np.cos(ang).astype(np.float32)
    sin = np.sin(ang).astype(np.float32) * np.where((d % 32) < 16, -1.0, 1.0)[None, :].astype(np.float32)
    return jnp.asarray(np.tile(cos, (1, 2))), jnp.asarray(np.tile(sin, (1, 2)))


LOG2E = math.log2(math.e)


def _stack_queries(z, sink_ref, kv):
    rows = z.shape[0]
    heads = range(kv * Q_GROUP, (kv + 1) * Q_GROUP)
    qs = jnp.concatenate([z[:, h * HEAD_DIM:(h + 1) * HEAD_DIM] for h in heads], axis=0)
    sink = jnp.concatenate([jnp.full((1, rows), sink_ref[h] * LOG2E, f32) for h in heads], axis=1)
    return (qs * (LOG2E / math.sqrt(HEAD_DIM))).astype(bf16), sink


def _scores_t(k, qs_bf):
    return lax.dot_general(k.astype(bf16), qs_bf, (((1,), (1,)), ((), ())), preferred_element_type=f32)


def _softmax_pv_t(st, sink, vals, rows):
    m = jnp.maximum(jnp.max(st, axis=0, keepdims=True), sink)
    pt = jnp.exp2(st - m).astype(bf16)
    v_bf = vals.astype(bf16)
    v_ext = jnp.concatenate([v_bf, jnp.ones_like(v_bf)], axis=1)
    ovt = lax.dot_general(v_ext, pt, (((0,), (0,)), ((), ())), preferred_element_type=f32)
    ot = ovt[:HEAD_DIM] / (ovt[HEAD_DIM:HEAD_DIM + 1] + jnp.exp2(sink - m))
    return [ot[:, g * rows:(g + 1) * rows].T for g in range(Q_GROUP)]


CTX_BATCHES = 4
LAT_QBLOCKS = 8


def _ctx_attn_kernel(seq_len, sink_ref, z_ref, o_ref):
    for s in range(CTX_BATCHES):
        z = z_ref[s * seq_len:(s + 1) * seq_len, :]
        outs = []
        for kv in range(N_KV):
            k = z[:, ATTN_W + kv * HEAD_DIM:ATTN_W + (kv + 1) * HEAD_DIM]
            v = z[:, ATTN_W + KV_W + kv * HEAD_DIM:ATTN_W + KV_W + (kv + 1) * HEAD_DIM]
            qs, sink = _stack_queries(z, sink_ref, kv)
            outs += _softmax_pv_t(_scores_t(k, qs), sink, v, seq_len)
        o_ref[s * seq_len:(s + 1) * seq_len, :] = jnp.concatenate(outs, axis=1).astype(o_ref.dtype)


def _ctx_attention(z, sink, batch, seq_len):
    rows = CTX_BATCHES * seq_len
    return pl.pallas_call(
        functools.partial(_ctx_attn_kernel, seq_len),
        grid=(batch // CTX_BATCHES,),
        in_specs=[pl.BlockSpec(memory_space=pltpu.SMEM),
                  pl.BlockSpec((rows, QKV_W), lambda b: (b, 0))],
        out_specs=pl.BlockSpec((rows, ATTN_W), lambda b: (b, 0)),
        out_shape=jax.ShapeDtypeStruct((batch * seq_len, ATTN_W), bf16),
        compiler_params=_params("parallel"),
        name="ctx_attn",
    )(sink, z)


def _lat_attn_kernel(nb, sink_ref, zp_ref, zc_ref, zn_ref, ck_ref, cv_ref, o_ref):
    step = pl.program_id(1)
    zc_all = zc_ref[...]
    blocks = [zp_ref[...]] + [zc_all[q * BLOCK:(q + 1) * BLOCK] for q in range(LAT_QBLOCKS)] + [zn_ref[...]]
    ck, cv = ck_ref[0, 0], cv_ref[0, 0]
    width = Q_GROUP * BLOCK
    j = lax.broadcasted_iota(jnp.int32, (BLOCK, width), 0)
    r = lax.broadcasted_iota(jnp.int32, (BLOCK, width), 1) % BLOCK
    for q in range(LAT_QBLOCKS):
        i = step * LAT_QBLOCKS + q
        zp, zc, zn = blocks[q], blocks[q + 1], blocks[q + 2]
        ok_prev = j >= r + jnp.where(i > 0, 0, BLOCK)
        ok_next = j <= r - jnp.where(i < nb - 1, 0, BLOCK)
        outs = []
        for kv in range(N_KV):
            ks = slice(ATTN_W + kv * HEAD_DIM, ATTN_W + (kv + 1) * HEAD_DIM)
            vs = slice(ATTN_W + KV_W + kv * HEAD_DIM, ATTN_W + KV_W + (kv + 1) * HEAD_DIM)
            cs = slice(kv * HEAD_DIM, (kv + 1) * HEAD_DIM)
            qs, sink = _stack_queries(zc, sink_ref, kv)
            st = jnp.concatenate([
                jnp.where(ok_prev, _scores_t(zp[:, ks], qs), NEG),
                _scores_t(zc[:, ks], qs),
                jnp.where(ok_next, _scores_t(zn[:, ks], qs), NEG),
                _scores_t(ck[:, cs], qs)], axis=0)
            vals = jnp.concatenate([zp[:, vs], zc[:, vs], zn[:, vs], cv[:, cs]], axis=0)
            outs += _softmax_pv_t(st, sink, vals, BLOCK)
        o_ref[q * BLOCK:(q + 1) * BLOCK, :] = jnp.concatenate(outs, axis=1).astype(o_ref.dtype)


def _lat_attention(z, sink, ck, cv, layer, batch, seq_len):
    nb = seq_len // BLOCK
    ns = nb // LAT_QBLOCKS
    blk = lambda f: pl.BlockSpec((BLOCK, QKV_W), f)
    past = ck.shape[2]
    cache = pl.BlockSpec((1, 1, past, KV_W), lambda b, i: (b, layer, 0, 0))
    return pl.pallas_call(
        functools.partial(_lat_attn_kernel, nb),
        grid=(batch, ns),
        in_specs=[pl.BlockSpec(memory_space=pltpu.SMEM),
                  blk(lambda b, i: (b * nb + jnp.maximum(i * LAT_QBLOCKS - 1, 0), 0)),
                  pl.BlockSpec((LAT_QBLOCKS * BLOCK, QKV_W), lambda b, i: (b * ns + i, 0)),
                  blk(lambda b, i: (b * nb + jnp.minimum((i + 1) * LAT_QBLOCKS, nb - 1), 0)),
                  cache, cache],
        out_specs=pl.BlockSpec((LAT_QBLOCKS * BLOCK, ATTN_W), lambda b, i: (b * ns + i, 0)),
        out_shape=jax.ShapeDtypeStruct((batch * seq_len, ATTN_W), bf16),
        compiler_params=_params("parallel", "parallel"),
        name="lat_attn",
    )(sink, z, z, z, ck, cv)


HY_TL = 256
HALO_ROWS = 2 * SUBLANES


def _short_conv(z, prev_tile, next_tile, has_prev, has_next, w, b):
    z = z.astype(f32)
    rows = z.shape[0]
    row = lax.broadcasted_iota(jnp.int32, (SUBLANES, z.shape[1]), 0)
    prev_row = jnp.where(has_prev, prev_tile.astype(f32)[HALO_ROWS - 1:HALO_ROWS, :], 0.0)
    next_row = jnp.where(has_next, next_tile.astype(f32)[0:1, :], 0.0)
    z_prev = pltpu.roll(z, 1, axis=0)
    z_prev = jnp.concatenate([jnp.where(row == 0, prev_row, z_prev[:SUBLANES]), z_prev[SUBLANES:]], axis=0)
    z_next = pltpu.roll(z, rows - 1, axis=0)
    z_next = jnp.concatenate([z_next[:rows - SUBLANES],
                              jnp.where(row == SUBLANES - 1, next_row, z_next[rows - SUBLANES:])], axis=0)
    zc = z_prev * w[0:1] + z * w[1:2] + z_next * w[2:3] + b
    return zc[:, :HY_W], zc[:, HY_W:2 * HY_W] * zc[:, 2 * HY_W:]


def _hy_filter_kernel(seq_len, feats_ref, win_ref, w1_ref, b1_ref, w2_ref, b2_ref, w3_ref, fr_ref, o_ref):
    hi = lax.Precision.HIGHEST
    fr = fr_ref[...]
    h = jnp.sin(fr * (jnp.dot(w1_ref[...], feats_ref[...], precision=hi, preferred_element_type=f32) + b1_ref[...]))
    h = jnp.sin(fr * (jnp.dot(w2_ref[...], h, precision=hi, preferred_element_type=f32) + b2_ref[...]))
    h = jnp.dot(h.T, w3_ref[...], precision=hi, preferred_element_type=f32)
    win = win_ref[...]
    row = lax.broadcasted_iota(jnp.int32, (seq_len, HY_W), 0)
    hf = h[:, :HY_W] * win
    hb = jnp.where(row == 0, 0.0, h[:, HY_W:] * win)
    o_ref[0:seq_len, :] = hf + hb
    o_ref[seq_len:, :] = hb - hf


def _hy_filter(seq_len, consts, w1, b1, w2, b2, w3, freq):
    w1t = jnp.zeros((FILTER_HID, LANES), f32).at[:, :FILTER_EMB].set(w1.T)
    args = (consts["feats"], consts["window"], w1t, b1.reshape(-1, 1), w2.T, b2.reshape(-1, 1), w3, freq.reshape(-1, 1))
    return pl.pallas_call(
        functools.partial(_hy_filter_kernel, seq_len),
        in_specs=[pl.BlockSpec(memory_space=pltpu.VMEM)] * len(args),
        out_specs=pl.BlockSpec(memory_space=pltpu.VMEM),
        out_shape=jax.ShapeDtypeStruct((2 * seq_len, HY_W), f32),
        compiler_params=pltpu.CompilerParams(vmem_limit_bytes=VMEM_LIMIT),
        name="hy_filter",
    )(*args)


def _hy_tiles(seq_len):
    return min(TM, seq_len // 2), min(TM, seq_len)


def _hy_fwd_kernel(nb, tk, seq_len, m_ref, zh_ref, cw_ref, cb_ref, hb_ref, rz_ref, rhs_ref):
    i = pl.program_id(0)
    nc = seq_len // HY_TL

    @pl.when(i == 0)
    def _():
        w, bias = cw_ref[...], cb_ref[...]
        for b in range(nb):
            def chunk(c, carry, b=b):
                r0 = pl.multiple_of(c * HY_TL, HY_TL)
                prev = zh_ref[b, pl.ds(pl.multiple_of(jnp.maximum(r0 - HALO_ROWS, 0), HALO_ROWS), HALO_ROWS), :]
                nxt = zh_ref[b, pl.ds(pl.multiple_of(jnp.minimum(r0 + HY_TL, seq_len - HALO_ROWS), HALO_ROWS),
                                      HALO_ROWS), :]
                _, u = _short_conv(zh_ref[b, pl.ds(r0, HY_TL), :], prev, nxt, c > 0, c < nc - 1, w, bias)
                rhs_ref[pl.ds(r0, HY_TL), b * HY_W:(b + 1) * HY_W] = u.astype(bf16)
                return carry
            lax.fori_loop(0, nc, chunk, 0)
        rhs_ref[:, nb * HY_W:(nb + 1) * HY_W] = hb_ref[0].astype(bf16)
        rhs_ref[:, (nb + 1) * HY_W:] = hb_ref[1].astype(bf16)

    c0 = nb * HY_W
    acc = jnp.dot(m_ref[...], rhs_ref[:, :c0], preferred_element_type=f32)
    p, w = acc[:tk], acc[tk:]
    hr = jnp.dot(m_ref[:tk, :], rhs_ref[:, c0:c0 + HY_W], preferred_element_type=f32)
    hi = jnp.dot(m_ref[tk:, :], rhs_ref[:, c0 + HY_W:], preferred_element_type=f32)
    qs = jnp.dot(m_ref[tk:tk + HALO_ROWS, :], rhs_ref[:, c0:c0 + HY_W], preferred_element_type=f32)[0:1]
    first = (lax.broadcasted_iota(jnp.int32, (tk, HY_W), 0) == 0) & (i == 0)
    for b in range(nb):
        cols = slice(b * HY_W, (b + 1) * HY_W)
        pu, wu = p[:, cols], w[:, cols]
        rz_ref[0, :, cols] = jnp.where(first, pu * hr, 2.0 * (pu * hr + wu * hi)).astype(bf16)
        rz_ref[1, :, cols] = jnp.where(first, wu * qs, 2.0 * (wu * hr - pu * hi)).astype(bf16)


def _hy_fwd(mat, zh, conv_w, conv_b, hb, batch, seq_len):
    tk, _ = _hy_tiles(seq_len)
    return pl.pallas_call(
        functools.partial(_hy_fwd_kernel, batch, tk, seq_len),
        grid=(seq_len // tk,),
        in_specs=[pl.BlockSpec((2 * tk, seq_len), lambda i: (i, 0)),
                  pl.BlockSpec((batch, seq_len, 3 * HY_W), lambda i: (0, 0, 0)),
                  pl.BlockSpec((3, 3 * HY_W), lambda i: (0, 0)),
                  pl.BlockSpec((1, 3 * HY_W), lambda i: (0, 0)),
                  pl.BlockSpec((2, seq_len, HY_W), lambda i: (0, 0, 0))],
        out_specs=pl.BlockSpec((2, tk, batch * HY_W), lambda i: (0, i, 0)),
        out_shape=jax.ShapeDtypeStruct((2, seq_len, batch * HY_W), bf16),
        scratch_shapes=[pltpu.VMEM((seq_len, (batch + 2) * HY_W), bf16)],
        compiler_params=_params("arbitrary"),
        name="hy_fwd",
    )(mat, zh, conv_w, conv_b.reshape(1, 3 * HY_W), hb)


def _hy_inv_kernel(nb, seq_len, nt, m_ref, rz_ref, zp_ref, zc_ref, zn_ref, cw_ref, cb_ref, d_ref, o_ref):
    i = pl.program_id(0)
    acc = jnp.dot(m_ref[...], rz_ref[...], preferred_element_type=f32)
    d, w, bias = d_ref[...], cw_ref[...], cb_ref[...]
    for b in range(nb):
        x0, u = _short_conv(zc_ref[b], zp_ref[b], zn_ref[b], i > 0, i < nt - 1, w, bias)
        y = acc[:, b * HY_W:(b + 1) * HY_W] * (1.0 / (2 * seq_len)) + u * d
        o_ref[b] = (x0 * y).astype(o_ref.dtype)


def _hy_inv(mat, rz, zh, conv_w, conv_b, d_bias, batch, seq_len):
    _, tm = _hy_tiles(seq_len)
    nt = seq_len // tm
    rh = tm // HALO_ROWS
    last = seq_len // HALO_ROWS - 1
    halo = lambda f: pl.BlockSpec((batch, HALO_ROWS, 3 * HY_W), f)
    return pl.pallas_call(
        functools.partial(_hy_inv_kernel, batch, seq_len, nt),
        grid=(nt,),
        in_specs=[pl.BlockSpec((tm, 2 * seq_len), lambda i: (i, 0)),
                  pl.BlockSpec((2 * seq_len, batch * HY_W), lambda i: (0, 0)),
                  halo(lambda i: (0, jnp.maximum(i * rh - 1, 0), 0)),
                  pl.BlockSpec((batch, tm, 3 * HY_W), lambda i: (0, i, 0)),
                  halo(lambda i: (0, jnp.minimum((i + 1) * rh, last), 0)),
                  pl.BlockSpec((3, 3 * HY_W), lambda i: (0, 0)),
                  pl.BlockSpec((1, 3 * HY_W), lambda i: (0, 0)),
                  pl.BlockSpec((1, HY_W), lambda i: (0, 0))],
        out_specs=pl.BlockSpec((batch, tm, HY_W), lambda i: (0, i, 0)),
        out_shape=jax.ShapeDtypeStruct((batch, seq_len, HY_W), bf16),
        compiler_params=_params("parallel"),
        name="hy_inv",
    )(mat, rz.reshape(2 * seq_len, batch * HY_W), zh, zh, zh, conv_w, conv_b.reshape(1, 3 * HY_W),
      d_bias.reshape(1, HY_W))


@functools.lru_cache(maxsize=None)
def _hy_consts_np(seq_len):
    n = 2 * seq_len
    k = np.arange(seq_len, dtype=np.int64)
    ang = (2.0 * np.pi / n) * ((k[:, None] * k[None, :]) % n).astype(np.float64)
    cosm = np.cos(ang)
    sinm = np.sin(ang)
    sinm[0, :] = np.where(k % 2 == 0, 1.0, -1.0)
    tk, _ = _hy_tiles(seq_len)
    fwd = np.concatenate([np.concatenate([cosm[i:i + tk], sinm[i:i + tk]], axis=0)
                          for i in range(0, seq_len, tk)], axis=0)
    inv = np.concatenate([cosm, sinm.T], axis=1)
    t = np.linspace(0.0, 1.0, seq_len, dtype=np.float32)[:, None]
    bands = (FILTER_EMB - 1) // 2
    fb = np.linspace(1e-4, bands - 1, bands, dtype=np.float32)[None, :]
    w = (np.float32(2.0 * math.pi) * np.arange(seq_len, dtype=np.float32)[:, None] / np.float32(seq_len)).astype(np.float32)
    feats = np.concatenate([t, np.cos(fb * w), -np.sin(fb * w)], axis=-1).astype(np.float32)
    feats = np.ascontiguousarray(np.pad(feats, ((0, 0), (0, LANES - FILTER_EMB))).T)
    decay_hi = math.log(HY_DECAY_TARGET) / HY_DECAY_HI_PCT
    decay_lo = math.log(HY_DECAY_TARGET) / HY_DECAY_LO_PCT
    deltas = np.abs(np.linspace(decay_lo, decay_hi, HY_W, dtype=np.float32))
    window = np.exp(-t * deltas[None, :]).astype(np.float32)
    return fwd, inv, feats, window


def _hy_consts(seq_len):
    fwd, inv, feats, window = _hy_consts_np(seq_len)
    as_bf = lambda m: jnp.asarray(m, dtype=f32).astype(bf16)
    return {"fwd": as_bf(fwd), "inv": as_bf(inv), "feats": jnp.asarray(feats), "window": jnp.asarray(window)}


def _hyena(zh, lp, consts, batch, seq_len):
    zh = zh.reshape(batch, seq_len, 3 * HY_W)
    hb = _hy_filter(seq_len, consts, lp["hy_f_w1"], lp["hy_f_b1"], lp["hy_f_w2"], lp["hy_f_b2"],
                    lp["hy_f_w3"], lp["hy_freq"]).reshape(2, seq_len, HY_W)
    rz = _hy_fwd(consts["fwd"], zh, lp["hy_conv_w"], lp["hy_conv_b"], hb, batch, seq_len)
    yh = _hy_inv(consts["inv"], rz, zh, lp["hy_conv_w"], lp["hy_conv_b"], lp["hy_bias"], batch, seq_len)
    return yh.reshape(batch * seq_len, HY_W)


def _gelu(x):
    return 0.5 * x * (1.0 + jnp.tanh(math.sqrt(2.0 / math.pi) * (x + 0.044715 * (x * x * x))))


def _gmlp(z, ws_ref, bias):
    g = _gelu(z)
    outs = []
    for c in range(TM // GM_CHUNK):
        rows = slice(c * GM_CHUNK, (c + 1) * GM_CHUNK)
        u = g[rows, :GM_W]
        v = g[rows, GM_W:].astype(bf16)
        s = jnp.concatenate(
            [jnp.dot(ws_ref[h], v[:, h * GM_HEAD_DIM:(h + 1) * GM_HEAD_DIM], preferred_element_type=f32)
             for h in range(GM_HEADS)], axis=1)
        outs.append(u * (s + bias))
    return jnp.concatenate(outs, axis=0)


def _out_kernel(a_ref, yh_ref, zg_ref, ws_ref, gb_ref, x_ref, mod_ref, og_ref, w_ref, g2_ref, rw_ref,
                x1_ref, h2_ref, aff_ref):
    og = og_ref[...]
    yg = _gmlp(zg_ref[...].astype(f32), ws_ref, gb_ref[...])
    mixed_in = jnp.concatenate([
        _rms(a_ref[...].astype(f32)) * og[:, :ATTN_W],
        _rms(yh_ref[...].astype(f32)) * og[:, ATTN_W:ATTN_W + HY_W],
        _rms(yg) * og[:, ATTN_W + HY_W:]], axis=1)
    mixed = jnp.dot(mixed_in.astype(bf16), w_ref[...], preferred_element_type=f32)
    m = mod_ref[0]
    x1 = x_ref[...] + m[2:3] * mixed
    x1_ref[...] = x1
    h2 = _rms(x1) * g2_ref[...] * (1.0 + m[4:5]) + m[3:4]
    for j in range(ROW_CHUNKS):
        h2_ref[pl.ds(j, TM, stride=ROW_CHUNKS), :] = h2[:, j * LANES:(j + 1) * LANES]
    nt = (((1,), (1,)), ((), ()))
    h_hi = h2.astype(bf16)
    h_lo = (h2 - h_hi.astype(f32)).astype(bf16)
    rw = rw_ref[...]
    t = lax.dot_general(rw, h_hi, nt, preferred_element_type=f32)
    logits = (t[:N_EXPERTS] + t[N_EXPERTS:]) + lax.dot_general(rw[:N_EXPERTS], h_lo, nt, preferred_element_type=f32)
    e = jnp.exp(logits - jnp.max(logits, axis=0, keepdims=True))
    aff_ref[...] = e / jnp.sum(e, axis=0, keepdims=True)


def _out_proj(a, yh, zg, ws_bf, gm_b, x, mod, out_g, w_out_bf, g2, router_wt):
    n = x.shape[0]
    per_mod = n // mod.shape[0]
    row = lambda w: pl.BlockSpec((TM, w), lambda i: (i, 0))
    const = lambda s: pl.BlockSpec(s, lambda i: (0,) * len(s))
    gm_bias = jnp.repeat(gm_b.T, GM_HEAD_DIM, axis=1)
    return pl.pallas_call(
        _out_kernel,
        grid=(n // TM,),
        in_specs=[row(ATTN_W), row(HY_W), row(2 * GM_W), const((GM_HEADS, GM_CHUNK, GM_CHUNK)),
                  const((GM_CHUNK, GM_W)), row(D_MODEL),
                  pl.BlockSpec((1, 6, D_MODEL), lambda i: (i * TM // per_mod, 0, 0)),
                  const((1, MIX_W)), const((MIX_W, D_MODEL)), const((1, D_MODEL)), const((2 * N_EXPERTS, D_MODEL))],
        out_specs=[row(D_MODEL), pl.BlockSpec((TM * ROW_CHUNKS, LANES), lambda i: (i, 0)),
                   pl.BlockSpec((N_EXPERTS, TM), lambda i: (0, i))],
        out_shape=[jax.ShapeDtypeStruct((n, D_MODEL), f32), jax.ShapeDtypeStruct((n * ROW_CHUNKS, LANES), f32),
                   jax.ShapeDtypeStruct((N_EXPERTS, n), f32)],
        compiler_params=_params("parallel"),
        name="out_proj",
    )(a, yh, zg, ws_bf, gm_bias, x, mod, out_g.reshape(1, MIX_W), w_out_bf, g2.reshape(1, D_MODEL), router_wt)


RT_CHUNK = 512


def _prefix_incl(x01, tri):
    n = x01.shape[1]
    carry = jnp.zeros((x01.shape[0], 1), f32)
    parts = []
    for c in range(n // RT_CHUNK):
        piece = x01[:, c * RT_CHUNK:(c + 1) * RT_CHUNK]
        parts.append(jnp.dot(piece.astype(bf16), tri, preferred_element_type=f32) + carry)
        carry = carry + jnp.sum(piece, axis=1, keepdims=True)
    return jnp.concatenate(parts, axis=1)


def _route_kernel(cap, aff_ref, idx_ref, gate_ref):
    aff = aff_ref[...]
    n = aff.shape[1]

    def step(it, t):
        cand = t | (jnp.int32(1) << (30 - it))
        cnt = jnp.sum(jnp.where(aff >= lax.bitcast_convert_type(cand, f32), 1.0, 0.0), axis=1, keepdims=True)
        return jnp.where(cnt >= cap, cand, t)

    thr_bits = lax.fori_loop(0, 31, step, jnp.zeros((aff.shape[0], 1), jnp.int32))
    thr = lax.bitcast_convert_type(thr_bits, f32)
    gt = jnp.where(aff > thr, 1.0, 0.0)
    eq = jnp.where(aff == thr, 1.0, 0.0)
    room = cap - jnp.sum(gt, axis=1, keepdims=True)
    r = lax.broadcasted_iota(jnp.int32, (RT_CHUNK, RT_CHUNK), 0)
    c = lax.broadcasted_iota(jnp.int32, (RT_CHUNK, RT_CHUNK), 1)
    tri = jnp.where(r <= c, 1.0, 0.0).astype(bf16)
    sel = jnp.maximum(gt, jnp.where(_prefix_incl(eq, tri) <= room, eq, 0.0))
    slot = _prefix_incl(sel, tri) - 1.0

    tok = lax.broadcasted_iota(jnp.int32, aff.shape, 1)
    dist = jnp.where(sel > 0.0, tok - slot.astype(jnp.int32), 0)
    g = aff
    for b in range(max(1, (n - 1).bit_length())):
        sh = 1 << b
        dist_s = pltpu.roll(dist, n - sh, axis=1)
        take = (dist_s & sh) != 0
        leave = (dist & sh) != 0
        tok = jnp.where(take, pltpu.roll(tok, n - sh, axis=1), tok)
        g = jnp.where(take, pltpu.roll(g, n - sh, axis=1), g)
        dist = jnp.where(take, dist_s, jnp.where(leave, 0, dist))
    idx_ref[...] = tok[:, :cap] * ROW_CHUNKS
    gate_ref[...] = g[:, :cap]


def _route(aff_t, cap):
    return pl.pallas_call(
        functools.partial(_route_kernel, cap),
        in_specs=[pl.BlockSpec(memory_space=pltpu.VMEM)],
        out_specs=[pl.BlockSpec(memory_space=pltpu.VMEM)] * 2,
        out_shape=[jax.ShapeDtypeStruct((N_EXPERTS, cap), jnp.int32), jax.ShapeDtypeStruct((N_EXPERTS, cap), f32)],
        compiler_params=pltpu.CompilerParams(vmem_limit_bytes=VMEM_LIMIT),
        name="route",
    )(aff_t)


GATHER_UNROLL = 8
SCATTER_UNROLL = 8
CMB_EXPERTS = 4
CMB_TM = 1024


def _moe_ffn_kernel(cap, stride, idx_ref, h_ref, w1_ref, w3_ref, w2_ref, y_ref, tile_ref, xb_ref, acc_ref):
    e = pl.program_id(0)
    slab = ROW_CHUNKS * stride
    cur = pl.multiple_of((e % 2) * slab, SUBLANES)
    nxt = pl.multiple_of(((e + 1) % 2) * slab, SUBLANES)

    def gather_row(expert, slab_row, r):
        src = pl.multiple_of(idx_ref[expert * cap + r], ROW_CHUNKS)
        tile_ref[pl.ds(slab_row + r, ROW_CHUNKS, stride=stride), :] = h_ref[pl.ds(src, ROW_CHUNKS), :]

    @pl.when(e == 0)
    def _():
        def gather(q, carry):
            for t in range(GATHER_UNROLL):
                gather_row(0, cur, q * GATHER_UNROLL + t)
            return carry
        lax.fori_loop(0, cap // GATHER_UNROLL, gather, 0)

    xb_ref[...] = jnp.concatenate(
        [tile_ref[pl.ds(cur + j * stride, cap), :] for j in range(ROW_CHUNKS)], axis=1).astype(bf16)

    e_next = jnp.minimum(e + 1, N_EXPERTS - 1)
    n_f = D_EXPERT // MOE_F_TILE
    xb = xb_ref[...]
    for f in range(n_f):
        cols = slice(f * MOE_F_TILE, (f + 1) * MOE_F_TILE)
        a = jnp.dot(xb, w1_ref[0, 0, :, cols].astype(bf16), preferred_element_type=f32)
        b = jnp.dot(xb, w3_ref[0, 0, :, cols].astype(bf16), preferred_element_type=f32)
        he = (a * _sigmoid(a) * b).astype(bf16)
        part = jnp.dot(he, w2_ref[0, 0, cols, :].astype(bf16), preferred_element_type=f32)
        if f == 0:
            acc_ref[...] = part
        else:
            acc_ref[...] += part
        for r in range(f * cap // n_f, (f + 1) * cap // n_f):
            gather_row(e_next, nxt, r)

    y = acc_ref[...]
    for j in range(ROW_CHUNKS):
        y_ref[0, pl.ds(j, cap, stride=ROW_CHUNKS), :] = y[:, j * LANES:(j + 1) * LANES]


def _moe_ffn(layer, h3, idx, w1, w3, w2):
    cap = idx.shape[1]
    stride = cap + SUBLANES
    grid_spec = pltpu.PrefetchScalarGridSpec(
        num_scalar_prefetch=1,
        grid=(N_EXPERTS,),
        in_specs=[
            pl.BlockSpec(memory_space=pltpu.VMEM),
            pl.BlockSpec((1, 1, D_MODEL, D_EXPERT), lambda e, idx: (layer, e, 0, 0)),
            pl.BlockSpec((1, 1, D_MODEL, D_EXPERT), lambda e, idx: (layer, e, 0, 0)),
            pl.BlockSpec((1, 1, D_EXPERT, D_MODEL), lambda e, idx: (layer, e, 0, 0)),
        ],
        out_specs=pl.BlockSpec((1, ROW_CHUNKS * cap, LANES), lambda e, idx: (e, 0, 0)),
        scratch_shapes=[pltpu.VMEM((2 * ROW_CHUNKS * stride, LANES), f32),
                        pltpu.VMEM((cap, D_MODEL), bf16),
                        pltpu.VMEM((cap, D_MODEL), f32)],
    )
    return pl.pallas_call(
        functools.partial(_moe_ffn_kernel, cap, stride),
        grid_spec=grid_spec,
        out_shape=jax.ShapeDtypeStruct((N_EXPERTS, ROW_CHUNKS * cap, LANES), f32),
        compiler_params=_params("arbitrary"),
        name="moe_ffn",
    )(idx.reshape(-1), h3, w1, w3, w2)


def _moe_combine_kernel(cap, final, idx_ref, gate_ref, y_ref, x_ref, mod_ref, g_ref, o_ref, acc_ref):
    s = pl.program_id(0)
    scatter_steps = N_EXPERTS // CMB_EXPERTS

    @pl.when(s == 0)
    def _():
        acc_ref[...] = jnp.zeros_like(acc_ref)

    @pl.when(s < scatter_steps)
    def _():
        for k in range(CMB_EXPERTS):
            base = (s * CMB_EXPERTS + k) * cap

            def scatter(q, carry, k=k, base=base):
                group = SCATTER_UNROLL * ROW_CHUNKS
                y = y_ref[k, pl.ds(pl.multiple_of(q * group, group), group), :]
                rows = [pl.multiple_of(idx_ref[base + q * SCATTER_UNROLL + t], ROW_CHUNKS)
                        for t in range(SCATTER_UNROLL)]
                new = [acc_ref[pl.ds(rows[t], ROW_CHUNKS), :]
                       + gate_ref[base + q * SCATTER_UNROLL + t] * y[t * ROW_CHUNKS:(t + 1) * ROW_CHUNKS]
                       for t in range(SCATTER_UNROLL)]
                for t in range(SCATTER_UNROLL):
                    acc_ref[pl.ds(rows[t], ROW_CHUNKS), :] = new[t]
                return carry
            lax.fori_loop(0, cap // SCATTER_UNROLL, scatter, 0)

    @pl.when(s >= scatter_steps)
    def _():
        first = pl.multiple_of((s - scatter_steps) * (CMB_TM * ROW_CHUNKS), CMB_TM * ROW_CHUNKS)
        moe = jnp.concatenate([acc_ref[pl.ds(first + j, CMB_TM, stride=ROW_CHUNKS), :] for j in range(ROW_CHUNKS)],
                              axis=1)
        x = x_ref[...] + mod_ref[0][5:6] * moe
        o_ref[...] = _rms(x) * g_ref[...] if final else x


def _moe_combine(idx, gate, y_cm, x1, mod, final_g, final):
    n = x1.shape[0]
    cap = idx.shape[1]
    per_mod = n // mod.shape[0]
    scatter_steps = N_EXPERTS // CMB_EXPERTS
    tile = lambda s: jnp.maximum(s - scatter_steps, 0)
    row = pl.BlockSpec((CMB_TM, D_MODEL), lambda s, *_: (tile(s), 0))
    grid_spec = pltpu.PrefetchScalarGridSpec(
        num_scalar_prefetch=2,
        grid=(scatter_steps + n // CMB_TM,),
        in_specs=[pl.BlockSpec((CMB_EXPERTS, ROW_CHUNKS * cap, LANES),
                               lambda s, *_: (jnp.minimum(s, scatter_steps - 1), 0, 0)),
                  row,
                  pl.BlockSpec((1, 6, D_MODEL), lambda s, *_: (tile(s) * CMB_TM // per_mod, 0, 0)),
                  pl.BlockSpec((1, D_MODEL), lambda s, *_: (0, 0))],
        out_specs=row,
        scratch_shapes=[pltpu.VMEM((n * ROW_CHUNKS, LANES), f32)],
    )
    return pl.pallas_call(
        functools.partial(_moe_combine_kernel, cap, final),
        grid_spec=grid_spec,
        out_shape=jax.ShapeDtypeStruct((n, D_MODEL), f32),
        compiler_params=_params("arbitrary"),
        name="moe_combine",
    )(idx.reshape(-1), gate.reshape(-1), y_cm, x1, mod, final_g.reshape(1, D_MODEL))


def _expert_choice(layer, h3, aff_t, x1, mod, final_g, final, w1, w3, w2):
    n = x1.shape[0]
    cap = max(1, EC_CAPACITY * n // N_EXPERTS)
    idx, gate = _route(aff_t, cap)
    y_cm = _moe_ffn(layer, h3, idx, w1, w3, w2)
    return _moe_combine(idx, gate, y_cm, x1, mod, final_g, final)


def _split_bf16(w):
    hi = w.astype(bf16)
    lo = (w - hi.astype(f32)).astype(bf16)
    return jnp.concatenate([hi, lo], axis=0)


def _stream(x, mods, layers, experts, batch, seq_len, final_g, caches=None):
    consts = _hy_consts(seq_len)
    rope_tabs = _rope_tables(seq_len) if caches is not None else None
    new_caches = None
    for l, lp in enumerate(layers):
        if caches is None:
            qkv, zh, zg, *new_caches = _in_proj(x, mods[l], lp["norm1_g"], lp["w_in"], seq_len, cache_layer=l,
                                                caches_so_far=new_caches)
            a = _ctx_attention(qkv, lp["attn_sink"], batch, seq_len)
        else:
            qkv, zh, zg = _in_proj(x, mods[l], lp["norm1_g"], lp["w_in"], seq_len, rope_tabs=rope_tabs)
            a = _lat_attention(qkv, lp["attn_sink"], caches[0], caches[1], l, batch, seq_len)
        yh = _hyena(zh, lp, consts, batch, seq_len)
        x1, h3, aff_t = _out_proj(a, yh, zg, lp["gm_ws"], lp["gm_b"], x, mods[l], lp["out_norm_g"], lp["w_out"],
                                  lp["norm2_g"], lp["router_wt"])
        x = _expert_choice(l, h3, aff_t, x1, mods[l], final_g, l == len(layers) - 1, *experts)
    return x, new_caches


def kernel(x_prompt, x_sample, c, cache_k, cache_v, c_ctx, norm1_g, norm2_g, ada_w, ada_b, w_in, attn_sink,
           hy_conv_w, hy_conv_b, hy_f_w1, hy_f_b1, hy_f_w2, hy_f_b2, hy_f_w3, hy_freq, hy_bias, gm_ws, gm_b,
           out_norm_g, w_out, router_w, exp_w1, exp_w3, exp_w2, final_g):
    batch, seq, _ = x_prompt.shape
    dbatch, dseq, _ = x_sample.shape
    past = cache_k.shape[2]

    cvec = jnp.concatenate([c_ctx[None], c, jnp.zeros((SUBLANES - 1 - dbatch, D_MODEL), f32)], axis=0)
    mod = _ada(cvec, ada_w, ada_b)
    mods_ctx = [mod[l, 0:1].reshape(1, 6, D_MODEL) for l in range(DEPTH)]
    mods_lat = [mod[l, 1:1 + dbatch].reshape(dbatch, 6, D_MODEL) for l in range(DEPTH)]

    w_in_bf, w_out_bf, gm_ws_bf = w_in.astype(bf16), w_out.astype(bf16), gm_ws.astype(bf16)
    layers = []
    for l in range(DEPTH):
        layers.append({
            "norm1_g": norm1_g[l], "norm2_g": norm2_g[l], "w_in": w_in_bf[l], "attn_sink": attn_sink[l],
            "hy_conv_w": hy_conv_w[l], "hy_conv_b": hy_conv_b[l], "hy_f_w1": hy_f_w1[l], "hy_f_b1": hy_f_b1[l],
            "hy_f_w2": hy_f_w2[l], "hy_f_b2": hy_f_b2[l], "hy_f_w3": hy_f_w3[l], "hy_freq": hy_freq[l],
            "hy_bias": hy_bias[l], "gm_ws": gm_ws_bf[l], "gm_b": gm_b[l], "out_norm_g": out_norm_g[l],
            "w_out": w_out_bf[l], "router_wt": _split_bf16(router_w[l].T)})
    experts = (exp_w1, exp_w3, exp_w2)

    yp, kvs = _stream(x_prompt.reshape(batch * seq, D_MODEL), mods_ctx, layers, experts, batch, seq, final_g)
    caches = (cache_k.reshape(dbatch, DEPTH, past, KV_W), cache_v.reshape(dbatch, DEPTH, past, KV_W))
    ys, _ = _stream(x_sample.reshape(dbatch * dseq, D_MODEL), mods_lat, layers, experts, dbatch, dseq, final_g,
                   caches=caches)

    new_k, new_v = (a.reshape(batch, DEPTH, seq, N_KV, HEAD_DIM) for a in kvs)
    return (yp.reshape(batch, seq, D_MODEL), ys.reshape(dbatch, dseq, D_MODEL), new_k, new_v)
```
